```python
import jax, jax.numpy as jnp
from jax import lax
import numpy as np

D_MODEL = 1024
BATCH = 16
SEQ = 256
DEPTH = 4
DEC_BATCH = 4
DEC_SEQ = 1024
PAST_LEN = 512

GRID_W = 64
WIN_H = 8
WIN_W = 16
HEAD_DIM = 64
A_HEADS = 8
A_WIDTH = A_HEADS * HEAD_DIM
R_HEADS = 4
R_KEY_DIM = 64
R_VAL_DIM = 64
R_WIDTH = R_HEADS * R_KEY_DIM
F_GROUPS = 4
F_GROUP_DIM = 64
F_WIDTH = F_GROUPS * F_GROUP_DIM
MIX_WIDTH = A_WIDTH + R_WIDTH + F_WIDTH
IN_COLS = 4 * A_WIDTH + 5 * R_WIDTH + 2 * F_WIDTH
CHUNK = 64
Q_BLOCK = 128
EPS = 1e-6

kernel_name = "hybrid_natten_hgrn2_fnet_diffusion_step"


def _rmsnorm(x, g):
    x32 = x.astype(jnp.float32)
    y = x32 * lax.rsqrt(jnp.mean(x32 * x32, axis=-1, keepdims=True) + EPS)
    return (y * g.astype(jnp.float32)).astype(x.dtype)


def _split_columns(p):
    sizes = [A_WIDTH] * 4 + [R_WIDTH] * 5 + [F_WIDTH] * 2
    out, start = [], 0
    for s in sizes:
        out.append(p[..., start:start + s])
        start += s
    return out


def _heads(t, n):
    return t.reshape(t.shape[:-1] + (n, t.shape[-1] // n))


def _context_attention(q, k, v):
    B, S, H, dh = q.shape
    nb = S // Q_BLOCK
    qb = q.reshape(B, nb, Q_BLOCK, H, dh).transpose(1, 0, 2, 3, 4)
    scale = dh ** -0.5

    def blk(qi):
        s = jnp.einsum('bqhd,bkhd->bhqk', qi, k).astype(jnp.float32) * scale
        p = jax.nn.softmax(s, axis=-1).astype(v.dtype)
        return jnp.einsum('bhqk,bkhd->bqhd', p, v)

    o = lax.map(blk, qb)
    return o.transpose(1, 0, 2, 3, 4).reshape(B, S, H, dh)


def _neighbourhood_attention(q, k, v, k_ctx, v_ctx, rpb):
    B, N, H, dh = q.shape
    rows = N // GRID_W
    kh = min(WIN_H, rows)
    qg = q.reshape(B, rows, GRID_W, H, dh)
    kg = k.reshape(B, rows, GRID_W, H, dh)
    vg = v.reshape(B, rows, GRID_W, H, dh)
    cols = jnp.arange(GRID_W)
    c0 = jnp.clip(cols - WIN_W // 2, 0, GRID_W - WIN_W)
    col_in = (cols[None, :] >= c0[:, None]) & (cols[None, :] < c0[:, None] + WIN_W)
    col_idx = jnp.clip(cols[None, :] - cols[:, None] + WIN_W - 1, 0, 2 * WIN_W - 2)
    scale = dh ** -0.5

    def one_row(r):
        r0 = jnp.clip(r - kh // 2, 0, rows - kh)
        q_r = lax.dynamic_index_in_dim(qg, r, axis=1, keepdims=False)
        k_r = lax.dynamic_slice_in_dim(kg, r0, kh, axis=1)
        v_r = lax.dynamic_slice_in_dim(vg, r0, kh, axis=1)
        row_idx = r0 + jnp.arange(kh) - r + WIN_H - 1
        bias = rpb[:, row_idx[:, None, None], col_idx[None]]
        s_lat = jnp.einsum('bqhd,bjkhd->bhqjk', q_r, k_r).astype(jnp.float32) * scale
        s_lat = s_lat + bias.transpose(0, 2, 1, 3)[None].astype(jnp.float32)
        s_lat = jnp.where(col_in[:, None, :], s_lat, -jnp.inf)
        s_ctx = jnp.einsum('bqhd,bphd->bhqp', q_r, k_ctx).astype(jnp.float32) * scale
        s = jnp.concatenate([s_lat.reshape(B, H, GRID_W, kh * GRID_W), s_ctx], axis=-1)
        p = jax.nn.softmax(s, axis=-1).astype(v.dtype)
        v_all = jnp.concatenate([v_r.reshape(B, kh * GRID_W, H, dh), v_ctx], axis=1)
        return jnp.einsum('bhqk,bkhd->bqhd', p, v_all)

    o = lax.map(one_row, jnp.arange(rows))
    return o.transpose(1, 0, 2, 3, 4).reshape(B, N, H, dh)


def _hgrn_gates(z, lb):
    log_f = jnp.logaddexp(jnp.log(lb), jnp.log1p(-lb) + jax.nn.log_sigmoid(z))
    k = (1.0 - lb) * jax.nn.sigmoid(-z)
    return log_f, k


def _hgrn_scan(q, k, v, log_f, s0):
    B, N, H, _ = q.shape
    nc = N // CHUNK

    def to_chunks(t):
        return t.reshape(B, nc, CHUNK, H, t.shape[-1]).transpose(1, 0, 3, 2, 4)

    causal = jnp.tril(jnp.ones((CHUNK, CHUNK), dtype=bool))

    def step(s, inp):
        qc, kc, vc, gc = inp
        G = jnp.cumsum(gc, axis=2)
        diff = G[:, :, :, None, :] - G[:, :, None, :, :]
        decay = jnp.exp(jnp.where(causal[:, :, None], diff, -jnp.inf))
        attn = jnp.einsum('bhtd,bhsd,bhtsd->bhts', qc, kc, decay)
        o = jnp.einsum('bhtd,bhde->bhte', qc * jnp.exp(G), s) + jnp.einsum('bhts,bhse->bhte', attn, vc)
        G_last = G[:, :, -1:, :]
        s_new = jnp.exp(G_last[:, :, 0, :])[..., None] * s + jnp.einsum(
            'bhsd,bhse->bhde', kc * jnp.exp(G_last - G), vc)
        return s_new, o

    s_fin, o = lax.scan(step, s0, (to_chunks(q), to_chunks(k), to_chunks(v), to_chunks(log_f)))
    o = o.transpose(1, 0, 3, 2, 4).reshape(B, N, H, v.shape[-1])
    return o, s_fin


def _hgrn_bidir(qr, zf, zb, ir, lb_f, lb_b, s0):
    f32 = jnp.float32
    q = jax.nn.silu(_heads(qr.astype(f32), R_HEADS))
    v = _heads(ir.astype(f32), R_HEADS)
    s0 = s0.astype(f32)
    lf_f, k_f = _hgrn_gates(_heads(zf.astype(f32), R_HEADS), lb_f.reshape(R_HEADS, R_KEY_DIM))
    lf_b, k_b = _hgrn_gates(_heads(zb.astype(f32), R_HEADS), lb_b.reshape(R_HEADS, R_KEY_DIM))
    flip = lambda t: t[:, ::-1]
    o_f, s_f = _hgrn_scan(q, k_f, v, lf_f, s0[:, 0])
    o_b, s_b = _hgrn_scan(flip(q), flip(k_b), flip(v), flip(lf_b), s0[:, 1])
    return o_f + flip(o_b), jnp.stack([s_f, s_b], axis=1)


def _fourier(u, w_f):
    B, N, _ = u.shape
    ug = u.astype(jnp.float32).reshape(B, N, F_GROUPS, F_GROUP_DIM)
    y = jnp.fft.fftn(ug, axes=(1, 3), norm='ortho').real.reshape(B, N, F_WIDTH).astype(u.dtype)
    return y @ w_f


def _trunk_layer(x, mod, g_pre, w_in, rpb, lb_f, lb_b, g_hgrn, w_fnet, w_out, g_post,
                 ctx_kv=None, ctx_state=None):
    shift, scale, gate = jnp.split(mod, 3, axis=-1)
    h = _rmsnorm(x, g_pre) * (1 + scale) + shift
    proj = h @ w_in
    qa, ka, va, ga, qr, zf, zb, ir, gr, uf, gf = _split_columns(proj)
    qa, ka, va = _heads(qa, A_HEADS), _heads(ka, A_HEADS), _heads(va, A_HEADS)
    B, N, _ = x.shape
    if ctx_kv is None:
        oa = _context_attention(qa, ka, va)
        s0 = jnp.zeros((B, 2, R_HEADS, R_KEY_DIM, R_VAL_DIM), jnp.float32)
    else:
        oa = _neighbourhood_attention(qa, ka, va, ctx_kv[0], ctx_kv[1], rpb)
        s0 = ctx_state
    o_r, s_fin = _hgrn_bidir(qr, zf, zb, ir, lb_f, lb_b, s0)
    o_r = _rmsnorm(o_r, g_hgrn.reshape(R_HEADS, R_VAL_DIM)).reshape(B, N, R_WIDTH).astype(x.dtype)
    o_f = _fourier(uf, w_fnet)
    mixed = jnp.concatenate([
        oa.reshape(B, N, A_WIDTH) * jax.nn.silu(ga),
        o_r * jax.nn.silu(gr),
        o_f * jax.nn.silu(gf)], axis=-1)
    out = mixed @ w_out
    x = x + gate * _rmsnorm(out, g_post)
    return x, ka, va, s_fin


def setup_inputs(seed: int = 0) -> dict:
    key = jax.random.key(seed)
    ks = jax.random.split(key, 17)
    nrm = jax.random.normal
    f32 = jnp.float32
    return {
        'x_prompt': nrm(ks[0], (BATCH, SEQ, D_MODEL), f32),
        'x_sample': nrm(ks[1], (DEC_BATCH, DEC_SEQ, D_MODEL), f32),
        'cache_attn_k': nrm(ks[2], (DEC_BATCH, DEPTH, PAST_LEN, A_HEADS, HEAD_DIM), f32),
        'cache_attn_v': nrm(ks[3], (DEC_BATCH, DEPTH, PAST_LEN, A_HEADS, HEAD_DIM), f32),
        'state_hgrn': 0.5 * nrm(ks[4], (DEC_BATCH, DEPTH, 2, R_HEADS, R_KEY_DIM, R_VAL_DIM), f32),
        'c': nrm(ks[5], (DEC_BATCH, D_MODEL), f32),
        'c_ctx': nrm(ks[6], (D_MODEL,), f32),
        'w_ada': 0.5 * D_MODEL ** -0.5 * nrm(ks[7], (DEPTH, D_MODEL, 3 * D_MODEL), f32),
        'b_ada': 0.02 * nrm(ks[8], (DEPTH, 3 * D_MODEL), f32),
        'g_pre': 1.0 + 0.02 * nrm(ks[9], (DEPTH, D_MODEL), f32),
        'w_in': D_MODEL ** -0.5 * nrm(ks[10], (DEPTH, D_MODEL, IN_COLS), f32),
        'rpb': 0.1 * nrm(ks[11], (DEPTH, A_HEADS, 2 * WIN_H - 1, 2 * WIN_W - 1), f32),
        'lb_logits': nrm(ks[12], (2, DEPTH, R_WIDTH), f32),
        'g_hgrn': 1.0 + 0.02 * nrm(ks[13], (DEPTH, R_WIDTH), f32),
        'w_fnet': F_WIDTH ** -0.5 * nrm(ks[14], (DEPTH, F_WIDTH, F_WIDTH), f32),
        'w_out': MIX_WIDTH ** -0.5 * nrm(ks[15], (DEPTH, MIX_WIDTH, D_MODEL), f32),
        'g_post': 1.0 + 0.02 * nrm(ks[16], (DEPTH, D_MODEL), f32),
    }


def reference(x_prompt, x_sample, cache_attn_k, cache_attn_v, state_hgrn, c, c_ctx,
              w_ada, b_ada, g_pre, w_in, rpb, lb_logits, g_hgrn, w_fnet, w_out, g_post):
    lb = jnp.cumsum(jax.nn.softmax(lb_logits.astype(jnp.float32), axis=1), axis=1)
    lb = jnp.maximum(lb - lb[:, :1], 0.0)
    yp, ys = x_prompt, x_sample
    new_k, new_v, new_s = [], [], []
    for l in range(DEPTH):
        params = (g_pre[l], w_in[l], rpb[l], lb[0, l], lb[1, l], g_hgrn[l], w_fnet[l], w_out[l], g_post[l])
        mod_ctx = jax.nn.silu(c_ctx) @ w_ada[l] + b_ada[l]
        yp, k_l, v_l, s_l = _trunk_layer(yp, mod_ctx, *params)
        new_k.append(k_l)
        new_v.append(v_l)
        new_s.append(s_l.astype(x_prompt.dtype))
        mod_lat = (jax.nn.silu(c) @ w_ada[l] + b_ada[l])[:, None, :]
        ys, _, _, _ = _trunk_layer(ys, mod_lat, *params,
                                   ctx_kv=(cache_attn_k[:, l], cache_attn_v[:, l]),
                                   ctx_state=state_hgrn[:, l])
    new_cache_k = jnp.stack(new_k, axis=1)
    new_cache_v = jnp.stack(new_v, axis=1)
    new_state_hgrn = jnp.stack(new_s, axis=1)
    return (yp, ys, new_cache_k, new_cache_v, new_state_hgrn)
```

```python
import functools

import numpy as np
import jax
import jax.numpy as jnp
from jax import lax
from jax.experimental import pallas as pl
from jax.experimental.pallas import tpu as pltpu

F32 = jnp.float32
BF16 = jnp.bfloat16

D_MODEL = 1024
DEPTH = 4
GRID_W = 64
WIN_H = 8
WIN_W = 16
HEAD_DIM = 64
A_HEADS = 8
A_WIDTH = A_HEADS * HEAD_DIM
R_HEADS = 4
R_DIM = 64
R_WIDTH = R_HEADS * R_DIM
F_GROUPS = 4
F_GROUP_DIM = 64
F_WIDTH = F_GROUPS * F_GROUP_DIM
PA_COLS = 4 * A_WIDTH
PR_COLS = 5 * R_WIDTH
PF_COLS = 2 * F_WIDTH
IN_COLS = PA_COLS + PR_COLS + PF_COLS
CHUNK = 64
EPS = 1e-6
LANES = 128
NEG_INF = float("-inf")
VMEM_LIMIT = 56 * 1024 * 1024

LEVEL_HALVES = (32, 16, 8, 4, 2, 1)
N_DROWS = 2 + len(LEVEL_HALVES)


def _cparams(sem):
    return pltpu.CompilerParams(dimension_semantics=sem, vmem_limit_bytes=VMEM_LIMIT)


def _silu(x):
    return x / (1.0 + jnp.exp(-x))


def _dot(a, b):
    return jnp.dot(a, b, preferred_element_type=F32)


def _dot_nt(a, b):
    return lax.dot_general(a, b, (((1,), (1,)), ((), ())), preferred_element_type=F32)


def _dot_tn(a, b):
    return lax.dot_general(a, b, (((0,), (0,)), ((), ())), preferred_element_type=F32)


def _split2(x):
    hi = x.astype(BF16)
    lo = (x - hi.astype(F32)).astype(BF16)
    return hi, lo


def _mod_kernel(cc_ref, w_ref, b_ref, o_ref):
    a_hi, a_lo = _split2(_silu(cc_ref[...]))
    w_hi, w_lo = _split2(w_ref[...])
    acc = _dot(a_hi, w_hi) + _dot(a_hi, w_lo) + _dot(a_lo, w_hi)
    o_ref[...] = acc + b_ref[...]


def _modulations(cc, w_ada, b_ada):
    rows = cc.shape[0]
    tn = 1024
    return pl.pallas_call(
        _mod_kernel,
        out_shape=jax.ShapeDtypeStruct((DEPTH, rows, 3 * D_MODEL), F32),
        grid=(DEPTH, 3 * D_MODEL // tn),
        in_specs=[
            pl.BlockSpec((rows, D_MODEL), lambda l, j: (0, 0)),
            pl.BlockSpec((None, D_MODEL, tn), lambda l, j: (l, 0, j)),
            pl.BlockSpec((None, 1, tn), lambda l, j: (l, 0, j)),
        ],
        out_specs=pl.BlockSpec((None, rows, tn), lambda l, j: (l, 0, j)),
        compiler_params=_cparams(("arbitrary", "arbitrary")),
        name="adaln_mod",
    )(cc, w_ada, b_ada.reshape(DEPTH, 1, 3 * D_MODEL))


def _lb_kernel(x_ref, o_ref):
    xs = [x_ref[i] for i in range(DEPTH)]
    m = functools.reduce(jnp.maximum, xs)
    es = [jnp.exp(x - m) for x in xs]
    tot = functools.reduce(lambda a, b: a + b, es)
    cum = None
    first = None
    for i in range(DEPTH):
        p = es[i] / tot
        cum = p if cum is None else cum + p
        if first is None:
            first = cum
        lb = jnp.maximum(cum - first, 0.0)
        o_ref[0, i] = jnp.log(lb)
        o_ref[1, i] = jnp.log1p(-lb)
        o_ref[2, i] = 1.0 - lb


def _lower_bounds(lb_logits):
    x = jnp.transpose(lb_logits, (1, 0, 2))
    return pl.pallas_call(
        _lb_kernel,
        out_shape=jax.ShapeDtypeStruct((3, DEPTH, 2, R_WIDTH), F32),
        name="hgrn_lower_bounds",
    )(x)


def _inproj_kernel(x_ref, mod_ref, g_ref, w_ref, pa_ref, pr_ref, pf_ref):
    x = x_ref[...]
    ms = jnp.mean(x * x, axis=-1, keepdims=True)
    y = x * lax.rsqrt(ms + EPS) * g_ref[...]
    h = (y * (1.0 + mod_ref[1:2, :]) + mod_ref[0:1, :]).astype(BF16)
    pa_ref[...] = _dot(h, w_ref[:, 0:PA_COLS])
    pr_ref[...] = _dot(h, w_ref[:, PA_COLS:PA_COLS + PR_COLS])
    pf_ref[...] = _dot(h, w_ref[:, PA_COLS + PR_COLS:IN_COLS])


def _in_projection(x, mod, g_pre, w_in_bf16):
    b, n, _ = x.shape
    tm = min(n, 512)
    tiles = n // tm
    tok = lambda i: (i // tiles, i % tiles, 0)
    nb_mod = mod.shape[0]
    mod_idx = (lambda i: (i // tiles, 0, 0)) if nb_mod > 1 else (lambda i: (0, 0, 0))
    return pl.pallas_call(
        _inproj_kernel,
        out_shape=(
            jax.ShapeDtypeStruct((b, n, PA_COLS), F32),
            jax.ShapeDtypeStruct((b, n, PR_COLS), F32),
            jax.ShapeDtypeStruct((b, n, PF_COLS), F32),
        ),
        grid=(b * tiles,),
        in_specs=[
            pl.BlockSpec((None, tm, D_MODEL), tok),
            pl.BlockSpec((None, 3, D_MODEL), mod_idx),
            pl.BlockSpec((1, D_MODEL), lambda i: (0, 0)),
            pl.BlockSpec((D_MODEL, IN_COLS), lambda i: (0, 0)),
        ],
        out_specs=(
            pl.BlockSpec((None, tm, PA_COLS), tok),
            pl.BlockSpec((None, tm, PR_COLS), tok),
            pl.BlockSpec((None, tm, PF_COLS), tok),
        ),
        compiler_params=_cparams(("arbitrary",)),
        name="in_projection",
    )(x, mod, g_pre.reshape(1, D_MODEL), w_in_bf16)


def _lane_lo():
    return lax.broadcasted_iota(jnp.int32, (1, LANES), 1) < HEAD_DIM


def _ctx_attn_kernel(pa_ref, o_ref):
    lo = _lane_lo()
    for p in range(A_HEADS // 2):
        c = p * LANES
        q2 = pa_ref[:, c:c + LANES] * (HEAD_DIM ** -0.5)
        k2 = pa_ref[:, A_WIDTH + c:A_WIDTH + c + LANES].astype(BF16)
        v2 = pa_ref[:, 2 * A_WIDTH + c:2 * A_WIDTH + c + LANES].astype(BF16)
        g2 = pa_ref[:, 3 * A_WIDTH + c:3 * A_WIDTH + c + LANES]
        outs = []
        for hh in range(2):
            sel = lo if hh == 0 else jnp.logical_not(lo)
            qh = jnp.where(sel, q2, 0.0).astype(BF16)
            s = _dot_nt(qh, k2)
            m = jnp.max(s, axis=-1, keepdims=True)
            e = jnp.exp(s - m)
            l = jnp.sum(e, axis=-1, keepdims=True)
            outs.append(_dot(e.astype(BF16), v2) / l)
        o2 = jnp.where(lo, outs[0], outs[1])
        o_ref[:, c:c + LANES] = (o2 * _silu(g2)).astype(BF16)


def _context_attention(pa):
    b, n, _ = pa.shape
    return pl.pallas_call(
        _ctx_attn_kernel,
        out_shape=jax.ShapeDtypeStruct((b, n, A_WIDTH), BF16),
        grid=(b,),
        in_specs=[pl.BlockSpec((None, n, PA_COLS), lambda i: (i, 0, 0))],
        out_specs=pl.BlockSpec((None, n, A_WIDTH), lambda i: (i, 0, 0)),
        compiler_params=_cparams(("arbitrary",)),
        name="context_attention",
    )(pa)


QROWS = 4


def _nbr_blocks(rows):
    kh = min(WIN_H, rows)
    out = []
    for r_first in range(0, rows, QROWS):
        r0s = [min(max(r - kh // 2, 0), rows - kh) for r in range(r_first, r_first + QROWS)]
        lo, hi = min(r0s), max(r0s) + kh
        lo -= lo % 2
        span = hi - lo
        span += (-span) % 4
        if lo + span > rows:
            lo = rows - span
        assert lo >= 0 and lo % 2 == 0
        out.append((r_first, lo, span, r0s))
    return out, kh


def _nbr_attn_kernel(q_ref, k_ref, v_ref, g_ref, kc_ref, vc_ref, te_ref, to_ref, o_ref, s_scr, *, rows):
    lo = _lane_lo()
    hi = jnp.logical_not(lo)
    blocks, kh = _nbr_blocks(rows)
    kcx = kc_ref[...].astype(BF16)
    vcx = vc_ref[...].astype(BF16)
    for (r_first, u0, span, r0s) in blocks:
        qs = r_first * GRID_W
        nq = QROWS * GRID_W
        nk = span * GRID_W
        q2 = q_ref[qs:qs + nq, :] * (HEAD_DIM ** -0.5)
        ku = k_ref[u0 * GRID_W:u0 * GRID_W + nk, :].astype(BF16)
        vu = v_ref[u0 * GRID_W:u0 * GRID_W + nk, :].astype(BF16)
        outs = []
        for hh in range(2):
            sel = lo if hh == 0 else hi
            qh = jnp.where(sel, q2, 0.0).astype(BF16)
            s_scr[:, 0:nk] = _dot_nt(qh, ku)
            for rl in range(QROWS):
                r = r_first + rl
                r0 = r0s[rl]
                rs = slice(rl * GRID_W, (rl + 1) * GRID_W)
                for t in range(span // 2):
                    kr0 = u0 + 2 * t
                    cs = slice(t * LANES, (t + 1) * LANES)
                    ok0 = r0 <= kr0 < r0 + kh
                    ok1 = r0 <= kr0 + 1 < r0 + kh
                    if not (ok0 or ok1):
                        s_scr[rs, cs] = jnp.full((GRID_W, LANES), NEG_INF, F32)
                        continue
                    i0 = kr0 - r + WIN_H - 1
                    if i0 % 2 == 0:
                        assert 0 <= i0 and i0 + 2 <= 2 * WIN_H
                        bias = te_ref[hh, :, i0 * GRID_W:(i0 + 2) * GRID_W]
                    else:
                        assert 1 <= i0 and i0 + 1 <= 2 * WIN_H - 1
                        bias = to_ref[hh, :, (i0 - 1) * GRID_W:(i0 + 1) * GRID_W]
                    if not ok0:
                        bias = jnp.where(lo, NEG_INF, bias)
                    if not ok1:
                        bias = jnp.where(hi, NEG_INF, bias)
                    s_scr[rs, cs] = s_scr[rs, cs] + bias
            s_lat = s_scr[:, 0:nk]
            s_ctx = _dot_nt(qh, kcx)
            m = jnp.maximum(jnp.max(s_lat, axis=-1, keepdims=True), jnp.max(s_ctx, axis=-1, keepdims=True))
            e_lat = jnp.exp(s_lat - m)
            e_ctx = jnp.exp(s_ctx - m)
            l = jnp.sum(e_lat, axis=-1, keepdims=True) + jnp.sum(e_ctx, axis=-1, keepdims=True)
            o = _dot(e_lat.astype(BF16), vu) + _dot(e_ctx.astype(BF16), vcx)
            outs.append(o / l)
        o2 = jnp.where(lo, outs[0], outs[1])
        o_ref[qs:qs + nq, :] = (o2 * _silu(g_ref[qs:qs + nq, :])).astype(BF16)


def _bias_tables(rpb_l):
    cols = np.arange(GRID_W)
    c0 = np.clip(cols - WIN_W // 2, 0, GRID_W - WIN_W)
    col_in = (cols[None, :] >= c0[:, None]) & (cols[None, :] < c0[:, None] + WIN_W)
    col_idx = np.clip(cols[None, :] - cols[:, None] + WIN_W - 1, 0, 2 * WIN_W - 2)
    tt = rpb_l[:, :, col_idx]
    tt = jnp.where(col_in[None, None], tt, NEG_INF)
    tt = jnp.transpose(tt, (0, 2, 1, 3)).reshape(A_HEADS, GRID_W, (2 * WIN_H - 1) * GRID_W)
    pad = jnp.full((A_HEADS, GRID_W, GRID_W), NEG_INF, F32)
    te = jnp.concatenate([tt, pad], axis=-1)
    to = jnp.concatenate([tt[:, :, GRID_W:], pad, pad], axis=-1)
    return te, to


def _neighbourhood_attention(pa, cache_k, cache_v, layer, te, to):
    b, n, _ = pa.shape
    rows = n // GRID_W
    past = cache_k.shape[2]
    ck = cache_k.reshape(b, DEPTH, past, A_WIDTH)
    cv = cache_v.reshape(b, DEPTH, past, A_WIDTH)
    blocks, _ = _nbr_blocks(rows)
    max_nk = max(s for (_, _, s, _) in blocks) * GRID_W
    npair = A_HEADS // 2
    col = lambda off: (lambda i, p: (i, 0, off + p))
    tw = te.shape[-1]
    return pl.pallas_call(
        functools.partial(_nbr_attn_kernel, rows=rows),
        out_shape=jax.ShapeDtypeStruct((b, n, A_WIDTH), BF16),
        grid=(b, npair),
        in_specs=[
            pl.BlockSpec((None, n, LANES), col(0)),
            pl.BlockSpec((None, n, LANES), col(npair)),
            pl.BlockSpec((None, n, LANES), col(2 * npair)),
            pl.BlockSpec((None, n, LANES), col(3 * npair)),
            pl.BlockSpec((None, None, past, LANES), lambda i, p: (i, layer, 0, p)),
            pl.BlockSpec((None, None, past, LANES), lambda i, p: (i, layer, 0, p)),
            pl.BlockSpec((2, GRID_W, tw), lambda i, p: (p, 0, 0)),
            pl.BlockSpec((2, GRID_W, tw), lambda i, p: (p, 0, 0)),
        ],
        out_specs=pl.BlockSpec((None, n, LANES), col(0)),
        scratch_shapes=[pltpu.VMEM((QROWS * GRID_W, max_nk), F32)],
        compiler_params=_cparams(("arbitrary", "arbitrary")),
        name="neighbourhood_attention",
    )(pa, pa, pa, pa, ck, cv, te, to)


def _hgrn_constants():
    c = CHUNK
    idx = np.arange(c)
    mats = [np.tril(np.ones((c, c))), (idx[None, :] > idx[:, None]).astype(np.float64)]
    masks = []
    for h in LEVEL_HALVES:
        blk = idx // (2 * h)
        mid = blk * 2 * h + h - 1
        upper = idx > mid
        m = np.zeros((c, c))
        for i in range(c):
            if upper[i]:
                m[i, mid[i] + 1:i + 1] = 1.0
            else:
                m[i, i + 1:mid[i] + 1] = 1.0
        mats.append(m)
        same = blk[:, None] == blk[None, :]
        masks.append((same & upper[:, None] & (~upper)[None, :]).astype(np.float64))
    masks.append(np.eye(c))
    fwd = np.concatenate(mats, axis=0)
    bwd = np.concatenate([m[::-1, ::-1] for m in mats], axis=0)
    mk_f = np.stack([np.tile(m, (1, R_HEADS)) for m in masks])
    mk_b = np.stack([np.tile(m[::-1, ::-1], (1, R_HEADS)) for m in masks])
    hid = np.arange(R_WIDTH) // R_DIM
    bd = (hid[:, None] == hid[None, :]).astype(np.float64)
    return (jnp.asarray(np.stack([fwd, bwd]), BF16), jnp.asarray(np.stack([mk_f, mk_b]), F32),
            jnp.asarray(bd, BF16))


def _hgrn_kernel(*refs, nc, has_init):
    if has_init:
        pr_ref, lbp_ref, gh_ref, mstk_ref, lmask_ref, bd_ref, s0_ref = refs[:7]
        rest = refs[7:]
    else:
        pr_ref, lbp_ref, gh_ref, mstk_ref, lmask_ref, bd_ref = refs[:6]
        s0_ref = None
        rest = refs[6:]
    o_ref, sfin_ref, q_scr, k_scr, g_scr, o_scr, st_scr = rest
    w = R_WIDTH
    c = CHUNK
    bd = bd_ref[...]
    bd32 = bd.astype(F32)

    q_scr[...] = _silu(pr_ref[:, 0:w])
    for d in range(2):
        z = pr_ref[:, (1 + d) * w:(2 + d) * w]
        e = jnp.exp(-jnp.abs(z))
        log_sig = jnp.minimum(z, 0.0) - jnp.log1p(e)
        a = lbp_ref[0, d:d + 1, :]
        bb = lbp_ref[1, d:d + 1, :] + log_sig
        mx = jnp.maximum(a, bb)
        log_f = mx + jnp.log1p(jnp.exp(-jnp.abs(a - bb)))
        g_hi, g_lo = _split2(log_f)
        g_scr[d, :, 0:w] = g_hi
        g_scr[d, :, w:2 * w] = g_lo
        sig_neg = jnp.where(z >= 0.0, e, 1.0) / (1.0 + e)
        k_scr[d] = lbp_ref[2, d:d + 1, :] * sig_neg
        if has_init:
            st_scr[d] = s0_ref[d]
        else:
            st_scr[d] = jnp.zeros((w, w), F32)

    def tile_heads(x):
        return jnp.concatenate([x] * R_HEADS, axis=0) * bd

    def chunk_step(ci, carry):
        for d in range(2):
            cc = ci if d == 0 else nc - 1 - ci
            rs = pl.ds(pl.multiple_of(cc * c, c), c)
            dst = _dot(mstk_ref[d], g_scr[d, rs, :])
            dst = dst[:, 0:w] + dst[:, w:2 * w]
            gcum = dst[0:c]
            to_end = dst[c:2 * c]
            q = q_scr[rs, :]
            k = k_scr[d, rs, :]
            v = pr_ref[rs, 3 * w:4 * w].astype(BF16)
            a = _dot_nt(q.astype(BF16), tile_heads(k.astype(BF16))) * lmask_ref[d, len(LEVEL_HALVES)]
            for j in range(len(LEVEL_HALVES)):
                e = jnp.exp(dst[(2 + j) * c:(3 + j) * c])
                qt = (q * e).astype(BF16)
                kt = (k * e).astype(BF16)
                a = a + _dot_nt(qt, tile_heads(kt)) * lmask_ref[d, j]
            st = st_scr[d]
            o = _dot_nt((q * jnp.exp(gcum)).astype(BF16), st.astype(BF16)) + _dot(a.astype(BF16), tile_heads(v))
            o_scr[d, rs, :] = o
            kl = (k * jnp.exp(to_end)).astype(BF16)
            last = gcum[c - 1:c, :] if d == 0 else gcum[0:1, :]
            st_scr[d] = st * jnp.exp(last) + _dot_tn(v, kl) * bd32
        return carry

    lax.fori_loop(0, nc, chunk_step, 0)

    o = o_scr[0] + o_scr[1]
    x2_hi, x2_lo = _split2(o * o)
    ms = (_dot(x2_hi, bd) + _dot(x2_lo, bd)) * (1.0 / R_DIM)
    y = o * lax.rsqrt(ms + EPS) * gh_ref[...]
    o_ref[...] = (y * _silu(pr_ref[:, 4 * w:5 * w])).astype(BF16)
    sfin_ref[...] = st_scr[...]


def _hgrn(pr, lbp_l, g_hgrn_l, consts, s0=None):
    b, n, _ = pr.shape
    nc = n // CHUNK
    mstk, lmask, bd = consts
    w = R_WIDTH
    full = lambda *shape: pl.BlockSpec(shape, lambda i: (0,) * len(shape))
    in_specs = [
        pl.BlockSpec((None, n, PR_COLS), lambda i: (i, 0, 0)),
        full(3, 2, w),
        full(1, w),
        full(2, N_DROWS * CHUNK, CHUNK),
        full(2, len(LEVEL_HALVES) + 1, CHUNK, w),
        full(w, w),
    ]
    args = [pr, lbp_l, g_hgrn_l.reshape(1, w), mstk, lmask, bd]
    if s0 is not None:
        in_specs.append(pl.BlockSpec((None, 2, w, w), lambda i: (i, 0, 0, 0)))
        args.append(s0)
    return pl.pallas_call(
        functools.partial(_hgrn_kernel, nc=nc, has_init=s0 is not None),
        out_shape=(jax.ShapeDtypeStruct((b, n, w), BF16), jax.ShapeDtypeStruct((b, 2, w, w), F32)),
        grid=(b,),
        in_specs=in_specs,
        out_specs=(pl.BlockSpec((None, n, w), lambda i: (i, 0, 0)),
                   pl.BlockSpec((None, 2, w, w), lambda i: (i, 0, 0, 0))),
        scratch_shapes=[
            pltpu.VMEM((n, w), F32),
            pltpu.VMEM((2, n, w), F32),
            pltpu.VMEM((2, n, 2 * w), BF16),
            pltpu.VMEM((2, n, w), F32),
            pltpu.VMEM((2, w, w), F32),
        ],
        compiler_params=_cparams(("arbitrary",)),
        name="hgrn_scan",
    )(*args)


def _state_to_blockdiag(s):
    b = s.shape[0]
    st = jnp.swapaxes(s, -1, -2)
    eye = jnp.eye(R_HEADS, dtype=s.dtype)
    bdm = st[:, :, :, :, None, :] * eye[None, None, :, None, :, None]
    return bdm.reshape(b, 2, R_WIDTH, R_WIDTH)


def _blockdiag_to_state(st):
    b = st.shape[0]
    x = st.reshape(b, 2, R_HEADS, R_DIM, R_HEADS, R_DIM)
    x = jnp.stack([x[:, :, h, :, h, :] for h in range(R_HEADS)], axis=2)
    return jnp.swapaxes(x, -1, -2)


def _fnet_constants(n):
    j = np.arange(F_GROUP_DIM)
    ang = 2.0 * np.pi * ((j[:, None] * j[None, :]) % F_GROUP_DIM) / F_GROUP_DIM
    eye = np.eye(F_GROUPS)
    cs = np.concatenate([np.kron(eye, np.cos(ang)), np.kron(eye, np.sin(ang))], axis=1)
    t = np.arange(n)
    angn = 2.0 * np.pi * ((t[:, None] * t[None, :]) % n) / n
    return tuple(jnp.asarray(m, F32).astype(BF16) for m in (cs, np.cos(angn), np.sin(angn)))


def _fnet_kernel(pf_ref, cs_ref, cn_ref, sn_ref, wf_ref, o_ref, *, scale):
    w = F_WIDTH
    t = _dot(pf_ref[:, 0:w].astype(BF16), cs_ref[...])
    y = (_dot(cn_ref[...], t[:, 0:w].astype(BF16)) - _dot(sn_ref[...], t[:, w:2 * w].astype(BF16))) * scale
    of = _dot(y.astype(BF16), wf_ref[...])
    o_ref[...] = (of * _silu(pf_ref[:, w:2 * w])).astype(BF16)


def _fourier(pf, consts, w_fnet_bf16):
    b, n, _ = pf.shape
    cs, cn, sn = consts
    w = F_WIDTH
    full = lambda *shape: pl.BlockSpec(shape, lambda i: (0,) * len(shape))
    return pl.pallas_call(
        functools.partial(_fnet_kernel, scale=float((n * F_GROUP_DIM) ** -0.5)),
        out_shape=jax.ShapeDtypeStruct((b, n, w), BF16),
        grid=(b,),
        in_specs=[pl.BlockSpec((None, n, PF_COLS), lambda i: (i, 0, 0)),
                  full(w, 2 * w), full(n, n), full(n, n), full(w, w)],
        out_specs=pl.BlockSpec((None, n, w), lambda i: (i, 0, 0)),
        compiler_params=_cparams(("arbitrary",)),
        name="fourier_mixing",
    )(pf, cs, cn, sn, w_fnet_bf16)


def _outproj_kernel(ma_ref, mr_ref, mf_ref, w_ref, x_ref, mod_ref, g_ref, y_ref):
    out = (_dot(ma_ref[...], w_ref[0:A_WIDTH, :])
           + _dot(mr_ref[...], w_ref[A_WIDTH:A_WIDTH + R_WIDTH, :])
           + _dot(mf_ref[...], w_ref[A_WIDTH + R_WIDTH:D_MODEL, :]))
    ms = jnp.mean(out * out, axis=-1, keepdims=True)
    y_ref[...] = x_ref[...] + mod_ref[2:3, :] * (out * lax.rsqrt(ms + EPS) * g_ref[...])


def _out_projection(ma, mr, mf, w_out_bf16, x, mod, g_post):
    b, n, _ = x.shape
    tm = min(n, 512)
    tiles = n // tm
    tok = lambda i: (i // tiles, i % tiles, 0)
    nb_mod = mod.shape[0]
    mod_idx = (lambda i: (i // tiles, 0, 0)) if nb_mod > 1 else (lambda i: (0, 0, 0))
    return pl.pallas_call(
        _outproj_kernel,
        out_shape=jax.ShapeDtypeStruct((b, n, D_MODEL), F32),
        grid=(b * tiles,),
        in_specs=[
            pl.BlockSpec((None, tm, A_WIDTH), tok),
            pl.BlockSpec((None, tm, R_WIDTH), tok),
            pl.BlockSpec((None, tm, F_WIDTH), tok),
            pl.BlockSpec((D_MODEL, D_MODEL), lambda i: (0, 0)),
            pl.BlockSpec((None, tm, D_MODEL), tok),
            pl.BlockSpec((None, 3, D_MODEL), mod_idx),
            pl.BlockSpec((1, D_MODEL), lambda i: (0, 0)),
        ],
        out_specs=pl.BlockSpec((None, tm, D_MODEL), tok),
        compiler_params=_cparams(("arbitrary",)),
        name="out_projection",
    )(ma, mr, mf, w_out_bf16, x, mod, g_post.reshape(1, D_MODEL))


def kernel(x_prompt, x_sample, cache_attn_k, cache_attn_v, state_hgrn, c, c_ctx,
           w_ada, b_ada, g_pre, w_in, rpb, lb_logits, g_hgrn, w_fnet, w_out, g_post):
    nb_ctx, n_ctx, _ = x_prompt.shape
    nb_lat, n_lat, _ = x_sample.shape

    pad_rows = (-(1 + nb_lat)) % 8
    cc = jnp.concatenate([c_ctx[None, :], c, jnp.zeros((pad_rows, D_MODEL), F32)], axis=0)
    mods = _modulations(cc, w_ada, b_ada)
    lbp = _lower_bounds(lb_logits)

    w_in_b = w_in.astype(BF16)
    w_out_b = w_out.astype(BF16)
    w_fnet_b = w_fnet.astype(BF16)
    hconsts = _hgrn_constants()
    fconsts_ctx = _fnet_constants(n_ctx)
    fconsts_lat = _fnet_constants(n_lat)

    yp, ys = x_prompt, x_sample
    new_k, new_v, new_s = [], [], []
    for l in range(DEPTH):
        lbp_l = lbp[:, l]
        mod = mods[l, 0].reshape(1, 3, D_MODEL)
        pa, pr, pf = _in_projection(yp, mod, g_pre[l], w_in_b[l])
        new_k.append(pa[:, :, A_WIDTH:2 * A_WIDTH].reshape(nb_ctx, n_ctx, A_HEADS, HEAD_DIM))
        new_v.append(pa[:, :, 2 * A_WIDTH:3 * A_WIDTH].reshape(nb_ctx, n_ctx, A_HEADS, HEAD_DIM))
        ma = _context_attention(pa)
        mr, sfin = _hgrn(pr, lbp_l, g_hgrn[l], hconsts)
        new_s.append(_blockdiag_to_state(sfin))
        mf = _fourier(pf, fconsts_ctx, w_fnet_b[l])
        yp = _out_projection(ma, mr, mf, w_out_b[l], yp, mod, g_post[l])
        mod = mods[l, 1:1 + nb_lat].reshape(nb_lat, 3, D_MODEL)
        pa, pr, pf = _in_projection(ys, mod, g_pre[l], w_in_b[l])
        te, to = _bias_tables(rpb[l])
        ma = _neighbourhood_attention(pa, cache_attn_k, cache_attn_v, l, te, to)
        mr, _ = _hgrn(pr, lbp_l, g_hgrn[l], hconsts, s0=_state_to_blockdiag(state_hgrn[:, l]))
        mf = _fourier(pf, fconsts_lat, w_fnet_b[l])
        ys = _out_projection(ma, mr, mf, w_out_b[l], ys, mod, g_post[l])

    return (yp, ys, jnp.stack(new_k, axis=1), jnp.stack(new_v, axis=1), jnp.stack(new_s, axis=1))
```

```python
import functools

import numpy as np
import jax
import jax.numpy as jnp
from jax import lax
from jax.experimental import pallas as pl
from jax.experimental.pallas import tpu as pltpu

F32 = jnp.float32
BF16 = jnp.bfloat16

D_MODEL = 1024
DEPTH = 4
GRID_W = 64
WIN_H = 8
WIN_W = 16
HEAD_DIM = 64
A_HEADS = 8
A_WIDTH = A_HEADS * HEAD_DIM
R_HEADS = 4
R_DIM = 64
R_WIDTH = R_HEADS * R_DIM
F_GROUPS = 4
F_GROUP_DIM = 64
F_WIDTH = F_GROUPS * F_GROUP_DIM
PA_COLS = 4 * A_WIDTH
PR_COLS = 5 * R_WIDTH
PF_COLS = 2 * F_WIDTH
IN_COLS = PA_COLS + PR_COLS + PF_COLS
CHUNK = 64
EPS = 1e-6
LANES = 128
NEG_INF = float("-inf")
VMEM_LIMIT = 56 * 1024 * 1024


def _cparams(sem):
    return pltpu.CompilerParams(dimension_semantics=sem, vmem_limit_bytes=VMEM_LIMIT)


def _silu(x):
    return x / (1.0 + jnp.exp(-x))


def _dot(a, b):
    return jnp.dot(a, b, preferred_element_type=F32)


def _dot_nt(a, b):
    return lax.dot_general(a, b, (((1,), (1,)), ((), ())), preferred_element_type=F32)


def _dot_tn(a, b):
    return lax.dot_general(a, b, (((0,), (0,)), ((), ())), preferred_element_type=F32)


def _split2(x):
    hi = x.astype(BF16)
    lo = (x - hi.astype(F32)).astype(BF16)
    return hi, lo


def _mod_kernel(cc_ref, w_ref, b_ref, o_ref):
    a_hi, a_lo = _split2(_silu(cc_ref[...]))
    w_hi, w_lo = _split2(w_ref[...])
    acc = _dot(a_hi, w_hi) + _dot(a_hi, w_lo) + _dot(a_lo, w_hi)
    o_ref[...] = acc + b_ref[...]


def _modulations(cc, w_ada, b_ada):
    rows = cc.shape[0]
    tn = 1024
    return pl.pallas_call(
        _mod_kernel,
        out_shape=jax.ShapeDtypeStruct((DEPTH, rows, 3 * D_MODEL), F32),
        grid=(DEPTH, 3 * D_MODEL // tn),
        in_specs=[
            pl.BlockSpec((rows, D_MODEL), lambda l, j: (0, 0)),
            pl.BlockSpec((None, D_MODEL, tn), lambda l, j: (l, 0, j)),
            pl.BlockSpec((None, 1, tn), lambda l, j: (l, 0, j)),
        ],
        out_specs=pl.BlockSpec((None, rows, tn), lambda l, j: (l, 0, j)),
        compiler_params=_cparams(("arbitrary", "arbitrary")),
        name="adaln_mod",
    )(cc, w_ada, b_ada.reshape(DEPTH, 1, 3 * D_MODEL))


def _lb_kernel(x_ref, o_ref):
    xs = [x_ref[i] for i in range(DEPTH)]
    m = functools.reduce(jnp.maximum, xs)
    es = [jnp.exp(x - m) for x in xs]
    tot = functools.reduce(lambda a, b: a + b, es)
    cum = None
    first = None
    for i in range(DEPTH):
        p = es[i] / tot
        cum = p if cum is None else cum + p
        if first is None:
            first = cum
        lb = jnp.maximum(cum - first, 0.0)
        o_ref[0, i] = jnp.log(lb)
        o_ref[1, i] = jnp.log1p(-lb)
        o_ref[2, i] = 1.0 - lb


def _lower_bounds(lb_logits):
    x = jnp.transpose(lb_logits, (1, 0, 2))
    return pl.pallas_call(
        _lb_kernel,
        out_shape=jax.ShapeDtypeStruct((3, DEPTH, 2, R_WIDTH), F32),
        name="hgrn_lower_bounds",
    )(x)


def _inproj_kernel(*refs, n_prev, with_cache):
    x_ref, mod_ref, g_ref, w_ref = refs[:4]
    outs = refs[4 + n_prev:]
    pa_ref, pr_ref, pf_ref = outs[:3]
    x = x_ref[...]
    ms = jnp.mean(x * x, axis=-1, keepdims=True)
    y = x * lax.rsqrt(ms + EPS) * g_ref[...]
    h = (y * (1.0 + mod_ref[1:2, :]) + mod_ref[0:1, :]).astype(BF16)
    pa = _dot(h, w_ref[:, 0:PA_COLS])
    pa_ref[...] = pa
    pr_ref[...] = _dot(h, w_ref[:, PA_COLS:PA_COLS + PR_COLS])
    pf_ref[...] = _dot(h, w_ref[:, PA_COLS + PR_COLS:IN_COLS])
    if with_cache:
        ko_ref, vo_ref = outs[3:5]
        tm = pa.shape[0]
        ko_ref[...] = pa[:, A_WIDTH:2 * A_WIDTH].T.reshape(A_HEADS, HEAD_DIM, tm)
        vo_ref[...] = pa[:, 2 * A_WIDTH:3 * A_WIDTH].T.reshape(A_HEADS, HEAD_DIM, tm)


def _in_projection(x, mod, g_pre, w_in_bf16, cache=None):
    b, n, _ = x.shape
    tm = min(n, 512)
    tiles = n // tm
    tok = lambda i: (i // tiles, i % tiles, 0)
    nb_mod = mod.shape[0]
    mod_idx = (lambda i: (i // tiles, 0, 0)) if nb_mod > 1 else (lambda i: (0, 0, 0))
    in_specs = [
        pl.BlockSpec((None, tm, D_MODEL), tok),
        pl.BlockSpec((None, 3, D_MODEL), mod_idx),
        pl.BlockSpec((1, D_MODEL), lambda i: (0, 0)),
        pl.BlockSpec((D_MODEL, IN_COLS), lambda i: (0, 0)),
    ]
    args = [x, mod, g_pre.reshape(1, D_MODEL), w_in_bf16]
    out_shape = [
        jax.ShapeDtypeStruct((b, n, PA_COLS), F32),
        jax.ShapeDtypeStruct((b, n, PR_COLS), F32),
        jax.ShapeDtypeStruct((b, n, PF_COLS), F32),
    ]
    out_specs = [
        pl.BlockSpec((None, tm, PA_COLS), tok),
        pl.BlockSpec((None, tm, PR_COLS), tok),
        pl.BlockSpec((None, tm, PF_COLS), tok),
    ]
    aliases = {}
    n_prev = 0
    if cache is not None:
        layer, prev = cache
        cache_shape = jax.ShapeDtypeStruct((b, DEPTH, A_HEADS, HEAD_DIM, n), F32)
        cache_spec = pl.BlockSpec((None, None, A_HEADS, HEAD_DIM, tm),
                                  lambda i: (i // tiles, layer, 0, 0, i % tiles))
        out_shape += [cache_shape, cache_shape]
        out_specs += [cache_spec, cache_spec]
        if prev is not None:
            n_prev = 2
            in_specs += [pl.BlockSpec(memory_space=pl.ANY)] * 2
            args += list(prev)
            aliases = {4: 3, 5: 4}
    return pl.pallas_call(
        functools.partial(_inproj_kernel, n_prev=n_prev, with_cache=cache is not None),
        out_shape=tuple(out_shape),
        grid=(b * tiles,),
        in_specs=in_specs,
        out_specs=tuple(out_specs),
        input_output_aliases=aliases,
        compiler_params=_cparams(("arbitrary",)),
        name="in_projection",
    )(*args)


def _lane_lo():
    return lax.broadcasted_iota(jnp.int32, (1, LANES), 1) < HEAD_DIM


def _ctx_attn_kernel(pa_ref, o_ref):
    lo = _lane_lo()
    for p in range(A_HEADS // 2):
        c = p * LANES
        q2 = pa_ref[:, c:c + LANES] * (HEAD_DIM ** -0.5)
        k2 = pa_ref[:, A_WIDTH + c:A_WIDTH + c + LANES].astype(BF16)
        v2 = pa_ref[:, 2 * A_WIDTH + c:2 * A_WIDTH + c + LANES].astype(BF16)
        g2 = pa_ref[:, 3 * A_WIDTH + c:3 * A_WIDTH + c + LANES]
        outs = []
        for hh in range(2):
            sel = lo if hh == 0 else jnp.logical_not(lo)
            qh = jnp.where(sel, q2, 0.0).astype(BF16)
            s = _dot_nt(qh, k2)
            m = jnp.max(s, axis=-1, keepdims=True)
            e = jnp.exp(s - m)
            l = jnp.sum(e, axis=-1, keepdims=True)
            outs.append(_dot(e.astype(BF16), v2) / l)
        o2 = jnp.where(lo, outs[0], outs[1])
        o_ref[:, c:c + LANES] = (o2 * _silu(g2)).astype(BF16)


def _context_attention(pa):
    b, n, _ = pa.shape
    return pl.pallas_call(
        _ctx_attn_kernel,
        out_shape=jax.ShapeDtypeStruct((b, n, A_WIDTH), BF16),
        grid=(b,),
        in_specs=[pl.BlockSpec((None, n, PA_COLS), lambda i: (i, 0, 0))],
        out_specs=pl.BlockSpec((None, n, A_WIDTH), lambda i: (i, 0, 0)),
        compiler_params=_cparams(("arbitrary",)),
        name="context_attention",
    )(pa)


QROWS = 4


def _nbr_blocks(rows):
    kh = min(WIN_H, rows)
    out = []
    for r_first in range(0, rows, QROWS):
        r0s = [min(max(r - kh // 2, 0), rows - kh) for r in range(r_first, r_first + QROWS)]
        lo, hi = min(r0s), max(r0s) + kh
        lo -= lo % 2
        span = hi - lo
        span += (-span) % 4
        if lo + span > rows:
            lo = rows - span
        assert lo >= 0 and lo % 2 == 0
        out.append((r_first, lo, span, r0s))
    return out, kh


N_REL_ROWS = 2 * WIN_H - 1
TABLE_W = 2 * WIN_H * GRID_W


def _build_bias_tables(base_ref, te_scr, to_scr, heads):
    lane = lax.broadcasted_iota(jnp.int32, (GRID_W, LANES), 1)
    qc = lax.broadcasted_iota(jnp.int32, (GRID_W, LANES), 0)
    kc = lane % GRID_W
    lo = lane < GRID_W
    c0 = jnp.clip(qc - WIN_W // 2, 0, GRID_W - WIN_W)
    col_in = (kc >= c0) & (kc < c0 + WIN_W)
    neg = jnp.full((GRID_W, LANES), NEG_INF, F32)

    for h in heads:
        def tile(i, lane_off):
            row = jnp.broadcast_to(base_ref[h, i:i + 1, :], (GRID_W, LANES))
            return pltpu.roll(row, lane_off, 1, stride=1, stride_axis=0)

        for j in range(WIN_H):
            for scr, ia in ((te_scr, 2 * j), (to_scr, 2 * j + 1)):
                ib = ia + 1
                if ia >= N_REL_ROWS:
                    pair = neg
                elif ib >= N_REL_ROWS:
                    pair = jnp.where(col_in & lo, tile(ia, 0), NEG_INF)
                else:
                    pair = jnp.where(col_in, jnp.where(lo, tile(ia, 0), tile(ib, GRID_W)), NEG_INF)
                scr[h, :, j * LANES:(j + 1) * LANES] = pair


def _nbr_attn_kernel(q_ref, k_ref, v_ref, g_ref, kc_ref, vc_ref, base_ref, o_ref, s_scr, te_scr, to_scr, *, rows):
    p = pl.program_id(1)

    @pl.when(pl.program_id(0) == 0)
    def _():
        _build_bias_tables(base_ref, te_scr, to_scr, (2 * p, 2 * p + 1))

    lo = _lane_lo()
    hi = jnp.logical_not(lo)
    blocks, kh = _nbr_blocks(rows)
    nq = QROWS * GRID_W
    if True:
        past = kc_ref.shape[-1]
        kct = kc_ref[...].reshape(LANES, past).astype(BF16)
        vct = vc_ref[...].reshape(LANES, past).astype(BF16)
        for (r_first, u0, span, r0s) in blocks:
            qs = r_first * GRID_W
            nk = span * GRID_W
            ks = u0 * GRID_W
            q2 = q_ref[qs:qs + nq, :] * (HEAD_DIM ** -0.5)
            ku = k_ref[ks:ks + nk, :].astype(BF16)
            vu = v_ref[ks:ks + nk, :].astype(BF16)
            outs = []
            for hh in range(2):
                head = 2 * p + hh
                sel = lo if hh == 0 else hi
                qh = jnp.where(sel, q2, 0.0).astype(BF16)
                s_scr[:, 0:nk] = _dot_nt(qh, ku)
                for rl in range(QROWS):
                    r = r_first + rl
                    r0 = r0s[rl]
                    rs = slice(rl * GRID_W, (rl + 1) * GRID_W)
                    for t in range(span // 2):
                        kr0 = u0 + 2 * t
                        cs = slice(t * LANES, (t + 1) * LANES)
                        ok0 = r0 <= kr0 < r0 + kh
                        ok1 = r0 <= kr0 + 1 < r0 + kh
                        if not (ok0 or ok1):
                            s_scr[rs, cs] = jnp.full((GRID_W, LANES), NEG_INF, F32)
                            continue
                        i0 = kr0 - r + WIN_H - 1
                        if i0 % 2 == 0:
                            assert 0 <= i0 and i0 + 2 <= 2 * WIN_H
                            bias = te_scr[head, :, i0 * GRID_W:(i0 + 2) * GRID_W]
                        else:
                            assert 1 <= i0 and i0 + 1 <= 2 * WIN_H - 1
                            bias = to_scr[head, :, (i0 - 1) * GRID_W:(i0 + 1) * GRID_W]
                        if not ok0:
                            bias = jnp.where(lo, NEG_INF, bias)
                        if not ok1:
                            bias = jnp.where(hi, NEG_INF, bias)
                        s_scr[rs, cs] = s_scr[rs, cs] + bias
                s_lat = s_scr[:, 0:nk]
                s_ctx = _dot(qh, kct)
                m = jnp.maximum(jnp.max(s_lat, axis=-1, keepdims=True), jnp.max(s_ctx, axis=-1, keepdims=True))
                e_lat = jnp.exp(s_lat - m)
                e_ctx = jnp.exp(s_ctx - m)
                l = jnp.sum(e_lat, axis=-1, keepdims=True) + jnp.sum(e_ctx, axis=-1, keepdims=True)
                o = _dot(e_lat.astype(BF16), vu) + _dot_nt(e_ctx.astype(BF16), vct)
                outs.append(o / l)
            o2 = jnp.where(lo, outs[0], outs[1])
            o_ref[qs:qs + nq, :] = (o2 * _silu(g_ref[qs:qs + nq, :])).astype(BF16)


def _bias_base(rpb_l):
    mid = WIN_W - 1
    zeros = jnp.zeros(rpb_l.shape[:2] + (LANES - (2 * WIN_W - 1),), F32)
    base = jnp.concatenate([rpb_l[..., mid:], zeros, rpb_l[..., :mid]], axis=-1)
    return jnp.pad(base, ((0, 0), (0, 2 * WIN_H - N_REL_ROWS), (0, 0)))


def _neighbourhood_attention(pa, cache_kt, cache_vt, layer, rpb_l):
    b, n, _ = pa.shape
    rows = n // GRID_W
    past = cache_kt.shape[-1]
    blocks, _ = _nbr_blocks(rows)
    max_nk = max(s for (_, _, s, _) in blocks) * GRID_W
    npair = A_HEADS // 2
    col = lambda off: (lambda i, p: (i, 0, off + p))
    cache_spec = pl.BlockSpec((None, None, 2, HEAD_DIM, past), lambda i, p: (i, layer, p, 0, 0))
    return pl.pallas_call(
        functools.partial(_nbr_attn_kernel, rows=rows),
        out_shape=jax.ShapeDtypeStruct((b, n, A_WIDTH), BF16),
        grid=(b, npair),
        in_specs=[
            pl.BlockSpec((None, n, LANES), col(0)),
            pl.BlockSpec((None, n, LANES), col(npair)),
            pl.BlockSpec((None, n, LANES), col(2 * npair)),
            pl.BlockSpec((None, n, LANES), col(3 * npair)),
            cache_spec,
            cache_spec,
            pl.BlockSpec((A_HEADS, 2 * WIN_H, LANES), lambda i, p: (0, 0, 0)),
        ],
        out_specs=pl.BlockSpec((None, n, LANES), col(0)),
        scratch_shapes=[
            pltpu.VMEM((QROWS * GRID_W, max_nk), F32),
            pltpu.VMEM((A_HEADS, GRID_W, TABLE_W), F32),
            pltpu.VMEM((A_HEADS, GRID_W, TABLE_W), F32),
        ],
        compiler_params=_cparams(("arbitrary", "arbitrary")),
        name="neighbourhood_attention",
    )(pa, pa, pa, pa, cache_kt, cache_vt, _bias_base(rpb_l))


COARSE_HALVES = (32, 16, 8)
FINE_HALVES = (4, 2, 1)
N_LEVELS = len(COARSE_HALVES) + len(FINE_HALVES)


def _hgrn_constants():
    c = CHUNK
    idx = np.arange(c)
    mats = [np.tril(np.ones((c, c)))]
    masks = []
    for h in COARSE_HALVES + FINE_HALVES:
        blk = idx // (2 * h)
        mid = blk * 2 * h + h - 1
        upper = idx > mid
        if h in FINE_HALVES:
            m = np.zeros((c, c))
            for i in range(c):
                if upper[i]:
                    m[i, mid[i] + 1:i + 1] = 1.0
                else:
                    m[i, i + 1:mid[i] + 1] = 1.0
            mats.append(m)
        same = blk[:, None] == blk[None, :]
        masks.append((same & upper[:, None] & (~upper)[None, :]).astype(np.float64))
    masks.append(np.eye(c))
    fwd = np.concatenate(mats, axis=0)
    bwd = np.concatenate([m[::-1, ::-1] for m in mats], axis=0)
    mk_f = np.stack([np.tile(m, (1, R_HEADS)) for m in masks])
    mk_b = np.stack([np.tile(m[::-1, ::-1], (1, R_HEADS)) for m in masks])
    hid = np.arange(R_WIDTH) // R_DIM
    bd = (hid[:, None] == hid[None, :]).astype(np.float64)
    return (jnp.asarray(np.stack([fwd, bwd]), BF16), jnp.asarray(np.stack([mk_f, mk_b]), F32),
            jnp.asarray(bd, BF16))


def _hgrn_kernel(*refs, nc, has_init):
    pr_ref, lbp_ref, gh_ref, mstk_ref, lmask_ref, bd_ref = refs[:6]
    if has_init:
        s0_ref, o_ref = refs[6:8]
        sfin_ref = None
    else:
        o_ref, sfin_ref = refs[6:8]
        s0_ref = None
    q_scr, k_scr, g_scr, qg_scr, upd_scr, dec_scr, sbd_scr = refs[8:]
    w = R_WIDTH
    c = CHUNK
    bd = bd_ref[...]
    hid = lax.broadcasted_iota(jnp.int32, (1, w), 1) // R_DIM

    q_scr[...] = _silu(pr_ref[:, 0:w])
    for d in range(2):
        z = pr_ref[:, (1 + d) * w:(2 + d) * w]
        e = jnp.exp(-jnp.abs(z))
        log_sig = jnp.minimum(z, 0.0) - jnp.log1p(e)
        a = lbp_ref[0, d:d + 1, :]
        bb = lbp_ref[1, d:d + 1, :] + log_sig
        mx = jnp.maximum(a, bb)
        log_f = mx + jnp.log1p(jnp.exp(-jnp.abs(a - bb)))
        g_hi, g_lo = _split2(log_f)
        g_scr[d, :, 0:w] = g_hi
        g_scr[d, :, w:2 * w] = g_lo
        sig_neg = jnp.where(z >= 0.0, e, 1.0) / (1.0 + e)
        k_scr[d] = lbp_ref[2, d:d + 1, :] * sig_neg

    def tile4(x):
        return jnp.concatenate([x] * R_HEADS, axis=0)

    def head_diagonal(full):
        out = full[(R_HEADS - 1) * R_DIM:R_HEADS * R_DIM]
        for h in range(R_HEADS - 2, -1, -1):
            out = jnp.where(hid == h, full[h * R_DIM:(h + 1) * R_DIM], out)
        return out

    def chunk_rows(ci):
        return pl.ds(pl.multiple_of(ci * c, c), c)

    def cumulative(d, rs, n_mats):
        s = _dot(mstk_ref[d, 0:n_mats * c, :], g_scr[d, rs, :])
        return s[:, 0:w] + s[:, w:2 * w]

    def increment_step(ci, carry):
        rs = chunk_rows(ci)
        q = q_scr[rs, :]
        v = pr_ref[rs, 3 * w:4 * w].astype(BF16)
        for d in range(2):
            gcum = cumulative(d, rs, 1)
            last = gcum[c - 1:c, :] if d == 0 else gcum[0:1, :]
            kl = (k_scr[d, rs, :] * jnp.exp(last - gcum)).astype(BF16)
            upd_scr[d, ci] = head_diagonal(_dot_tn(v, kl))
            dec_scr[d, ci] = jnp.broadcast_to(jnp.exp(last), (8, w))
            qg_scr[d, rs, :] = (q * jnp.exp(gcum)).astype(BF16)
        return carry

    lax.fori_loop(0, nc, increment_step, 0)

    def scan_step(ci, carry):
        sf, sb = carry
        cb = nc - 1 - ci
        sbd_scr[0, ci] = tile4(sf.astype(BF16)) * bd
        sbd_scr[1, cb] = tile4(sb.astype(BF16)) * bd
        sf = sf * dec_scr[0, ci][0:1, :] + upd_scr[0, ci]
        sb = sb * dec_scr[1, cb][0:1, :] + upd_scr[1, cb]
        return sf, sb

    if has_init:
        init = tuple(jnp.concatenate([s0_ref[d], jnp.zeros((w, LANES - R_DIM), F32)], axis=1).T[0:R_DIM]
                     for d in range(2))
    else:
        init = (jnp.zeros((R_DIM, w), F32), jnp.zeros((R_DIM, w), F32))
    finals = lax.fori_loop(0, nc, scan_step, init)
    if sfin_ref is not None:
        for d in range(2):
            padded = jnp.concatenate([finals[d], jnp.zeros((LANES - R_DIM, w), F32)], axis=0)
            sfin_ref[d] = padded.T[:, 0:R_DIM]

    def readout_step(ci, carry):
        rs = chunk_rows(ci)
        q = q_scr[rs, :]
        qb = q.astype(BF16)
        vbd = tile4(pr_ref[rs, 3 * w:4 * w].astype(BF16)) * bd
        dsts = [cumulative(d, rs, 1 + len(FINE_HALVES)) for d in range(2)]
        exps = [[], []]
        for d in range(2):
            gcum = dsts[d][0:c]
            for h in COARSE_HALVES:
                parts = []
                for s0 in range(0, c, 2 * h):
                    anchor = s0 + h - 1 + d
                    parts.append(-jnp.abs(gcum[s0:s0 + 2 * h] - gcum[anchor:anchor + 1]))
                exps[d].append(parts[0] if len(parts) == 1 else jnp.concatenate(parts, axis=0))
            for j in range(len(FINE_HALVES)):
                exps[d].append(dsts[d][(1 + j) * c:(2 + j) * c])
        kbd = [tile4(k_scr[d, rs, :].astype(BF16)) * bd for d in range(2)]
        a = [_dot_nt(qb, kbd[d]) * lmask_ref[d, N_LEVELS] for d in range(2)]
        for j in range(N_LEVELS):
            for d in range(2):
                e = jnp.exp(exps[d][j])
                a[d] = a[d] + _dot_nt((q * e).astype(BF16), kbd[d] * tile4(e.astype(BF16))) * lmask_ref[d, j]
        o = None
        for d in range(2):
            od = _dot(a[d].astype(BF16), vbd) + _dot_nt(qg_scr[d, rs, :], sbd_scr[d, ci])
            o = od if o is None else o + od
        x2_hi, x2_lo = _split2(o * o)
        ms = (_dot(x2_hi, bd) + _dot(x2_lo, bd)) * (1.0 / R_DIM)
        y = o * lax.rsqrt(ms + EPS) * gh_ref[...]
        o_ref[rs, :] = (y * _silu(pr_ref[rs, 4 * w:5 * w])).astype(BF16)
        return carry

    lax.fori_loop(0, nc, readout_step, 0, unroll=2)


def _hgrn(pr, lbp_l, g_hgrn_l, consts, state=None, layer=None):
    b, n, _ = pr.shape
    nc = n // CHUNK
    mstk, lmask, bd = consts
    w = R_WIDTH
    has_init = state is not None
    full = lambda *shape: pl.BlockSpec(shape, lambda i: (0,) * len(shape))
    in_specs = [
        pl.BlockSpec((None, n, PR_COLS), lambda i: (i, 0, 0)),
        full(3, 2, w),
        full(1, w),
        full(*mstk.shape),
        full(*lmask.shape),
        full(w, w),
    ]
    args = [pr, lbp_l, g_hgrn_l.reshape(1, w), mstk, lmask, bd]
    out_shape = [jax.ShapeDtypeStruct((b, n, w), BF16)]
    out_specs = [pl.BlockSpec((None, n, w), lambda i: (i, 0, 0))]
    if has_init:
        in_specs.append(pl.BlockSpec((None, None, 2, w, R_DIM), lambda i: (i, layer, 0, 0, 0)))
        args.append(state)
    else:
        out_shape.append(jax.ShapeDtypeStruct((b, 2, w, R_DIM), F32))
        out_specs.append(pl.BlockSpec((None, 2, w, R_DIM), lambda i: (i, 0, 0, 0)))
    return pl.pallas_call(
        functools.partial(_hgrn_kernel, nc=nc, has_init=has_init),
        out_shape=tuple(out_shape),
        grid=(b,),
        in_specs=in_specs,
        out_specs=tuple(out_specs),
        scratch_shapes=[
            pltpu.VMEM((n, w), F32),
            pltpu.VMEM((2, n, w), F32),
            pltpu.VMEM((2, n, 2 * w), BF16),
            pltpu.VMEM((2, n, w), BF16),
            pltpu.VMEM((2, nc, R_DIM, w), F32),
            pltpu.VMEM((2, nc, 8, w), F32),
            pltpu.VMEM((2, nc, w, w), BF16),
        ],
        compiler_params=_cparams(("arbitrary",)),
        name="hgrn_scan",
    )(*args)


def _fnet_constants(n):
    j = np.arange(F_GROUP_DIM)
    ang = 2.0 * np.pi * ((j[:, None] * j[None, :]) % F_GROUP_DIM) / F_GROUP_DIM
    eye = np.eye(F_GROUPS)
    cs = np.concatenate([np.kron(eye, np.cos(ang)), np.kron(eye, np.sin(ang))], axis=1)
    t = np.arange(n)
    angn = 2.0 * np.pi * ((t[:, None] * t[None, :]) % n) / n
    return tuple(jnp.asarray(m, F32).astype(BF16) for m in (cs, np.cos(angn), np.sin(angn)))


def _fnet_kernel(pf_ref, cs_ref, cn_ref, sn_ref, wf_ref, o_ref, *, scale):
    w = F_WIDTH
    t = _dot(pf_ref[:, 0:w].astype(BF16), cs_ref[...])
    y = (_dot(cn_ref[...], t[:, 0:w].astype(BF16)) - _dot(sn_ref[...], t[:, w:2 * w].astype(BF16))) * scale
    of = _dot(y.astype(BF16), wf_ref[...])
    o_ref[...] = (of * _silu(pf_ref[:, w:2 * w])).astype(BF16)


def _fourier(pf, consts, w_fnet_bf16):
    b, n, _ = pf.shape
    cs, cn, sn = consts
    w = F_WIDTH
    full = lambda *shape: pl.BlockSpec(shape, lambda i: (0,) * len(shape))
    return pl.pallas_call(
        functools.partial(_fnet_kernel, scale=float((n * F_GROUP_DIM) ** -0.5)),
        out_shape=jax.ShapeDtypeStruct((b, n, w), BF16),
        grid=(b,),
        in_specs=[pl.BlockSpec((None, n, PF_COLS), lambda i: (i, 0, 0)),
                  full(w, 2 * w), full(n, n), full(n, n), full(w, w)],
        out_specs=pl.BlockSpec((None, n, w), lambda i: (i, 0, 0)),
        compiler_params=_cparams(("arbitrary",)),
        name="fourier_mixing",
    )(pf, cs, cn, sn, w_fnet_bf16)


def _outproj_kernel(ma_ref, mr_ref, mf_ref, w_ref, x_ref, mod_ref, g_ref, y_ref):
    out = (_dot(ma_ref[...], w_ref[0:A_WIDTH, :])
           + _dot(mr_ref[...], w_ref[A_WIDTH:A_WIDTH + R_WIDTH, :])
           + _dot(mf_ref[...], w_ref[A_WIDTH + R_WIDTH:D_MODEL, :]))
    ms = jnp.mean(out * out, axis=-1, keepdims=True)
    y_ref[...] = x_ref[...] + mod_ref[2:3, :] * (out * lax.rsqrt(ms + EPS) * g_ref[...])


def _out_projection(ma, mr, mf, w_out_bf16, x, mod, g_post):
    b, n, _ = x.shape
    tm = min(n, 512)
    tiles = n // tm
    tok = lambda i: (i // tiles, i % tiles, 0)
    nb_mod = mod.shape[0]
    mod_idx = (lambda i: (i // tiles, 0, 0)) if nb_mod > 1 else (lambda i: (0, 0, 0))
    return pl.pallas_call(
        _outproj_kernel,
        out_shape=jax.ShapeDtypeStruct((b, n, D_MODEL), F32),
        grid=(b * tiles,),
        in_specs=[
            pl.BlockSpec((None, tm, A_WIDTH), tok),
            pl.BlockSpec((None, tm, R_WIDTH), tok),
            pl.BlockSpec((None, tm, F_WIDTH), tok),
            pl.BlockSpec((D_MODEL, D_MODEL), lambda i: (0, 0)),
            pl.BlockSpec((None, tm, D_MODEL), tok),
            pl.BlockSpec((None, 3, D_MODEL), mod_idx),
            pl.BlockSpec((1, D_MODEL), lambda i: (0, 0)),
        ],
        out_specs=pl.BlockSpec((None, tm, D_MODEL), tok),
        compiler_params=_cparams(("arbitrary",)),
        name="out_projection",
    )(ma, mr, mf, w_out_bf16, x, mod, g_post.reshape(1, D_MODEL))


def kernel(x_prompt, x_sample, cache_attn_k, cache_attn_v, state_hgrn, c, c_ctx,
           w_ada, b_ada, g_pre, w_in, rpb, lb_logits, g_hgrn, w_fnet, w_out, g_post):
    nb_ctx, n_ctx, _ = x_prompt.shape
    nb_lat, n_lat, _ = x_sample.shape

    pad_rows = (-(1 + nb_lat)) % 8
    cc = jnp.concatenate([c_ctx[None, :], c, jnp.zeros((pad_rows, D_MODEL), F32)], axis=0)
    mods = _modulations(cc, w_ada, b_ada)
    lbp = _lower_bounds(lb_logits)

    w_in_b = w_in.astype(BF16)
    w_out_b = w_out.astype(BF16)
    w_fnet_b = w_fnet.astype(BF16)
    hconsts = _hgrn_constants()
    fconsts_ctx = _fnet_constants(n_ctx)
    fconsts_lat = _fnet_constants(n_lat)

    state_rows = state_hgrn.reshape(nb_lat, DEPTH, 2, R_WIDTH, R_DIM)
    cache_kt = jnp.transpose(cache_attn_k, (0, 1, 3, 4, 2))
    cache_vt = jnp.transpose(cache_attn_v, (0, 1, 3, 4, 2))

    yp, ys = x_prompt, x_sample
    new_kv, new_s = None, []
    for l in range(DEPTH):
        lbp_l = lbp[:, l]
        mod = mods[l, 0].reshape(1, 3, D_MODEL)
        pa, pr, pf, *new_kv = _in_projection(yp, mod, g_pre[l], w_in_b[l], cache=(l, new_kv))
        ma = _context_attention(pa)
        mr, sfin = _hgrn(pr, lbp_l, g_hgrn[l], hconsts)
        new_s.append(sfin)
        mf = _fourier(pf, fconsts_ctx, w_fnet_b[l])
        yp = _out_projection(ma, mr, mf, w_out_b[l], yp, mod, g_post[l])
        mod = mods[l, 1:1 + nb_lat].reshape(nb_lat, 3, D_MODEL)
        pa, pr, pf = _in_projection(ys, mod, g_pre[l], w_in_b[l])
        ma = _neighbourhood_attention(pa, cache_kt, cache_vt, l, rpb[l])
        (mr,) = _hgrn(pr, lbp_l, g_hgrn[l], hconsts, state=state_rows, layer=l)
        mf = _fourier(pf, fconsts_lat, w_fnet_b[l])
        ys = _out_projection(ma, mr, mf, w_out_b[l], ys, mod, g_post[l])

    new_state = jnp.stack(new_s, axis=1).reshape(nb_ctx, DEPTH, 2, R_HEADS, R_DIM, R_DIM)
    new_k, new_v = (jnp.transpose(t, (0, 1, 4, 2, 3)) for t in new_kv)
    return (yp, ys, new_k, new_v, new_state)
```

```python
import functools

import numpy as np
import jax
import jax.numpy as jnp
from jax import lax
from jax.experimental import pallas as pl
from jax.experimental.pallas import tpu as pltpu

F32 = jnp.float32
BF16 = jnp.bfloat16

D_MODEL = 1024
DEPTH = 4
GRID_W = 64
WIN_H = 8
WIN_W = 16
HEAD_DIM = 64
A_HEADS = 8
A_WIDTH = A_HEADS * HEAD_DIM
R_HEADS = 4
R_DIM = 64
R_WIDTH = R_HEADS * R_DIM
F_GROUPS = 4
F_GROUP_DIM = 64
F_WIDTH = F_GROUPS * F_GROUP_DIM
PA_COLS = 4 * A_WIDTH
PR_COLS = 5 * R_WIDTH
PF_COLS = 2 * F_WIDTH
IN_COLS = PA_COLS + PR_COLS + PF_COLS
CHUNK = 64
EPS = 1e-6
LANES = 128
NEG_INF = float("-inf")
VMEM_LIMIT = 56 * 1024 * 1024


def _cparams(sem):
    return pltpu.CompilerParams(dimension_semantics=sem, vmem_limit_bytes=VMEM_LIMIT)


def _silu(x):
    return x / (1.0 + jnp.exp(-x))


def _dot(a, b):
    return jnp.dot(a, b, preferred_element_type=F32)


def _dot_nt(a, b):
    return lax.dot_general(a, b, (((1,), (1,)), ((), ())), preferred_element_type=F32)


def _dot_tn(a, b):
    return lax.dot_general(a, b, (((0,), (0,)), ((), ())), preferred_element_type=F32)


def _split2(x):
    hi = x.astype(BF16)
    lo = (x - hi.astype(F32)).astype(BF16)
    return hi, lo


def _mod_kernel(cc_ref, w_ref, b_ref, o_ref):
    a_hi, a_lo = _split2(_silu(cc_ref[...]))
    w_hi, w_lo = _split2(w_ref[...])
    acc = _dot(a_hi, w_hi) + _dot(a_hi, w_lo) + _dot(a_lo, w_hi)
    o_ref[...] = acc + b_ref[...]


def _modulations(cc, w_ada, b_ada):
    rows = cc.shape[0]
    tn = 1024
    return pl.pallas_call(
        _mod_kernel,
        out_shape=jax.ShapeDtypeStruct((DEPTH, rows, 3 * D_MODEL), F32),
        grid=(DEPTH, 3 * D_MODEL // tn),
        in_specs=[
            pl.BlockSpec((rows, D_MODEL), lambda l, j: (0, 0)),
            pl.BlockSpec((None, D_MODEL, tn), lambda l, j: (l, 0, j)),
            pl.BlockSpec((None, 1, tn), lambda l, j: (l, 0, j)),
        ],
        out_specs=pl.BlockSpec((None, rows, tn), lambda l, j: (l, 0, j)),
        compiler_params=_cparams(("arbitrary", "arbitrary")),
        name="adaln_mod",
    )(cc, w_ada, b_ada.reshape(DEPTH, 1, 3 * D_MODEL))


def _lb_kernel(x_ref, o_ref):
    xs = [x_ref[i] for i in range(DEPTH)]
    m = functools.reduce(jnp.maximum, xs)
    es = [jnp.exp(x - m) for x in xs]
    tot = functools.reduce(lambda a, b: a + b, es)
    cum = None
    first = None
    for i in range(DEPTH):
        p = es[i] / tot
        cum = p if cum is None else cum + p
        if first is None:
            first = cum
        lb = jnp.maximum(cum - first, 0.0)
        o_ref[0, i] = jnp.log(lb)
        o_ref[1, i] = jnp.log1p(-lb)
        o_ref[2, i] = 1.0 - lb


def _lower_bounds(lb_logits):
    x = jnp.transpose(lb_logits, (1, 0, 2))
    return pl.pallas_call(
        _lb_kernel,
        out_shape=jax.ShapeDtypeStruct((3, DEPTH, 2, R_WIDTH), F32),
        name="hgrn_lower_bounds",
    )(x)


def _inproj_kernel(*refs, n_prev, with_cache):
    x_ref, mod_ref, g_ref, w_ref = refs[:4]
    outs = refs[4 + n_prev:]
    pa_ref, pr_ref, pf_ref = outs[:3]
    x = x_ref[...]
    ms = jnp.mean(x * x, axis=-1, keepdims=True)
    y = x * lax.rsqrt(ms + EPS) * g_ref[...]
    h = (y * (1.0 + mod_ref[1:2, :]) + mod_ref[0:1, :]).astype(BF16)
    pa = _dot(h, w_ref[:, 0:PA_COLS])
    pa_ref[...] = pa
    pr_ref[...] = _dot(h, w_ref[:, PA_COLS:PA_COLS + PR_COLS])
    pf_ref[...] = _dot(h, w_ref[:, PA_COLS + PR_COLS:IN_COLS])
    if with_cache:
        ko_ref, vo_ref = outs[3:5]
        tm = pa.shape[0]
        ko_ref[...] = pa[:, A_WIDTH:2 * A_WIDTH].T.reshape(A_HEADS, HEAD_DIM, tm)
        vo_ref[...] = pa[:, 2 * A_WIDTH:3 * A_WIDTH].T.reshape(A_HEADS, HEAD_DIM, tm)


def _in_projection(x, mod, g_pre, w_in_bf16, cache=None):
    b, n, _ = x.shape
    tm = min(n, 512)
    tiles = n // tm
    tok = lambda i: (i // tiles, i % tiles, 0)
    nb_mod = mod.shape[0]
    mod_idx = (lambda i: (i // tiles, 0, 0)) if nb_mod > 1 else (lambda i: (0, 0, 0))
    in_specs = [
        pl.BlockSpec((None, tm, D_MODEL), tok),
        pl.BlockSpec((None, 3, D_MODEL), mod_idx),
        pl.BlockSpec((1, D_MODEL), lambda i: (0, 0)),
        pl.BlockSpec((D_MODEL, IN_COLS), lambda i: (0, 0)),
    ]
    args = [x, mod, g_pre.reshape(1, D_MODEL), w_in_bf16]
    out_shape = [
        jax.ShapeDtypeStruct((b, n, PA_COLS), F32),
        jax.ShapeDtypeStruct((b, n, PR_COLS), F32),
        jax.ShapeDtypeStruct((b, n, PF_COLS), F32),
    ]
    out_specs = [
        pl.BlockSpec((None, tm, PA_COLS), tok),
        pl.BlockSpec((None, tm, PR_COLS), tok),
        pl.BlockSpec((None, tm, PF_COLS), tok),
    ]
    aliases = {}
    n_prev = 0
    if cache is not None:
        layer, prev = cache
        cache_shape = jax.ShapeDtypeStruct((b, DEPTH, A_HEADS, HEAD_DIM, n), F32)
        cache_spec = pl.BlockSpec((None, None, A_HEADS, HEAD_DIM, tm),
                                  lambda i: (i // tiles, layer, 0, 0, i % tiles))
        out_shape += [cache_shape, cache_shape]
        out_specs += [cache_spec, cache_spec]
        if prev is not None:
            n_prev = 2
            in_specs += [pl.BlockSpec(memory_space=pl.ANY)] * 2
            args += list(prev)
            aliases = {4: 3, 5: 4}
    return pl.pallas_call(
        functools.partial(_inproj_kernel, n_prev=n_prev, with_cache=cache is not None),
        out_shape=tuple(out_shape),
        grid=(b * tiles,),
        in_specs=in_specs,
        out_specs=tuple(out_specs),
        input_output_aliases=aliases,
        compiler_params=_cparams(("arbitrary",)),
        name="in_projection",
    )(*args)


def _lane_lo():
    return lax.broadcasted_iota(jnp.int32, (1, LANES), 1) < HEAD_DIM


def _ctx_attn_kernel(pa_ref, o_ref):
    lo = _lane_lo()
    for p in range(A_HEADS // 2):
        c = p * LANES
        q2 = pa_ref[:, c:c + LANES] * (HEAD_DIM ** -0.5)
        k2 = pa_ref[:, A_WIDTH + c:A_WIDTH + c + LANES].astype(BF16)
        v2 = pa_ref[:, 2 * A_WIDTH + c:2 * A_WIDTH + c + LANES].astype(BF16)
        g2 = pa_ref[:, 3 * A_WIDTH + c:3 * A_WIDTH + c + LANES]
        outs = []
        for hh in range(2):
            sel = lo if hh == 0 else jnp.logical_not(lo)
            qh = jnp.where(sel, q2, 0.0).astype(BF16)
            s = _dot_nt(qh, k2)
            m = jnp.max(s, axis=-1, keepdims=True)
            e = jnp.exp(s - m)
            l = jnp.sum(e, axis=-1, keepdims=True)
            outs.append(_dot(e.astype(BF16), v2) / l)
        o2 = jnp.where(lo, outs[0], outs[1])
        o_ref[:, c:c + LANES] = (o2 * _silu(g2)).astype(BF16)


def _context_attention(pa):
    b, n, _ = pa.shape
    return pl.pallas_call(
        _ctx_attn_kernel,
        out_shape=jax.ShapeDtypeStruct((b, n, A_WIDTH), BF16),
        grid=(b,),
        in_specs=[pl.BlockSpec((None, n, PA_COLS), lambda i: (i, 0, 0))],
        out_specs=pl.BlockSpec((None, n, A_WIDTH), lambda i: (i, 0, 0)),
        compiler_params=_cparams(("arbitrary",)),
        name="context_attention",
    )(pa)


QROWS = 4


def _nbr_blocks(rows):
    kh = min(WIN_H, rows)
    out = []
    for r_first in range(0, rows, QROWS):
        r0s = [min(max(r - kh // 2, 0), rows - kh) for r in range(r_first, r_first + QROWS)]
        lo, hi = min(r0s), max(r0s) + kh
        lo -= lo % 2
        span = hi - lo
        span += (-span) % 4
        if lo + span > rows:
            lo = rows - span
        assert lo >= 0 and lo % 2 == 0
        out.append((r_first, lo, span, r0s))
    return out, kh


N_REL_ROWS = 2 * WIN_H - 1
TABLE_W = 2 * WIN_H * GRID_W


def _build_bias_tables(base_ref, te_scr, to_scr, heads):
    lane = lax.broadcasted_iota(jnp.int32, (GRID_W, LANES), 1)
    qc = lax.broadcasted_iota(jnp.int32, (GRID_W, LANES), 0)
    kc = lane % GRID_W
    lo = lane < GRID_W
    c0 = jnp.clip(qc - WIN_W // 2, 0, GRID_W - WIN_W)
    col_in = (kc >= c0) & (kc < c0 + WIN_W)
    neg = jnp.full((GRID_W, LANES), NEG_INF, F32)

    for h in heads:
        def tile(i, lane_off):
            row = jnp.broadcast_to(base_ref[h, i:i + 1, :], (GRID_W, LANES))
            return pltpu.roll(row, lane_off, 1, stride=1, stride_axis=0)

        for j in range(WIN_H):
            for scr, ia in ((te_scr, 2 * j), (to_scr, 2 * j + 1)):
                ib = ia + 1
                if ia >= N_REL_ROWS:
                    pair = neg
                elif ib >= N_REL_ROWS:
                    pair = jnp.where(col_in & lo, tile(ia, 0), NEG_INF)
                else:
                    pair = jnp.where(col_in, jnp.where(lo, tile(ia, 0), tile(ib, GRID_W)), NEG_INF)
                scr[h, :, j * LANES:(j + 1) * LANES] = pair


def _nbr_attn_kernel(q_ref, k_ref, v_ref, g_ref, kc_ref, vc_ref, base_ref, o_ref, s_scr, te_scr, to_scr, *, rows):
    p = pl.program_id(1)

    @pl.when(pl.program_id(0) == 0)
    def _():
        _build_bias_tables(base_ref, te_scr, to_scr, (2 * p, 2 * p + 1))

    lo = _lane_lo()
    hi = jnp.logical_not(lo)
    blocks, kh = _nbr_blocks(rows)
    nq = QROWS * GRID_W
    if True:
        past = kc_ref.shape[-1]
        kct = kc_ref[...].reshape(LANES, past).astype(BF16)
        vct = vc_ref[...].reshape(LANES, past).astype(BF16)
        for (r_first, u0, span, r0s) in blocks:
            qs = r_first * GRID_W
            nk = span * GRID_W
            ks = u0 * GRID_W
            q2 = q_ref[qs:qs + nq, :] * (HEAD_DIM ** -0.5)
            ku = k_ref[ks:ks + nk, :].astype(BF16)
            vu = v_ref[ks:ks + nk, :].astype(BF16)
            outs = []
            for hh in range(2):
                head = 2 * p + hh
                sel = lo if hh == 0 else hi
                qh = jnp.where(sel, q2, 0.0).astype(BF16)
                s_scr[:, 0:nk] = _dot_nt(qh, ku)
                for rl in range(QROWS):
                    r = r_first + rl
                    r0 = r0s[rl]
                    rs = slice(rl * GRID_W, (rl + 1) * GRID_W)
                    for t in range(span // 2):
                        kr0 = u0 + 2 * t
                        cs = slice(t * LANES, (t + 1) * LANES)
                        ok0 = r0 <= kr0 < r0 + kh
                        ok1 = r0 <= kr0 + 1 < r0 + kh
                        if not (ok0 or ok1):
                            s_scr[rs, cs] = jnp.full((GRID_W, LANES), NEG_INF, F32)
                            continue
                        i0 = kr0 - r + WIN_H - 1
                        if i0 % 2 == 0:
                            assert 0 <= i0 and i0 + 2 <= 2 * WIN_H
                            bias = te_scr[head, :, i0 * GRID_W:(i0 + 2) * GRID_W]
                        else:
                            assert 1 <= i0 and i0 + 1 <= 2 * WIN_H - 1
                            bias = to_scr[head, :, (i0 - 1) * GRID_W:(i0 + 1) * GRID_W]
                        if not ok0:
                            bias = jnp.where(lo, NEG_INF, bias)
                        if not ok1:
                            bias = jnp.where(hi, NEG_INF, bias)
                        s_scr[rs, cs] = s_scr[rs, cs] + bias
                s_lat = s_scr[:, 0:nk]
                s_ctx = _dot(qh, kct)
                m = jnp.maximum(jnp.max(s_lat, axis=-1, keepdims=True), jnp.max(s_ctx, axis=-1, keepdims=True))
                e_lat = jnp.exp(s_lat - m)
                e_ctx = jnp.exp(s_ctx - m)
                l = jnp.sum(e_lat, axis=-1, keepdims=True) + jnp.sum(e_ctx, axis=-1, keepdims=True)
                o = _dot(e_lat.astype(BF16), vu) + _dot_nt(e_ctx.astype(BF16), vct)
                outs.append(o / l)
            o2 = jnp.where(lo, outs[0], outs[1])
            o_ref[qs:qs + nq, :] = (o2 * _silu(g_ref[qs:qs + nq, :])).astype(BF16)


def _bias_base(rpb_l):
    mid = WIN_W - 1
    zeros = jnp.zeros(rpb_l.shape[:2] + (LANES - (2 * WIN_W - 1),), F32)
    base = jnp.concatenate([rpb_l[..., mid:], zeros, rpb_l[..., :mid]], axis=-1)
    return jnp.pad(base, ((0, 0), (0, 2 * WIN_H - N_REL_ROWS), (0, 0)))


def _neighbourhood_attention(pa, cache_kt, cache_vt, layer, rpb_l):
    b, n, _ = pa.shape
    rows = n // GRID_W
    past = cache_kt.shape[-1]
    blocks, _ = _nbr_blocks(rows)
    max_nk = max(s for (_, _, s, _) in blocks) * GRID_W
    npair = A_HEADS // 2
    col = lambda off: (lambda i, p: (i, 0, off + p))
    cache_spec = pl.BlockSpec((None, None, 2, HEAD_DIM, past), lambda i, p: (i, layer, p, 0, 0))
    return pl.pallas_call(
        functools.partial(_nbr_attn_kernel, rows=rows),
        out_shape=jax.ShapeDtypeStruct((b, n, A_WIDTH), BF16),
        grid=(b, npair),
        in_specs=[
            pl.BlockSpec((None, n, LANES), col(0)),
            pl.BlockSpec((None, n, LANES), col(npair)),
            pl.BlockSpec((None, n, LANES), col(2 * npair)),
            pl.BlockSpec((None, n, LANES), col(3 * npair)),
            cache_spec,
            cache_spec,
            pl.BlockSpec((A_HEADS, 2 * WIN_H, LANES), lambda i, p: (0, 0, 0)),
        ],
        out_specs=pl.BlockSpec((None, n, LANES), col(0)),
        scratch_shapes=[
            pltpu.VMEM((QROWS * GRID_W, max_nk), F32),
            pltpu.VMEM((A_HEADS, GRID_W, TABLE_W), F32),
            pltpu.VMEM((A_HEADS, GRID_W, TABLE_W), F32),
        ],
        compiler_params=_cparams(("arbitrary", "arbitrary")),
        name="neighbourhood_attention",
    )(pa, pa, pa, pa, cache_kt, cache_vt, _bias_base(rpb_l))


COARSE_HALVES = (32, 16, 8)
FINE_HALVES = (4, 2, 1)
N_LEVELS = len(COARSE_HALVES) + len(FINE_HALVES)
ANCHOR_BLOCK = 16
ANCHOR_LEVELS = 2
ANCHOR_MAX_EXPONENT = 60.0
MASK_DIAG = N_LEVELS
MASK_ANCHOR = N_LEVELS + 1
GROUP = 4


def _hgrn_constants():
    c = CHUNK
    idx = np.arange(c)
    mats = [np.tril(np.ones((c, c)))]
    masks = []
    for h in COARSE_HALVES + FINE_HALVES:
        blk = idx // (2 * h)
        mid = blk * 2 * h + h - 1
        upper = idx > mid
        if h in FINE_HALVES:
            m = np.zeros((c, c))
            for i in range(c):
                if upper[i]:
                    m[i, mid[i] + 1:i + 1] = 1.0
                else:
                    m[i, i + 1:mid[i] + 1] = 1.0
            mats.append(m)
        same = blk[:, None] == blk[None, :]
        masks.append((same & upper[:, None] & (~upper)[None, :]).astype(np.float64))
    masks.append(np.eye(c))
    same_block = (idx[:, None] // ANCHOR_BLOCK) == (idx[None, :] // ANCHOR_BLOCK)
    masks.append((same_block & (idx[None, :] <= idx[:, None])).astype(np.float64))
    fwd = np.concatenate(mats, axis=0)
    bwd = np.concatenate([m[::-1, ::-1] for m in mats], axis=0)
    mk_f = np.stack([np.tile(m, (1, R_HEADS)) for m in masks])
    mk_b = np.stack([np.tile(m[::-1, ::-1], (1, R_HEADS)) for m in masks])
    hid = np.arange(R_WIDTH) // R_DIM
    bd = (hid[:, None] == hid[None, :]).astype(np.float64)
    return (jnp.asarray(np.stack([fwd, bwd]), BF16), jnp.asarray(np.stack([mk_f, mk_b]), F32),
            jnp.asarray(bd, BF16))


def _hgrn_kernel(*refs, nc, has_init):
    pr_ref, lbp_ref, gh_ref, mstk_ref, lmask_ref, bd_ref = refs[:6]
    if has_init:
        s0_ref, o_ref = refs[6:8]
        sfin_ref = None
    else:
        o_ref, sfin_ref = refs[6:8]
        s0_ref = None
    q_scr, k_scr, g_scr, gc_scr, qg_scr, upd_scr, dec_scr, sbd_scr = refs[8:]
    w = R_WIDTH
    c = CHUNK
    bd = bd_ref[...]
    hid = lax.broadcasted_iota(jnp.int32, (1, w), 1) // R_DIM

    q_scr[...] = _silu(pr_ref[:, 0:w])
    for d in range(2):
        z = pr_ref[:, (1 + d) * w:(2 + d) * w]
        e = jnp.exp(-jnp.abs(z))
        log_sig = jnp.minimum(z, 0.0) - jnp.log(1.0 + e)
        a = lbp_ref[0, d:d + 1, :]
        bb = lbp_ref[1, d:d + 1, :] + log_sig
        mx = jnp.maximum(a, bb)
        log_f = mx + jnp.log(1.0 + jnp.exp(-jnp.abs(a - bb)))
        g_hi, g_lo = _split2(log_f)
        g_scr[d, :, 0:w] = g_hi
        g_scr[d, :, w:2 * w] = g_lo
        sig_neg = jnp.where(z >= 0.0, e, 1.0) / (1.0 + e)
        k_scr[d] = lbp_ref[2, d:d + 1, :] * sig_neg

    def tile4(x):
        return jnp.concatenate([x] * R_HEADS, axis=0)

    def head_diagonal(full):
        out = full[(R_HEADS - 1) * R_DIM:R_HEADS * R_DIM]
        for h in range(R_HEADS - 2, -1, -1):
            out = jnp.where(hid == h, full[h * R_DIM:(h + 1) * R_DIM], out)
        return out

    def chunk_rows(ci):
        return pl.ds(pl.multiple_of(ci * c, c), c)

    def cumulative(d, rs, n_mats):
        s = _dot(mstk_ref[d, 0:n_mats * c, :], g_scr[d, rs, :])
        return s[:, 0:w] + s[:, w:2 * w]

    def level_exponents(gcum, d, halves):
        out = []
        for h in halves:
            parts = []
            for s0 in range(0, c, 2 * h):
                anchor = s0 + h - 1 + d
                parts.append(-jnp.abs(gcum[s0:s0 + 2 * h] - gcum[anchor:anchor + 1]))
            out.append(parts[0] if len(parts) == 1 else jnp.concatenate(parts, axis=0))
        return out

    def anchor_shift(gcum, d):
        parts = []
        for s0 in range(0, c, ANCHOR_BLOCK):
            anchor = s0 + ANCHOR_BLOCK // 2 - 1 + d
            parts.append(gcum[s0:s0 + ANCHOR_BLOCK] - gcum[anchor:anchor + 1])
        return jnp.concatenate(parts, axis=0)

    groups = nc // GROUP
    pairs = [(j, d) for j in range(GROUP) for d in range(2)]

    def increment_group(gi, worst):
        rows = [chunk_rows(gi * GROUP + j) for j in range(GROUP)]
        gcums = [cumulative(d, rows[j], 1) for j, d in pairs]
        lasts = [g[c - 1:c, :] if d == 0 else g[0:1, :] for g, (j, d) in zip(gcums, pairs)]
        kls = [(k_scr[d, rows[j], :] * jnp.exp(last - g)).astype(BF16)
               for g, last, (j, d) in zip(gcums, lasts, pairs)]
        vs = [pr_ref[rows[j], 3 * w:4 * w].astype(BF16) for j in range(GROUP)]
        upds = [_dot_tn(vs[j], kl) for kl, (j, d) in zip(kls, pairs)]
        for g, last, upd, (j, d) in zip(gcums, lasts, upds, pairs):
            ci = gi * GROUP + j
            gc_scr[d, rows[j], :] = g
            upd_scr[d, ci] = head_diagonal(upd)
            dec_scr[d, ci] = jnp.broadcast_to(jnp.exp(last), (8, w))
            qg_scr[d, rows[j], :] = (q_scr[rows[j], :] * jnp.exp(g)).astype(BF16)
            sh = jnp.abs(anchor_shift(g, d))
            for s0 in range(0, c, 8):
                worst = jnp.maximum(worst, sh[s0:s0 + 8])
        return worst

    worst = lax.fori_loop(0, groups, increment_group, jnp.zeros((8, w), F32))
    anchor_ok = jnp.max(worst) < ANCHOR_MAX_EXPONENT

    def scan_step(ci, carry):
        sf, sb = carry
        cb = nc - 1 - ci
        sbd_scr[0, ci] = tile4(sf.astype(BF16)) * bd
        sbd_scr[1, cb] = tile4(sb.astype(BF16)) * bd
        sf = sf * dec_scr[0, ci][0:1, :] + upd_scr[0, ci]
        sb = sb * dec_scr[1, cb][0:1, :] + upd_scr[1, cb]
        return sf, sb

    if has_init:
        init = tuple(jnp.concatenate([s0_ref[d], jnp.zeros((w, LANES - R_DIM), F32)], axis=1).T[0:R_DIM]
                     for d in range(2))
    else:
        init = (jnp.zeros((R_DIM, w), F32), jnp.zeros((R_DIM, w), F32))
    finals = lax.fori_loop(0, nc, scan_step, init)
    if sfin_ref is not None:
        for d in range(2):
            padded = jnp.concatenate([finals[d], jnp.zeros((LANES - R_DIM, w), F32)], axis=0)
            sfin_ref[d] = padded.T[:, 0:R_DIM]

    def finish(rs, o):
        x2_hi, x2_lo = _split2(o * o)
        ms = (_dot(x2_hi, bd) + _dot(x2_lo, bd)) * (1.0 / R_DIM)
        y = o * lax.rsqrt(ms + EPS) * gh_ref[...]
        o_ref[rs, :] = (y * _silu(pr_ref[rs, 4 * w:5 * w])).astype(BF16)

    def readout_group_anchor(gi, carry):
        rows = [chunk_rows(gi * GROUP + j) for j in range(GROUP)]
        qs = [q_scr[rows[j], :] for j in range(GROUP)]
        gcums = [gc_scr[d, rows[j], :] for j, d in pairs]
        factors = []
        for g, (j, d) in zip(gcums, pairs):
            es = [jnp.exp(x) for x in level_exponents(g, d, COARSE_HALVES[:ANCHOR_LEVELS])]
            sh = anchor_shift(g, d)
            fs = [(e, e, lv) for lv, e in enumerate(es)]
            fs.append((jnp.exp(sh), jnp.exp(-sh), MASK_ANCHOR))
            factors.append(fs)
        kbds = [tile4(k_scr[d, rows[j], :].astype(BF16)) * bd for j, d in pairs]
        accs = [None] * len(pairs)
        for lv in range(ANCHOR_LEVELS + 1):
            for i, (j, d) in enumerate(pairs):
                eq, ek, mask = factors[i][lv]
                part = _dot_nt((qs[j] * eq).astype(BF16), kbds[i] * tile4(ek.astype(BF16))) * lmask_ref[d, mask]
                accs[i] = part if accs[i] is None else accs[i] + part
        vbds = [tile4(pr_ref[rows[j], 3 * w:4 * w].astype(BF16)) * bd for j in range(GROUP)]
        outs = [None] * GROUP
        for i, (j, d) in enumerate(pairs):
            ci = gi * GROUP + j
            od = _dot(accs[i].astype(BF16), vbds[j]) + _dot_nt(qg_scr[d, rows[j], :], sbd_scr[d, ci])
            outs[j] = od if outs[j] is None else outs[j] + od
        for j in range(GROUP):
            finish(rows[j], outs[j])
        return carry

    def readout_step_split(ci, carry):
        rs = chunk_rows(ci)
        q = q_scr[rs, :]
        qb = q.astype(BF16)
        vbd = tile4(pr_ref[rs, 3 * w:4 * w].astype(BF16)) * bd
        dsts = [cumulative(d, rs, 1 + len(FINE_HALVES)) for d in range(2)]
        exps = [level_exponents(dsts[d][0:c], d, COARSE_HALVES)
                + [dsts[d][(1 + j) * c:(2 + j) * c] for j in range(len(FINE_HALVES))] for d in range(2)]
        kbd = [tile4(k_scr[d, rs, :].astype(BF16)) * bd for d in range(2)]
        a = [_dot_nt(qb, kbd[d]) * lmask_ref[d, MASK_DIAG] for d in range(2)]
        for j in range(N_LEVELS):
            for d in range(2):
                e = jnp.exp(exps[d][j])
                a[d] = a[d] + _dot_nt((q * e).astype(BF16), kbd[d] * tile4(e.astype(BF16))) * lmask_ref[d, j]
        o = None
        for d in range(2):
            od = _dot(a[d].astype(BF16), vbd) + _dot_nt(qg_scr[d, rs, :], sbd_scr[d, ci])
            o = od if o is None else o + od
        finish(rs, o)
        return carry

    @pl.when(anchor_ok)
    def _():
        lax.fori_loop(0, groups, readout_group_anchor, 0)

    @pl.when(jnp.logical_not(anchor_ok))
    def _():
        lax.fori_loop(0, nc, readout_step_split, 0)


def _hgrn(pr, lbp_l, g_hgrn_l, consts, state=None, layer=None):
    b, n, _ = pr.shape
    nc = n // CHUNK
    mstk, lmask, bd = consts
    w = R_WIDTH
    has_init = state is not None
    full = lambda *shape: pl.BlockSpec(shape, lambda i: (0,) * len(shape))
    in_specs = [
        pl.BlockSpec((None, n, PR_COLS), lambda i: (i, 0, 0)),
        full(3, 2, w),
        full(1, w),
        full(*mstk.shape),
        full(*lmask.shape),
        full(w, w),
    ]
    args = [pr, lbp_l, g_hgrn_l.reshape(1, w), mstk, lmask, bd]
    out_shape = [jax.ShapeDtypeStruct((b, n, w), BF16)]
    out_specs = [pl.BlockSpec((None, n, w), lambda i: (i, 0, 0))]
    if has_init:
        in_specs.append(pl.BlockSpec((None, None, 2, w, R_DIM), lambda i: (i, layer, 0, 0, 0)))
        args.append(state)
    else:
        out_shape.append(jax.ShapeDtypeStruct((b, 2, w, R_DIM), F32))
        out_specs.append(pl.BlockSpec((None, 2, w, R_DIM), lambda i: (i, 0, 0, 0)))
    return pl.pallas_call(
        functools.partial(_hgrn_kernel, nc=nc, has_init=has_init),
        out_shape=tuple(out_shape),
        grid=(b,),
        in_specs=in_specs,
        out_specs=tuple(out_specs),
        scratch_shapes=[
            pltpu.VMEM((n, w), F32),
            pltpu.VMEM((2, n, w), F32),
            pltpu.VMEM((2, n, 2 * w), BF16),
            pltpu.VMEM((2, n, w), F32),
            pltpu.VMEM((2, n, w), BF16),
            pltpu.VMEM((2, nc, R_DIM, w), F32),
            pltpu.VMEM((2, nc, 8, w), F32),
            pltpu.VMEM((2, nc, w, w), BF16),
        ],
        compiler_params=_cparams(("arbitrary",)),
        name="hgrn_scan",
    )(*args)


def _fnet_constants(n):
    j = np.arange(F_GROUP_DIM)
    ang = 2.0 * np.pi * ((j[:, None] * j[None, :]) % F_GROUP_DIM) / F_GROUP_DIM
    eye = np.eye(F_GROUPS)
    cs = np.concatenate([np.kron(eye, np.cos(ang)), np.kron(eye, np.sin(ang))], axis=1)
    t = np.arange(n)
    angn = 2.0 * np.pi * ((t[:, None] * t[None, :]) % n) / n
    return tuple(jnp.asarray(m, F32).astype(BF16) for m in (cs, np.cos(angn), np.sin(angn)))


def _fnet_kernel(pf_ref, cs_ref, cn_ref, sn_ref, wf_ref, o_ref, *, scale):
    w = F_WIDTH
    t = _dot(pf_ref[:, 0:w].astype(BF16), cs_ref[...])
    y = (_dot(cn_ref[...], t[:, 0:w].astype(BF16)) - _dot(sn_ref[...], t[:, w:2 * w].astype(BF16))) * scale
    of = _dot(y.astype(BF16), wf_ref[...])
    o_ref[...] = (of * _silu(pf_ref[:, w:2 * w])).astype(BF16)


def _fourier(pf, consts, w_fnet_bf16):
    b, n, _ = pf.shape
    cs, cn, sn = consts
    w = F_WIDTH
    full = lambda *shape: pl.BlockSpec(shape, lambda i: (0,) * len(shape))
    return pl.pallas_call(
        functools.partial(_fnet_kernel, scale=float((n * F_GROUP_DIM) ** -0.5)),
        out_shape=jax.ShapeDtypeStruct((b, n, w), BF16),
        grid=(b,),
        in_specs=[pl.BlockSpec((None, n, PF_COLS), lambda i: (i, 0, 0)),
                  full(w, 2 * w), full(n, n), full(n, n), full(w, w)],
        out_specs=pl.BlockSpec((None, n, w), lambda i: (i, 0, 0)),
        compiler_params=_cparams(("arbitrary",)),
        name="fourier_mixing",
    )(pf, cs, cn, sn, w_fnet_bf16)


def _outproj_kernel(ma_ref, mr_ref, mf_ref, w_ref, x_ref, mod_ref, g_ref, y_ref):
    out = (_dot(ma_ref[...], w_ref[0:A_WIDTH, :])
           + _dot(mr_ref[...], w_ref[A_WIDTH:A_WIDTH + R_WIDTH, :])
           + _dot(mf_ref[...], w_ref[A_WIDTH + R_WIDTH:D_MODEL, :]))
    ms = jnp.mean(out * out, axis=-1, keepdims=True)
    y_ref[...] = x_ref[...] + mod_ref[2:3, :] * (out * lax.rsqrt(ms + EPS) * g_ref[...])


def _out_projection(ma, mr, mf, w_out_bf16, x, mod, g_post):
    b, n, _ = x.shape
    tm = min(n, 512)
    tiles = n // tm
    tok = lambda i: (i // tiles, i % tiles, 0)
    nb_mod = mod.shape[0]
    mod_idx = (lambda i: (i // tiles, 0, 0)) if nb_mod > 1 else (lambda i: (0, 0, 0))
    return pl.pallas_call(
        _outproj_kernel,
        out_shape=jax.ShapeDtypeStruct((b, n, D_MODEL), F32),
        grid=(b * tiles,),
        in_specs=[
            pl.BlockSpec((None, tm, A_WIDTH), tok),
            pl.BlockSpec((None, tm, R_WIDTH), tok),
            pl.BlockSpec((None, tm, F_WIDTH), tok),
            pl.BlockSpec((D_MODEL, D_MODEL), lambda i: (0, 0)),
            pl.BlockSpec((None, tm, D_MODEL), tok),
            pl.BlockSpec((None, 3, D_MODEL), mod_idx),
            pl.BlockSpec((1, D_MODEL), lambda i: (0, 0)),
        ],
        out_specs=pl.BlockSpec((None, tm, D_MODEL), tok),
        compiler_params=_cparams(("arbitrary",)),
        name="out_projection",
    )(ma, mr, mf, w_out_bf16, x, mod, g_post.reshape(1, D_MODEL))


def kernel(x_prompt, x_sample, cache_attn_k, cache_attn_v, state_hgrn, c, c_ctx,
           w_ada, b_ada, g_pre, w_in, rpb, lb_logits, g_hgrn, w_fnet, w_out, g_post):
    nb_ctx, n_ctx, _ = x_prompt.shape
    nb_lat, n_lat, _ = x_sample.shape

    pad_rows = (-(1 + nb_lat)) % 8
    cc = jnp.concatenate([c_ctx[None, :], c, jnp.zeros((pad_rows, D_MODEL), F32)], axis=0)
    mods = _modulations(cc, w_ada, b_ada)
    lbp = _lower_bounds(lb_logits)

    w_in_b = w_in.astype(BF16)
    w_out_b = w_out.astype(BF16)
    w_fnet_b = w_fnet.astype(BF16)
    hconsts = _hgrn_constants()
    fconsts_ctx = _fnet_constants(n_ctx)
    fconsts_lat = _fnet_constants(n_lat)

    state_rows = state_hgrn.reshape(nb_lat, DEPTH, 2, R_WIDTH, R_DIM)
    cache_kt = jnp.transpose(cache_attn_k, (0, 1, 3, 4, 2))
    cache_vt = jnp.transpose(cache_attn_v, (0, 1, 3, 4, 2))

    yp, ys = x_prompt, x_sample
    new_kv, new_s = None, []
    for l in range(DEPTH):
        lbp_l = lbp[:, l]
        mod = mods[l, 0].reshape(1, 3, D_MODEL)
        pa, pr, pf, *new_kv = _in_projection(yp, mod, g_pre[l], w_in_b[l], cache=(l, new_kv))
        ma = _context_attention(pa)
        mr, sfin = _hgrn(pr, lbp_l, g_hgrn[l], hconsts)
        new_s.append(sfin)
        mf = _fourier(pf, fconsts_ctx, w_fnet_b[l])
        yp = _out_projection(ma, mr, mf, w_out_b[l], yp, mod, g_post[l])
        mod = mods[l, 1:1 + nb_lat].reshape(nb_lat, 3, D_MODEL)
        pa, pr, pf = _in_projection(ys, mod, g_pre[l], w_in_b[l])
        ma = _neighbourhood_attention(pa, cache_kt, cache_vt, l, rpb[l])
        (mr,) = _hgrn(pr, lbp_l, g_hgrn[l], hconsts, state=state_rows, layer=l)
        mf = _fourier(pf, fconsts_lat, w_fnet_b[l])
        ys = _out_projection(ma, mr, mf, w_out_b[l], ys, mod, g_post[l])

    new_state = jnp.stack(new_s, axis=1).reshape(nb_ctx, DEPTH, 2, R_HEADS, R_DIM, R_DIM)
    new_k, new_v = (jnp.transpose(t, (0, 1, 4, 2, 3)) for t in new_kv)
    return (yp, ys, new_k, new_v, new_state)
```

```python
import functools

import numpy as np
import jax
import jax.numpy as jnp
from jax import lax
from jax.experimental import pallas as pl
from jax.experimental.pallas import tpu as pltpu

F32 = jnp.float32
BF16 = jnp.bfloat16

D_MODEL = 1024
DEPTH = 4
GRID_W = 64
WIN_H = 8
WIN_W = 16
HEAD_DIM = 64
A_HEADS = 8
A_WIDTH = A_HEADS * HEAD_DIM
R_HEADS = 4
R_DIM = 64
R_WIDTH = R_HEADS * R_DIM
F_GROUPS = 4
F_GROUP_DIM = 64
F_WIDTH = F_GROUPS * F_GROUP_DIM
PA_COLS = 4 * A_WIDTH
PR_COLS = 5 * R_WIDTH
PF_COLS = 2 * F_WIDTH
IN_COLS = PA_COLS + PR_COLS + PF_COLS
CHUNK = 64
EPS = 1e-6
LANES = 128
NEG_INF = float("-inf")
VMEM_LIMIT = 56 * 1024 * 1024


def _cparams(sem):
    return pltpu.CompilerParams(dimension_semantics=sem, vmem_limit_bytes=VMEM_LIMIT)


def _silu(x):
    return x / (1.0 + jnp.exp(-x))


def _dot(a, b):
    return jnp.dot(a, b, preferred_element_type=F32)


def _dot_nt(a, b):
    return lax.dot_general(a, b, (((1,), (1,)), ((), ())), preferred_element_type=F32)


def _dot_tn(a, b):
    return lax.dot_general(a, b, (((0,), (0,)), ((), ())), preferred_element_type=F32)


def _split2(x):
    hi = x.astype(BF16)
    lo = (x - hi.astype(F32)).astype(BF16)
    return hi, lo


def _mod_kernel(cc_ref, w_ref, b_ref, o_ref):
    a_hi, a_lo = _split2(_silu(cc_ref[...]))
    w_hi, w_lo = _split2(w_ref[...])
    acc = _dot(a_hi, w_hi) + _dot(a_hi, w_lo) + _dot(a_lo, w_hi)
    o_ref[...] = acc + b_ref[...]


def _modulations(cc, w_ada, b_ada):
    rows = cc.shape[0]
    tn = 1024
    return pl.pallas_call(
        _mod_kernel,
        out_shape=jax.ShapeDtypeStruct((DEPTH, rows, 3 * D_MODEL), F32),
        grid=(DEPTH, 3 * D_MODEL // tn),
        in_specs=[
            pl.BlockSpec((rows, D_MODEL), lambda l, j: (0, 0)),
            pl.BlockSpec((None, D_MODEL, tn), lambda l, j: (l, 0, j)),
            pl.BlockSpec((None, 1, tn), lambda l, j: (l, 0, j)),
        ],
        out_specs=pl.BlockSpec((None, rows, tn), lambda l, j: (l, 0, j)),
        compiler_params=_cparams(("arbitrary", "arbitrary")),
        name="adaln_mod",
    )(cc, w_ada, b_ada.reshape(DEPTH, 1, 3 * D_MODEL))


def _lb_kernel(x_ref, o_ref):
    xs = [x_ref[i] for i in range(DEPTH)]
    m = functools.reduce(jnp.maximum, xs)
    es = [jnp.exp(x - m) for x in xs]
    tot = functools.reduce(lambda a, b: a + b, es)
    cum = None
    first = None
    for i in range(DEPTH):
        p = es[i] / tot
        cum = p if cum is None else cum + p
        if first is None:
            first = cum
        lb = jnp.maximum(cum - first, 0.0)
        o_ref[0, i] = jnp.log(lb)
        o_ref[1, i] = jnp.log1p(-lb)
        o_ref[2, i] = 1.0 - lb


def _lower_bounds(lb_logits):
    x = jnp.transpose(lb_logits, (1, 0, 2))
    return pl.pallas_call(
        _lb_kernel,
        out_shape=jax.ShapeDtypeStruct((3, DEPTH, 2, R_WIDTH), F32),
        name="hgrn_lower_bounds",
    )(x)


def _inproj_kernel(*refs, n_prev, with_cache):
    x_ref, mod_ref, g_ref, w_ref = refs[:4]
    outs = refs[4 + n_prev:]
    pa_ref, pr_ref, pf_ref = outs[:3]
    x = x_ref[...]
    ms = jnp.mean(x * x, axis=-1, keepdims=True)
    y = x * lax.rsqrt(ms + EPS) * g_ref[...]
    h = (y * (1.0 + mod_ref[1:2, :]) + mod_ref[0:1, :]).astype(BF16)
    pa = _dot(h, w_ref[:, 0:PA_COLS])
    pa_ref[...] = pa
    pr_ref[...] = _dot(h, w_ref[:, PA_COLS:PA_COLS + PR_COLS])
    pf_ref[...] = _dot(h, w_ref[:, PA_COLS + PR_COLS:IN_COLS])
    if with_cache:
        ko_ref, vo_ref = outs[3:5]
        tm = pa.shape[0]
        ko_ref[...] = pa[:, A_WIDTH:2 * A_WIDTH].T.reshape(A_HEADS, HEAD_DIM, tm)
        vo_ref[...] = pa[:, 2 * A_WIDTH:3 * A_WIDTH].T.reshape(A_HEADS, HEAD_DIM, tm)


def _in_projection(x, mod, g_pre, w_in_bf16, cache=None):
    b, n, _ = x.shape
    tm = min(n, 512)
    tiles = n // tm
    tok = lambda i: (i // tiles, i % tiles, 0)
    nb_mod = mod.shape[0]
    mod_idx = (lambda i: (i // tiles, 0, 0)) if nb_mod > 1 else (lambda i: (0, 0, 0))
    in_specs = [
        pl.BlockSpec((None, tm, D_MODEL), tok),
        pl.BlockSpec((None, 3, D_MODEL), mod_idx),
        pl.BlockSpec((1, D_MODEL), lambda i: (0, 0)),
        pl.BlockSpec((D_MODEL, IN_COLS), lambda i: (0, 0)),
    ]
    args = [x, mod, g_pre.reshape(1, D_MODEL), w_in_bf16]
    out_shape = [
        jax.ShapeDtypeStruct((b, n, PA_COLS), F32),
        jax.ShapeDtypeStruct((b, n, PR_COLS), F32),
        jax.ShapeDtypeStruct((b, n, PF_COLS), F32),
    ]
    out_specs = [
        pl.BlockSpec((None, tm, PA_COLS), tok),
        pl.BlockSpec((None, tm, PR_COLS), tok),
        pl.BlockSpec((None, tm, PF_COLS), tok),
    ]
    aliases = {}
    n_prev = 0
    if cache is not None:
        layer, prev = cache
        cache_shape = jax.ShapeDtypeStruct((b, DEPTH, A_HEADS, HEAD_DIM, n), F32)
        cache_spec = pl.BlockSpec((None, None, A_HEADS, HEAD_DIM, tm),
                                  lambda i: (i // tiles, layer, 0, 0, i % tiles))
        out_shape += [cache_shape, cache_shape]
        out_specs += [cache_spec, cache_spec]
        if prev is not None:
            n_prev = 2
            in_specs += [pl.BlockSpec(memory_space=pl.ANY)] * 2
            args += list(prev)
            aliases = {4: 3, 5: 4}
    return pl.pallas_call(
        functools.partial(_inproj_kernel, n_prev=n_prev, with_cache=cache is not None),
        out_shape=tuple(out_shape),
        grid=(b * tiles,),
        in_specs=in_specs,
        out_specs=tuple(out_specs),
        input_output_aliases=aliases,
        compiler_params=_cparams(("arbitrary",)),
        name="in_projection",
    )(*args)


def _lane_lo():
    return lax.broadcasted_iota(jnp.int32, (1, LANES), 1) < HEAD_DIM


def _ctx_attn_kernel(pa_ref, o_ref, s_scr):
    lo = _lane_lo()
    npair = A_HEADS // 2

    def scores(p):
        c = p * LANES
        q2 = pa_ref[:, c:c + LANES] * (HEAD_DIM ** -0.5)
        k2 = pa_ref[:, A_WIDTH + c:A_WIDTH + c + LANES].astype(BF16)
        for hh in range(2):
            sel = lo if hh == 0 else jnp.logical_not(lo)
            s_scr[p % 2, hh] = _dot_nt(jnp.where(sel, q2, 0.0).astype(BF16), k2)

    def finish(p):
        c = p * LANES
        v2 = pa_ref[:, 2 * A_WIDTH + c:2 * A_WIDTH + c + LANES].astype(BF16)
        outs = []
        for hh in range(2):
            s = s_scr[p % 2, hh]
            e = jnp.exp(s - jnp.max(s, axis=-1, keepdims=True))
            inv = 1.0 / jnp.sum(e, axis=-1, keepdims=True)
            outs.append(_dot(e.astype(BF16), v2) * inv)
        o2 = jnp.where(lo, outs[0], outs[1])
        o_ref[:, c:c + LANES] = (o2 * _silu(pa_ref[:, 3 * A_WIDTH + c:3 * A_WIDTH + c + LANES])).astype(BF16)

    scores(0)
    for p in range(npair):
        if p + 1 < npair:
            scores(p + 1)
        finish(p)


def _context_attention(pa):
    b, n, _ = pa.shape
    return pl.pallas_call(
        _ctx_attn_kernel,
        out_shape=jax.ShapeDtypeStruct((b, n, A_WIDTH), BF16),
        grid=(b,),
        in_specs=[pl.BlockSpec((None, n, PA_COLS), lambda i: (i, 0, 0))],
        out_specs=pl.BlockSpec((None, n, A_WIDTH), lambda i: (i, 0, 0)),
        scratch_shapes=[pltpu.VMEM((2, 2, n, n), F32)],
        compiler_params=_cparams(("arbitrary",)),
        name="context_attention",
    )(pa)


QROWS = 4


def _nbr_blocks(rows):
    kh = min(WIN_H, rows)
    out = []
    for r_first in range(0, rows, QROWS):
        r0s = [min(max(r - kh // 2, 0), rows - kh) for r in range(r_first, r_first + QROWS)]
        lo, hi = min(r0s), max(r0s) + kh
        lo -= lo % 2
        span = hi - lo
        span += (-span) % 4
        if lo + span > rows:
            lo = rows - span
        assert lo >= 0 and lo % 2 == 0
        out.append((r_first, lo, span, r0s))
    return out, kh


N_REL_ROWS = 2 * WIN_H - 1
TABLE_W = 2 * WIN_H * GRID_W


def _build_bias_tables(base_ref, te_scr, to_scr, heads):
    lane = lax.broadcasted_iota(jnp.int32, (GRID_W, LANES), 1)
    qc = lax.broadcasted_iota(jnp.int32, (GRID_W, LANES), 0)
    kc = lane % GRID_W
    lo = lane < GRID_W
    c0 = jnp.clip(qc - WIN_W // 2, 0, GRID_W - WIN_W)
    col_in = (kc >= c0) & (kc < c0 + WIN_W)
    neg = jnp.full((GRID_W, LANES), NEG_INF, F32)

    for h in heads:
        def tile(i, lane_off):
            row = jnp.broadcast_to(base_ref[h, i:i + 1, :], (GRID_W, LANES))
            return pltpu.roll(row, lane_off, 1, stride=1, stride_axis=0)

        for j in range(WIN_H):
            for scr, ia in ((te_scr, 2 * j), (to_scr, 2 * j + 1)):
                ib = ia + 1
                if ia >= N_REL_ROWS:
                    pair = neg
                elif ib >= N_REL_ROWS:
                    pair = jnp.where(col_in & lo, tile(ia, 0), NEG_INF)
                else:
                    pair = jnp.where(col_in, jnp.where(lo, tile(ia, 0), tile(ib, GRID_W)), NEG_INF)
                scr[h, :, j * LANES:(j + 1) * LANES] = pair


def _nbr_attn_kernel(q_ref, k_ref, v_ref, g_ref, kc_ref, vc_ref, base_ref, o_ref, s_scr, p_scr, te_scr, to_scr, *, rows):
    p = pl.program_id(1)

    @pl.when(pl.program_id(0) == 0)
    def _():
        _build_bias_tables(base_ref, te_scr, to_scr, (2 * p, 2 * p + 1))

    lo = _lane_lo()
    hi = jnp.logical_not(lo)
    blocks, kh = _nbr_blocks(rows)
    nq = QROWS * GRID_W
    past = kc_ref.shape[-1]
    ctx0 = s_scr.shape[-1] - past
    kct = kc_ref[...].reshape(LANES, past).astype(BF16)
    vct = vc_ref[...].reshape(LANES, past).astype(BF16)
    ctx_tiles = [slice(ctx0 + t * LANES, ctx0 + (t + 1) * LANES) for t in range(past // LANES)]

    def window_tiles(bi, rl, head):
        r_first, u0, span, r0s = blocks[bi]
        r, r0 = r_first + rl, r0s[rl]
        out = []
        for t in range(span // 2):
            kr0 = u0 + 2 * t
            cs = slice(t * LANES, (t + 1) * LANES)
            ok0 = r0 <= kr0 < r0 + kh
            ok1 = r0 <= kr0 + 1 < r0 + kh
            if not (ok0 or ok1):
                out.append((cs, None))
                continue
            i0 = kr0 - r + WIN_H - 1
            if i0 % 2 == 0:
                assert 0 <= i0 and i0 + 2 <= 2 * WIN_H
                bias = te_scr[head, :, i0 * GRID_W:(i0 + 2) * GRID_W]
            else:
                assert 1 <= i0 and i0 + 1 <= 2 * WIN_H - 1
                bias = to_scr[head, :, (i0 - 1) * GRID_W:(i0 + 1) * GRID_W]
            if not ok0:
                bias = jnp.where(lo, NEG_INF, bias)
            if not ok1:
                bias = jnp.where(hi, NEG_INF, bias)
            out.append((cs, bias))
        return out

    def scores(bi):
        r_first, u0, span, _ = blocks[bi]
        qs, ks, nk = r_first * GRID_W, u0 * GRID_W, span * GRID_W
        q2 = q_ref[qs:qs + nq, :] * (HEAD_DIM ** -0.5)
        ku = k_ref[ks:ks + nk, :].astype(BF16)
        for hh in range(2):
            qh = jnp.where(lo if hh == 0 else hi, q2, 0.0).astype(BF16)
            s_scr[bi % 2, hh, :, 0:nk] = _dot_nt(qh, ku)
            s_scr[bi % 2, hh, :, ctx0:ctx0 + past] = _dot(qh, kct)

    def softmax(bi):
        par = bi % 2
        inv = []
        for hh in range(2):
            parts = []
            for rl in range(QROWS):
                rs = slice(rl * GRID_W, (rl + 1) * GRID_W)
                tiles = window_tiles(bi, rl, 2 * p + hh)
                m_acc = None
                for cs, bias in tiles:
                    if bias is None:
                        continue
                    s = s_scr[par, hh, rs, cs] + bias
                    s_scr[par, hh, rs, cs] = s
                    m_acc = s if m_acc is None else jnp.maximum(m_acc, s)
                for cs in ctx_tiles:
                    m_acc = jnp.maximum(m_acc, s_scr[par, hh, rs, cs])
                m = jnp.max(m_acc, axis=-1, keepdims=True)
                l_acc = None
                for cs, bias in tiles:
                    if bias is None:
                        p_scr[par, hh, rs, cs] = jnp.zeros((GRID_W, LANES), BF16)
                        continue
                    e = jnp.exp(s_scr[par, hh, rs, cs] - m)
                    p_scr[par, hh, rs, cs] = e.astype(BF16)
                    l_acc = e if l_acc is None else l_acc + e
                for cs in ctx_tiles:
                    e = jnp.exp(s_scr[par, hh, rs, cs] - m)
                    p_scr[par, hh, rs, cs] = e.astype(BF16)
                    l_acc = l_acc + e
                parts.append(1.0 / jnp.sum(l_acc, axis=-1, keepdims=True))
            inv.append(jnp.concatenate(parts, axis=0))
        return inv

    def values(bi, inv):
        r_first, u0, span, _ = blocks[bi]
        qs, ks, nk = r_first * GRID_W, u0 * GRID_W, span * GRID_W
        vu = v_ref[ks:ks + nk, :].astype(BF16)
        outs = []
        for hh in range(2):
            o = _dot(p_scr[bi % 2, hh, :, 0:nk], vu) + _dot_nt(p_scr[bi % 2, hh, :, ctx0:ctx0 + past], vct)
            outs.append(o * inv[hh])
        o2 = jnp.where(lo, outs[0], outs[1])
        o_ref[qs:qs + nq, :] = (o2 * _silu(g_ref[qs:qs + nq, :])).astype(BF16)

    scores(0)
    for bi in range(len(blocks)):
        if bi + 1 < len(blocks):
            scores(bi + 1)
        values(bi, softmax(bi))


def _bias_base(rpb_l):
    mid = WIN_W - 1
    zeros = jnp.zeros(rpb_l.shape[:2] + (LANES - (2 * WIN_W - 1),), F32)
    base = jnp.concatenate([rpb_l[..., mid:], zeros, rpb_l[..., :mid]], axis=-1)
    return jnp.pad(base, ((0, 0), (0, 2 * WIN_H - N_REL_ROWS), (0, 0)))


def _neighbourhood_attention(pa, cache_kt, cache_vt, layer, rpb_l):
    b, n, _ = pa.shape
    rows = n // GRID_W
    past = cache_kt.shape[-1]
    blocks, _ = _nbr_blocks(rows)
    max_nk = max(s for (_, _, s, _) in blocks) * GRID_W
    npair = A_HEADS // 2
    col = lambda off: (lambda i, p: (i, 0, off + p))
    cache_spec = pl.BlockSpec((None, None, 2, HEAD_DIM, past), lambda i, p: (i, layer, p, 0, 0))
    return pl.pallas_call(
        functools.partial(_nbr_attn_kernel, rows=rows),
        out_shape=jax.ShapeDtypeStruct((b, n, A_WIDTH), BF16),
        grid=(b, npair),
        in_specs=[
            pl.BlockSpec((None, n, LANES), col(0)),
            pl.BlockSpec((None, n, LANES), col(npair)),
            pl.BlockSpec((None, n, LANES), col(2 * npair)),
            pl.BlockSpec((None, n, LANES), col(3 * npair)),
            cache_spec,
            cache_spec,
            pl.BlockSpec((A_HEADS, 2 * WIN_H, LANES), lambda i, p: (0, 0, 0)),
        ],
        out_specs=pl.BlockSpec((None, n, LANES), col(0)),
        scratch_shapes=[
            pltpu.VMEM((2, 2, QROWS * GRID_W, max_nk + past), F32),
            pltpu.VMEM((2, 2, QROWS * GRID_W, max_nk + past), BF16),
            pltpu.VMEM((A_HEADS, GRID_W, TABLE_W), F32),
            pltpu.VMEM((A_HEADS, GRID_W, TABLE_W), F32),
        ],
        compiler_params=_cparams(("arbitrary", "arbitrary")),
        name="neighbourhood_attention",
    )(pa, pa, pa, pa, cache_kt, cache_vt, _bias_base(rpb_l))


COARSE_HALVES = (32, 16, 8)
FINE_HALVES = (4, 2, 1)
N_LEVELS = len(COARSE_HALVES) + len(FINE_HALVES)
ANCHOR_BLOCK = 16
ANCHOR_LEVELS = 2
ANCHOR_MAX_EXPONENT = 60.0
MASK_DIAG = N_LEVELS
MASK_ANCHOR = N_LEVELS + 1
GROUP = 4


def _hgrn_constants():
    c = CHUNK
    idx = np.arange(c)
    mats = [np.tril(np.ones((c, c)))]
    masks = []
    for h in COARSE_HALVES + FINE_HALVES:
        blk = idx // (2 * h)
        mid = blk * 2 * h + h - 1
        upper = idx > mid
        if h in FINE_HALVES:
            m = np.zeros((c, c))
            for i in range(c):
                if upper[i]:
                    m[i, mid[i] + 1:i + 1] = 1.0
                else:
                    m[i, i + 1:mid[i] + 1] = 1.0
            mats.append(m)
        same = blk[:, None] == blk[None, :]
        masks.append((same & upper[:, None] & (~upper)[None, :]).astype(np.float64))
    masks.append(np.eye(c))
    same_block = (idx[:, None] // ANCHOR_BLOCK) == (idx[None, :] // ANCHOR_BLOCK)
    masks.append((same_block & (idx[None, :] <= idx[:, None])).astype(np.float64))
    fwd = np.concatenate(mats, axis=0)
    bwd = np.concatenate([m[::-1, ::-1] for m in mats], axis=0)
    mk_f = np.stack([np.tile(m, (1, R_HEADS)) for m in masks])
    mk_b = np.stack([np.tile(m[::-1, ::-1], (1, R_HEADS)) for m in masks])
    hid = np.arange(R_WIDTH) // R_DIM
    bd = (hid[:, None] == hid[None, :]).astype(np.float64)
    return (jnp.asarray(np.stack([fwd, bwd]), BF16), jnp.asarray(np.stack([mk_f, mk_b]), F32),
            jnp.asarray(bd, BF16))


def _hgrn_kernel(*refs, nc, has_init):
    pr_ref, lbp_ref, gh_ref, mstk_ref, lmask_ref, bd_ref = refs[:6]
    if has_init:
        s0_ref, o_ref = refs[6:8]
        sfin_ref = None
    else:
        o_ref, sfin_ref = refs[6:8]
        s0_ref = None
    q_scr, k_scr, g_scr, gc_scr, qg_scr, upd_scr, dec_scr, sbd_scr = refs[8:]
    w = R_WIDTH
    c = CHUNK
    bd = bd_ref[...]
    hid = lax.broadcasted_iota(jnp.int32, (1, w), 1) // R_DIM

    q_scr[...] = _silu(pr_ref[:, 0:w])
    for d in range(2):
        z = pr_ref[:, (1 + d) * w:(2 + d) * w]
        e = jnp.exp(-jnp.abs(z))
        log_sig = jnp.minimum(z, 0.0) - jnp.log(1.0 + e)
        a = lbp_ref[0, d:d + 1, :]
        bb = lbp_ref[1, d:d + 1, :] + log_sig
        mx = jnp.maximum(a, bb)
        log_f = mx + jnp.log(1.0 + jnp.exp(-jnp.abs(a - bb)))
        g_hi, g_lo = _split2(log_f)
        g_scr[d, :, 0:w] = g_hi
        g_scr[d, :, w:2 * w] = g_lo
        sig_neg = jnp.where(z >= 0.0, e, 1.0) / (1.0 + e)
        k_scr[d] = lbp_ref[2, d:d + 1, :] * sig_neg

    def tile4(x):
        return jnp.concatenate([x] * R_HEADS, axis=0)

    def head_diagonal(full):
        out = full[(R_HEADS - 1) * R_DIM:R_HEADS * R_DIM]
        for h in range(R_HEADS - 2, -1, -1):
            out = jnp.where(hid == h, full[h * R_DIM:(h + 1) * R_DIM], out)
        return out

    def chunk_rows(ci):
        return pl.ds(pl.multiple_of(ci * c, c), c)

    def cumulative(d, rs, n_mats):
        s = _dot(mstk_ref[d, 0:n_mats * c, :], g_scr[d, rs, :])
        return s[:, 0:w] + s[:, w:2 * w]

    def level_exponents(gcum, d, halves):
        out = []
        for h in halves:
            parts = []
            for s0 in range(0, c, 2 * h):
                anchor = s0 + h - 1 + d
                parts.append(-jnp.abs(gcum[s0:s0 + 2 * h] - gcum[anchor:anchor + 1]))
            out.append(parts[0] if len(parts) == 1 else jnp.concatenate(parts, axis=0))
        return out

    def anchor_shift(gcum, d):
        parts = []
        for s0 in range(0, c, ANCHOR_BLOCK):
            anchor = s0 + ANCHOR_BLOCK // 2 - 1 + d
            parts.append(gcum[s0:s0 + ANCHOR_BLOCK] - gcum[anchor:anchor + 1])
        return jnp.concatenate(parts, axis=0)

    groups = nc // GROUP
    pairs = [(j, d) for j in range(GROUP) for d in range(2)]

    def increment_group(gi, worst):
        rows = [chunk_rows(gi * GROUP + j) for j in range(GROUP)]
        gcums = [cumulative(d, rows[j], 1) for j, d in pairs]
        lasts = [g[c - 1:c, :] if d == 0 else g[0:1, :] for g, (j, d) in zip(gcums, pairs)]
        kls = [(k_scr[d, rows[j], :] * jnp.exp(last - g)).astype(BF16)
               for g, last, (j, d) in zip(gcums, lasts, pairs)]
        vs = [pr_ref[rows[j], 3 * w:4 * w].astype(BF16) for j in range(GROUP)]
        upds = [_dot_tn(vs[j], kl) for kl, (j, d) in zip(kls, pairs)]
        for g, last, upd, (j, d) in zip(gcums, lasts, upds, pairs):
            ci = gi * GROUP + j
            gc_scr[d, rows[j], :] = g
            upd_scr[d, ci] = head_diagonal(upd)
            dec_scr[d, ci] = jnp.broadcast_to(jnp.exp(last), (8, w))
            qg_scr[d, rows[j], :] = (q_scr[rows[j], :] * jnp.exp(g)).astype(BF16)
            sh = jnp.abs(anchor_shift(g, d))
            for s0 in range(0, c, 8):
                worst = jnp.maximum(worst, sh[s0:s0 + 8])
        return worst

    worst = lax.fori_loop(0, groups, increment_group, jnp.zeros((8, w), F32))
    anchor_ok = jnp.max(worst) < ANCHOR_MAX_EXPONENT

    def scan_step(ci, carry):
        sf, sb = carry
        cb = nc - 1 - ci
        sbd_scr[0, ci] = tile4(sf.astype(BF16)) * bd
        sbd_scr[1, cb] = tile4(sb.astype(BF16)) * bd
        sf = sf * dec_scr[0, ci][0:1, :] + upd_scr[0, ci]
        sb = sb * dec_scr[1, cb][0:1, :] + upd_scr[1, cb]
        return sf, sb

    if has_init:
        init = tuple(jnp.concatenate([s0_ref[d], jnp.zeros((w, LANES - R_DIM), F32)], axis=1).T[0:R_DIM]
                     for d in range(2))
    else:
        init = (jnp.zeros((R_DIM, w), F32), jnp.zeros((R_DIM, w), F32))
    finals = lax.fori_loop(0, nc, scan_step, init)
    if sfin_ref is not None:
        for d in range(2):
            padded = jnp.concatenate([finals[d], jnp.zeros((LANES - R_DIM, w), F32)], axis=0)
            sfin_ref[d] = padded.T[:, 0:R_DIM]

    def finish(rs, o):
        x2_hi, x2_lo = _split2(o * o)
        ms = (_dot(x2_hi, bd) + _dot(x2_lo, bd)) * (1.0 / R_DIM)
        y = o * lax.rsqrt(ms + EPS) * gh_ref[...]
        o_ref[rs, :] = (y * _silu(pr_ref[rs, 4 * w:5 * w])).astype(BF16)

    def readout_group_anchor(gi, carry):
        rows = [chunk_rows(gi * GROUP + j) for j in range(GROUP)]
        qs = [q_scr[rows[j], :] for j in range(GROUP)]
        gcums = [gc_scr[d, rows[j], :] for j, d in pairs]
        factors = []
        for g, (j, d) in zip(gcums, pairs):
            es = [jnp.exp(x) for x in level_exponents(g, d, COARSE_HALVES[:ANCHOR_LEVELS])]
            sh = anchor_shift(g, d)
            fs = [(e, e, lv) for lv, e in enumerate(es)]
            fs.append((jnp.exp(sh), jnp.exp(-sh), MASK_ANCHOR))
            factors.append(fs)
        kbds = [tile4(k_scr[d, rows[j], :].astype(BF16)) * bd for j, d in pairs]
        accs = [None] * len(pairs)
        for lv in range(ANCHOR_LEVELS + 1):
            for i, (j, d) in enumerate(pairs):
                eq, ek, mask = factors[i][lv]
                part = _dot_nt((qs[j] * eq).astype(BF16), kbds[i] * tile4(ek.astype(BF16))) * lmask_ref[d, mask]
                accs[i] = part if accs[i] is None else accs[i] + part
        vbds = [tile4(pr_ref[rows[j], 3 * w:4 * w].astype(BF16)) * bd for j in range(GROUP)]
        outs = [None] * GROUP
        for i, (j, d) in enumerate(pairs):
            ci = gi * GROUP + j
            od = _dot(accs[i].astype(BF16), vbds[j]) + _dot_nt(qg_scr[d, rows[j], :], sbd_scr[d, ci])
            outs[j] = od if outs[j] is None else outs[j] + od
        for j in range(GROUP):
            finish(rows[j], outs[j])
        return carry

    def readout_step_split(ci, carry):
        rs = chunk_rows(ci)
        q = q_scr[rs, :]
        qb = q.astype(BF16)
        vbd = tile4(pr_ref[rs, 3 * w:4 * w].astype(BF16)) * bd
        dsts = [cumulative(d, rs, 1 + len(FINE_HALVES)) for d in range(2)]
        exps = [level_exponents(dsts[d][0:c], d, COARSE_HALVES)
                + [dsts[d][(1 + j) * c:(2 + j) * c] for j in range(len(FINE_HALVES))] for d in range(2)]
        kbd = [tile4(k_scr[d, rs, :].astype(BF16)) * bd for d in range(2)]
        a = [_dot_nt(qb, kbd[d]) * lmask_ref[d, MASK_DIAG] for d in range(2)]
        for j in range(N_LEVELS):
            for d in range(2):
                e = jnp.exp(exps[d][j])
                a[d] = a[d] + _dot_nt((q * e).astype(BF16), kbd[d] * tile4(e.astype(BF16))) * lmask_ref[d, j]
        o = None
        for d in range(2):
            od = _dot(a[d].astype(BF16), vbd) + _dot_nt(qg_scr[d, rs, :], sbd_scr[d, ci])
            o = od if o is None else o + od
        finish(rs, o)
        return carry

    @pl.when(anchor_ok)
    def _():
        lax.fori_loop(0, groups, readout_group_anchor, 0)

    @pl.when(jnp.logical_not(anchor_ok))
    def _():
        lax.fori_loop(0, nc, readout_step_split, 0)


def _hgrn(pr, lbp_l, g_hgrn_l, consts, state=None, layer=None):
    b, n, _ = pr.shape
    nc = n // CHUNK
    mstk, lmask, bd = consts
    w = R_WIDTH
    has_init = state is not None
    full = lambda *shape: pl.BlockSpec(shape, lambda i: (0,) * len(shape))
    in_specs = [
        pl.BlockSpec((None, n, PR_COLS), lambda i: (i, 0, 0)),
        full(3, 2, w),
        full(1, w),
        full(*mstk.shape),
        full(*lmask.shape),
        full(w, w),
    ]
    args = [pr, lbp_l, g_hgrn_l.reshape(1, w), mstk, lmask, bd]
    out_shape = [jax.ShapeDtypeStruct((b, n, w), BF16)]
    out_specs = [pl.BlockSpec((None, n, w), lambda i: (i, 0, 0))]
    if has_init:
        in_specs.append(pl.BlockSpec((None, None, 2, w, R_DIM), lambda i: (i, layer, 0, 0, 0)))
        args.append(state)
    else:
        out_shape.append(jax.ShapeDtypeStruct((b, 2, w, R_DIM), F32))
        out_specs.append(pl.BlockSpec((None, 2, w, R_DIM), lambda i: (i, 0, 0, 0)))
    return pl.pallas_call(
        functools.partial(_hgrn_kernel, nc=nc, has_init=has_init),
        out_shape=tuple(out_shape),
        grid=(b,),
        in_specs=in_specs,
        out_specs=tuple(out_specs),
        scratch_shapes=[
            pltpu.VMEM((n, w), F32),
            pltpu.VMEM((2, n, w), F32),
            pltpu.VMEM((2, n, 2 * w), BF16),
            pltpu.VMEM((2, n, w), F32),
            pltpu.VMEM((2, n, w), BF16),
            pltpu.VMEM((2, nc, R_DIM, w), F32),
            pltpu.VMEM((2, nc, 8, w), F32),
            pltpu.VMEM((2, nc, w, w), BF16),
        ],
        compiler_params=_cparams(("arbitrary",)),
        name="hgrn_scan",
    )(*args)


def _fnet_constants(n):
    j = np.arange(F_GROUP_DIM)
    ang = 2.0 * np.pi * ((j[:, None] * j[None, :]) % F_GROUP_DIM) / F_GROUP_DIM
    eye = np.eye(F_GROUPS)
    cs = np.concatenate([np.kron(eye, np.cos(ang)), np.kron(eye, np.sin(ang))], axis=1)
    t = np.arange(n)
    angn = 2.0 * np.pi * ((t[:, None] * t[None, :]) % n) / n
    return tuple(jnp.asarray(m, F32).astype(BF16) for m in (cs, np.cos(angn), np.sin(angn)))


def _fnet_kernel(pf_ref, cs_ref, cn_ref, sn_ref, wf_ref, o_ref, *, scale):
    w = F_WIDTH
    t = _dot(pf_ref[:, 0:w].astype(BF16), cs_ref[...])
    y = (_dot(cn_ref[...], t[:, 0:w].astype(BF16)) - _dot(sn_ref[...], t[:, w:2 * w].astype(BF16))) * scale
    of = _dot(y.astype(BF16), wf_ref[...])
    o_ref[...] = (of * _silu(pf_ref[:, w:2 * w])).astype(BF16)


def _fourier(pf, consts, w_fnet_bf16):
    b, n, _ = pf.shape
    cs, cn, sn = consts
    w = F_WIDTH
    full = lambda *shape: pl.BlockSpec(shape, lambda i: (0,) * len(shape))
    return pl.pallas_call(
        functools.partial(_fnet_kernel, scale=float((n * F_GROUP_DIM) ** -0.5)),
        out_shape=jax.ShapeDtypeStruct((b, n, w), BF16),
        grid=(b,),
        in_specs=[pl.BlockSpec((None, n, PF_COLS), lambda i: (i, 0, 0)),
                  full(w, 2 * w), full(n, n), full(n, n), full(w, w)],
        out_specs=pl.BlockSpec((None, n, w), lambda i: (i, 0, 0)),
        compiler_params=_cparams(("arbitrary",)),
        name="fourier_mixing",
    )(pf, cs, cn, sn, w_fnet_bf16)


def _outproj_kernel(ma_ref, mr_ref, mf_ref, w_ref, x_ref, mod_ref, g_ref, y_ref):
    out = (_dot(ma_ref[...], w_ref[0:A_WIDTH, :])
           + _dot(mr_ref[...], w_ref[A_WIDTH:A_WIDTH + R_WIDTH, :])
           + _dot(mf_ref[...], w_ref[A_WIDTH + R_WIDTH:D_MODEL, :]))
    ms = jnp.mean(out * out, axis=-1, keepdims=True)
    y_ref[...] = x_ref[...] + mod_ref[2:3, :] * (out * lax.rsqrt(ms + EPS) * g_ref[...])


def _out_projection(ma, mr, mf, w_out_bf16, x, mod, g_post):
    b, n, _ = x.shape
    tm = min(n, 512)
    tiles = n // tm
    tok = lambda i: (i // tiles, i % tiles, 0)
    nb_mod = mod.shape[0]
    mod_idx = (lambda i: (i // tiles, 0, 0)) if nb_mod > 1 else (lambda i: (0, 0, 0))
    return pl.pallas_call(
        _outproj_kernel,
        out_shape=jax.ShapeDtypeStruct((b, n, D_MODEL), F32),
        grid=(b * tiles,),
        in_specs=[
            pl.BlockSpec((None, tm, A_WIDTH), tok),
            pl.BlockSpec((None, tm, R_WIDTH), tok),
            pl.BlockSpec((None, tm, F_WIDTH), tok),
            pl.BlockSpec((D_MODEL, D_MODEL), lambda i: (0, 0)),
            pl.BlockSpec((None, tm, D_MODEL), tok),
            pl.BlockSpec((None, 3, D_MODEL), mod_idx),
            pl.BlockSpec((1, D_MODEL), lambda i: (0, 0)),
        ],
        out_specs=pl.BlockSpec((None, tm, D_MODEL), tok),
        compiler_params=_cparams(("arbitrary",)),
        name="out_projection",
    )(ma, mr, mf, w_out_bf16, x, mod, g_post.reshape(1, D_MODEL))


def kernel(x_prompt, x_sample, cache_attn_k, cache_attn_v, state_hgrn, c, c_ctx,
           w_ada, b_ada, g_pre, w_in, rpb, lb_logits, g_hgrn, w_fnet, w_out, g_post):
    nb_ctx, n_ctx, _ = x_prompt.shape
    nb_lat, n_lat, _ = x_sample.shape

    pad_rows = (-(1 + nb_lat)) % 8
    cc = jnp.concatenate([c_ctx[None, :], c, jnp.zeros((pad_rows, D_MODEL), F32)], axis=0)
    mods = _modulations(cc, w_ada, b_ada)
    lbp = _lower_bounds(lb_logits)

    w_in_b = w_in.astype(BF16)
    w_out_b = w_out.astype(BF16)
    w_fnet_b = w_fnet.astype(BF16)
    hconsts = _hgrn_constants()
    fconsts_ctx = _fnet_constants(n_ctx)
    fconsts_lat = _fnet_constants(n_lat)

    state_rows = state_hgrn.reshape(nb_lat, DEPTH, 2, R_WIDTH, R_DIM)
    cache_kt = jnp.transpose(cache_attn_k, (0, 1, 3, 4, 2))
    cache_vt = jnp.transpose(cache_attn_v, (0, 1, 3, 4, 2))

    yp, ys = x_prompt, x_sample
    new_kv = [jnp.zeros((nb_ctx, DEPTH, A_HEADS, HEAD_DIM, n_ctx), F32) for _ in range(2)]
    new_s = []
    for l in range(DEPTH):
        lbp_l = lbp[:, l]
        mod = mods[l, 0].reshape(1, 3, D_MODEL)
        pa, pr, pf, *new_kv = _in_projection(yp, mod, g_pre[l], w_in_b[l], cache=(l, new_kv))
        ma = _context_attention(pa)
        mr, sfin = _hgrn(pr, lbp_l, g_hgrn[l], hconsts)
        new_s.append(sfin)
        mf = _fourier(pf, fconsts_ctx, w_fnet_b[l])
        yp = _out_projection(ma, mr, mf, w_out_b[l], yp, mod, g_post[l])
        mod = mods[l, 1:1 + nb_lat].reshape(nb_lat, 3, D_MODEL)
        pa, pr, pf = _in_projection(ys, mod, g_pre[l], w_in_b[l])
        ma = _neighbourhood_attention(pa, cache_kt, cache_vt, l, rpb[l])
        (mr,) = _hgrn(pr, lbp_l, g_hgrn[l], hconsts, state=state_rows, layer=l)
        mf = _fourier(pf, fconsts_lat, w_fnet_b[l])
        ys = _out_projection(ma, mr, mf, w_out_b[l], ys, mod, g_post[l])

    new_state = jnp.stack(new_s, axis=1).reshape(nb_ctx, DEPTH, 2, R_HEADS, R_DIM, R_DIM)
    new_k, new_v = (jnp.transpose(t, (0, 1, 4, 2, 3)) for t in new_kv)
    return (yp, ys, new_k, new_v, new_state)
```

```python
import functools

import numpy as np
import jax
import jax.numpy as jnp
from jax import lax
from jax.experimental import pallas as pl
from jax.experimental.pallas import tpu as pltpu

F32 = jnp.float32
BF16 = jnp.bfloat16

D_MODEL = 1024
DEPTH = 4
GRID_W = 64
WIN_H = 8
WIN_W = 16
HEAD_DIM = 64
A_HEADS = 8
A_WIDTH = A_HEADS * HEAD_DIM
R_HEADS = 4
R_DIM = 64
R_WIDTH = R_HEADS * R_DIM
F_GROUPS = 4
F_GROUP_DIM = 64
F_WIDTH = F_GROUPS * F_GROUP_DIM
PA_COLS = 4 * A_WIDTH
PR_COLS = 5 * R_WIDTH
PF_COLS = 2 * F_WIDTH
IN_COLS = PA_COLS + PR_COLS + PF_COLS
CHUNK = 64
EPS = 1e-6
LANES = 128
NEG_INF = float("-inf")
VMEM_LIMIT = 56 * 1024 * 1024


def _cparams(sem):
    return pltpu.CompilerParams(dimension_semantics=sem, vmem_limit_bytes=VMEM_LIMIT)


def _silu(x):
    return x * (1.0 / (1.0 + jnp.exp(-x)))


def _dot(a, b):
    return jnp.dot(a, b, preferred_element_type=F32)


def _dot_nt(a, b):
    return lax.dot_general(a, b, (((1,), (1,)), ((), ())), preferred_element_type=F32)


def _dot_tn(a, b):
    return lax.dot_general(a, b, (((0,), (0,)), ((), ())), preferred_element_type=F32)


def _split2(x):
    hi = x.astype(BF16)
    lo = (x - hi.astype(F32)).astype(BF16)
    return hi, lo


def _mod_kernel(cc_ref, w_ref, b_ref, o_ref):
    a_hi, a_lo = _split2(_silu(cc_ref[...]))
    w_hi, w_lo = _split2(w_ref[...])
    acc = _dot(a_hi, w_hi) + _dot(a_hi, w_lo) + _dot(a_lo, w_hi)
    o_ref[...] = acc + b_ref[...]


def _modulations(cc, w_ada, b_ada):
    rows = cc.shape[0]
    tn = 1024
    return pl.pallas_call(
        _mod_kernel,
        out_shape=jax.ShapeDtypeStruct((DEPTH, rows, 3 * D_MODEL), F32),
        grid=(DEPTH, 3 * D_MODEL // tn),
        in_specs=[
            pl.BlockSpec((rows, D_MODEL), lambda l, j: (0, 0)),
            pl.BlockSpec((None, D_MODEL, tn), lambda l, j: (l, 0, j)),
            pl.BlockSpec((None, 1, tn), lambda l, j: (l, 0, j)),
        ],
        out_specs=pl.BlockSpec((None, rows, tn), lambda l, j: (l, 0, j)),
        compiler_params=_cparams(("arbitrary", "arbitrary")),
        name="adaln_mod",
    )(cc, w_ada, b_ada.reshape(DEPTH, 1, 3 * D_MODEL))


def _lb_kernel(x_ref, o_ref):
    xs = [x_ref[i] for i in range(DEPTH)]
    m = functools.reduce(jnp.maximum, xs)
    es = [jnp.exp(x - m) for x in xs]
    tot = functools.reduce(lambda a, b: a + b, es)
    cum = None
    first = None
    for i in range(DEPTH):
        p = es[i] / tot
        cum = p if cum is None else cum + p
        if first is None:
            first = cum
        lb = jnp.maximum(cum - first, 0.0)
        o_ref[0, i] = lb
        o_ref[1, i] = jnp.log1p(-lb)
        o_ref[2, i] = 1.0 - lb


def _lower_bounds(lb_logits):
    x = jnp.transpose(lb_logits, (1, 0, 2))
    return pl.pallas_call(
        _lb_kernel,
        out_shape=jax.ShapeDtypeStruct((3, DEPTH, 2, R_WIDTH), F32),
        name="hgrn_lower_bounds",
    )(x)


def _inproj_kernel(*refs, n_prev, with_cache):
    x_ref, mod_ref, g_ref, w_ref = refs[:4]
    outs = refs[4 + n_prev:]
    pa_ref, pr_ref, pf_ref = outs[:3]
    x = x_ref[...]
    ms = jnp.mean(x * x, axis=-1, keepdims=True)
    y = x * lax.rsqrt(ms + EPS) * g_ref[...]
    h = (y * (1.0 + mod_ref[1:2, :]) + mod_ref[0:1, :]).astype(BF16)
    pa = _dot(h, w_ref[:, 0:PA_COLS])
    pa_ref[...] = pa
    pr_ref[...] = _dot(h, w_ref[:, PA_COLS:PA_COLS + PR_COLS])
    pf_ref[...] = _dot(h, w_ref[:, PA_COLS + PR_COLS:IN_COLS])
    if with_cache:
        ko_ref, vo_ref = outs[3:5]
        tm = pa.shape[0]
        ko_ref[...] = pa[:, A_WIDTH:2 * A_WIDTH].T.reshape(A_HEADS, HEAD_DIM, tm)
        vo_ref[...] = pa[:, 2 * A_WIDTH:3 * A_WIDTH].T.reshape(A_HEADS, HEAD_DIM, tm)


def _in_projection(x, mod, g_pre, w_in_bf16, cache=None):
    b, n, _ = x.shape
    tm = min(n, 512)
    tiles = n // tm
    tok = lambda i: (i // tiles, i % tiles, 0)
    nb_mod = mod.shape[0]
    mod_idx = (lambda i: (i // tiles, 0, 0)) if nb_mod > 1 else (lambda i: (0, 0, 0))
    in_specs = [
        pl.BlockSpec((None, tm, D_MODEL), tok),
        pl.BlockSpec((None, 3, D_MODEL), mod_idx),
        pl.BlockSpec((1, D_MODEL), lambda i: (0, 0)),
        pl.BlockSpec((D_MODEL, IN_COLS), lambda i: (0, 0)),
    ]
    args = [x, mod, g_pre.reshape(1, D_MODEL), w_in_bf16]
    out_shape = [
        jax.ShapeDtypeStruct((b, n, PA_COLS), F32),
        jax.ShapeDtypeStruct((b, n, PR_COLS), F32),
        jax.ShapeDtypeStruct((b, n, PF_COLS), F32),
    ]
    out_specs = [
        pl.BlockSpec((None, tm, PA_COLS), tok),
        pl.BlockSpec((None, tm, PR_COLS), tok),
        pl.BlockSpec((None, tm, PF_COLS), tok),
    ]
    aliases = {}
    n_prev = 0
    if cache is not None:
        layer, prev = cache
        cache_shape = jax.ShapeDtypeStruct((b, DEPTH, A_HEADS, HEAD_DIM, n), F32)
        cache_spec = pl.BlockSpec((None, None, A_HEADS, HEAD_DIM, tm),
                                  lambda i: (i // tiles, layer, 0, 0, i % tiles))
        out_shape += [cache_shape, cache_shape]
        out_specs += [cache_spec, cache_spec]
        if prev is not None:
            n_prev = 2
            in_specs += [pl.BlockSpec(memory_space=pl.ANY)] * 2
            args += list(prev)
            aliases = {4: 3, 5: 4}
    return pl.pallas_call(
        functools.partial(_inproj_kernel, n_prev=n_prev, with_cache=cache is not None),
        out_shape=tuple(out_shape),
        grid=(b * tiles,),
        in_specs=in_specs,
        out_specs=tuple(out_specs),
        input_output_aliases=aliases,
        compiler_params=_cparams(("arbitrary",)),
        name="in_projection",
    )(*args)


def _lane_lo():
    return lax.broadcasted_iota(jnp.int32, (1, LANES), 1) < HEAD_DIM


def _ctx_attn_kernel(pa_ref, o_ref, s_scr):
    lo = _lane_lo()
    npair = A_HEADS // 2

    def scores(p):
        c = p * LANES
        q2 = pa_ref[:, c:c + LANES] * (HEAD_DIM ** -0.5)
        k2 = pa_ref[:, A_WIDTH + c:A_WIDTH + c + LANES].astype(BF16)
        for hh in range(2):
            sel = lo if hh == 0 else jnp.logical_not(lo)
            s_scr[p % 2, hh] = _dot_nt(jnp.where(sel, q2, 0.0).astype(BF16), k2)

    def finish(p):
        c = p * LANES
        v2 = pa_ref[:, 2 * A_WIDTH + c:2 * A_WIDTH + c + LANES].astype(BF16)
        outs = []
        for hh in range(2):
            s = s_scr[p % 2, hh]
            e = jnp.exp(s - jnp.max(s, axis=-1, keepdims=True))
            inv = 1.0 / jnp.sum(e, axis=-1, keepdims=True)
            outs.append(_dot(e.astype(BF16), v2) * inv)
        o2 = jnp.where(lo, outs[0], outs[1])
        o_ref[:, c:c + LANES] = (o2 * _silu(pa_ref[:, 3 * A_WIDTH + c:3 * A_WIDTH + c + LANES])).astype(BF16)

    scores(0)
    for p in range(npair):
        if p + 1 < npair:
            scores(p + 1)
        finish(p)


def _context_attention(pa):
    b, n, _ = pa.shape
    return pl.pallas_call(
        _ctx_attn_kernel,
        out_shape=jax.ShapeDtypeStruct((b, n, A_WIDTH), BF16),
        grid=(b,),
        in_specs=[pl.BlockSpec((None, n, PA_COLS), lambda i: (i, 0, 0))],
        out_specs=pl.BlockSpec((None, n, A_WIDTH), lambda i: (i, 0, 0)),
        scratch_shapes=[pltpu.VMEM((2, 2, n, n), F32)],
        compiler_params=_cparams(("arbitrary",)),
        name="context_attention",
    )(pa)


QROWS = 4


def _nbr_blocks(rows):
    kh = min(WIN_H, rows)
    out = []
    for r_first in range(0, rows, QROWS):
        r0s = [min(max(r - kh // 2, 0), rows - kh) for r in range(r_first, r_first + QROWS)]
        lo, hi = min(r0s), max(r0s) + kh
        lo -= lo % 2
        span = hi - lo
        span += (-span) % 4
        if lo + span > rows:
            lo = rows - span
        assert lo >= 0 and lo % 2 == 0
        out.append((r_first, lo, span, r0s))
    return out, kh


N_REL_ROWS = 2 * WIN_H - 1
TABLE_W = 2 * WIN_H * GRID_W


def _build_bias_tables(base_ref, te_scr, to_scr, heads):
    lane = lax.broadcasted_iota(jnp.int32, (GRID_W, LANES), 1)
    qc = lax.broadcasted_iota(jnp.int32, (GRID_W, LANES), 0)
    kc = lane % GRID_W
    lo = lane < GRID_W
    c0 = jnp.clip(qc - WIN_W // 2, 0, GRID_W - WIN_W)
    col_in = (kc >= c0) & (kc < c0 + WIN_W)
    neg = jnp.full((GRID_W, LANES), NEG_INF, F32)

    for h in heads:
        def tile(i, lane_off):
            row = jnp.broadcast_to(base_ref[h, i:i + 1, :], (GRID_W, LANES))
            return pltpu.roll(row, lane_off, 1, stride=1, stride_axis=0)

        for j in range(WIN_H):
            for scr, ia in ((te_scr, 2 * j), (to_scr, 2 * j + 1)):
                ib = ia + 1
                if ia >= N_REL_ROWS:
                    pair = neg
                elif ib >= N_REL_ROWS:
                    pair = jnp.where(col_in & lo, tile(ia, 0), NEG_INF)
                else:
                    pair = jnp.where(col_in, jnp.where(lo, tile(ia, 0), tile(ib, GRID_W)), NEG_INF)
                scr[h, :, j * LANES:(j + 1) * LANES] = pair


def _nbr_attn_kernel(q_ref, k_ref, v_ref, g_ref, kc_ref, vc_ref, base_ref, o_ref, s_scr, p_scr, te_scr, to_scr, *, rows):
    p = pl.program_id(1)

    @pl.when(pl.program_id(0) == 0)
    def _():
        _build_bias_tables(base_ref, te_scr, to_scr, (2 * p, 2 * p + 1))

    lo = _lane_lo()
    hi = jnp.logical_not(lo)
    blocks, kh = _nbr_blocks(rows)
    nq = QROWS * GRID_W
    past = kc_ref.shape[-1]
    ctx0 = s_scr.shape[-1] - past
    kct = kc_ref[...].reshape(LANES, past).astype(BF16)
    vct = vc_ref[...].reshape(LANES, past).astype(BF16)
    ctx_tiles = [slice(ctx0 + t * LANES, ctx0 + (t + 1) * LANES) for t in range(past // LANES)]

    def window_tiles(bi, rl, head):
        r_first, u0, span, r0s = blocks[bi]
        r, r0 = r_first + rl, r0s[rl]
        out = []
        for t in range(span // 2):
            kr0 = u0 + 2 * t
            cs = slice(t * LANES, (t + 1) * LANES)
            ok0 = r0 <= kr0 < r0 + kh
            ok1 = r0 <= kr0 + 1 < r0 + kh
            if not (ok0 or ok1):
                out.append((cs, None))
                continue
            i0 = kr0 - r + WIN_H - 1
            if i0 % 2 == 0:
                assert 0 <= i0 and i0 + 2 <= 2 * WIN_H
                bias = te_scr[head, :, i0 * GRID_W:(i0 + 2) * GRID_W]
            else:
                assert 1 <= i0 and i0 + 1 <= 2 * WIN_H - 1
                bias = to_scr[head, :, (i0 - 1) * GRID_W:(i0 + 1) * GRID_W]
            if not ok0:
                bias = jnp.where(lo, NEG_INF, bias)
            if not ok1:
                bias = jnp.where(hi, NEG_INF, bias)
            out.append((cs, bias))
        return out

    def scores(bi):
        r_first, u0, span, _ = blocks[bi]
        qs, ks, nk = r_first * GRID_W, u0 * GRID_W, span * GRID_W
        q2 = q_ref[qs:qs + nq, :] * (HEAD_DIM ** -0.5)
        ku = k_ref[ks:ks + nk, :].astype(BF16)
        for hh in range(2):
            qh = jnp.where(lo if hh == 0 else hi, q2, 0.0).astype(BF16)
            s_scr[bi % 2, hh, :, 0:nk] = _dot_nt(qh, ku)
            s_scr[bi % 2, hh, :, ctx0:ctx0 + past] = _dot(qh, kct)

    def softmax(bi):
        par = bi % 2
        inv = []
        for hh in range(2):
            parts = []
            for rl in range(QROWS):
                rs = slice(rl * GRID_W, (rl + 1) * GRID_W)
                tiles = window_tiles(bi, rl, 2 * p + hh)
                m_acc = None
                for cs, bias in tiles:
                    if bias is None:
                        continue
                    s = s_scr[par, hh, rs, cs] + bias
                    s_scr[par, hh, rs, cs] = s
                    m_acc = s if m_acc is None else jnp.maximum(m_acc, s)
                for cs in ctx_tiles:
                    m_acc = jnp.maximum(m_acc, s_scr[par, hh, rs, cs])
                m = jnp.max(m_acc, axis=-1, keepdims=True)
                l_acc = None
                for cs, bias in tiles:
                    if bias is None:
                        p_scr[par, hh, rs, cs] = jnp.zeros((GRID_W, LANES), BF16)
                        continue
                    e = jnp.exp(s_scr[par, hh, rs, cs] - m)
                    p_scr[par, hh, rs, cs] = e.astype(BF16)
                    l_acc = e if l_acc is None else l_acc + e
                for cs in ctx_tiles:
                    e = jnp.exp(s_scr[par, hh, rs, cs] - m)
                    p_scr[par, hh, rs, cs] = e.astype(BF16)
                    l_acc = l_acc + e
                parts.append(1.0 / jnp.sum(l_acc, axis=-1, keepdims=True))
            inv.append(jnp.concatenate(parts, axis=0))
        return inv

    def values(bi, inv):
        r_first, u0, span, _ = blocks[bi]
        qs, ks, nk = r_first * GRID_W, u0 * GRID_W, span * GRID_W
        vu = v_ref[ks:ks + nk, :].astype(BF16)
        outs = []
        for hh in range(2):
            o = _dot(p_scr[bi % 2, hh, :, 0:nk], vu) + _dot_nt(p_scr[bi % 2, hh, :, ctx0:ctx0 + past], vct)
            outs.append(o * inv[hh])
        o2 = jnp.where(lo, outs[0], outs[1])
        o_ref[qs:qs + nq, :] = (o2 * _silu(g_ref[qs:qs + nq, :])).astype(BF16)

    scores(0)
    for bi in range(len(blocks)):
        if bi + 1 < len(blocks):
            scores(bi + 1)
        values(bi, softmax(bi))


def _bias_base(rpb_l):
    mid = WIN_W - 1
    zeros = jnp.zeros(rpb_l.shape[:2] + (LANES - (2 * WIN_W - 1),), F32)
    base = jnp.concatenate([rpb_l[..., mid:], zeros, rpb_l[..., :mid]], axis=-1)
    return jnp.pad(base, ((0, 0), (0, 2 * WIN_H - N_REL_ROWS), (0, 0)))


def _neighbourhood_attention(pa, cache_kt, cache_vt, layer, rpb_l):
    b, n, _ = pa.shape
    rows = n // GRID_W
    past = cache_kt.shape[-1]
    blocks, _ = _nbr_blocks(rows)
    max_nk = max(s for (_, _, s, _) in blocks) * GRID_W
    npair = A_HEADS // 2
    col = lambda off: (lambda i, p: (i, 0, off + p))
    cache_spec = pl.BlockSpec((None, None, 2, HEAD_DIM, past), lambda i, p: (i, layer, p, 0, 0))
    return pl.pallas_call(
        functools.partial(_nbr_attn_kernel, rows=rows),
        out_shape=jax.ShapeDtypeStruct((b, n, A_WIDTH), BF16),
        grid=(b, npair),
        in_specs=[
            pl.BlockSpec((None, n, LANES), col(0)),
            pl.BlockSpec((None, n, LANES), col(npair)),
            pl.BlockSpec((None, n, LANES), col(2 * npair)),
            pl.BlockSpec((None, n, LANES), col(3 * npair)),
            cache_spec,
            cache_spec,
            pl.BlockSpec((A_HEADS, 2 * WIN_H, LANES), lambda i, p: (0, 0, 0)),
        ],
        out_specs=pl.BlockSpec((None, n, LANES), col(0)),
        scratch_shapes=[
            pltpu.VMEM((2, 2, QROWS * GRID_W, max_nk + past), F32),
            pltpu.VMEM((2, 2, QROWS * GRID_W, max_nk + past), BF16),
            pltpu.VMEM((A_HEADS, GRID_W, TABLE_W), F32),
            pltpu.VMEM((A_HEADS, GRID_W, TABLE_W), F32),
        ],
        compiler_params=_cparams(("arbitrary", "arbitrary")),
        name="neighbourhood_attention",
    )(pa, pa, pa, pa, cache_kt, cache_vt, _bias_base(rpb_l))


COARSE_HALVES = (32, 16, 8)
FINE_HALVES = (4, 2, 1)
N_LEVELS = len(COARSE_HALVES) + len(FINE_HALVES)
ANCHOR_BLOCK = CHUNK
ANCHOR_LEVELS = 0
ANCHOR_MAX_EXPONENT = 80.0
MASK_DIAG = N_LEVELS
MASK_ANCHOR = N_LEVELS + 1
GROUP = 4


def _hgrn_constants():
    c = CHUNK
    idx = np.arange(c)
    mats = [np.tril(np.ones((c, c)))]
    masks = []
    for h in COARSE_HALVES + FINE_HALVES:
        blk = idx // (2 * h)
        mid = blk * 2 * h + h - 1
        upper = idx > mid
        if h in FINE_HALVES:
            m = np.zeros((c, c))
            for i in range(c):
                if upper[i]:
                    m[i, mid[i] + 1:i + 1] = 1.0
                else:
                    m[i, i + 1:mid[i] + 1] = 1.0
            mats.append(m)
        same = blk[:, None] == blk[None, :]
        masks.append((same & upper[:, None] & (~upper)[None, :]).astype(np.float64))
    masks.append(np.eye(c))
    same_block = (idx[:, None] // ANCHOR_BLOCK) == (idx[None, :] // ANCHOR_BLOCK)
    masks.append((same_block & (idx[None, :] <= idx[:, None])).astype(np.float64))
    fwd = np.concatenate(mats, axis=0)
    bwd = np.concatenate([m[::-1, ::-1] for m in mats], axis=0)
    mk_f = np.stack([np.tile(m, (1, R_HEADS)) for m in masks])
    mk_b = np.stack([np.tile(m[::-1, ::-1], (1, R_HEADS)) for m in masks])
    hid = np.arange(R_WIDTH) // R_DIM
    bd = (hid[:, None] == hid[None, :]).astype(np.float64)
    return (jnp.asarray(np.stack([fwd, bwd]), BF16), jnp.asarray(np.stack([mk_f, mk_b]), F32),
            jnp.asarray(bd, BF16))


def _hgrn_kernel(*refs, nc, has_init):
    pr_ref, lbp_ref, gh_ref, mstk_ref, lmask_ref, bd_ref = refs[:6]
    if has_init:
        s0_ref, o_ref = refs[6:8]
        sfin_ref = None
    else:
        o_ref, sfin_ref = refs[6:8]
        s0_ref = None
    q_scr, k_scr, g_scr, gc_scr, qg_scr, upd_scr, dec_scr, sbd_scr = refs[8:]
    w = R_WIDTH
    c = CHUNK
    bd = bd_ref[...]
    hid = lax.broadcasted_iota(jnp.int32, (1, w), 1) // R_DIM

    q_scr[...] = _silu(pr_ref[:, 0:w])
    for d in range(2):
        z = pr_ref[:, (1 + d) * w:(2 + d) * w]
        e = jnp.exp(-jnp.abs(z))
        r = 1.0 / (1.0 + e)
        pos = z >= 0.0
        one_m_lb = lbp_ref[2, d:d + 1, :]
        f = lbp_ref[0, d:d + 1, :] + one_m_lb * (jnp.where(pos, 1.0, e) * r)
        log_f = jnp.where(f > 0.0, jnp.log(f), lbp_ref[1, d:d + 1, :] + z)
        g_hi, g_lo = _split2(log_f)
        g_scr[d, :, 0:w] = g_hi
        g_scr[d, :, w:2 * w] = g_lo
        k_scr[d] = one_m_lb * (jnp.where(pos, e, 1.0) * r)

    def tile4(x):
        return jnp.concatenate([x] * R_HEADS, axis=0)

    def head_diagonal(full):
        out = full[(R_HEADS - 1) * R_DIM:R_HEADS * R_DIM]
        for h in range(R_HEADS - 2, -1, -1):
            out = jnp.where(hid == h, full[h * R_DIM:(h + 1) * R_DIM], out)
        return out

    def chunk_rows(ci):
        return pl.ds(pl.multiple_of(ci * c, c), c)

    def cumulative(d, rs, n_mats):
        s = _dot(mstk_ref[d, 0:n_mats * c, :], g_scr[d, rs, :])
        return s[:, 0:w] + s[:, w:2 * w]

    def level_exponents(gcum, d, halves):
        out = []
        for h in halves:
            parts = []
            for s0 in range(0, c, 2 * h):
                anchor = s0 + h - 1 + d
                parts.append(-jnp.abs(gcum[s0:s0 + 2 * h] - gcum[anchor:anchor + 1]))
            out.append(parts[0] if len(parts) == 1 else jnp.concatenate(parts, axis=0))
        return out

    def anchor_shift(gcum, d):
        parts = []
        for s0 in range(0, c, ANCHOR_BLOCK):
            anchor = s0 + ANCHOR_BLOCK // 2 - 1 + d
            parts.append(gcum[s0:s0 + ANCHOR_BLOCK] - gcum[anchor:anchor + 1])
        return jnp.concatenate(parts, axis=0)

    groups = nc // GROUP
    pairs = [(j, d) for j in range(GROUP) for d in range(2)]

    def increment_group(gi, worst):
        rows = [chunk_rows(gi * GROUP + j) for j in range(GROUP)]
        gcums = [cumulative(d, rows[j], 1) for j, d in pairs]
        lasts = [g[c - 1:c, :] if d == 0 else g[0:1, :] for g, (j, d) in zip(gcums, pairs)]
        kls = [(k_scr[d, rows[j], :] * jnp.exp(last - g)).astype(BF16)
               for g, last, (j, d) in zip(gcums, lasts, pairs)]
        vs = [pr_ref[rows[j], 3 * w:4 * w].astype(BF16) for j in range(GROUP)]
        upds = [_dot_tn(vs[j], kl) for kl, (j, d) in zip(kls, pairs)]
        for g, last, upd, (j, d) in zip(gcums, lasts, upds, pairs):
            ci = gi * GROUP + j
            gc_scr[d, rows[j], :] = g
            upd_scr[d, ci] = head_diagonal(upd)
            dec_scr[d, ci] = jnp.broadcast_to(jnp.exp(last), (8, w))
            qg_scr[d, rows[j], :] = (q_scr[rows[j], :] * jnp.exp(g)).astype(BF16)
            sh = jnp.abs(anchor_shift(g, d))
            for s0 in range(0, c, 8):
                worst = jnp.maximum(worst, sh[s0:s0 + 8])
        return worst

    worst = lax.fori_loop(0, groups, increment_group, jnp.zeros((8, w), F32))
    q_max = jnp.max(jnp.max(jnp.abs(q_scr[...]), axis=0, keepdims=True), axis=1, keepdims=True)
    anchor_ok = jnp.max(worst + jnp.log(jnp.maximum(q_max, 1.0))) < ANCHOR_MAX_EXPONENT

    def scan_step(ci, carry):
        sf, sb = carry
        cb = nc - 1 - ci
        sbd_scr[0, ci] = tile4(sf.astype(BF16)) * bd
        sbd_scr[1, cb] = tile4(sb.astype(BF16)) * bd
        sf = sf * dec_scr[0, ci][0:1, :] + upd_scr[0, ci]
        sb = sb * dec_scr[1, cb][0:1, :] + upd_scr[1, cb]
        return sf, sb

    if has_init:
        init = tuple(jnp.concatenate([s0_ref[d], jnp.zeros((w, LANES - R_DIM), F32)], axis=1).T[0:R_DIM]
                     for d in range(2))
    else:
        init = (jnp.zeros((R_DIM, w), F32), jnp.zeros((R_DIM, w), F32))
    finals = lax.fori_loop(0, nc, scan_step, init)
    if sfin_ref is not None:
        for d in range(2):
            padded = jnp.concatenate([finals[d], jnp.zeros((LANES - R_DIM, w), F32)], axis=0)
            sfin_ref[d] = padded.T[:, 0:R_DIM]

    def finish(rs, o):
        x2_hi, x2_lo = _split2(o * o)
        ms = (_dot(x2_hi, bd) + _dot(x2_lo, bd)) * (1.0 / R_DIM)
        y = o * lax.rsqrt(ms + EPS) * gh_ref[...]
        o_ref[rs, :] = (y * _silu(pr_ref[rs, 4 * w:5 * w])).astype(BF16)

    def readout_group_anchor(gi, carry):
        rows = [chunk_rows(gi * GROUP + j) for j in range(GROUP)]
        qs = [q_scr[rows[j], :] for j in range(GROUP)]
        gcums = [gc_scr[d, rows[j], :] for j, d in pairs]
        factors = []
        for g, (j, d) in zip(gcums, pairs):
            es = [jnp.exp(x) for x in level_exponents(g, d, COARSE_HALVES[:ANCHOR_LEVELS])]
            sh = anchor_shift(g, d)
            fs = [(e, e, lv) for lv, e in enumerate(es)]
            fs.append((jnp.exp(sh), jnp.exp(-sh), MASK_ANCHOR))
            factors.append(fs)
        kbds = [tile4(k_scr[d, rows[j], :].astype(BF16)) * bd for j, d in pairs]
        accs = [None] * len(pairs)
        for lv in range(ANCHOR_LEVELS + 1):
            for i, (j, d) in enumerate(pairs):
                eq, ek, mask = factors[i][lv]
                part = jnp.where(lmask_ref[d, mask] > 0.5,
                                 _dot_nt((qs[j] * eq).astype(BF16), kbds[i] * tile4(ek.astype(BF16))), 0.0)
                accs[i] = part if accs[i] is None else accs[i] + part
        vbds = [tile4(pr_ref[rows[j], 3 * w:4 * w].astype(BF16)) * bd for j in range(GROUP)]
        outs = [None] * GROUP
        for i, (j, d) in enumerate(pairs):
            ci = gi * GROUP + j
            od = _dot(accs[i].astype(BF16), vbds[j]) + _dot_nt(qg_scr[d, rows[j], :], sbd_scr[d, ci])
            outs[j] = od if outs[j] is None else outs[j] + od
        for j in range(GROUP):
            finish(rows[j], outs[j])
        return carry

    def readout_step_split(ci, carry):
        rs = chunk_rows(ci)
        q = q_scr[rs, :]
        qb = q.astype(BF16)
        vbd = tile4(pr_ref[rs, 3 * w:4 * w].astype(BF16)) * bd
        dsts = [cumulative(d, rs, 1 + len(FINE_HALVES)) for d in range(2)]
        exps = [level_exponents(dsts[d][0:c], d, COARSE_HALVES)
                + [dsts[d][(1 + j) * c:(2 + j) * c] for j in range(len(FINE_HALVES))] for d in range(2)]
        kbd = [tile4(k_scr[d, rs, :].astype(BF16)) * bd for d in range(2)]
        a = [_dot_nt(qb, kbd[d]) * lmask_ref[d, MASK_DIAG] for d in range(2)]
        for j in range(N_LEVELS):
            for d in range(2):
                e = jnp.exp(exps[d][j])
                a[d] = a[d] + _dot_nt((q * e).astype(BF16), kbd[d] * tile4(e.astype(BF16))) * lmask_ref[d, j]
        o = None
        for d in range(2):
            od = _dot(a[d].astype(BF16), vbd) + _dot_nt(qg_scr[d, rs, :], sbd_scr[d, ci])
            o = od if o is None else o + od
        finish(rs, o)
        return carry

    @pl.when(anchor_ok)
    def _():
        lax.fori_loop(0, groups, readout_group_anchor, 0)

    @pl.when(jnp.logical_not(anchor_ok))
    def _():
        lax.fori_loop(0, nc, readout_step_split, 0)


def _hgrn(pr, lbp_l, g_hgrn_l, consts, state=None, layer=None):
    b, n, _ = pr.shape
    nc = n // CHUNK
    mstk, lmask, bd = consts
    w = R_WIDTH
    has_init = state is not None
    full = lambda *shape: pl.BlockSpec(shape, lambda i: (0,) * len(shape))
    in_specs = [
        pl.BlockSpec((None, n, PR_COLS), lambda i: (i, 0, 0)),
        full(3, 2, w),
        full(1, w),
        full(*mstk.shape),
        full(*lmask.shape),
        full(w, w),
    ]
    args = [pr, lbp_l, g_hgrn_l.reshape(1, w), mstk, lmask, bd]
    out_shape = [jax.ShapeDtypeStruct((b, n, w), BF16)]
    out_specs = [pl.BlockSpec((None, n, w), lambda i: (i, 0, 0))]
    if has_init:
        in_specs.append(pl.BlockSpec((None, None, 2, w, R_DIM), lambda i: (i, layer, 0, 0, 0)))
        args.append(state)
    else:
        out_shape.append(jax.ShapeDtypeStruct((b, 2, w, R_DIM), F32))
        out_specs.append(pl.BlockSpec((None, 2, w, R_DIM), lambda i: (i, 0, 0, 0)))
    return pl.pallas_call(
        functools.partial(_hgrn_kernel, nc=nc, has_init=has_init),
        out_shape=tuple(out_shape),
        grid=(b,),
        in_specs=in_specs,
        out_specs=tuple(out_specs),
        scratch_shapes=[
            pltpu.VMEM((n, w), F32),
            pltpu.VMEM((2, n, w), F32),
            pltpu.VMEM((2, n, 2 * w), BF16),
            pltpu.VMEM((2, n, w), F32),
            pltpu.VMEM((2, n, w), BF16),
            pltpu.VMEM((2, nc, R_DIM, w), F32),
            pltpu.VMEM((2, nc, 8, w), F32),
            pltpu.VMEM((2, nc, w, w), BF16),
        ],
        compiler_params=_cparams(("arbitrary",)),
        name="hgrn_scan",
    )(*args)


def _fnet_constants(n):
    j = np.arange(F_GROUP_DIM)
    ang = 2.0 * np.pi * ((j[:, None] * j[None, :]) % F_GROUP_DIM) / F_GROUP_DIM
    eye = np.eye(F_GROUPS)
    cs = np.concatenate([np.kron(eye, np.cos(ang)), np.kron(eye, np.sin(ang))], axis=1)
    t = np.arange(n)
    angn = 2.0 * np.pi * ((t[:, None] * t[None, :]) % n) / n
    return tuple(jnp.asarray(m, F32).astype(BF16) for m in (cs, np.cos(angn), np.sin(angn)))


def _fnet_kernel(pf_ref, cs_ref, cn_ref, sn_ref, wf_ref, o_ref, *, scale):
    w = F_WIDTH
    t = _dot(pf_ref[:, 0:w].astype(BF16), cs_ref[...])
    y = (_dot(cn_ref[...], t[:, 0:w].astype(BF16)) - _dot(sn_ref[...], t[:, w:2 * w].astype(BF16))) * scale
    of = _dot(y.astype(BF16), wf_ref[...])
    o_ref[...] = (of * _silu(pf_ref[:, w:2 * w])).astype(BF16)


def _fourier(pf, consts, w_fnet_bf16):
    b, n, _ = pf.shape
    cs, cn, sn = consts
    w = F_WIDTH
    full = lambda *shape: pl.BlockSpec(shape, lambda i: (0,) * len(shape))
    return pl.pallas_call(
        functools.partial(_fnet_kernel, scale=float((n * F_GROUP_DIM) ** -0.5)),
        out_shape=jax.ShapeDtypeStruct((b, n, w), BF16),
        grid=(b,),
        in_specs=[pl.BlockSpec((None, n, PF_COLS), lambda i: (i, 0, 0)),
                  full(w, 2 * w), full(n, n), full(n, n), full(w, w)],
        out_specs=pl.BlockSpec((None, n, w), lambda i: (i, 0, 0)),
        compiler_params=_cparams(("arbitrary",)),
        name="fourier_mixing",
    )(pf, cs, cn, sn, w_fnet_bf16)


def _outproj_kernel(ma_ref, mr_ref, mf_ref, w_ref, x_ref, mod_ref, g_ref, y_ref):
    out = (_dot(ma_ref[...], w_ref[0:A_WIDTH, :])
           + _dot(mr_ref[...], w_ref[A_WIDTH:A_WIDTH + R_WIDTH, :])
           + _dot(mf_ref[...], w_ref[A_WIDTH + R_WIDTH:D_MODEL, :]))
    ms = jnp.mean(out * out, axis=-1, keepdims=True)
    y_ref[...] = x_ref[...] + mod_ref[2:3, :] * (out * lax.rsqrt(ms + EPS) * g_ref[...])


def _out_projection(ma, mr, mf, w_out_bf16, x, mod, g_post):
    b, n, _ = x.shape
    tm = min(n, 512)
    tiles = n // tm
    tok = lambda i: (i // tiles, i % tiles, 0)
    nb_mod = mod.shape[0]
    mod_idx = (lambda i: (i // tiles, 0, 0)) if nb_mod > 1 else (lambda i: (0, 0, 0))
    return pl.pallas_call(
        _outproj_kernel,
        out_shape=jax.ShapeDtypeStruct((b, n, D_MODEL), F32),
        grid=(b * tiles,),
        in_specs=[
            pl.BlockSpec((None, tm, A_WIDTH), tok),
            pl.BlockSpec((None, tm, R_WIDTH), tok),
            pl.BlockSpec((None, tm, F_WIDTH), tok),
            pl.BlockSpec((D_MODEL, D_MODEL), lambda i: (0, 0)),
            pl.BlockSpec((None, tm, D_MODEL), tok),
            pl.BlockSpec((None, 3, D_MODEL), mod_idx),
            pl.BlockSpec((1, D_MODEL), lambda i: (0, 0)),
        ],
        out_specs=pl.BlockSpec((None, tm, D_MODEL), tok),
        compiler_params=_cparams(("arbitrary",)),
        name="out_projection",
    )(ma, mr, mf, w_out_bf16, x, mod, g_post.reshape(1, D_MODEL))


def kernel(x_prompt, x_sample, cache_attn_k, cache_attn_v, state_hgrn, c, c_ctx,
           w_ada, b_ada, g_pre, w_in, rpb, lb_logits, g_hgrn, w_fnet, w_out, g_post):
    nb_ctx, n_ctx, _ = x_prompt.shape
    nb_lat, n_lat, _ = x_sample.shape

    pad_rows = (-(1 + nb_lat)) % 8
    cc = jnp.concatenate([c_ctx[None, :], c, jnp.zeros((pad_rows, D_MODEL), F32)], axis=0)
    mods = _modulations(cc, w_ada, b_ada)
    lbp = _lower_bounds(lb_logits)

    w_in_b = w_in.astype(BF16)
    w_out_b = w_out.astype(BF16)
    w_fnet_b = w_fnet.astype(BF16)
    hconsts = _hgrn_constants()
    fconsts_ctx = _fnet_constants(n_ctx)
    fconsts_lat = _fnet_constants(n_lat)

    state_rows = state_hgrn.reshape(nb_lat, DEPTH, 2, R_WIDTH, R_DIM)
    cache_kt = jnp.transpose(cache_attn_k, (0, 1, 3, 4, 2))
    cache_vt = jnp.transpose(cache_attn_v, (0, 1, 3, 4, 2))

    yp, ys = x_prompt, x_sample
    new_kv = [jnp.zeros((nb_ctx, DEPTH, A_HEADS, HEAD_DIM, n_ctx), F32) for _ in range(2)]
    new_s = []
    for l in range(DEPTH):
        lbp_l = lbp[:, l]
        mod = mods[l, 0].reshape(1, 3, D_MODEL)
        pa, pr, pf, *new_kv = _in_projection(yp, mod, g_pre[l], w_in_b[l], cache=(l, new_kv))
        ma = _context_attention(pa)
        mr, sfin = _hgrn(pr, lbp_l, g_hgrn[l], hconsts)
        new_s.append(sfin)
        mf = _fourier(pf, fconsts_ctx, w_fnet_b[l])
        yp = _out_projection(ma, mr, mf, w_out_b[l], yp, mod, g_post[l])
        mod = mods[l, 1:1 + nb_lat].reshape(nb_lat, 3, D_MODEL)
        pa, pr, pf = _in_projection(ys, mod, g_pre[l], w_in_b[l])
        ma = _neighbourhood_attention(pa, cache_kt, cache_vt, l, rpb[l])
        (mr,) = _hgrn(pr, lbp_l, g_hgrn[l], hconsts, state=state_rows, layer=l)
        mf = _fourier(pf, fconsts_lat, w_fnet_b[l])
        ys = _out_projection(ma, mr, mf, w_out_b[l], ys, mod, g_post[l])

    new_state = jnp.stack(new_s, axis=1).reshape(nb_ctx, DEPTH, 2, R_HEADS, R_DIM, R_DIM)
    new_k, new_v = (jnp.transpose(t, (0, 1, 4, 2, 3)) for t in new_kv)
    return (yp, ys, new_k, new_v, new_state)
```

```python
import functools

import numpy as np
import jax
import jax.numpy as jnp
from jax import lax
from jax.experimental import pallas as pl
from jax.experimental.pallas import tpu as pltpu

F32 = jnp.float32
BF16 = jnp.bfloat16

D_MODEL = 1024
DEPTH = 4
GRID_W = 64
WIN_H = 8
WIN_W = 16
HEAD_DIM = 64
A_HEADS = 8
A_WIDTH = A_HEADS * HEAD_DIM
R_HEADS = 4
R_DIM = 64
R_WIDTH = R_HEADS * R_DIM
F_GROUPS = 4
F_GROUP_DIM = 64
F_WIDTH = F_GROUPS * F_GROUP_DIM
PA_COLS = 4 * A_WIDTH
PR_COLS = 5 * R_WIDTH
PF_COLS = 2 * F_WIDTH
IN_COLS = PA_COLS + PR_COLS + PF_COLS
CHUNK = 64
EPS = 1e-6
LANES = 128
NEG_INF = float("-inf")
VMEM_LIMIT = 56 * 1024 * 1024


def _cparams(sem):
    return pltpu.CompilerParams(dimension_semantics=sem, vmem_limit_bytes=VMEM_LIMIT)


def _silu(x):
    return x * (1.0 / (1.0 + jnp.exp(-x)))


def _dot(a, b):
    return jnp.dot(a, b, preferred_element_type=F32)


def _dot_nt(a, b):
    return lax.dot_general(a, b, (((1,), (1,)), ((), ())), preferred_element_type=F32)


def _dot_tn(a, b):
    return lax.dot_general(a, b, (((0,), (0,)), ((), ())), preferred_element_type=F32)


def _split2(x):
    hi = x.astype(BF16)
    lo = (x - hi.astype(F32)).astype(BF16)
    return hi, lo


def _mod_kernel(cc_ref, w_ref, b_ref, o_ref):
    a_hi, a_lo = _split2(_silu(cc_ref[...]))
    w_hi, w_lo = _split2(w_ref[...])
    acc = _dot(a_hi, w_hi) + _dot(a_hi, w_lo) + _dot(a_lo, w_hi)
    o_ref[...] = acc + b_ref[...]


def _modulations(cc, w_ada, b_ada):
    rows = cc.shape[0]
    tn = 1024
    return pl.pallas_call(
        _mod_kernel,
        out_shape=jax.ShapeDtypeStruct((DEPTH, rows, 3 * D_MODEL), F32),
        grid=(DEPTH, 3 * D_MODEL // tn),
        in_specs=[
            pl.BlockSpec((rows, D_MODEL), lambda l, j: (0, 0)),
            pl.BlockSpec((None, D_MODEL, tn), lambda l, j: (l, 0, j)),
            pl.BlockSpec((None, 1, tn), lambda l, j: (l, 0, j)),
        ],
        out_specs=pl.BlockSpec((None, rows, tn), lambda l, j: (l, 0, j)),
        compiler_params=_cparams(("arbitrary", "arbitrary")),
        name="adaln_mod",
    )(cc, w_ada, b_ada.reshape(DEPTH, 1, 3 * D_MODEL))


def _lb_kernel(x_ref, o_ref):
    xs = [x_ref[i] for i in range(DEPTH)]
    m = functools.reduce(jnp.maximum, xs)
    es = [jnp.exp(x - m) for x in xs]
    tot = functools.reduce(lambda a, b: a + b, es)
    cum = None
    first = None
    for i in range(DEPTH):
        p = es[i] / tot
        cum = p if cum is None else cum + p
        if first is None:
            first = cum
        lb = jnp.maximum(cum - first, 0.0)
        o_ref[0, i] = lb
        o_ref[1, i] = jnp.log1p(-lb)
        o_ref[2, i] = 1.0 - lb


def _lower_bounds(lb_logits):
    x = jnp.transpose(lb_logits, (1, 0, 2))
    return pl.pallas_call(
        _lb_kernel,
        out_shape=jax.ShapeDtypeStruct((3, DEPTH, 2, R_WIDTH), F32),
        name="hgrn_lower_bounds",
    )(x)


IN_TILE = 512


def _inproj_kernel(xc_ref, xl_ref, mod_ref, g_ref, w_ref, kprev_ref, vprev_ref,
                   pa_ref, pr_ref, pf_ref, ko_ref, vo_ref, wb_scr, *, ctx_tiles):
    del kprev_ref, vprev_ref
    i = pl.program_id(0)

    @pl.when(i == 0)
    def _():
        wb_scr[...] = w_ref[...].astype(BF16)

    is_ctx = i < ctx_tiles
    x = jnp.where(is_ctx, xc_ref[...], xl_ref[...])
    ms = jnp.mean(x * x, axis=-1, keepdims=True)
    y = x * lax.rsqrt(ms + EPS) * g_ref[...]
    h = (y * (1.0 + mod_ref[1:2, :]) + mod_ref[0:1, :]).astype(BF16)
    pa_ref[...] = _dot(h, wb_scr[:, 0:PA_COLS])
    pr_ref[...] = _dot(h, wb_scr[:, PA_COLS:PA_COLS + PR_COLS])
    pf_ref[...] = _dot(h, wb_scr[:, PA_COLS + PR_COLS:IN_COLS])

    @pl.when(is_ctx)
    def _():
        seqs, _, _, n = ko_ref.shape
        for s in range(seqs):
            rows = slice(s * n, (s + 1) * n)
            ko_ref[s] = pa_ref[rows, A_WIDTH:2 * A_WIDTH].T.reshape(A_HEADS, HEAD_DIM, n)
            vo_ref[s] = pa_ref[rows, 2 * A_WIDTH:3 * A_WIDTH].T.reshape(A_HEADS, HEAD_DIM, n)


def _in_projection(x_ctx, x_lat, mod, g_pre, w_in, layer, cache):
    bc, nc_, _ = x_ctx.shape
    bl, nl, _ = x_lat.shape
    tm = IN_TILE
    seqs_per_tile = tm // nc_
    lat_tiles_per_seq = nl // tm
    ctx_tiles = bc * nc_ // tm
    lat_tiles = bl * nl // tm
    tokens = (ctx_tiles + lat_tiles) * tm
    last_ctx = ctx_tiles - 1
    ctx_idx = lambda i: (jnp.minimum(i, last_ctx), 0)
    lat_idx = lambda i: (jnp.maximum(i - ctx_tiles, 0), 0)
    mod_idx = lambda i: (jnp.where(i < ctx_tiles, 0, 1 + jnp.maximum(i - ctx_tiles, 0) // lat_tiles_per_seq), 0, 0)
    tok = lambda i: (i, 0)
    cache_spec = pl.BlockSpec((seqs_per_tile, None, A_HEADS, HEAD_DIM, nc_),
                              lambda i: (jnp.minimum(i, last_ctx), layer, 0, 0, 0))
    cache_shape = jax.ShapeDtypeStruct(cache[0].shape, F32)
    return pl.pallas_call(
        functools.partial(_inproj_kernel, ctx_tiles=ctx_tiles),
        out_shape=(
            jax.ShapeDtypeStruct((tokens, PA_COLS), F32),
            jax.ShapeDtypeStruct((tokens, PR_COLS), F32),
            jax.ShapeDtypeStruct((tokens, PF_COLS), F32),
            cache_shape, cache_shape,
        ),
        grid=(ctx_tiles + lat_tiles,),
        in_specs=[
            pl.BlockSpec((tm, D_MODEL), ctx_idx),
            pl.BlockSpec((tm, D_MODEL), lat_idx),
            pl.BlockSpec((None, 3, D_MODEL), mod_idx),
            pl.BlockSpec((None, 1, D_MODEL), lambda i: (layer, 0, 0)),
            pl.BlockSpec((None, D_MODEL, IN_COLS), lambda i: (layer, 0, 0), pipeline_mode=pl.Buffered(1)),
            pl.BlockSpec(memory_space=pl.ANY),
            pl.BlockSpec(memory_space=pl.ANY),
        ],
        out_specs=(
            pl.BlockSpec((tm, PA_COLS), tok),
            pl.BlockSpec((tm, PR_COLS), tok),
            pl.BlockSpec((tm, PF_COLS), tok),
            cache_spec, cache_spec,
        ),
        scratch_shapes=[pltpu.VMEM((D_MODEL, IN_COLS), BF16)],
        input_output_aliases={5: 3, 6: 4},
        compiler_params=_cparams(("arbitrary",)),
        name="in_projection",
    )(x_ctx.reshape(bc * nc_, D_MODEL), x_lat.reshape(bl * nl, D_MODEL), mod,
      g_pre.reshape(DEPTH, 1, D_MODEL), w_in, cache[0], cache[1])


def _lane_lo():
    return lax.broadcasted_iota(jnp.int32, (1, LANES), 1) < HEAD_DIM


def _ctx_attn_kernel(pa_ref, o_ref, s_scr):
    lo = _lane_lo()
    npair = A_HEADS // 2

    def scores(p):
        c = p * LANES
        q2 = pa_ref[:, c:c + LANES] * (HEAD_DIM ** -0.5)
        k2 = pa_ref[:, A_WIDTH + c:A_WIDTH + c + LANES].astype(BF16)
        for hh in range(2):
            sel = lo if hh == 0 else jnp.logical_not(lo)
            s_scr[p % 2, hh] = _dot_nt(jnp.where(sel, q2, 0.0).astype(BF16), k2)

    def finish(p):
        c = p * LANES
        v2 = pa_ref[:, 2 * A_WIDTH + c:2 * A_WIDTH + c + LANES].astype(BF16)
        outs = []
        for hh in range(2):
            s = s_scr[p % 2, hh]
            e = jnp.exp(s - jnp.max(s, axis=-1, keepdims=True))
            inv = 1.0 / jnp.sum(e, axis=-1, keepdims=True)
            outs.append(_dot(e.astype(BF16), v2) * inv)
        o2 = jnp.where(lo, outs[0], outs[1])
        o_ref[:, c:c + LANES] = (o2 * _silu(pa_ref[:, 3 * A_WIDTH + c:3 * A_WIDTH + c + LANES])).astype(BF16)

    scores(0)
    for p in range(npair):
        if p + 1 < npair:
            scores(p + 1)
        finish(p)


def _context_attention(pa, seqs):
    b0, b = seqs
    n = pa.shape[1]
    return pl.pallas_call(
        _ctx_attn_kernel,
        out_shape=jax.ShapeDtypeStruct((b, n, A_WIDTH), BF16),
        grid=(b,),
        in_specs=[pl.BlockSpec((None, n, PA_COLS), lambda i: (b0 + i, 0, 0))],
        out_specs=pl.BlockSpec((None, n, A_WIDTH), lambda i: (i, 0, 0)),
        scratch_shapes=[pltpu.VMEM((2, 2, n, n), F32)],
        compiler_params=_cparams(("arbitrary",)),
        name="context_attention",
    )(pa)


QROWS = 4


def _nbr_blocks(rows):
    kh = min(WIN_H, rows)
    out = []
    for r_first in range(0, rows, QROWS):
        r0s = [min(max(r - kh // 2, 0), rows - kh) for r in range(r_first, r_first + QROWS)]
        lo, hi = min(r0s), max(r0s) + kh
        lo -= lo % 2
        span = hi - lo
        span += (-span) % 4
        if lo + span > rows:
            lo = rows - span
        assert lo >= 0 and lo % 2 == 0
        out.append((r_first, lo, span, r0s))
    return out, kh


N_REL_ROWS = 2 * WIN_H - 1
TABLE_W = 2 * WIN_H * GRID_W


def _build_bias_tables(base_ref, te_scr, to_scr, heads):
    lane = lax.broadcasted_iota(jnp.int32, (GRID_W, LANES), 1)
    qc = lax.broadcasted_iota(jnp.int32, (GRID_W, LANES), 0)
    kc = lane % GRID_W
    lo = lane < GRID_W
    c0 = jnp.clip(qc - WIN_W // 2, 0, GRID_W - WIN_W)
    col_in = (kc >= c0) & (kc < c0 + WIN_W)
    neg = jnp.full((GRID_W, LANES), NEG_INF, F32)

    for h in heads:
        def tile(i, lane_off):
            row = jnp.broadcast_to(base_ref[h, i:i + 1, :], (GRID_W, LANES))
            return pltpu.roll(row, lane_off, 1, stride=1, stride_axis=0)

        for j in range(WIN_H):
            for scr, ia in ((te_scr, 2 * j), (to_scr, 2 * j + 1)):
                ib = ia + 1
                if ia >= N_REL_ROWS:
                    pair = neg
                elif ib >= N_REL_ROWS:
                    pair = jnp.where(col_in & lo, tile(ia, 0), NEG_INF)
                else:
                    pair = jnp.where(col_in, jnp.where(lo, tile(ia, 0), tile(ib, GRID_W)), NEG_INF)
                scr[h, :, j * LANES:(j + 1) * LANES] = pair


def _nbr_attn_kernel(q_ref, k_ref, v_ref, g_ref, kc_ref, vc_ref, base_ref, o_ref, s_scr, p_scr, te_scr, to_scr, *, rows):
    p = pl.program_id(1)

    @pl.when(pl.program_id(0) == 0)
    def _():
        _build_bias_tables(base_ref, te_scr, to_scr, (2 * p, 2 * p + 1))

    lo = _lane_lo()
    hi = jnp.logical_not(lo)
    blocks, kh = _nbr_blocks(rows)
    nq = QROWS * GRID_W
    past = kc_ref.shape[-1]
    ctx0 = s_scr.shape[-1] - past
    kct = kc_ref[...].reshape(LANES, past).astype(BF16)
    vct = vc_ref[...].reshape(LANES, past).astype(BF16)
    ctx_tiles = [slice(ctx0 + t * LANES, ctx0 + (t + 1) * LANES) for t in range(past // LANES)]

    def window_tiles(bi, rl, head):
        r_first, u0, span, r0s = blocks[bi]
        r, r0 = r_first + rl, r0s[rl]
        out = []
        for t in range(span // 2):
            kr0 = u0 + 2 * t
            cs = slice(t * LANES, (t + 1) * LANES)
            ok0 = r0 <= kr0 < r0 + kh
            ok1 = r0 <= kr0 + 1 < r0 + kh
            if not (ok0 or ok1):
                out.append((cs, None))
                continue
            i0 = kr0 - r + WIN_H - 1
            if i0 % 2 == 0:
                assert 0 <= i0 and i0 + 2 <= 2 * WIN_H
                bias = te_scr[head, :, i0 * GRID_W:(i0 + 2) * GRID_W]
            else:
                assert 1 <= i0 and i0 + 1 <= 2 * WIN_H - 1
                bias = to_scr[head, :, (i0 - 1) * GRID_W:(i0 + 1) * GRID_W]
            if not ok0:
                bias = jnp.where(lo, NEG_INF, bias)
            if not ok1:
                bias = jnp.where(hi, NEG_INF, bias)
            out.append((cs, bias))
        return out

    def scores(bi):
        r_first, u0, span, _ = blocks[bi]
        qs, ks, nk = r_first * GRID_W, u0 * GRID_W, span * GRID_W
        q2 = q_ref[qs:qs + nq, :] * (HEAD_DIM ** -0.5)
        ku = k_ref[ks:ks + nk, :].astype(BF16)
        for hh in range(2):
            qh = jnp.where(lo if hh == 0 else hi, q2, 0.0).astype(BF16)
            s_scr[bi % 2, hh, :, 0:nk] = _dot_nt(qh, ku)
            s_scr[bi % 2, hh, :, ctx0:ctx0 + past] = _dot(qh, kct)

    def softmax(bi):
        par = bi % 2
        inv = []
        for hh in range(2):
            parts = []
            for rl in range(QROWS):
                rs = slice(rl * GRID_W, (rl + 1) * GRID_W)
                tiles = window_tiles(bi, rl, 2 * p + hh)
                m_acc = None
                for cs, bias in tiles:
                    if bias is None:
                        continue
                    s = s_scr[par, hh, rs, cs] + bias
                    s_scr[par, hh, rs, cs] = s
                    m_acc = s if m_acc is None else jnp.maximum(m_acc, s)
                for cs in ctx_tiles:
                    m_acc = jnp.maximum(m_acc, s_scr[par, hh, rs, cs])
                m = jnp.max(m_acc, axis=-1, keepdims=True)
                l_acc = None
                for cs, bias in tiles:
                    if bias is None:
                        p_scr[par, hh, rs, cs] = jnp.zeros((GRID_W, LANES), BF16)
                        continue
                    e = jnp.exp(s_scr[par, hh, rs, cs] - m)
                    p_scr[par, hh, rs, cs] = e.astype(BF16)
                    l_acc = e if l_acc is None else l_acc + e
                for cs in ctx_tiles:
                    e = jnp.exp(s_scr[par, hh, rs, cs] - m)
                    p_scr[par, hh, rs, cs] = e.astype(BF16)
                    l_acc = l_acc + e
                parts.append(1.0 / jnp.sum(l_acc, axis=-1, keepdims=True))
            inv.append(jnp.concatenate(parts, axis=0))
        return inv

    def values(bi, inv):
        r_first, u0, span, _ = blocks[bi]
        qs, ks, nk = r_first * GRID_W, u0 * GRID_W, span * GRID_W
        vu = v_ref[ks:ks + nk, :].astype(BF16)
        outs = []
        for hh in range(2):
            o = _dot(p_scr[bi % 2, hh, :, 0:nk], vu) + _dot_nt(p_scr[bi % 2, hh, :, ctx0:ctx0 + past], vct)
            outs.append(o * inv[hh])
        o2 = jnp.where(lo, outs[0], outs[1])
        o_ref[qs:qs + nq, :] = (o2 * _silu(g_ref[qs:qs + nq, :])).astype(BF16)

    scores(0)
    for bi in range(len(blocks)):
        if bi + 1 < len(blocks):
            scores(bi + 1)
        values(bi, softmax(bi))


def _bias_base(rpb_l):
    mid = WIN_W - 1
    zeros = jnp.zeros(rpb_l.shape[:2] + (LANES - (2 * WIN_W - 1),), F32)
    base = jnp.concatenate([rpb_l[..., mid:], zeros, rpb_l[..., :mid]], axis=-1)
    return jnp.pad(base, ((0, 0), (0, 2 * WIN_H - N_REL_ROWS), (0, 0)))


def _neighbourhood_attention(pa, seqs, cache_kt, cache_vt, layer, rpb_l):
    b0, b = seqs
    n = pa.shape[1]
    rows = n // GRID_W
    past = cache_kt.shape[-1]
    blocks, _ = _nbr_blocks(rows)
    max_nk = max(s for (_, _, s, _) in blocks) * GRID_W
    npair = A_HEADS // 2
    col = lambda off: (lambda i, p: (b0 + i, 0, off + p))
    cache_spec = pl.BlockSpec((None, None, 2, HEAD_DIM, past), lambda i, p: (i, layer, p, 0, 0))
    return pl.pallas_call(
        functools.partial(_nbr_attn_kernel, rows=rows),
        out_shape=jax.ShapeDtypeStruct((b, n, A_WIDTH), BF16),
        grid=(b, npair),
        in_specs=[
            pl.BlockSpec((None, n, LANES), col(0)),
            pl.BlockSpec((None, n, LANES), col(npair)),
            pl.BlockSpec((None, n, LANES), col(2 * npair)),
            pl.BlockSpec((None, n, LANES), col(3 * npair)),
            cache_spec,
            cache_spec,
            pl.BlockSpec((A_HEADS, 2 * WIN_H, LANES), lambda i, p: (0, 0, 0)),
        ],
        out_specs=pl.BlockSpec((None, n, LANES), lambda i, p: (i, 0, p)),
        scratch_shapes=[
            pltpu.VMEM((2, 2, QROWS * GRID_W, max_nk + past), F32),
            pltpu.VMEM((2, 2, QROWS * GRID_W, max_nk + past), BF16),
            pltpu.VMEM((A_HEADS, GRID_W, TABLE_W), F32),
            pltpu.VMEM((A_HEADS, GRID_W, TABLE_W), F32),
        ],
        compiler_params=_cparams(("arbitrary", "arbitrary")),
        name="neighbourhood_attention",
    )(pa, pa, pa, pa, cache_kt, cache_vt, _bias_base(rpb_l))


COARSE_HALVES = (32, 16, 8)
FINE_HALVES = (4, 2, 1)
N_LEVELS = len(COARSE_HALVES) + len(FINE_HALVES)
ANCHOR_BLOCK = CHUNK
ANCHOR_LEVELS = 0
ANCHOR_MAX_EXPONENT = 80.0
MASK_DIAG = N_LEVELS
MASK_ANCHOR = N_LEVELS + 1
GROUP = 4


def _hgrn_constants():
    c = CHUNK
    idx = np.arange(c)
    mats = [np.tril(np.ones((c, c)))]
    masks = []
    for h in COARSE_HALVES + FINE_HALVES:
        blk = idx // (2 * h)
        mid = blk * 2 * h + h - 1
        upper = idx > mid
        if h in FINE_HALVES:
            m = np.zeros((c, c))
            for i in range(c):
                if upper[i]:
                    m[i, mid[i] + 1:i + 1] = 1.0
                else:
                    m[i, i + 1:mid[i] + 1] = 1.0
            mats.append(m)
        same = blk[:, None] == blk[None, :]
        masks.append((same & upper[:, None] & (~upper)[None, :]).astype(np.float64))
    masks.append(np.eye(c))
    same_block = (idx[:, None] // ANCHOR_BLOCK) == (idx[None, :] // ANCHOR_BLOCK)
    masks.append((same_block & (idx[None, :] <= idx[:, None])).astype(np.float64))
    fwd = np.concatenate(mats, axis=0)
    bwd = np.concatenate([m[::-1, ::-1] for m in mats], axis=0)
    mk_f = np.stack([np.tile(m, (1, R_HEADS)) for m in masks])
    mk_b = np.stack([np.tile(m[::-1, ::-1], (1, R_HEADS)) for m in masks])
    hid = np.arange(R_WIDTH) // R_DIM
    bd = (hid[:, None] == hid[None, :]).astype(np.float64)
    return (jnp.asarray(np.stack([fwd, bwd]), BF16), jnp.asarray(np.stack([mk_f, mk_b]), F32),
            jnp.asarray(bd, BF16))


def _hgrn_kernel(*refs, nc, has_init):
    pr_ref, lbp_ref, gh_ref, mstk_ref, lmask_ref, bd_ref = refs[:6]
    if has_init:
        s0_ref, o_ref = refs[6:8]
        sfin_ref = None
    else:
        o_ref, sfin_ref = refs[6:8]
        s0_ref = None
    q_scr, k_scr, g_scr, gc_scr, qg_scr, upd_scr, dec_scr, sbd_scr = refs[8:]
    w = R_WIDTH
    c = CHUNK
    bd = bd_ref[...]
    hid = lax.broadcasted_iota(jnp.int32, (1, w), 1) // R_DIM

    q_scr[...] = _silu(pr_ref[:, 0:w])
    for d in range(2):
        z = pr_ref[:, (1 + d) * w:(2 + d) * w]
        e = jnp.exp(-jnp.abs(z))
        r = 1.0 / (1.0 + e)
        pos = z >= 0.0
        one_m_lb = lbp_ref[2, d:d + 1, :]
        f = lbp_ref[0, d:d + 1, :] + one_m_lb * (jnp.where(pos, 1.0, e) * r)
        log_f = jnp.where(f > 0.0, jnp.log(f), lbp_ref[1, d:d + 1, :] + z)
        g_hi, g_lo = _split2(log_f)
        g_scr[d, :, 0:w] = g_hi
        g_scr[d, :, w:2 * w] = g_lo
        k_scr[d] = one_m_lb * (jnp.where(pos, e, 1.0) * r)

    def tile4(x):
        return jnp.concatenate([x] * R_HEADS, axis=0)

    def head_diagonal(full):
        out = full[(R_HEADS - 1) * R_DIM:R_HEADS * R_DIM]
        for h in range(R_HEADS - 2, -1, -1):
            out = jnp.where(hid == h, full[h * R_DIM:(h + 1) * R_DIM], out)
        return out

    def chunk_rows(ci):
        return pl.ds(pl.multiple_of(ci * c, c), c)

    def cumulative(d, rs, n_mats):
        s = _dot(mstk_ref[d, 0:n_mats * c, :], g_scr[d, rs, :])
        return s[:, 0:w] + s[:, w:2 * w]

    def level_exponents(gcum, d, halves):
        out = []
        for h in halves:
            parts = []
            for s0 in range(0, c, 2 * h):
                anchor = s0 + h - 1 + d
                parts.append(-jnp.abs(gcum[s0:s0 + 2 * h] - gcum[anchor:anchor + 1]))
            out.append(parts[0] if len(parts) == 1 else jnp.concatenate(parts, axis=0))
        return out

    def anchor_shift(gcum, d):
        parts = []
        for s0 in range(0, c, ANCHOR_BLOCK):
            anchor = s0 + ANCHOR_BLOCK // 2 - 1 + d
            parts.append(gcum[s0:s0 + ANCHOR_BLOCK] - gcum[anchor:anchor + 1])
        return jnp.concatenate(parts, axis=0)

    groups = nc // GROUP
    pairs = [(j, d) for j in range(GROUP) for d in range(2)]

    def increment_group(gi, worst):
        rows = [chunk_rows(gi * GROUP + j) for j in range(GROUP)]
        gcums = [cumulative(d, rows[j], 1) for j, d in pairs]
        lasts = [g[c - 1:c, :] if d == 0 else g[0:1, :] for g, (j, d) in zip(gcums, pairs)]
        kls = [(k_scr[d, rows[j], :] * jnp.exp(last - g)).astype(BF16)
               for g, last, (j, d) in zip(gcums, lasts, pairs)]
        vs = [pr_ref[rows[j], 3 * w:4 * w].astype(BF16) for j in range(GROUP)]
        upds = [_dot_tn(vs[j], kl) for kl, (j, d) in zip(kls, pairs)]
        for g, last, upd, (j, d) in zip(gcums, lasts, upds, pairs):
            ci = gi * GROUP + j
            gc_scr[d, rows[j], :] = g
            upd_scr[d, ci] = head_diagonal(upd)
            dec_scr[d, ci] = jnp.broadcast_to(jnp.exp(last), (8, w))
            qg_scr[d, rows[j], :] = (q_scr[rows[j], :] * jnp.exp(g)).astype(BF16)
            sh = jnp.abs(anchor_shift(g, d))
            for s0 in range(0, c, 8):
                worst = jnp.maximum(worst, sh[s0:s0 + 8])
        return worst

    worst = lax.fori_loop(0, groups, increment_group, jnp.zeros((8, w), F32))
    q_max = jnp.max(jnp.max(jnp.abs(q_scr[...]), axis=0, keepdims=True), axis=1, keepdims=True)
    anchor_ok = jnp.max(worst + jnp.log(jnp.maximum(q_max, 1.0))) < ANCHOR_MAX_EXPONENT

    def scan_step(ci, carry):
        sf, sb = carry
        cb = nc - 1 - ci
        sbd_scr[0, ci] = tile4(sf.astype(BF16)) * bd
        sbd_scr[1, cb] = tile4(sb.astype(BF16)) * bd
        sf = sf * dec_scr[0, ci][0:1, :] + upd_scr[0, ci]
        sb = sb * dec_scr[1, cb][0:1, :] + upd_scr[1, cb]
        return sf, sb

    if has_init:
        init = tuple(jnp.concatenate([s0_ref[d], jnp.zeros((w, LANES - R_DIM), F32)], axis=1).T[0:R_DIM]
                     for d in range(2))
    else:
        init = (jnp.zeros((R_DIM, w), F32), jnp.zeros((R_DIM, w), F32))
    finals = lax.fori_loop(0, nc, scan_step, init)
    if sfin_ref is not None:
        for d in range(2):
            padded = jnp.concatenate([finals[d], jnp.zeros((LANES - R_DIM, w), F32)], axis=0)
            sfin_ref[d] = padded.T[:, 0:R_DIM]

    def finish(rs, o):
        x2_hi, x2_lo = _split2(o * o)
        ms = (_dot(x2_hi, bd) + _dot(x2_lo, bd)) * (1.0 / R_DIM)
        y = o * lax.rsqrt(ms + EPS) * gh_ref[...]
        o_ref[rs, :] = (y * _silu(pr_ref[rs, 4 * w:5 * w])).astype(BF16)

    def readout_group_anchor(gi, carry):
        rows = [chunk_rows(gi * GROUP + j) for j in range(GROUP)]
        qs = [q_scr[rows[j], :] for j in range(GROUP)]
        gcums = [gc_scr[d, rows[j], :] for j, d in pairs]
        factors = []
        for g, (j, d) in zip(gcums, pairs):
            es = [jnp.exp(x) for x in level_exponents(g, d, COARSE_HALVES[:ANCHOR_LEVELS])]
            sh = anchor_shift(g, d)
            fs = [(e, e, lv) for lv, e in enumerate(es)]
            fs.append((jnp.exp(sh), jnp.exp(-sh), MASK_ANCHOR))
            factors.append(fs)
        kbds = [tile4(k_scr[d, rows[j], :].astype(BF16)) * bd for j, d in pairs]
        accs = [None] * len(pairs)
        for lv in range(ANCHOR_LEVELS + 1):
            for i, (j, d) in enumerate(pairs):
                eq, ek, mask = factors[i][lv]
                part = jnp.where(lmask_ref[d, mask] > 0.5,
                                 _dot_nt((qs[j] * eq).astype(BF16), kbds[i] * tile4(ek.astype(BF16))), 0.0)
                accs[i] = part if accs[i] is None else accs[i] + part
        vbds = [tile4(pr_ref[rows[j], 3 * w:4 * w].astype(BF16)) * bd for j in range(GROUP)]
        outs = [None] * GROUP
        for i, (j, d) in enumerate(pairs):
            ci = gi * GROUP + j
            od = _dot(accs[i].astype(BF16), vbds[j]) + _dot_nt(qg_scr[d, rows[j], :], sbd_scr[d, ci])
            outs[j] = od if outs[j] is None else outs[j] + od
        for j in range(GROUP):
            finish(rows[j], outs[j])
        return carry

    def readout_step_split(ci, carry):
        rs = chunk_rows(ci)
        q = q_scr[rs, :]
        qb = q.astype(BF16)
        vbd = tile4(pr_ref[rs, 3 * w:4 * w].astype(BF16)) * bd
        dsts = [cumulative(d, rs, 1 + len(FINE_HALVES)) for d in range(2)]
        exps = [level_exponents(dsts[d][0:c], d, COARSE_HALVES)
                + [dsts[d][(1 + j) * c:(2 + j) * c] for j in range(len(FINE_HALVES))] for d in range(2)]
        kbd = [tile4(k_scr[d, rs, :].astype(BF16)) * bd for d in range(2)]
        a = [_dot_nt(qb, kbd[d]) * lmask_ref[d, MASK_DIAG] for d in range(2)]
        for j in range(N_LEVELS):
            for d in range(2):
                e = jnp.exp(exps[d][j])
                a[d] = a[d] + _dot_nt((q * e).astype(BF16), kbd[d] * tile4(e.astype(BF16))) * lmask_ref[d, j]
        o = None
        for d in range(2):
            od = _dot(a[d].astype(BF16), vbd) + _dot_nt(qg_scr[d, rs, :], sbd_scr[d, ci])
            o = od if o is None else o + od
        finish(rs, o)
        return carry

    @pl.when(anchor_ok)
    def _():
        lax.fori_loop(0, groups, readout_group_anchor, 0)

    @pl.when(jnp.logical_not(anchor_ok))
    def _():
        lax.fori_loop(0, nc, readout_step_split, 0)


def _hgrn(pr, seqs, lbp_l, g_hgrn_l, consts, state=None, layer=None):
    b0, b = seqs
    n = pr.shape[1]
    nc = n // CHUNK
    mstk, lmask, bd = consts
    w = R_WIDTH
    has_init = state is not None
    full = lambda *shape: pl.BlockSpec(shape, lambda i: (0,) * len(shape))
    in_specs = [
        pl.BlockSpec((None, n, PR_COLS), lambda i: (b0 + i, 0, 0)),
        full(3, 2, w),
        full(1, w),
        full(*mstk.shape),
        full(*lmask.shape),
        full(w, w),
    ]
    args = [pr, lbp_l, g_hgrn_l.reshape(1, w), mstk, lmask, bd]
    out_shape = [jax.ShapeDtypeStruct((b, n, w), BF16)]
    out_specs = [pl.BlockSpec((None, n, w), lambda i: (i, 0, 0))]
    if has_init:
        in_specs.append(pl.BlockSpec((None, None, 2, w, R_DIM), lambda i: (i, layer, 0, 0, 0)))
        args.append(state)
    else:
        out_shape.append(jax.ShapeDtypeStruct((b, 2, w, R_DIM), F32))
        out_specs.append(pl.BlockSpec((None, 2, w, R_DIM), lambda i: (i, 0, 0, 0)))
    return pl.pallas_call(
        functools.partial(_hgrn_kernel, nc=nc, has_init=has_init),
        out_shape=tuple(out_shape),
        grid=(b,),
        in_specs=in_specs,
        out_specs=tuple(out_specs),
        scratch_shapes=[
            pltpu.VMEM((n, w), F32),
            pltpu.VMEM((2, n, w), F32),
            pltpu.VMEM((2, n, 2 * w), BF16),
            pltpu.VMEM((2, n, w), F32),
            pltpu.VMEM((2, n, w), BF16),
            pltpu.VMEM((2, nc, R_DIM, w), F32),
            pltpu.VMEM((2, nc, 8, w), F32),
            pltpu.VMEM((2, nc, w, w), BF16),
        ],
        compiler_params=_cparams(("arbitrary",)),
        name="hgrn_scan",
    )(*args)


def _fnet_constants(n):
    j = np.arange(F_GROUP_DIM)
    ang = 2.0 * np.pi * ((j[:, None] * j[None, :]) % F_GROUP_DIM) / F_GROUP_DIM
    eye = np.eye(F_GROUPS)
    cs = np.concatenate([np.kron(eye, np.cos(ang)), np.kron(eye, np.sin(ang))], axis=1)
    t = np.arange(n)
    angn = 2.0 * np.pi * ((t[:, None] * t[None, :]) % n) / n
    return tuple(jnp.asarray(m, F32).astype(BF16) for m in (cs, np.cos(angn), np.sin(angn)))


def _fnet_kernel(pf_ref, cs_ref, cn_ref, sn_ref, wf_ref, o_ref, *, scale):
    w = F_WIDTH
    t = _dot(pf_ref[:, 0:w].astype(BF16), cs_ref[...])
    y = (_dot(cn_ref[...], t[:, 0:w].astype(BF16)) - _dot(sn_ref[...], t[:, w:2 * w].astype(BF16))) * scale
    of = _dot(y.astype(BF16), wf_ref[...])
    o_ref[...] = (of * _silu(pf_ref[:, w:2 * w])).astype(BF16)


def _fourier(pf, seqs, consts, w_fnet_bf16):
    b0, b = seqs
    n = pf.shape[1]
    cs, cn, sn = consts
    w = F_WIDTH
    full = lambda *shape: pl.BlockSpec(shape, lambda i: (0,) * len(shape))
    return pl.pallas_call(
        functools.partial(_fnet_kernel, scale=float((n * F_GROUP_DIM) ** -0.5)),
        out_shape=jax.ShapeDtypeStruct((b, n, w), BF16),
        grid=(b,),
        in_specs=[pl.BlockSpec((None, n, PF_COLS), lambda i: (b0 + i, 0, 0)),
                  full(w, 2 * w), full(n, n), full(n, n), full(w, w)],
        out_specs=pl.BlockSpec((None, n, w), lambda i: (i, 0, 0)),
        compiler_params=_cparams(("arbitrary",)),
        name="fourier_mixing",
    )(pf, cs, cn, sn, w_fnet_bf16)


def _outproj_kernel(ma_ref, mr_ref, mf_ref, w_ref, x_ref, mod_ref, g_ref, y_ref):
    out = (_dot(ma_ref[...], w_ref[0:A_WIDTH, :])
           + _dot(mr_ref[...], w_ref[A_WIDTH:A_WIDTH + R_WIDTH, :])
           + _dot(mf_ref[...], w_ref[A_WIDTH + R_WIDTH:D_MODEL, :]))
    ms = jnp.mean(out * out, axis=-1, keepdims=True)
    y_ref[...] = x_ref[...] + mod_ref[2:3, :] * (out * lax.rsqrt(ms + EPS) * g_ref[...])


def _out_projection(ma, mr, mf, w_out_bf16, x, mod, g_post):
    b, n, _ = x.shape
    tm = min(n, 512)
    tiles = n // tm
    tok = lambda i: (i // tiles, i % tiles, 0)
    nb_mod = mod.shape[0]
    mod_idx = (lambda i: (i // tiles, 0, 0)) if nb_mod > 1 else (lambda i: (0, 0, 0))
    return pl.pallas_call(
        _outproj_kernel,
        out_shape=jax.ShapeDtypeStruct((b, n, D_MODEL), F32),
        grid=(b * tiles,),
        in_specs=[
            pl.BlockSpec((None, tm, A_WIDTH), tok),
            pl.BlockSpec((None, tm, R_WIDTH), tok),
            pl.BlockSpec((None, tm, F_WIDTH), tok),
            pl.BlockSpec((D_MODEL, D_MODEL), lambda i: (0, 0)),
            pl.BlockSpec((None, tm, D_MODEL), tok),
            pl.BlockSpec((None, 3, D_MODEL), mod_idx),
            pl.BlockSpec((1, D_MODEL), lambda i: (0, 0)),
        ],
        out_specs=pl.BlockSpec((None, tm, D_MODEL), tok),
        compiler_params=_cparams(("arbitrary",)),
        name="out_projection",
    )(ma, mr, mf, w_out_bf16, x, mod, g_post.reshape(1, D_MODEL))


def kernel(x_prompt, x_sample, cache_attn_k, cache_attn_v, state_hgrn, c, c_ctx,
           w_ada, b_ada, g_pre, w_in, rpb, lb_logits, g_hgrn, w_fnet, w_out, g_post):
    nb_ctx, n_ctx, _ = x_prompt.shape
    nb_lat, n_lat, _ = x_sample.shape

    pad_rows = (-(1 + nb_lat)) % 8
    cc = jnp.concatenate([c_ctx[None, :], c, jnp.zeros((pad_rows, D_MODEL), F32)], axis=0)
    mods = _modulations(cc, w_ada, b_ada)
    lbp = _lower_bounds(lb_logits)

    w_out_b = w_out.astype(BF16)
    w_fnet_b = w_fnet.astype(BF16)
    hconsts = _hgrn_constants()
    fconsts_ctx = _fnet_constants(n_ctx)
    fconsts_lat = _fnet_constants(n_lat)

    state_rows = state_hgrn.reshape(nb_lat, DEPTH, 2, R_WIDTH, R_DIM)
    cache_kt = jnp.transpose(cache_attn_k, (0, 1, 3, 4, 2))
    cache_vt = jnp.transpose(cache_attn_v, (0, 1, 3, 4, 2))

    yp, ys = x_prompt, x_sample
    new_kv = [jnp.zeros((nb_ctx, DEPTH, A_HEADS, HEAD_DIM, n_ctx), F32) for _ in range(2)]
    new_s = []
    ctx_tokens = nb_ctx * n_ctx
    ctx_seqs = (0, nb_ctx)
    lat_seqs = (ctx_tokens // n_lat, nb_lat)
    for l in range(DEPTH):
        lbp_l = lbp[:, l]
        mod_all = mods[l, 0:1 + nb_lat].reshape(1 + nb_lat, 3, D_MODEL)
        pa, pr, pf, *new_kv = _in_projection(yp, ys, mod_all, g_pre, w_in, l, new_kv)
        as_ctx = lambda t: t.reshape(-1, n_ctx, t.shape[-1])
        ma = _context_attention(as_ctx(pa), ctx_seqs)
        mr, sfin = _hgrn(as_ctx(pr), ctx_seqs, lbp_l, g_hgrn[l], hconsts)
        new_s.append(sfin)
        mf = _fourier(as_ctx(pf), ctx_seqs, fconsts_ctx, w_fnet_b[l])
        yp = _out_projection(ma, mr, mf, w_out_b[l], yp, mod_all[0:1], g_post[l])
        as_lat = lambda t: t.reshape(-1, n_lat, t.shape[-1])
        ma = _neighbourhood_attention(as_lat(pa), lat_seqs, cache_kt, cache_vt, l, rpb[l])
        (mr,) = _hgrn(as_lat(pr), lat_seqs, lbp_l, g_hgrn[l], hconsts, state=state_rows, layer=l)
        mf = _fourier(as_lat(pf), lat_seqs, fconsts_lat, w_fnet_b[l])
        ys = _out_projection(ma, mr, mf, w_out_b[l], ys, mod_all[1:], g_post[l])

    new_state = jnp.stack(new_s, axis=1).reshape(nb_ctx, DEPTH, 2, R_HEADS, R_DIM, R_DIM)
    new_k, new_v = (jnp.transpose(t, (0, 1, 4, 2, 3)) for t in new_kv)
    return (yp, ys, new_k, new_v, new_state)
```

```python
import functools

import numpy as np
import jax
import jax.numpy as jnp
from jax import lax
from jax.experimental import pallas as pl
from jax.experimental.pallas import tpu as pltpu

F32 = jnp.float32
BF16 = jnp.bfloat16

D_MODEL = 1024
DEPTH = 4
GRID_W = 64
WIN_H = 8
WIN_W = 16
HEAD_DIM = 64
A_HEADS = 8
A_WIDTH = A_HEADS * HEAD_DIM
R_HEADS = 4
R_DIM = 64
R_WIDTH = R_HEADS * R_DIM
F_GROUPS = 4
F_GROUP_DIM = 64
F_WIDTH = F_GROUPS * F_GROUP_DIM
PA_COLS = 4 * A_WIDTH
PR_COLS = 5 * R_WIDTH
PF_COLS = 2 * F_WIDTH
IN_COLS = PA_COLS + PR_COLS + PF_COLS
CHUNK = 64
EPS = 1e-6
LANES = 128
NEG_INF = float("-inf")
VMEM_LIMIT = 56 * 1024 * 1024


def _cparams(sem):
    return pltpu.CompilerParams(dimension_semantics=sem, vmem_limit_bytes=VMEM_LIMIT)


def _silu(x):
    return x * (1.0 / (1.0 + jnp.exp(-x)))


def _dot(a, b):
    return jnp.dot(a, b, preferred_element_type=F32)


def _dot_nt(a, b):
    return lax.dot_general(a, b, (((1,), (1,)), ((), ())), preferred_element_type=F32)


def _dot_tn(a, b):
    return lax.dot_general(a, b, (((0,), (0,)), ((), ())), preferred_element_type=F32)


def _split2(x):
    hi = x.astype(BF16)
    lo = (x - hi.astype(F32)).astype(BF16)
    return hi, lo


def _mod_kernel(cc_ref, w_ref, b_ref, o_ref):
    a_hi, a_lo = _split2(_silu(cc_ref[...]))
    w_hi, w_lo = _split2(w_ref[...])
    acc = _dot(a_hi, w_hi) + _dot(a_hi, w_lo) + _dot(a_lo, w_hi)
    o_ref[...] = acc + b_ref[...]


def _modulations(cc, w_ada, b_ada):
    rows = cc.shape[0]
    tn = 3 * D_MODEL // 2
    return pl.pallas_call(
        _mod_kernel,
        out_shape=jax.ShapeDtypeStruct((DEPTH, rows, 3 * D_MODEL), F32),
        grid=(DEPTH, 3 * D_MODEL // tn),
        in_specs=[
            pl.BlockSpec((rows, D_MODEL), lambda l, j: (0, 0)),
            pl.BlockSpec((None, D_MODEL, tn), lambda l, j: (l, 0, j)),
            pl.BlockSpec((None, 1, tn), lambda l, j: (l, 0, j)),
        ],
        out_specs=pl.BlockSpec((None, rows, tn), lambda l, j: (l, 0, j)),
        compiler_params=_cparams(("arbitrary", "arbitrary")),
        name="adaln_mod",
    )(cc, w_ada, b_ada.reshape(DEPTH, 1, 3 * D_MODEL))


def _lb_kernel(x_ref, o_ref):
    xs = [x_ref[i] for i in range(DEPTH)]
    m = functools.reduce(jnp.maximum, xs)
    es = [jnp.exp(x - m) for x in xs]
    tot = functools.reduce(lambda a, b: a + b, es)
    cum = None
    first = None
    for i in range(DEPTH):
        p = es[i] / tot
        cum = p if cum is None else cum + p
        if first is None:
            first = cum
        lb = jnp.maximum(cum - first, 0.0)
        o_ref[0, i] = lb
        o_ref[1, i] = jnp.log1p(-lb)
        o_ref[2, i] = 1.0 - lb


def _lower_bounds(lb_logits):
    x = jnp.transpose(lb_logits, (1, 0, 2))
    return pl.pallas_call(
        _lb_kernel,
        out_shape=jax.ShapeDtypeStruct((3, DEPTH, 2, R_WIDTH), F32),
        name="hgrn_lower_bounds",
    )(x)


IN_TILE = 512


def _inproj_kernel(xc_ref, xl_ref, mod_ref, g_ref, w_ref, kprev_ref, vprev_ref,
                   pa_ref, pr_ref, pf_ref, ko_ref, vo_ref, wb_scr, *, ctx_tiles):
    del kprev_ref, vprev_ref
    i = pl.program_id(0)

    @pl.when(i == 0)
    def _():
        wb_scr[...] = w_ref[...].astype(BF16)

    is_ctx = i < ctx_tiles
    x = jnp.where(is_ctx, xc_ref[...], xl_ref[...])
    ms = jnp.mean(x * x, axis=-1, keepdims=True)
    y = x * lax.rsqrt(ms + EPS) * g_ref[...]
    h = (y * (1.0 + mod_ref[1:2, :]) + mod_ref[0:1, :]).astype(BF16)
    pa_ref[...] = _dot(h, wb_scr[:, 0:PA_COLS])
    pr_ref[...] = _dot(h, wb_scr[:, PA_COLS:PA_COLS + PR_COLS])
    pf_ref[...] = _dot(h, wb_scr[:, PA_COLS + PR_COLS:IN_COLS])

    @pl.when(is_ctx)
    def _():
        seqs, _, _, n = ko_ref.shape
        for s in range(seqs):
            rows = slice(s * n, (s + 1) * n)
            ko_ref[s] = pa_ref[rows, A_WIDTH:2 * A_WIDTH].T.reshape(A_HEADS, HEAD_DIM, n)
            vo_ref[s] = pa_ref[rows, 2 * A_WIDTH:3 * A_WIDTH].T.reshape(A_HEADS, HEAD_DIM, n)


def _in_projection(x_ctx, x_lat, mod, g_pre, w_in, layer, cache):
    bc, nc_, _ = x_ctx.shape
    bl, nl, _ = x_lat.shape
    tm = IN_TILE
    seqs_per_tile = tm // nc_
    lat_tiles_per_seq = nl // tm
    ctx_tiles = bc * nc_ // tm
    lat_tiles = bl * nl // tm
    tokens = (ctx_tiles + lat_tiles) * tm
    last_ctx = ctx_tiles - 1
    ctx_idx = lambda i: (jnp.minimum(i, last_ctx), 0)
    lat_idx = lambda i: (jnp.maximum(i - ctx_tiles, 0), 0)
    mod_idx = lambda i: (jnp.where(i < ctx_tiles, 0, 1 + jnp.maximum(i - ctx_tiles, 0) // lat_tiles_per_seq), 0, 0)
    tok = lambda i: (i, 0)
    cache_spec = pl.BlockSpec((seqs_per_tile, None, A_HEADS, HEAD_DIM, nc_),
                              lambda i: (jnp.minimum(i, last_ctx), layer, 0, 0, 0))
    cache_shape = jax.ShapeDtypeStruct(cache[0].shape, F32)
    return pl.pallas_call(
        functools.partial(_inproj_kernel, ctx_tiles=ctx_tiles),
        out_shape=(
            jax.ShapeDtypeStruct((tokens, PA_COLS), F32),
            jax.ShapeDtypeStruct((tokens, PR_COLS), F32),
            jax.ShapeDtypeStruct((tokens, PF_COLS), F32),
            cache_shape, cache_shape,
        ),
        grid=(ctx_tiles + lat_tiles,),
        in_specs=[
            pl.BlockSpec((tm, D_MODEL), ctx_idx),
            pl.BlockSpec((tm, D_MODEL), lat_idx),
            pl.BlockSpec((None, 3, D_MODEL), mod_idx),
            pl.BlockSpec((None, 1, D_MODEL), lambda i: (layer, 0, 0)),
            pl.BlockSpec((None, D_MODEL, IN_COLS), lambda i: (layer, 0, 0), pipeline_mode=pl.Buffered(1)),
            pl.BlockSpec(memory_space=pl.ANY),
            pl.BlockSpec(memory_space=pl.ANY),
        ],
        out_specs=(
            pl.BlockSpec((tm, PA_COLS), tok),
            pl.BlockSpec((tm, PR_COLS), tok),
            pl.BlockSpec((tm, PF_COLS), tok),
            cache_spec, cache_spec,
        ),
        scratch_shapes=[pltpu.VMEM((D_MODEL, IN_COLS), BF16)],
        input_output_aliases={5: 3, 6: 4},
        compiler_params=_cparams(("arbitrary",)),
        name="in_projection",
    )(x_ctx.reshape(bc * nc_, D_MODEL), x_lat.reshape(bl * nl, D_MODEL), mod,
      g_pre.reshape(DEPTH, 1, D_MODEL), w_in, cache[0], cache[1])


def _lane_lo():
    return lax.broadcasted_iota(jnp.int32, (1, LANES), 1) < HEAD_DIM


def _ctx_attn_kernel(pa_ref, o_ref, s_scr):
    lo = _lane_lo()
    npair = A_HEADS // 2

    def scores(p):
        c = p * LANES
        q2 = pa_ref[:, c:c + LANES] * (HEAD_DIM ** -0.5)
        k2 = pa_ref[:, A_WIDTH + c:A_WIDTH + c + LANES].astype(BF16)
        for hh in range(2):
            sel = lo if hh == 0 else jnp.logical_not(lo)
            s_scr[p % 2, hh] = _dot_nt(jnp.where(sel, q2, 0.0).astype(BF16), k2)

    def finish(p):
        c = p * LANES
        v2 = pa_ref[:, 2 * A_WIDTH + c:2 * A_WIDTH + c + LANES].astype(BF16)
        outs = []
        for hh in range(2):
            s = s_scr[p % 2, hh]
            e = jnp.exp(s - jnp.max(s, axis=-1, keepdims=True))
            inv = 1.0 / jnp.sum(e, axis=-1, keepdims=True)
            outs.append(_dot(e.astype(BF16), v2) * inv)
        o2 = jnp.where(lo, outs[0], outs[1])
        o_ref[:, c:c + LANES] = (o2 * _silu(pa_ref[:, 3 * A_WIDTH + c:3 * A_WIDTH + c + LANES])).astype(BF16)

    scores(0)
    for p in range(npair):
        if p + 1 < npair:
            scores(p + 1)
        finish(p)


def _context_attention(pa, seqs):
    b0, b = seqs
    n = pa.shape[1]
    return pl.pallas_call(
        _ctx_attn_kernel,
        out_shape=jax.ShapeDtypeStruct((b, n, A_WIDTH), BF16),
        grid=(b,),
        in_specs=[pl.BlockSpec((None, n, PA_COLS), lambda i: (b0 + i, 0, 0))],
        out_specs=pl.BlockSpec((None, n, A_WIDTH), lambda i: (i, 0, 0)),
        scratch_shapes=[pltpu.VMEM((2, 2, n, n), F32)],
        compiler_params=_cparams(("arbitrary",)),
        name="context_attention",
    )(pa)


QROWS = 4


def _nbr_blocks(rows):
    kh = min(WIN_H, rows)
    out = []
    for r_first in range(0, rows, QROWS):
        r0s = [min(max(r - kh // 2, 0), rows - kh) for r in range(r_first, r_first + QROWS)]
        lo, hi = min(r0s), max(r0s) + kh
        lo -= lo % 2
        span = hi - lo
        span += (-span) % 4
        if lo + span > rows:
            lo = rows - span
        assert lo >= 0 and lo % 2 == 0
        out.append((r_first, lo, span, r0s))
    return out, kh


N_REL_ROWS = 2 * WIN_H - 1
TABLE_W = 2 * WIN_H * GRID_W


def _build_bias_tables(base_ref, te_scr, to_scr, heads):
    lane = lax.broadcasted_iota(jnp.int32, (GRID_W, LANES), 1)
    qc = lax.broadcasted_iota(jnp.int32, (GRID_W, LANES), 0)
    kc = lane % GRID_W
    lo = lane < GRID_W
    c0 = jnp.clip(qc - WIN_W // 2, 0, GRID_W - WIN_W)
    col_in = (kc >= c0) & (kc < c0 + WIN_W)
    neg = jnp.full((GRID_W, LANES), NEG_INF, F32)

    for h in heads:
        def tile(i, lane_off):
            row = jnp.broadcast_to(base_ref[h, i:i + 1, :], (GRID_W, LANES))
            return pltpu.roll(row, lane_off, 1, stride=1, stride_axis=0)

        for j in range(WIN_H):
            for scr, ia in ((te_scr, 2 * j), (to_scr, 2 * j + 1)):
                ib = ia + 1
                if ia >= N_REL_ROWS:
                    pair = neg
                elif ib >= N_REL_ROWS:
                    pair = jnp.where(col_in & lo, tile(ia, 0), NEG_INF)
                else:
                    pair = jnp.where(col_in, jnp.where(lo, tile(ia, 0), tile(ib, GRID_W)), NEG_INF)
                scr[h, :, j * LANES:(j + 1) * LANES] = pair


def _nbr_attn_kernel(q_ref, k_ref, v_ref, g_ref, kc_ref, vc_ref, base_ref, o_ref, s_scr, p_scr, te_scr, to_scr, *, rows):
    p = pl.program_id(1)

    @pl.when(pl.program_id(0) == 0)
    def _():
        _build_bias_tables(base_ref, te_scr, to_scr, (2 * p, 2 * p + 1))

    lo = _lane_lo()
    hi = jnp.logical_not(lo)
    blocks, kh = _nbr_blocks(rows)
    nq = QROWS * GRID_W
    past = kc_ref.shape[-1]
    ctx0 = s_scr.shape[-1] - past
    kct = kc_ref[...].reshape(LANES, past).astype(BF16)
    vct = vc_ref[...].reshape(LANES, past).astype(BF16)
    ctx_tiles = [slice(ctx0 + t * LANES, ctx0 + (t + 1) * LANES) for t in range(past // LANES)]

    def window_tiles(bi, rl, head):
        r_first, u0, span, r0s = blocks[bi]
        r, r0 = r_first + rl, r0s[rl]
        out = []
        for t in range(span // 2):
            kr0 = u0 + 2 * t
            cs = slice(t * LANES, (t + 1) * LANES)
            ok0 = r0 <= kr0 < r0 + kh
            ok1 = r0 <= kr0 + 1 < r0 + kh
            if not (ok0 or ok1):
                out.append((cs, None))
                continue
            i0 = kr0 - r + WIN_H - 1
            if i0 % 2 == 0:
                assert 0 <= i0 and i0 + 2 <= 2 * WIN_H
                bias = te_scr[head, :, i0 * GRID_W:(i0 + 2) * GRID_W]
            else:
                assert 1 <= i0 and i0 + 1 <= 2 * WIN_H - 1
                bias = to_scr[head, :, (i0 - 1) * GRID_W:(i0 + 1) * GRID_W]
            if not ok0:
                bias = jnp.where(lo, NEG_INF, bias)
            if not ok1:
                bias = jnp.where(hi, NEG_INF, bias)
            out.append((cs, bias))
        return out

    def scores(bi):
        r_first, u0, span, _ = blocks[bi]
        qs, ks, nk = r_first * GRID_W, u0 * GRID_W, span * GRID_W
        q2 = q_ref[qs:qs + nq, :] * (HEAD_DIM ** -0.5)
        ku = k_ref[ks:ks + nk, :].astype(BF16)
        for hh in range(2):
            qh = jnp.where(lo if hh == 0 else hi, q2, 0.0).astype(BF16)
            s_scr[bi % 2, hh, :, 0:nk] = _dot_nt(qh, ku)
            s_scr[bi % 2, hh, :, ctx0:ctx0 + past] = _dot(qh, kct)

    def softmax(bi):
        par = bi % 2
        inv = []
        for hh in range(2):
            parts = []
            for rl in range(QROWS):
                rs = slice(rl * GRID_W, (rl + 1) * GRID_W)
                tiles = window_tiles(bi, rl, 2 * p + hh)
                m_acc = None
                for cs, bias in tiles:
                    if bias is None:
                        continue
                    s = s_scr[par, hh, rs, cs] + bias
                    s_scr[par, hh, rs, cs] = s
                    m_acc = s if m_acc is None else jnp.maximum(m_acc, s)
                for cs in ctx_tiles:
                    m_acc = jnp.maximum(m_acc, s_scr[par, hh, rs, cs])
                m = jnp.max(m_acc, axis=-1, keepdims=True)
                l_acc = None
                for cs, bias in tiles:
                    if bias is None:
                        p_scr[par, hh, rs, cs] = jnp.zeros((GRID_W, LANES), BF16)
                        continue
                    e = jnp.exp(s_scr[par, hh, rs, cs] - m)
                    p_scr[par, hh, rs, cs] = e.astype(BF16)
                    l_acc = e if l_acc is None else l_acc + e
                for cs in ctx_tiles:
                    e = jnp.exp(s_scr[par, hh, rs, cs] - m)
                    p_scr[par, hh, rs, cs] = e.astype(BF16)
                    l_acc = l_acc + e
                parts.append(1.0 / jnp.sum(l_acc, axis=-1, keepdims=True))
            inv.append(jnp.concatenate(parts, axis=0))
        return inv

    def values(bi, inv):
        r_first, u0, span, _ = blocks[bi]
        qs, ks, nk = r_first * GRID_W, u0 * GRID_W, span * GRID_W
        vu = v_ref[ks:ks + nk, :].astype(BF16)
        outs = []
        for hh in range(2):
            o = _dot(p_scr[bi % 2, hh, :, 0:nk], vu) + _dot_nt(p_scr[bi % 2, hh, :, ctx0:ctx0 + past], vct)
            outs.append(o * inv[hh])
        o2 = jnp.where(lo, outs[0], outs[1])
        o_ref[qs:qs + nq, :] = (o2 * _silu(g_ref[qs:qs + nq, :])).astype(BF16)

    scores(0)
    for bi in range(len(blocks)):
        if bi + 1 < len(blocks):
            scores(bi + 1)
        values(bi, softmax(bi))


def _bias_base(rpb_l):
    mid = WIN_W - 1
    zeros = jnp.zeros(rpb_l.shape[:2] + (LANES - (2 * WIN_W - 1),), F32)
    base = jnp.concatenate([rpb_l[..., mid:], zeros, rpb_l[..., :mid]], axis=-1)
    return jnp.pad(base, ((0, 0), (0, 2 * WIN_H - N_REL_ROWS), (0, 0)))


def _neighbourhood_attention(pa, seqs, cache_kt, cache_vt, layer, rpb_l):
    b0, b = seqs
    n = pa.shape[1]
    rows = n // GRID_W
    past = cache_kt.shape[-1]
    blocks, _ = _nbr_blocks(rows)
    max_nk = max(s for (_, _, s, _) in blocks) * GRID_W
    npair = A_HEADS // 2
    col = lambda off: (lambda i, p: (b0 + i, 0, off + p))
    cache_spec = pl.BlockSpec((None, None, 2, HEAD_DIM, past), lambda i, p: (i, layer, p, 0, 0))
    return pl.pallas_call(
        functools.partial(_nbr_attn_kernel, rows=rows),
        out_shape=jax.ShapeDtypeStruct((b, n, A_WIDTH), BF16),
        grid=(b, npair),
        in_specs=[
            pl.BlockSpec((None, n, LANES), col(0)),
            pl.BlockSpec((None, n, LANES), col(npair)),
            pl.BlockSpec((None, n, LANES), col(2 * npair)),
            pl.BlockSpec((None, n, LANES), col(3 * npair)),
            cache_spec,
            cache_spec,
            pl.BlockSpec((A_HEADS, 2 * WIN_H, LANES), lambda i, p: (0, 0, 0)),
        ],
        out_specs=pl.BlockSpec((None, n, LANES), lambda i, p: (i, 0, p)),
        scratch_shapes=[
            pltpu.VMEM((2, 2, QROWS * GRID_W, max_nk + past), F32),
            pltpu.VMEM((2, 2, QROWS * GRID_W, max_nk + past), BF16),
            pltpu.VMEM((A_HEADS, GRID_W, TABLE_W), F32),
            pltpu.VMEM((A_HEADS, GRID_W, TABLE_W), F32),
        ],
        compiler_params=_cparams(("arbitrary", "arbitrary")),
        name="neighbourhood_attention",
    )(pa, pa, pa, pa, cache_kt, cache_vt, _bias_base(rpb_l))


COARSE_HALVES = (32, 16, 8)
FINE_HALVES = (4, 2, 1)
N_LEVELS = len(COARSE_HALVES) + len(FINE_HALVES)
ANCHOR_BLOCK = CHUNK
ANCHOR_LEVELS = 0
ANCHOR_MAX_EXPONENT = 80.0
MASK_DIAG = N_LEVELS
MASK_ANCHOR = N_LEVELS + 1
GROUP = 4


def _hgrn_constants():
    c = CHUNK
    idx = np.arange(c)
    mats = [np.tril(np.ones((c, c)))]
    masks = []
    for h in COARSE_HALVES + FINE_HALVES:
        blk = idx // (2 * h)
        mid = blk * 2 * h + h - 1
        upper = idx > mid
        if h in FINE_HALVES:
            m = np.zeros((c, c))
            for i in range(c):
                if upper[i]:
                    m[i, mid[i] + 1:i + 1] = 1.0
                else:
                    m[i, i + 1:mid[i] + 1] = 1.0
            mats.append(m)
        same = blk[:, None] == blk[None, :]
        masks.append((same & upper[:, None] & (~upper)[None, :]).astype(np.float64))
    masks.append(np.eye(c))
    same_block = (idx[:, None] // ANCHOR_BLOCK) == (idx[None, :] // ANCHOR_BLOCK)
    masks.append((same_block & (idx[None, :] <= idx[:, None])).astype(np.float64))
    fwd = np.concatenate(mats, axis=0)
    bwd = np.concatenate([m[::-1, ::-1] for m in mats], axis=0)
    mk_f = np.stack([np.tile(m, (1, R_HEADS)) for m in masks])
    mk_b = np.stack([np.tile(m[::-1, ::-1], (1, R_HEADS)) for m in masks])
    hid = np.arange(R_WIDTH) // R_DIM
    bd = (hid[:, None] == hid[None, :]).astype(np.float64)
    return (jnp.asarray(np.stack([fwd, bwd]), BF16), jnp.asarray(np.stack([mk_f, mk_b]), F32),
            jnp.asarray(bd, BF16))


def _hgrn_kernel(*refs, nc, has_init):
    pr_ref, lbp_ref, gh_ref, mstk_ref, lmask_ref, bd_ref = refs[:6]
    if has_init:
        s0_ref, o_ref = refs[6:8]
        sfin_ref = None
    else:
        o_ref, sfin_ref = refs[6:8]
        s0_ref = None
    q_scr, k_scr, g_scr, gc_scr, qg_scr, upd_scr, dec_scr, sbd_scr = refs[8:]
    w = R_WIDTH
    c = CHUNK
    bd = bd_ref[...]
    hid = lax.broadcasted_iota(jnp.int32, (1, w), 1) // R_DIM

    q_scr[...] = _silu(pr_ref[:, 0:w])
    for d in range(2):
        z = pr_ref[:, (1 + d) * w:(2 + d) * w]
        e = jnp.exp(-jnp.abs(z))
        r = 1.0 / (1.0 + e)
        pos = z >= 0.0
        one_m_lb = lbp_ref[2, d:d + 1, :]
        f = lbp_ref[0, d:d + 1, :] + one_m_lb * (jnp.where(pos, 1.0, e) * r)
        log_f = jnp.where(f > 0.0, jnp.log(f), lbp_ref[1, d:d + 1, :] + z)
        g_hi, g_lo = _split2(log_f)
        g_scr[d, :, 0:w] = g_hi
        g_scr[d, :, w:2 * w] = g_lo
        k_scr[d] = one_m_lb * (jnp.where(pos, e, 1.0) * r)

    def tile4(x):
        return jnp.concatenate([x] * R_HEADS, axis=0)

    def head_diagonal(full):
        out = full[(R_HEADS - 1) * R_DIM:R_HEADS * R_DIM]
        for h in range(R_HEADS - 2, -1, -1):
            out = jnp.where(hid == h, full[h * R_DIM:(h + 1) * R_DIM], out)
        return out

    def chunk_rows(ci):
        return pl.ds(pl.multiple_of(ci * c, c), c)

    def cumulative(d, rs, n_mats):
        s = _dot(mstk_ref[d, 0:n_mats * c, :], g_scr[d, rs, :])
        return s[:, 0:w] + s[:, w:2 * w]

    def level_exponents(gcum, d, halves):
        out = []
        for h in halves:
            parts = []
            for s0 in range(0, c, 2 * h):
                anchor = s0 + h - 1 + d
                parts.append(-jnp.abs(gcum[s0:s0 + 2 * h] - gcum[anchor:anchor + 1]))
            out.append(parts[0] if len(parts) == 1 else jnp.concatenate(parts, axis=0))
        return out

    def anchor_shift(gcum, d):
        parts = []
        for s0 in range(0, c, ANCHOR_BLOCK):
            anchor = s0 + ANCHOR_BLOCK // 2 - 1 + d
            parts.append(gcum[s0:s0 + ANCHOR_BLOCK] - gcum[anchor:anchor + 1])
        return jnp.concatenate(parts, axis=0)

    groups = nc // GROUP
    pairs = [(j, d) for j in range(GROUP) for d in range(2)]

    def increment_group(gi, worst):
        rows = [chunk_rows(gi * GROUP + j) for j in range(GROUP)]
        gcums = [cumulative(d, rows[j], 1) for j, d in pairs]
        lasts = [g[c - 1:c, :] if d == 0 else g[0:1, :] for g, (j, d) in zip(gcums, pairs)]
        kls = [(k_scr[d, rows[j], :] * jnp.exp(last - g)).astype(BF16)
               for g, last, (j, d) in zip(gcums, lasts, pairs)]
        vs = [pr_ref[rows[j], 3 * w:4 * w].astype(BF16) for j in range(GROUP)]
        upds = [_dot_tn(vs[j], kl) for kl, (j, d) in zip(kls, pairs)]
        for g, last, upd, (j, d) in zip(gcums, lasts, upds, pairs):
            ci = gi * GROUP + j
            gc_scr[d, rows[j], :] = g
            upd_scr[d, ci] = head_diagonal(upd)
            dec_scr[d, ci] = jnp.broadcast_to(jnp.exp(last), (8, w))
            qg_scr[d, rows[j], :] = (q_scr[rows[j], :] * jnp.exp(g)).astype(BF16)
            sh = jnp.abs(anchor_shift(g, d))
            for s0 in range(0, c, 8):
                worst = jnp.maximum(worst, sh[s0:s0 + 8])
        return worst

    worst = lax.fori_loop(0, groups, increment_group, jnp.zeros((8, w), F32))
    q_max = jnp.max(jnp.max(jnp.abs(q_scr[...]), axis=0, keepdims=True), axis=1, keepdims=True)
    anchor_ok = jnp.max(worst + jnp.log(jnp.maximum(q_max, 1.0))) < ANCHOR_MAX_EXPONENT

    def scan_step(ci, carry):
        sf, sb = carry
        cb = nc - 1 - ci
        sbd_scr[0, ci] = tile4(sf.astype(BF16)) * bd
        sbd_scr[1, cb] = tile4(sb.astype(BF16)) * bd
        sf = sf * dec_scr[0, ci][0:1, :] + upd_scr[0, ci]
        sb = sb * dec_scr[1, cb][0:1, :] + upd_scr[1, cb]
        return sf, sb

    if has_init:
        init = tuple(jnp.concatenate([s0_ref[d], jnp.zeros((w, LANES - R_DIM), F32)], axis=1).T[0:R_DIM]
                     for d in range(2))
    else:
        init = (jnp.zeros((R_DIM, w), F32), jnp.zeros((R_DIM, w), F32))
    finals = lax.fori_loop(0, nc, scan_step, init)
    if sfin_ref is not None:
        for d in range(2):
            padded = jnp.concatenate([finals[d], jnp.zeros((LANES - R_DIM, w), F32)], axis=0)
            sfin_ref[d] = padded.T[:, 0:R_DIM]

    def finish(rs, o):
        x2_hi, x2_lo = _split2(o * o)
        ms = (_dot(x2_hi, bd) + _dot(x2_lo, bd)) * (1.0 / R_DIM)
        y = o * lax.rsqrt(ms + EPS) * gh_ref[...]
        o_ref[rs, :] = (y * _silu(pr_ref[rs, 4 * w:5 * w])).astype(BF16)

    def readout_group_anchor(gi, carry):
        rows = [chunk_rows(gi * GROUP + j) for j in range(GROUP)]
        qs = [q_scr[rows[j], :] for j in range(GROUP)]
        gcums = [gc_scr[d, rows[j], :] for j, d in pairs]
        factors = []
        for g, (j, d) in zip(gcums, pairs):
            es = [jnp.exp(x) for x in level_exponents(g, d, COARSE_HALVES[:ANCHOR_LEVELS])]
            sh = anchor_shift(g, d)
            fs = [(e, e, lv) for lv, e in enumerate(es)]
            fs.append((jnp.exp(sh), jnp.exp(-sh), MASK_ANCHOR))
            factors.append(fs)
        kbds = [tile4(k_scr[d, rows[j], :].astype(BF16)) * bd for j, d in pairs]
        accs = [None] * len(pairs)
        for lv in range(ANCHOR_LEVELS + 1):
            for i, (j, d) in enumerate(pairs):
                eq, ek, mask = factors[i][lv]
                part = jnp.where(lmask_ref[d, mask] > 0.5,
                                 _dot_nt((qs[j] * eq).astype(BF16), kbds[i] * tile4(ek.astype(BF16))), 0.0)
                accs[i] = part if accs[i] is None else accs[i] + part
        vbds = [tile4(pr_ref[rows[j], 3 * w:4 * w].astype(BF16)) * bd for j in range(GROUP)]
        outs = [None] * GROUP
        for i, (j, d) in enumerate(pairs):
            ci = gi * GROUP + j
            od = _dot(accs[i].astype(BF16), vbds[j]) + _dot_nt(qg_scr[d, rows[j], :], sbd_scr[d, ci])
            outs[j] = od if outs[j] is None else outs[j] + od
        for j in range(GROUP):
            finish(rows[j], outs[j])
        return carry

    def readout_step_split(ci, carry):
        rs = chunk_rows(ci)
        q = q_scr[rs, :]
        qb = q.astype(BF16)
        vbd = tile4(pr_ref[rs, 3 * w:4 * w].astype(BF16)) * bd
        dsts = [cumulative(d, rs, 1 + len(FINE_HALVES)) for d in range(2)]
        exps = [level_exponents(dsts[d][0:c], d, COARSE_HALVES)
                + [dsts[d][(1 + j) * c:(2 + j) * c] for j in range(len(FINE_HALVES))] for d in range(2)]
        kbd = [tile4(k_scr[d, rs, :].astype(BF16)) * bd for d in range(2)]
        a = [_dot_nt(qb, kbd[d]) * lmask_ref[d, MASK_DIAG] for d in range(2)]
        for j in range(N_LEVELS):
            for d in range(2):
                e = jnp.exp(exps[d][j])
                a[d] = a[d] + _dot_nt((q * e).astype(BF16), kbd[d] * tile4(e.astype(BF16))) * lmask_ref[d, j]
        o = None
        for d in range(2):
            od = _dot(a[d].astype(BF16), vbd) + _dot_nt(qg_scr[d, rs, :], sbd_scr[d, ci])
            o = od if o is None else o + od
        finish(rs, o)
        return carry

    @pl.when(anchor_ok)
    def _():
        lax.fori_loop(0, groups, readout_group_anchor, 0)

    @pl.when(jnp.logical_not(anchor_ok))
    def _():
        lax.fori_loop(0, nc, readout_step_split, 0)


def _hgrn(pr, seqs, lbp_l, g_hgrn_l, consts, state=None, layer=None):
    b0, b = seqs
    n = pr.shape[1]
    nc = n // CHUNK
    mstk, lmask, bd = consts
    w = R_WIDTH
    has_init = state is not None
    full = lambda *shape: pl.BlockSpec(shape, lambda i: (0,) * len(shape))
    in_specs = [
        pl.BlockSpec((None, n, PR_COLS), lambda i: (b0 + i, 0, 0)),
        full(3, 2, w),
        full(1, w),
        full(*mstk.shape),
        full(*lmask.shape),
        full(w, w),
    ]
    args = [pr, lbp_l, g_hgrn_l.reshape(1, w), mstk, lmask, bd]
    out_shape = [jax.ShapeDtypeStruct((b, n, w), BF16)]
    out_specs = [pl.BlockSpec((None, n, w), lambda i: (i, 0, 0))]
    if has_init:
        in_specs.append(pl.BlockSpec((None, None, 2, w, R_DIM), lambda i: (i, layer, 0, 0, 0)))
        args.append(state)
    else:
        out_shape.append(jax.ShapeDtypeStruct((b, 2, w, R_DIM), F32))
        out_specs.append(pl.BlockSpec((None, 2, w, R_DIM), lambda i: (i, 0, 0, 0)))
    return pl.pallas_call(
        functools.partial(_hgrn_kernel, nc=nc, has_init=has_init),
        out_shape=tuple(out_shape),
        grid=(b,),
        in_specs=in_specs,
        out_specs=tuple(out_specs),
        scratch_shapes=[
            pltpu.VMEM((n, w), F32),
            pltpu.VMEM((2, n, w), F32),
            pltpu.VMEM((2, n, 2 * w), BF16),
            pltpu.VMEM((2, n, w), F32),
            pltpu.VMEM((2, n, w), BF16),
            pltpu.VMEM((2, nc, R_DIM, w), F32),
            pltpu.VMEM((2, nc, 8, w), F32),
            pltpu.VMEM((2, nc, w, w), BF16),
        ],
        compiler_params=_cparams(("arbitrary",)),
        name="hgrn_scan",
    )(*args)


def _fnet_constants(n):
    j = np.arange(F_GROUP_DIM)
    ang = 2.0 * np.pi * ((j[:, None] * j[None, :]) % F_GROUP_DIM) / F_GROUP_DIM
    eye = np.eye(F_GROUPS)
    cs = np.concatenate([np.kron(eye, np.cos(ang)), np.kron(eye, np.sin(ang))], axis=1)
    t = np.arange(n)
    angn = 2.0 * np.pi * ((t[:, None] * t[None, :]) % n) / n
    return tuple(jnp.asarray(m, F32).astype(BF16) for m in (cs, np.cos(angn), np.sin(angn)))


def _fnet_kernel(pf_ref, cs_ref, cn_ref, sn_ref, wf_ref, o_ref, *, scale):
    w = F_WIDTH
    t = _dot(pf_ref[:, 0:w].astype(BF16), cs_ref[...])
    y = (_dot(cn_ref[...], t[:, 0:w].astype(BF16)) - _dot(sn_ref[...], t[:, w:2 * w].astype(BF16))) * scale
    of = _dot(y.astype(BF16), wf_ref[...])
    o_ref[...] = (of * _silu(pf_ref[:, w:2 * w])).astype(BF16)


def _fourier(pf, seqs, consts, w_fnet_bf16):
    b0, b = seqs
    n = pf.shape[1]
    cs, cn, sn = consts
    w = F_WIDTH
    full = lambda *shape: pl.BlockSpec(shape, lambda i: (0,) * len(shape))
    return pl.pallas_call(
        functools.partial(_fnet_kernel, scale=float((n * F_GROUP_DIM) ** -0.5)),
        out_shape=jax.ShapeDtypeStruct((b, n, w), BF16),
        grid=(b,),
        in_specs=[pl.BlockSpec((None, n, PF_COLS), lambda i: (b0 + i, 0, 0)),
                  full(w, 2 * w), full(n, n), full(n, n), full(w, w)],
        out_specs=pl.BlockSpec((None, n, w), lambda i: (i, 0, 0)),
        compiler_params=_cparams(("arbitrary",)),
        name="fourier_mixing",
    )(pf, cs, cn, sn, w_fnet_bf16)


def _outproj_kernel(mac_ref, mrc_ref, mfc_ref, mal_ref, mrl_ref, mfl_ref, xc_ref, xl_ref, mod_ref, g_ref, w_ref,
                    yc_ref, yl_ref, wb_scr, *, ctx_tiles):
    i = pl.program_id(0)

    @pl.when(i == 0)
    def _():
        wb_scr[...] = w_ref[...].astype(BF16)

    is_ctx = i < ctx_tiles
    pick = lambda c_ref, l_ref: jnp.where(is_ctx, c_ref[...], l_ref[...])
    out = (_dot(pick(mac_ref, mal_ref), wb_scr[0:A_WIDTH, :])
           + _dot(pick(mrc_ref, mrl_ref), wb_scr[A_WIDTH:A_WIDTH + R_WIDTH, :])
           + _dot(pick(mfc_ref, mfl_ref), wb_scr[A_WIDTH + R_WIDTH:D_MODEL, :]))
    ms = jnp.mean(out * out, axis=-1, keepdims=True)
    y = pick(xc_ref, xl_ref) + mod_ref[2:3, :] * (out * lax.rsqrt(ms + EPS) * g_ref[...])

    @pl.when(is_ctx)
    def _():
        yc_ref[...] = y

    @pl.when(jnp.logical_not(is_ctx))
    def _():
        yl_ref[...] = y


def _out_projection(mixed_ctx, mixed_lat, x_ctx, x_lat, mod, g_post, w_out, layer):
    bc, nc_, _ = x_ctx.shape
    bl, nl, _ = x_lat.shape
    tm = IN_TILE
    lat_tiles_per_seq = nl // tm
    ctx_tiles = bc * nc_ // tm
    lat_tiles = bl * nl // tm
    last_ctx = ctx_tiles - 1
    ctx_idx = lambda i: (jnp.minimum(i, last_ctx), 0)
    lat_idx = lambda i: (jnp.maximum(i - ctx_tiles, 0), 0)
    mod_idx = lambda i: (jnp.where(i < ctx_tiles, 0, 1 + jnp.maximum(i - ctx_tiles, 0) // lat_tiles_per_seq), 0, 0)
    widths = (A_WIDTH, R_WIDTH, F_WIDTH)
    flat = lambda t: t.reshape(-1, t.shape[-1])
    yc, yl = pl.pallas_call(
        functools.partial(_outproj_kernel, ctx_tiles=ctx_tiles),
        out_shape=(jax.ShapeDtypeStruct((bc * nc_, D_MODEL), F32), jax.ShapeDtypeStruct((bl * nl, D_MODEL), F32)),
        grid=(ctx_tiles + lat_tiles,),
        in_specs=(
            [pl.BlockSpec((tm, wd), ctx_idx) for wd in widths]
            + [pl.BlockSpec((tm, wd), lat_idx) for wd in widths]
            + [
                pl.BlockSpec((tm, D_MODEL), ctx_idx),
                pl.BlockSpec((tm, D_MODEL), lat_idx),
                pl.BlockSpec((None, 3, D_MODEL), mod_idx),
                pl.BlockSpec((None, 1, D_MODEL), lambda i: (layer, 0, 0)),
                pl.BlockSpec((None, D_MODEL, D_MODEL), lambda i: (layer, 0, 0), pipeline_mode=pl.Buffered(1)),
            ]),
        out_specs=(pl.BlockSpec((tm, D_MODEL), ctx_idx), pl.BlockSpec((tm, D_MODEL), lat_idx)),
        scratch_shapes=[pltpu.VMEM((D_MODEL, D_MODEL), BF16)],
        compiler_params=_cparams(("arbitrary",)),
        name="out_projection",
    )(*[flat(t) for t in mixed_ctx], *[flat(t) for t in mixed_lat], flat(x_ctx), flat(x_lat), mod,
      g_post.reshape(DEPTH, 1, D_MODEL), w_out)
    return yc.reshape(x_ctx.shape), yl.reshape(x_lat.shape)


def kernel(x_prompt, x_sample, cache_attn_k, cache_attn_v, state_hgrn, c, c_ctx,
           w_ada, b_ada, g_pre, w_in, rpb, lb_logits, g_hgrn, w_fnet, w_out, g_post):
    nb_ctx, n_ctx, _ = x_prompt.shape
    nb_lat, n_lat, _ = x_sample.shape

    pad_rows = (-(1 + nb_lat)) % 8
    cc = jnp.concatenate([c_ctx[None, :], c, jnp.zeros((pad_rows, D_MODEL), F32)], axis=0)
    mods = _modulations(cc, w_ada, b_ada)
    lbp = _lower_bounds(lb_logits)

    w_fnet_b = w_fnet.astype(BF16)
    hconsts = _hgrn_constants()
    fconsts_ctx = _fnet_constants(n_ctx)
    fconsts_lat = _fnet_constants(n_lat)

    state_rows = state_hgrn.reshape(nb_lat, DEPTH, 2, R_WIDTH, R_DIM)
    cache_kt = jnp.transpose(cache_attn_k, (0, 1, 3, 4, 2))
    cache_vt = jnp.transpose(cache_attn_v, (0, 1, 3, 4, 2))

    yp, ys = x_prompt, x_sample
    new_kv = [jnp.zeros((nb_ctx, DEPTH, A_HEADS, HEAD_DIM, n_ctx), F32) for _ in range(2)]
    new_s = []
    ctx_tokens = nb_ctx * n_ctx
    ctx_seqs = (0, nb_ctx)
    lat_seqs = (ctx_tokens // n_lat, nb_lat)
    for l in range(DEPTH):
        lbp_l = lbp[:, l]
        mod_all = mods[l, 0:1 + nb_lat].reshape(1 + nb_lat, 3, D_MODEL)
        pa, pr, pf, *new_kv = _in_projection(yp, ys, mod_all, g_pre, w_in, l, new_kv)
        as_ctx = lambda t: t.reshape(-1, n_ctx, t.shape[-1])
        ma_c = _context_attention(as_ctx(pa), ctx_seqs)
        mr_c, sfin = _hgrn(as_ctx(pr), ctx_seqs, lbp_l, g_hgrn[l], hconsts)
        new_s.append(sfin)
        mf_c = _fourier(as_ctx(pf), ctx_seqs, fconsts_ctx, w_fnet_b[l])
        as_lat = lambda t: t.reshape(-1, n_lat, t.shape[-1])
        ma_l = _neighbourhood_attention(as_lat(pa), lat_seqs, cache_kt, cache_vt, l, rpb[l])
        (mr_l,) = _hgrn(as_lat(pr), lat_seqs, lbp_l, g_hgrn[l], hconsts, state=state_rows, layer=l)
        mf_l = _fourier(as_lat(pf), lat_seqs, fconsts_lat, w_fnet_b[l])
        yp, ys = _out_projection((ma_c, mr_c, mf_c), (ma_l, mr_l, mf_l), yp, ys, mod_all, g_post, w_out, l)

    new_state = jnp.stack(new_s, axis=1).reshape(nb_ctx, DEPTH, 2, R_HEADS, R_DIM, R_DIM)
    new_k, new_v = (jnp.transpose(t, (0, 1, 4, 2, 3)) for t in new_kv)
    return (yp, ys, new_k, new_v, new_state)
```

```python
import functools

import numpy as np
import jax
import jax.numpy as jnp
from jax import lax
from jax.experimental import pallas as pl
from jax.experimental.pallas import tpu as pltpu

F32 = jnp.float32
BF16 = jnp.bfloat16

D_MODEL = 1024
DEPTH = 4
GRID_W = 64
WIN_H = 8
WIN_W = 16
HEAD_DIM = 64
A_HEADS = 8
A_WIDTH = A_HEADS * HEAD_DIM
R_HEADS = 4
R_DIM = 64
R_WIDTH = R_HEADS * R_DIM
F_GROUPS = 4
F_GROUP_DIM = 64
F_WIDTH = F_GROUPS * F_GROUP_DIM
PA_COLS = 4 * A_WIDTH
PR_COLS = 5 * R_WIDTH
PF_COLS = 2 * F_WIDTH
IN_COLS = PA_COLS + PR_COLS + PF_COLS
CHUNK = 64
EPS = 1e-6
LANES = 128
NEG_INF = float("-inf")
VMEM_LIMIT = 56 * 1024 * 1024


def _cparams(sem):
    return pltpu.CompilerParams(dimension_semantics=sem, vmem_limit_bytes=VMEM_LIMIT)


def _silu(x):
    return x * (1.0 / (1.0 + jnp.exp(-x)))


def _dot(a, b):
    return jnp.dot(a, b, preferred_element_type=F32)


def _dot_nt(a, b):
    return lax.dot_general(a, b, (((1,), (1,)), ((), ())), preferred_element_type=F32)


def _dot_tn(a, b):
    return lax.dot_general(a, b, (((0,), (0,)), ((), ())), preferred_element_type=F32)


def _split2(x):
    hi = x.astype(BF16)
    lo = (x - hi.astype(F32)).astype(BF16)
    return hi, lo


def _mod_kernel(cc_ref, w_ref, b_ref, o_ref):
    a_hi, a_lo = _split2(_silu(cc_ref[...]))
    w_hi, w_lo = _split2(w_ref[...])
    acc = _dot(a_hi, w_hi) + _dot(a_hi, w_lo) + _dot(a_lo, w_hi)
    o_ref[...] = acc + b_ref[...]


def _modulations(cc, w_ada, b_ada):
    rows = cc.shape[0]
    tn = 3 * D_MODEL // 2
    return pl.pallas_call(
        _mod_kernel,
        out_shape=jax.ShapeDtypeStruct((DEPTH, rows, 3 * D_MODEL), F32),
        grid=(DEPTH, 3 * D_MODEL // tn),
        in_specs=[
            pl.BlockSpec((rows, D_MODEL), lambda l, j: (0, 0)),
            pl.BlockSpec((None, D_MODEL, tn), lambda l, j: (l, 0, j)),
            pl.BlockSpec((None, 1, tn), lambda l, j: (l, 0, j)),
        ],
        out_specs=pl.BlockSpec((None, rows, tn), lambda l, j: (l, 0, j)),
        compiler_params=_cparams(("arbitrary", "arbitrary")),
        name="adaln_mod",
    )(cc, w_ada, b_ada.reshape(DEPTH, 1, 3 * D_MODEL))


def _lb_kernel(x_ref, o_ref):
    xs = [x_ref[i] for i in range(DEPTH)]
    m = functools.reduce(jnp.maximum, xs)
    es = [jnp.exp(x - m) for x in xs]
    tot = functools.reduce(lambda a, b: a + b, es)
    cum = None
    first = None
    for i in range(DEPTH):
        p = es[i] / tot
        cum = p if cum is None else cum + p
        if first is None:
            first = cum
        lb = jnp.maximum(cum - first, 0.0)
        o_ref[0, i] = lb
        o_ref[1, i] = jnp.log1p(-lb)
        o_ref[2, i] = 1.0 - lb


def _lower_bounds(lb_logits):
    x = jnp.transpose(lb_logits, (1, 0, 2))
    return pl.pallas_call(
        _lb_kernel,
        out_shape=jax.ShapeDtypeStruct((3, DEPTH, 2, R_WIDTH), F32),
        name="hgrn_lower_bounds",
    )(x)


IN_TILE = 512


def _inproj_kernel(xc_ref, xl_ref, mod_ref, g_ref, w_ref, kprev_ref, vprev_ref,
                   pa_ref, pr_ref, pf_ref, ko_ref, vo_ref, wb_scr, *, ctx_tiles):
    del kprev_ref, vprev_ref
    i = pl.program_id(0)

    @pl.when(i == 0)
    def _():
        wb_scr[...] = w_ref[...].astype(BF16)

    is_ctx = i < ctx_tiles
    x = jnp.where(is_ctx, xc_ref[...], xl_ref[...])
    ms = jnp.mean(x * x, axis=-1, keepdims=True)
    y = x * lax.rsqrt(ms + EPS) * g_ref[...]
    h = (y * (1.0 + mod_ref[1:2, :]) + mod_ref[0:1, :]).astype(BF16)
    pa_ref[...] = _dot(h, wb_scr[:, 0:PA_COLS])
    pr_ref[...] = _dot(h, wb_scr[:, PA_COLS:PA_COLS + PR_COLS])
    pf_ref[...] = _dot(h, wb_scr[:, PA_COLS + PR_COLS:IN_COLS])

    @pl.when(is_ctx)
    def _():
        seqs, _, _, n = ko_ref.shape
        for s in range(seqs):
            rows = slice(s * n, (s + 1) * n)
            ko_ref[s] = pa_ref[rows, A_WIDTH:2 * A_WIDTH].T.reshape(A_HEADS, HEAD_DIM, n)
            vo_ref[s] = pa_ref[rows, 2 * A_WIDTH:3 * A_WIDTH].T.reshape(A_HEADS, HEAD_DIM, n)


def _in_projection(x_ctx, x_lat, mod, g_pre, w_in, layer, cache):
    bc, nc_, _ = x_ctx.shape
    bl, nl, _ = x_lat.shape
    tm = IN_TILE
    seqs_per_tile = tm // nc_
    lat_tiles_per_seq = nl // tm
    ctx_tiles = bc * nc_ // tm
    lat_tiles = bl * nl // tm
    tokens = (ctx_tiles + lat_tiles) * tm
    last_ctx = ctx_tiles - 1
    ctx_idx = lambda i: (jnp.minimum(i, last_ctx), 0)
    lat_idx = lambda i: (jnp.maximum(i - ctx_tiles, 0), 0)
    mod_idx = lambda i: (jnp.where(i < ctx_tiles, 0, 1 + jnp.maximum(i - ctx_tiles, 0) // lat_tiles_per_seq), 0, 0)
    tok = lambda i: (i, 0)
    cache_spec = pl.BlockSpec((seqs_per_tile, None, A_HEADS, HEAD_DIM, nc_),
                              lambda i: (jnp.minimum(i, last_ctx), layer, 0, 0, 0))
    cache_shape = jax.ShapeDtypeStruct(cache[0].shape, F32)
    return pl.pallas_call(
        functools.partial(_inproj_kernel, ctx_tiles=ctx_tiles),
        out_shape=(
            jax.ShapeDtypeStruct((tokens, PA_COLS), F32),
            jax.ShapeDtypeStruct((tokens, PR_COLS), F32),
            jax.ShapeDtypeStruct((tokens, PF_COLS), F32),
            cache_shape, cache_shape,
        ),
        grid=(ctx_tiles + lat_tiles,),
        in_specs=[
            pl.BlockSpec((tm, D_MODEL), ctx_idx),
            pl.BlockSpec((tm, D_MODEL), lat_idx),
            pl.BlockSpec((None, 3, D_MODEL), mod_idx),
            pl.BlockSpec((None, 1, D_MODEL), lambda i: (layer, 0, 0)),
            pl.BlockSpec((None, D_MODEL, IN_COLS), lambda i: (layer, 0, 0), pipeline_mode=pl.Buffered(1)),
            pl.BlockSpec(memory_space=pl.ANY),
            pl.BlockSpec(memory_space=pl.ANY),
        ],
        out_specs=(
            pl.BlockSpec((tm, PA_COLS), tok),
            pl.BlockSpec((tm, PR_COLS), tok),
            pl.BlockSpec((tm, PF_COLS), tok),
            cache_spec, cache_spec,
        ),
        scratch_shapes=[pltpu.VMEM((D_MODEL, IN_COLS), BF16)],
        input_output_aliases={5: 3, 6: 4},
        compiler_params=_cparams(("arbitrary",)),
        name="in_projection",
    )(x_ctx.reshape(bc * nc_, D_MODEL), x_lat.reshape(bl * nl, D_MODEL), mod,
      g_pre.reshape(DEPTH, 1, D_MODEL), w_in, cache[0], cache[1])


LOG2E = 1.4426950408889634
Q_SCALE = HEAD_DIM ** -0.5 * LOG2E


def _lane_lo():
    return lax.broadcasted_iota(jnp.int32, (1, LANES), 1) < HEAD_DIM


def _ctx_attn_kernel(pa_ref, o_ref, s_scr):
    lo = _lane_lo()
    npair = A_HEADS // 2

    def scores(p):
        c = p * LANES
        q2 = pa_ref[:, c:c + LANES] * Q_SCALE
        k2 = pa_ref[:, A_WIDTH + c:A_WIDTH + c + LANES].astype(BF16)
        for hh in range(2):
            sel = lo if hh == 0 else jnp.logical_not(lo)
            s_scr[p % 2, hh] = _dot_nt(jnp.where(sel, q2, 0.0).astype(BF16), k2)

    def finish(p):
        c = p * LANES
        v2 = pa_ref[:, 2 * A_WIDTH + c:2 * A_WIDTH + c + LANES].astype(BF16)
        outs = []
        for hh in range(2):
            s = s_scr[p % 2, hh]
            e = jnp.exp2(s - jnp.max(s, axis=-1, keepdims=True))
            inv = 1.0 / jnp.sum(e, axis=-1, keepdims=True)
            outs.append(_dot(e.astype(BF16), v2) * inv)
        o2 = jnp.where(lo, outs[0], outs[1])
        o_ref[:, c:c + LANES] = (o2 * _silu(pa_ref[:, 3 * A_WIDTH + c:3 * A_WIDTH + c + LANES])).astype(BF16)

    scores(0)
    for p in range(npair):
        if p + 1 < npair:
            scores(p + 1)
        finish(p)


def _context_attention(pa, seqs):
    b0, b = seqs
    n = pa.shape[1]
    return pl.pallas_call(
        _ctx_attn_kernel,
        out_shape=jax.ShapeDtypeStruct((b, n, A_WIDTH), BF16),
        grid=(b,),
        in_specs=[pl.BlockSpec((None, n, PA_COLS), lambda i: (b0 + i, 0, 0))],
        out_specs=pl.BlockSpec((None, n, A_WIDTH), lambda i: (i, 0, 0)),
        scratch_shapes=[pltpu.VMEM((2, 2, n, n), F32)],
        compiler_params=_cparams(("arbitrary",)),
        name="context_attention",
    )(pa)


QROWS = 4


def _nbr_blocks(rows):
    kh = min(WIN_H, rows)
    out = []
    for r_first in range(0, rows, QROWS):
        r0s = [min(max(r - kh // 2, 0), rows - kh) for r in range(r_first, r_first + QROWS)]
        lo, hi = min(r0s), max(r0s) + kh
        lo -= lo % 2
        span = hi - lo
        span += (-span) % 4
        if lo + span > rows:
            lo = rows - span
        assert lo >= 0 and lo % 2 == 0
        out.append((r_first, lo, span, r0s))
    return out, kh


N_REL_ROWS = 2 * WIN_H - 1
TABLE_ROWS = WIN_H * GRID_W


def _build_bias_tables(base_ref, ta_scr, tb_scr, heads):
    lane = lax.broadcasted_iota(jnp.int32, (GRID_W, LANES), 1)
    kc = lax.broadcasted_iota(jnp.int32, (GRID_W, LANES), 0)
    qc = lane % GRID_W
    lo = lane < GRID_W
    c0 = jnp.clip(qc - WIN_W // 2, 0, GRID_W - WIN_W)
    col_in = (kc >= c0) & (kc < c0 + WIN_W)

    for h in heads:
        def tile(i, lane_off):
            if not 0 <= i < N_REL_ROWS:
                return jnp.full((GRID_W, LANES), NEG_INF, F32)
            row = jnp.broadcast_to(base_ref[h, i:i + 1, :] * LOG2E, (GRID_W, LANES))
            return pltpu.roll(row, lane_off, 1, stride=1, stride_axis=0)

        for j in range(WIN_H):
            rows = slice(j * GRID_W, (j + 1) * GRID_W)
            ta_scr[h, rows, :] = jnp.where(col_in, jnp.where(lo, tile(2 * j + 1, 0), tile(2 * j, GRID_W)), NEG_INF)
            tb_scr[h, rows, :] = jnp.where(col_in, jnp.where(lo, tile(2 * j, 0), tile(2 * j - 1, GRID_W)), NEG_INF)


def _nbr_attn_kernel(q_ref, k_ref, v_ref, g_ref, kc_ref, vc_ref, base_ref, o_ref, s_scr, p_scr, ta_scr, tb_scr, *, rows):
    p = pl.program_id(1)

    @pl.when(pl.program_id(0) == 0)
    def _():
        _build_bias_tables(base_ref, ta_scr, tb_scr, (2 * p, 2 * p + 1))

    lo = _lane_lo()
    hi = jnp.logical_not(lo)
    blocks, kh = _nbr_blocks(rows)
    nq = QROWS * GRID_W
    past = kc_ref.shape[-1]
    ctx0 = s_scr.shape[2] - past
    kc_rows = kc_ref[...].reshape(LANES, past).T.astype(BF16)
    vct = vc_ref[...].reshape(LANES, past).astype(BF16)
    head_rows = [slice(hh * HEAD_DIM, (hh + 1) * HEAD_DIM) for hh in range(2)]
    ctx_tiles = [slice(ctx0 + t * GRID_W, ctx0 + (t + 1) * GRID_W) for t in range(past // GRID_W)]

    def window_tiles(bi, c, head):
        r_first, u0, span, r0s = blocks[bi]
        out = []
        for t in range(span):
            kr = u0 + t
            rs = slice(t * GRID_W, (t + 1) * GRID_W)
            ok = [r0s[2 * c + e] <= kr < r0s[2 * c + e] + kh for e in range(2)]
            if not (ok[0] or ok[1]):
                out.append((rs, None))
                continue
            i = kr - (r_first + 2 * c) + WIN_H - 1
            assert 0 <= i <= N_REL_ROWS
            tab, j = (ta_scr, (i - 1) // 2) if i % 2 else (tb_scr, i // 2)
            bias = tab[head, j * GRID_W:(j + 1) * GRID_W, :]
            if not ok[0]:
                bias = jnp.where(lo, NEG_INF, bias)
            if not ok[1]:
                bias = jnp.where(hi, NEG_INF, bias)
            out.append((rs, bias))
        return out

    def scores(bi):
        r_first, u0, span, _ = blocks[bi]
        qs, ks, nk = r_first * GRID_W, u0 * GRID_W, span * GRID_W
        q2 = q_ref[qs:qs + nq, :] * Q_SCALE
        ku = k_ref[ks:ks + nk, :].astype(BF16)
        for hh in range(2):
            qh = jnp.where(lo if hh == 0 else hi, q2, 0.0).astype(BF16)
            s_scr[bi % 2, hh, 0:nk, :] = _dot_nt(ku, qh)
            s_scr[bi % 2, hh, ctx0:ctx0 + past, :] = _dot_nt(kc_rows, qh)

    def fold_rows(x, op):
        parts = [x[r:r + 8] for r in range(0, GRID_W, 8)]
        while len(parts) > 1:
            parts = [op(parts[i], parts[i + 1]) for i in range(0, len(parts), 2)]
        return parts[0]

    def softmax(bi):
        par = bi % 2
        inv = []
        for hh in range(2):
            parts = []
            for c in range(QROWS // 2):
                cs = slice(c * LANES, (c + 1) * LANES)
                tiles = window_tiles(bi, c, 2 * p + hh)
                m_acc = None
                for rs, bias in tiles:
                    if bias is None:
                        continue
                    s = s_scr[par, hh, rs, cs] + bias
                    s_scr[par, hh, rs, cs] = s
                    t = fold_rows(s, jnp.maximum)
                    m_acc = t if m_acc is None else jnp.maximum(m_acc, t)
                for rs in ctx_tiles:
                    m_acc = jnp.maximum(m_acc, fold_rows(s_scr[par, hh, rs, cs], jnp.maximum))
                m = jnp.max(m_acc, axis=0, keepdims=True)
                l_acc = None
                for rs, bias in tiles:
                    if bias is None:
                        p_scr[par, hh, rs, cs] = jnp.zeros((GRID_W, LANES), BF16)
                        continue
                    e = jnp.exp2(s_scr[par, hh, rs, cs] - m)
                    p_scr[par, hh, rs, cs] = e.astype(BF16)
                    t = fold_rows(e, jnp.add)
                    l_acc = t if l_acc is None else l_acc + t
                for rs in ctx_tiles:
                    e = jnp.exp2(s_scr[par, hh, rs, cs] - m)
                    p_scr[par, hh, rs, cs] = e.astype(BF16)
                    l_acc = l_acc + fold_rows(e, jnp.add)
                parts.append(1.0 / jnp.sum(l_acc, axis=0, keepdims=True))
            inv.append(jnp.concatenate(parts, axis=1))
        return inv

    def values(bi, inv):
        r_first, u0, span, _ = blocks[bi]
        qs, ks, nk = r_first * GRID_W, u0 * GRID_W, span * GRID_W
        vut = v_ref[ks:ks + nk, :].T.astype(BF16)
        outs = []
        for hh in range(2):
            o = (_dot(vut[head_rows[hh]], p_scr[bi % 2, hh, 0:nk, :])
                 + _dot(vct[head_rows[hh]], p_scr[bi % 2, hh, ctx0:ctx0 + past, :]))
            outs.append(o * inv[hh])
        o2 = jnp.concatenate(outs, axis=0).T
        o_ref[qs:qs + nq, :] = (o2 * _silu(g_ref[qs:qs + nq, :])).astype(BF16)

    scores(0)
    for bi in range(len(blocks)):
        if bi + 1 < len(blocks):
            scores(bi + 1)
        values(bi, softmax(bi))


def _bias_base(rpb_l):
    mid = WIN_W - 1
    zeros = jnp.zeros(rpb_l.shape[:2] + (LANES - (2 * WIN_W - 1),), F32)
    rev = rpb_l[..., ::-1]
    base = jnp.concatenate([rev[..., mid:], zeros, rev[..., :mid]], axis=-1)
    return jnp.pad(base, ((0, 0), (0, 2 * WIN_H - N_REL_ROWS), (0, 0)))


def _neighbourhood_attention(pa, seqs, cache_kt, cache_vt, layer, rpb_l):
    b0, b = seqs
    n = pa.shape[1]
    rows = n // GRID_W
    past = cache_kt.shape[-1]
    blocks, _ = _nbr_blocks(rows)
    max_nk = max(s for (_, _, s, _) in blocks) * GRID_W
    npair = A_HEADS // 2
    col = lambda off: (lambda i, p: (b0 + i, 0, off + p))
    cache_spec = pl.BlockSpec((None, None, 2, HEAD_DIM, past), lambda i, p: (i, layer, p, 0, 0))
    return pl.pallas_call(
        functools.partial(_nbr_attn_kernel, rows=rows),
        out_shape=jax.ShapeDtypeStruct((b, n, A_WIDTH), BF16),
        grid=(b, npair),
        in_specs=[
            pl.BlockSpec((None, n, LANES), col(0)),
            pl.BlockSpec((None, n, LANES), col(npair)),
            pl.BlockSpec((None, n, LANES), col(2 * npair)),
            pl.BlockSpec((None, n, LANES), col(3 * npair)),
            cache_spec,
            cache_spec,
            pl.BlockSpec((A_HEADS, 2 * WIN_H, LANES), lambda i, p: (0, 0, 0)),
        ],
        out_specs=pl.BlockSpec((None, n, LANES), lambda i, p: (i, 0, p)),
        scratch_shapes=[
            pltpu.VMEM((2, 2, max_nk + past, QROWS * GRID_W), F32),
            pltpu.VMEM((2, 2, max_nk + past, QROWS * GRID_W), BF16),
            pltpu.VMEM((A_HEADS, TABLE_ROWS, LANES), F32),
            pltpu.VMEM((A_HEADS, TABLE_ROWS, LANES), F32),
        ],
        compiler_params=_cparams(("arbitrary", "arbitrary")),
        name="neighbourhood_attention",
    )(pa, pa, pa, pa, cache_kt, cache_vt, _bias_base(rpb_l))


COARSE_HALVES = (32, 16, 8)
FINE_HALVES = (4, 2, 1)
N_LEVELS = len(COARSE_HALVES) + len(FINE_HALVES)
ANCHOR_BLOCK = CHUNK
ANCHOR_LEVELS = 0
ANCHOR_MAX_EXPONENT = 80.0
MASK_DIAG = N_LEVELS
MASK_ANCHOR = N_LEVELS + 1
GROUP = 4


def _hgrn_constants():
    c = CHUNK
    idx = np.arange(c)
    mats = [np.tril(np.ones((c, c)))]
    masks = []
    for h in COARSE_HALVES + FINE_HALVES:
        blk = idx // (2 * h)
        mid = blk * 2 * h + h - 1
        upper = idx > mid
        if h in FINE_HALVES:
            m = np.zeros((c, c))
            for i in range(c):
                if upper[i]:
                    m[i, mid[i] + 1:i + 1] = 1.0
                else:
                    m[i, i + 1:mid[i] + 1] = 1.0
            mats.append(m)
        same = blk[:, None] == blk[None, :]
        masks.append((same & upper[:, None] & (~upper)[None, :]).astype(np.float64))
    masks.append(np.eye(c))
    same_block = (idx[:, None] // ANCHOR_BLOCK) == (idx[None, :] // ANCHOR_BLOCK)
    masks.append((same_block & (idx[None, :] <= idx[:, None])).astype(np.float64))
    fwd = np.concatenate(mats, axis=0)
    bwd = np.concatenate([m[::-1, ::-1] for m in mats], axis=0)
    mk_f = np.stack([np.tile(m, (1, R_HEADS)) for m in masks])
    mk_b = np.stack([np.tile(m[::-1, ::-1], (1, R_HEADS)) for m in masks])
    hid = np.arange(R_WIDTH) // R_DIM
    bd = (hid[:, None] == hid[None, :]).astype(np.float64)
    return (jnp.asarray(np.stack([fwd, bwd]), BF16), jnp.asarray(np.stack([mk_f, mk_b]), F32),
            jnp.asarray(bd, BF16))


def _hgrn_kernel(*refs, nc, has_init):
    pr_ref, lbp_ref, gh_ref, mstk_ref, lmask_ref, bd_ref = refs[:6]
    if has_init:
        s0_ref, o_ref = refs[6:8]
        sfin_ref = None
    else:
        o_ref, sfin_ref = refs[6:8]
        s0_ref = None
    q_scr, k_scr, g_scr, gc_scr, qg_scr, upd_scr, dec_scr, sbd_scr = refs[8:]
    w = R_WIDTH
    c = CHUNK
    bd = bd_ref[...]
    hid = lax.broadcasted_iota(jnp.int32, (1, w), 1) // R_DIM

    q_scr[...] = _silu(pr_ref[:, 0:w])
    for d in range(2):
        z = pr_ref[:, (1 + d) * w:(2 + d) * w]
        e = jnp.exp(-jnp.abs(z))
        r = 1.0 / (1.0 + e)
        pos = z >= 0.0
        one_m_lb = lbp_ref[2, d:d + 1, :]
        f = lbp_ref[0, d:d + 1, :] + one_m_lb * (jnp.where(pos, 1.0, e) * r)
        log_f = jnp.where(f > 0.0, jnp.log(f), lbp_ref[1, d:d + 1, :] + z)
        g_hi, g_lo = _split2(log_f)
        g_scr[d, :, 0:w] = g_hi
        g_scr[d, :, w:2 * w] = g_lo
        k_scr[d] = one_m_lb * (jnp.where(pos, e, 1.0) * r)

    def tile4(x):
        return jnp.concatenate([x] * R_HEADS, axis=0)

    def head_diagonal(full):
        out = full[(R_HEADS - 1) * R_DIM:R_HEADS * R_DIM]
        for h in range(R_HEADS - 2, -1, -1):
            out = jnp.where(hid == h, full[h * R_DIM:(h + 1) * R_DIM], out)
        return out

    def chunk_rows(ci):
        return pl.ds(pl.multiple_of(ci * c, c), c)

    def cumulative(d, rs, n_mats):
        s = _dot(mstk_ref[d, 0:n_mats * c, :], g_scr[d, rs, :])
        return s[:, 0:w] + s[:, w:2 * w]

    def level_exponents(gcum, d, halves):
        out = []
        for h in halves:
            parts = []
            for s0 in range(0, c, 2 * h):
                anchor = s0 + h - 1 + d
                parts.append(-jnp.abs(gcum[s0:s0 + 2 * h] - gcum[anchor:anchor + 1]))
            out.append(parts[0] if len(parts) == 1 else jnp.concatenate(parts, axis=0))
        return out

    def anchor_shift(gcum, d):
        parts = []
        for s0 in range(0, c, ANCHOR_BLOCK):
            anchor = s0 + ANCHOR_BLOCK // 2 - 1 + d
            parts.append(gcum[s0:s0 + ANCHOR_BLOCK] - gcum[anchor:anchor + 1])
        return jnp.concatenate(parts, axis=0)

    groups = nc // GROUP
    pairs = [(j, d) for j in range(GROUP) for d in range(2)]

    def increment_group(gi, worst):
        rows = [chunk_rows(gi * GROUP + j) for j in range(GROUP)]
        gcums = [cumulative(d, rows[j], 1) for j, d in pairs]
        lasts = [g[c - 1:c, :] if d == 0 else g[0:1, :] for g, (j, d) in zip(gcums, pairs)]
        kls = [(k_scr[d, rows[j], :] * jnp.exp(last - g)).astype(BF16)
               for g, last, (j, d) in zip(gcums, lasts, pairs)]
        vs = [pr_ref[rows[j], 3 * w:4 * w].astype(BF16) for j in range(GROUP)]
        upds = [_dot_tn(vs[j], kl) for kl, (j, d) in zip(kls, pairs)]
        for g, last, upd, (j, d) in zip(gcums, lasts, upds, pairs):
            ci = gi * GROUP + j
            gc_scr[d, rows[j], :] = g
            upd_scr[d, ci] = head_diagonal(upd)
            dec_scr[d, ci] = jnp.broadcast_to(jnp.exp(last), (8, w))
            qg_scr[d, rows[j], :] = (q_scr[rows[j], :] * jnp.exp(g)).astype(BF16)
            sh = jnp.abs(anchor_shift(g, d))
            for s0 in range(0, c, 8):
                worst = jnp.maximum(worst, sh[s0:s0 + 8])
        return worst

    worst = lax.fori_loop(0, groups, increment_group, jnp.zeros((8, w), F32))
    q_max = jnp.max(jnp.max(jnp.abs(q_scr[...]), axis=0, keepdims=True), axis=1, keepdims=True)
    anchor_ok = jnp.max(worst + jnp.log(jnp.maximum(q_max, 1.0))) < ANCHOR_MAX_EXPONENT

    def scan_step(ci, carry):
        sf, sb = carry
        cb = nc - 1 - ci
        sbd_scr[0, ci] = tile4(sf.astype(BF16)) * bd
        sbd_scr[1, cb] = tile4(sb.astype(BF16)) * bd
        sf = sf * dec_scr[0, ci][0:1, :] + upd_scr[0, ci]
        sb = sb * dec_scr[1, cb][0:1, :] + upd_scr[1, cb]
        return sf, sb

    if has_init:
        init = tuple(jnp.concatenate([s0_ref[d], jnp.zeros((w, LANES - R_DIM), F32)], axis=1).T[0:R_DIM]
                     for d in range(2))
    else:
        init = (jnp.zeros((R_DIM, w), F32), jnp.zeros((R_DIM, w), F32))
    finals = lax.fori_loop(0, nc, scan_step, init)
    if sfin_ref is not None:
        for d in range(2):
            padded = jnp.concatenate([finals[d], jnp.zeros((LANES - R_DIM, w), F32)], axis=0)
            sfin_ref[d] = padded.T[:, 0:R_DIM]

    def finish(rs, o):
        x2_hi, x2_lo = _split2(o * o)
        ms = (_dot(x2_hi, bd) + _dot(x2_lo, bd)) * (1.0 / R_DIM)
        y = o * lax.rsqrt(ms + EPS) * gh_ref[...]
        o_ref[rs, :] = (y * _silu(pr_ref[rs, 4 * w:5 * w])).astype(BF16)

    def readout_group_anchor(gi, carry):
        rows = [chunk_rows(gi * GROUP + j) for j in range(GROUP)]
        qs = [q_scr[rows[j], :] for j in range(GROUP)]
        gcums = [gc_scr[d, rows[j], :] for j, d in pairs]
        factors = []
        for g, (j, d) in zip(gcums, pairs):
            es = [jnp.exp(x) for x in level_exponents(g, d, COARSE_HALVES[:ANCHOR_LEVELS])]
            sh = anchor_shift(g, d)
            fs = [(e, e, lv) for lv, e in enumerate(es)]
            fs.append((jnp.exp(sh), jnp.exp(-sh), MASK_ANCHOR))
            factors.append(fs)
        kbds = [tile4(k_scr[d, rows[j], :].astype(BF16)) * bd for j, d in pairs]
        accs = [None] * len(pairs)
        for lv in range(ANCHOR_LEVELS + 1):
            for i, (j, d) in enumerate(pairs):
                eq, ek, mask = factors[i][lv]
                part = jnp.where(lmask_ref[d, mask] > 0.5,
                                 _dot_nt((qs[j] * eq).astype(BF16), kbds[i] * tile4(ek.astype(BF16))), 0.0)
                accs[i] = part if accs[i] is None else accs[i] + part
        vbds = [tile4(pr_ref[rows[j], 3 * w:4 * w].astype(BF16)) * bd for j in range(GROUP)]
        outs = [None] * GROUP
        for i, (j, d) in enumerate(pairs):
            ci = gi * GROUP + j
            od = _dot(accs[i].astype(BF16), vbds[j]) + _dot_nt(qg_scr[d, rows[j], :], sbd_scr[d, ci])
            outs[j] = od if outs[j] is None else outs[j] + od
        for j in range(GROUP):
            finish(rows[j], outs[j])
        return carry

    def readout_step_split(ci, carry):
        rs = chunk_rows(ci)
        q = q_scr[rs, :]
        qb = q.astype(BF16)
        vbd = tile4(pr_ref[rs, 3 * w:4 * w].astype(BF16)) * bd
        dsts = [cumulative(d, rs, 1 + len(FINE_HALVES)) for d in range(2)]
        exps = [level_exponents(dsts[d][0:c], d, COARSE_HALVES)
                + [dsts[d][(1 + j) * c:(2 + j) * c] for j in range(len(FINE_HALVES))] for d in range(2)]
        kbd = [tile4(k_scr[d, rs, :].astype(BF16)) * bd for d in range(2)]
        a = [_dot_nt(qb, kbd[d]) * lmask_ref[d, MASK_DIAG] for d in range(2)]
        for j in range(N_LEVELS):
            for d in range(2):
                e = jnp.exp(exps[d][j])
                a[d] = a[d] + _dot_nt((q * e).astype(BF16), kbd[d] * tile4(e.astype(BF16))) * lmask_ref[d, j]
        o = None
        for d in range(2):
            od = _dot(a[d].astype(BF16), vbd) + _dot_nt(qg_scr[d, rs, :], sbd_scr[d, ci])
            o = od if o is None else o + od
        finish(rs, o)
        return carry

    @pl.when(anchor_ok)
    def _():
        lax.fori_loop(0, groups, readout_group_anchor, 0)

    @pl.when(jnp.logical_not(anchor_ok))
    def _():
        lax.fori_loop(0, nc, readout_step_split, 0)


def _hgrn(pr, seqs, lbp_l, g_hgrn_l, consts, state=None, layer=None):
    b0, b = seqs
    n = pr.shape[1]
    nc = n // CHUNK
    mstk, lmask, bd = consts
    w = R_WIDTH
    has_init = state is not None
    full = lambda *shape: pl.BlockSpec(shape, lambda i: (0,) * len(shape))
    in_specs = [
        pl.BlockSpec((None, n, PR_COLS), lambda i: (b0 + i, 0, 0)),
        full(3, 2, w),
        full(1, w),
        full(*mstk.shape),
        full(*lmask.shape),
        full(w, w),
    ]
    args = [pr, lbp_l, g_hgrn_l.reshape(1, w), mstk, lmask, bd]
    out_shape = [jax.ShapeDtypeStruct((b, n, w), BF16)]
    out_specs = [pl.BlockSpec((None, n, w), lambda i: (i, 0, 0))]
    if has_init:
        in_specs.append(pl.BlockSpec((None, None, 2, w, R_DIM), lambda i: (i, layer, 0, 0, 0)))
        args.append(state)
    else:
        out_shape.append(jax.ShapeDtypeStruct((b, 2, w, R_DIM), F32))
        out_specs.append(pl.BlockSpec((None, 2, w, R_DIM), lambda i: (i, 0, 0, 0)))
    return pl.pallas_call(
        functools.partial(_hgrn_kernel, nc=nc, has_init=has_init),
        out_shape=tuple(out_shape),
        grid=(b,),
        in_specs=in_specs,
        out_specs=tuple(out_specs),
        scratch_shapes=[
            pltpu.VMEM((n, w), F32),
            pltpu.VMEM((2, n, w), F32),
            pltpu.VMEM((2, n, 2 * w), BF16),
            pltpu.VMEM((2, n, w), F32),
            pltpu.VMEM((2, n, w), BF16),
            pltpu.VMEM((2, nc, R_DIM, w), F32),
            pltpu.VMEM((2, nc, 8, w), F32),
            pltpu.VMEM((2, nc, w, w), BF16),
        ],
        compiler_params=_cparams(("arbitrary",)),
        name="hgrn_scan",
    )(*args)


def _fnet_constants(n):
    j = np.arange(F_GROUP_DIM)
    ang = 2.0 * np.pi * ((j[:, None] * j[None, :]) % F_GROUP_DIM) / F_GROUP_DIM
    eye = np.eye(F_GROUPS)
    cs = np.concatenate([np.kron(eye, np.cos(ang)), np.kron(eye, np.sin(ang))], axis=1)
    t = np.arange(n)
    angn = 2.0 * np.pi * ((t[:, None] * t[None, :]) % n) / n
    return tuple(jnp.asarray(m, F32).astype(BF16) for m in (cs, np.cos(angn), np.sin(angn)))


def _fnet_kernel(pf_ref, cs_ref, cn_ref, sn_ref, wf_ref, o_ref, *, scale):
    w = F_WIDTH
    t = _dot(pf_ref[:, 0:w].astype(BF16), cs_ref[...])
    y = (_dot(cn_ref[...], t[:, 0:w].astype(BF16)) - _dot(sn_ref[...], t[:, w:2 * w].astype(BF16))) * scale
    of = _dot(y.astype(BF16), wf_ref[...])
    o_ref[...] = (of * _silu(pf_ref[:, w:2 * w])).astype(BF16)


def _fourier(pf, seqs, consts, w_fnet_bf16):
    b0, b = seqs
    n = pf.shape[1]
    cs, cn, sn = consts
    w = F_WIDTH
    full = lambda *shape: pl.BlockSpec(shape, lambda i: (0,) * len(shape))
    return pl.pallas_call(
        functools.partial(_fnet_kernel, scale=float((n * F_GROUP_DIM) ** -0.5)),
        out_shape=jax.ShapeDtypeStruct((b, n, w), BF16),
        grid=(b,),
        in_specs=[pl.BlockSpec((None, n, PF_COLS), lambda i: (b0 + i, 0, 0)),
                  full(w, 2 * w), full(n, n), full(n, n), full(w, w)],
        out_specs=pl.BlockSpec((None, n, w), lambda i: (i, 0, 0)),
        compiler_params=_cparams(("arbitrary",)),
        name="fourier_mixing",
    )(pf, cs, cn, sn, w_fnet_bf16)


def _outproj_kernel(mac_ref, mrc_ref, mfc_ref, mal_ref, mrl_ref, mfl_ref, xc_ref, xl_ref, mod_ref, g_ref, w_ref,
                    yc_ref, yl_ref, wb_scr, *, ctx_tiles):
    i = pl.program_id(0)

    @pl.when(i == 0)
    def _():
        wb_scr[...] = w_ref[...].astype(BF16)

    is_ctx = i < ctx_tiles
    pick = lambda c_ref, l_ref: jnp.where(is_ctx, c_ref[...], l_ref[...])
    out = (_dot(pick(mac_ref, mal_ref), wb_scr[0:A_WIDTH, :])
           + _dot(pick(mrc_ref, mrl_ref), wb_scr[A_WIDTH:A_WIDTH + R_WIDTH, :])
           + _dot(pick(mfc_ref, mfl_ref), wb_scr[A_WIDTH + R_WIDTH:D_MODEL, :]))
    ms = jnp.mean(out * out, axis=-1, keepdims=True)
    y = pick(xc_ref, xl_ref) + mod_ref[2:3, :] * (out * lax.rsqrt(ms + EPS) * g_ref[...])

    @pl.when(is_ctx)
    def _():
        yc_ref[...] = y

    @pl.when(jnp.logical_not(is_ctx))
    def _():
        yl_ref[...] = y


def _out_projection(mixed_ctx, mixed_lat, x_ctx, x_lat, mod, g_post, w_out, layer):
    bc, nc_, _ = x_ctx.shape
    bl, nl, _ = x_lat.shape
    tm = IN_TILE
    lat_tiles_per_seq = nl // tm
    ctx_tiles = bc * nc_ // tm
    lat_tiles = bl * nl // tm
    last_ctx = ctx_tiles - 1
    ctx_idx = lambda i: (jnp.minimum(i, last_ctx), 0)
    lat_idx = lambda i: (jnp.maximum(i - ctx_tiles, 0), 0)
    mod_idx = lambda i: (jnp.where(i < ctx_tiles, 0, 1 + jnp.maximum(i - ctx_tiles, 0) // lat_tiles_per_seq), 0, 0)
    widths = (A_WIDTH, R_WIDTH, F_WIDTH)
    flat = lambda t: t.reshape(-1, t.shape[-1])
    yc, yl = pl.pallas_call(
        functools.partial(_outproj_kernel, ctx_tiles=ctx_tiles),
        out_shape=(jax.ShapeDtypeStruct((bc * nc_, D_MODEL), F32), jax.ShapeDtypeStruct((bl * nl, D_MODEL), F32)),
        grid=(ctx_tiles + lat_tiles,),
        in_specs=(
            [pl.BlockSpec((tm, wd), ctx_idx) for wd in widths]
            + [pl.BlockSpec((tm, wd), lat_idx) for wd in widths]
            + [
                pl.BlockSpec((tm, D_MODEL), ctx_idx),
                pl.BlockSpec((tm, D_MODEL), lat_idx),
                pl.BlockSpec((None, 3, D_MODEL), mod_idx),
                pl.BlockSpec((None, 1, D_MODEL), lambda i: (layer, 0, 0)),
                pl.BlockSpec((None, D_MODEL, D_MODEL), lambda i: (layer, 0, 0), pipeline_mode=pl.Buffered(1)),
            ]),
        out_specs=(pl.BlockSpec((tm, D_MODEL), ctx_idx), pl.BlockSpec((tm, D_MODEL), lat_idx)),
        scratch_shapes=[pltpu.VMEM((D_MODEL, D_MODEL), BF16)],
        compiler_params=_cparams(("arbitrary",)),
        name="out_projection",
    )(*[flat(t) for t in mixed_ctx], *[flat(t) for t in mixed_lat], flat(x_ctx), flat(x_lat), mod,
      g_post.reshape(DEPTH, 1, D_MODEL), w_out)
    return yc.reshape(x_ctx.shape), yl.reshape(x_lat.shape)


def kernel(x_prompt, x_sample, cache_attn_k, cache_attn_v, state_hgrn, c, c_ctx,
           w_ada, b_ada, g_pre, w_in, rpb, lb_logits, g_hgrn, w_fnet, w_out, g_post):
    nb_ctx, n_ctx, _ = x_prompt.shape
    nb_lat, n_lat, _ = x_sample.shape

    pad_rows = (-(1 + nb_lat)) % 8
    cc = jnp.concatenate([c_ctx[None, :], c, jnp.zeros((pad_rows, D_MODEL), F32)], axis=0)
    mods = _modulations(cc, w_ada, b_ada)
    lbp = _lower_bounds(lb_logits)

    w_fnet_b = w_fnet.astype(BF16)
    hconsts = _hgrn_constants()
    fconsts_ctx = _fnet_constants(n_ctx)
    fconsts_lat = _fnet_constants(n_lat)

    state_rows = state_hgrn.reshape(nb_lat, DEPTH, 2, R_WIDTH, R_DIM)
    cache_kt = jnp.transpose(cache_attn_k, (0, 1, 3, 4, 2))
    cache_vt = jnp.transpose(cache_attn_v, (0, 1, 3, 4, 2))

    yp, ys = x_prompt, x_sample
    new_kv = [jnp.zeros((nb_ctx, DEPTH, A_HEADS, HEAD_DIM, n_ctx), F32) for _ in range(2)]
    new_s = []
    ctx_tokens = nb_ctx * n_ctx
    ctx_seqs = (0, nb_ctx)
    lat_seqs = (ctx_tokens // n_lat, nb_lat)
    for l in range(DEPTH):
        lbp_l = lbp[:, l]
        mod_all = mods[l, 0:1 + nb_lat].reshape(1 + nb_lat, 3, D_MODEL)
        pa, pr, pf, *new_kv = _in_projection(yp, ys, mod_all, g_pre, w_in, l, new_kv)
        as_ctx = lambda t: t.reshape(-1, n_ctx, t.shape[-1])
        ma_c = _context_attention(as_ctx(pa), ctx_seqs)
        mr_c, sfin = _hgrn(as_ctx(pr), ctx_seqs, lbp_l, g_hgrn[l], hconsts)
        new_s.append(sfin)
        mf_c = _fourier(as_ctx(pf), ctx_seqs, fconsts_ctx, w_fnet_b[l])
        as_lat = lambda t: t.reshape(-1, n_lat, t.shape[-1])
        ma_l = _neighbourhood_attention(as_lat(pa), lat_seqs, cache_kt, cache_vt, l, rpb[l])
        (mr_l,) = _hgrn(as_lat(pr), lat_seqs, lbp_l, g_hgrn[l], hconsts, state=state_rows, layer=l)
        mf_l = _fourier(as_lat(pf), lat_seqs, fconsts_lat, w_fnet_b[l])
        yp, ys = _out_projection((ma_c, mr_c, mf_c), (ma_l, mr_l, mf_l), yp, ys, mod_all, g_post, w_out, l)

    new_state = jnp.stack(new_s, axis=1).reshape(nb_ctx, DEPTH, 2, R_HEADS, R_DIM, R_DIM)
    new_k, new_v = (jnp.transpose(t, (0, 1, 4, 2, 3)) for t in new_kv)
    return (yp, ys, new_k, new_v, new_state)
```

```python
import functools

import numpy as np
import jax
import jax.numpy as jnp
from jax import lax
from jax.experimental import pallas as pl
from jax.experimental.pallas import tpu as pltpu

F32 = jnp.float32
BF16 = jnp.bfloat16

D_MODEL = 1024
DEPTH = 4
GRID_W = 64
WIN_H = 8
WIN_W = 16
HEAD_DIM = 64
A_HEADS = 8
A_WIDTH = A_HEADS * HEAD_DIM
R_HEADS = 4
R_DIM = 64
R_WIDTH = R_HEADS * R_DIM
F_GROUPS = 4
F_GROUP_DIM = 64
F_WIDTH = F_GROUPS * F_GROUP_DIM
PA_COLS = 4 * A_WIDTH
PR_COLS = 5 * R_WIDTH
PF_COLS = 2 * F_WIDTH
IN_COLS = PA_COLS + PR_COLS + PF_COLS
CHUNK = 64
EPS = 1e-6
LANES = 128
NEG_INF = float("-inf")
VMEM_LIMIT = 56 * 1024 * 1024


def _cparams(sem):
    return pltpu.CompilerParams(dimension_semantics=sem, vmem_limit_bytes=VMEM_LIMIT)


def _silu(x):
    return x * (1.0 / (1.0 + jnp.exp(-x)))


def _dot(a, b):
    return jnp.dot(a, b, preferred_element_type=F32)


def _dot_nt(a, b):
    return lax.dot_general(a, b, (((1,), (1,)), ((), ())), preferred_element_type=F32)


def _dot_tn(a, b):
    return lax.dot_general(a, b, (((0,), (0,)), ((), ())), preferred_element_type=F32)


def _split2(x):
    hi = x.astype(BF16)
    lo = (x - hi.astype(F32)).astype(BF16)
    return hi, lo


def _mod_kernel(cc_ref, w_ref, b_ref, o_ref):
    a_hi, a_lo = _split2(_silu(cc_ref[...]))
    w_hi, w_lo = _split2(w_ref[...])
    acc = _dot(a_hi, w_hi) + _dot(a_hi, w_lo) + _dot(a_lo, w_hi)
    o_ref[...] = acc + b_ref[...]


def _modulations(cc, w_ada, b_ada):
    rows = cc.shape[0]
    tn = 3 * D_MODEL // 2
    return pl.pallas_call(
        _mod_kernel,
        out_shape=jax.ShapeDtypeStruct((DEPTH, rows, 3 * D_MODEL), F32),
        grid=(DEPTH, 3 * D_MODEL // tn),
        in_specs=[
            pl.BlockSpec((rows, D_MODEL), lambda l, j: (0, 0)),
            pl.BlockSpec((None, D_MODEL, tn), lambda l, j: (l, 0, j)),
            pl.BlockSpec((None, 1, tn), lambda l, j: (l, 0, j)),
        ],
        out_specs=pl.BlockSpec((None, rows, tn), lambda l, j: (l, 0, j)),
        compiler_params=_cparams(("arbitrary", "arbitrary")),
        name="adaln_mod",
    )(cc, w_ada, b_ada.reshape(DEPTH, 1, 3 * D_MODEL))


def _lb_kernel(x_ref, o_ref):
    xs = [x_ref[i] for i in range(DEPTH)]
    m = functools.reduce(jnp.maximum, xs)
    es = [jnp.exp(x - m) for x in xs]
    tot = functools.reduce(lambda a, b: a + b, es)
    cum = None
    first = None
    for i in range(DEPTH):
        p = es[i] / tot
        cum = p if cum is None else cum + p
        if first is None:
            first = cum
        lb = jnp.maximum(cum - first, 0.0)
        o_ref[0, i] = lb
        o_ref[1, i] = jnp.log1p(-lb)
        o_ref[2, i] = 1.0 - lb


def _lower_bounds(lb_logits):
    x = jnp.transpose(lb_logits, (1, 0, 2))
    return pl.pallas_call(
        _lb_kernel,
        out_shape=jax.ShapeDtypeStruct((3, DEPTH, 2, R_WIDTH), F32),
        name="hgrn_lower_bounds",
    )(x)


IN_TILE = 512


def _inproj_kernel(xc_ref, xl_ref, mod_ref, g_ref, w_ref, kprev_ref, vprev_ref,
                   pa_ref, pr_ref, pf_ref, ko_ref, vo_ref, wb_scr, *, ctx_tiles):
    del kprev_ref, vprev_ref
    i = pl.program_id(0)

    @pl.when(i == 0)
    def _():
        wb_scr[...] = w_ref[...].astype(BF16)

    is_ctx = i < ctx_tiles
    x = jnp.where(is_ctx, xc_ref[...], xl_ref[...])
    ms = jnp.mean(x * x, axis=-1, keepdims=True)
    y = x * lax.rsqrt(ms + EPS) * g_ref[...]
    h = (y * (1.0 + mod_ref[1:2, :]) + mod_ref[0:1, :]).astype(BF16)
    pa_ref[...] = _dot(h, wb_scr[:, 0:PA_COLS])
    pr_ref[...] = _dot(h, wb_scr[:, PA_COLS:PA_COLS + PR_COLS])
    pf_ref[...] = _dot(h, wb_scr[:, PA_COLS + PR_COLS:IN_COLS])

    @pl.when(is_ctx)
    def _():
        seqs, _, _, n = ko_ref.shape
        for s in range(seqs):
            rows = slice(s * n, (s + 1) * n)
            ko_ref[s] = pa_ref[rows, A_WIDTH:2 * A_WIDTH].T.reshape(A_HEADS, HEAD_DIM, n)
            vo_ref[s] = pa_ref[rows, 2 * A_WIDTH:3 * A_WIDTH].T.reshape(A_HEADS, HEAD_DIM, n)


def _in_projection(x_ctx, x_lat, mod, g_pre, w_in, layer, cache):
    bc, nc_, _ = x_ctx.shape
    bl, nl, _ = x_lat.shape
    tm = IN_TILE
    seqs_per_tile = tm // nc_
    lat_tiles_per_seq = nl // tm
    ctx_tiles = bc * nc_ // tm
    lat_tiles = bl * nl // tm
    tokens = (ctx_tiles + lat_tiles) * tm
    last_ctx = ctx_tiles - 1
    ctx_idx = lambda i: (jnp.minimum(i, last_ctx), 0)
    lat_idx = lambda i: (jnp.maximum(i - ctx_tiles, 0), 0)
    mod_idx = lambda i: (jnp.where(i < ctx_tiles, 0, 1 + jnp.maximum(i - ctx_tiles, 0) // lat_tiles_per_seq), 0, 0)
    tok = lambda i: (i, 0)
    cache_spec = pl.BlockSpec((seqs_per_tile, None, A_HEADS, HEAD_DIM, nc_),
                              lambda i: (jnp.minimum(i, last_ctx), layer, 0, 0, 0))
    cache_shape = jax.ShapeDtypeStruct(cache[0].shape, F32)
    return pl.pallas_call(
        functools.partial(_inproj_kernel, ctx_tiles=ctx_tiles),
        out_shape=(
            jax.ShapeDtypeStruct((tokens, PA_COLS), F32),
            jax.ShapeDtypeStruct((tokens, PR_COLS), F32),
            jax.ShapeDtypeStruct((tokens, PF_COLS), F32),
            cache_shape, cache_shape,
        ),
        grid=(ctx_tiles + lat_tiles,),
        in_specs=[
            pl.BlockSpec((tm, D_MODEL), ctx_idx),
            pl.BlockSpec((tm, D_MODEL), lat_idx),
            pl.BlockSpec((None, 3, D_MODEL), mod_idx),
            pl.BlockSpec((None, 1, D_MODEL), lambda i: (layer, 0, 0)),
            pl.BlockSpec((None, D_MODEL, IN_COLS), lambda i: (layer, 0, 0), pipeline_mode=pl.Buffered(1)),
            pl.BlockSpec(memory_space=pl.ANY),
            pl.BlockSpec(memory_space=pl.ANY),
        ],
        out_specs=(
            pl.BlockSpec((tm, PA_COLS), tok),
            pl.BlockSpec((tm, PR_COLS), tok),
            pl.BlockSpec((tm, PF_COLS), tok),
            cache_spec, cache_spec,
        ),
        scratch_shapes=[pltpu.VMEM((D_MODEL, IN_COLS), BF16)],
        input_output_aliases={5: 3, 6: 4},
        compiler_params=_cparams(("arbitrary",)),
        name="in_projection",
    )(x_ctx.reshape(bc * nc_, D_MODEL), x_lat.reshape(bl * nl, D_MODEL), mod,
      g_pre.reshape(DEPTH, 1, D_MODEL), w_in, cache[0], cache[1])


LOG2E = 1.4426950408889634
Q_SCALE = HEAD_DIM ** -0.5 * LOG2E


def _lane_lo():
    return lax.broadcasted_iota(jnp.int32, (1, LANES), 1) < HEAD_DIM


def _ctx_attn_kernel(pas_ref, os_ref, s_scr):
    for s in range(pas_ref.shape[0]):
        _ctx_attn_sequence(pas_ref.at[s], os_ref.at[s], s_scr)


def _ctx_attn_sequence(pa_ref, o_ref, s_scr):
    lo = _lane_lo()
    npair = A_HEADS // 2

    def scores(p):
        c = p * LANES
        q2 = pa_ref[:, c:c + LANES] * Q_SCALE
        k2 = pa_ref[:, A_WIDTH + c:A_WIDTH + c + LANES].astype(BF16)
        for hh in range(2):
            sel = lo if hh == 0 else jnp.logical_not(lo)
            s_scr[p % 2, hh] = _dot_nt(jnp.where(sel, q2, 0.0).astype(BF16), k2)

    def finish(p):
        c = p * LANES
        v2 = pa_ref[:, 2 * A_WIDTH + c:2 * A_WIDTH + c + LANES].astype(BF16)
        outs = []
        for hh in range(2):
            s = s_scr[p % 2, hh]
            e = jnp.exp2(s - jnp.max(s, axis=-1, keepdims=True))
            inv = 1.0 / jnp.sum(e, axis=-1, keepdims=True)
            outs.append(_dot(e.astype(BF16), v2) * inv)
        o2 = jnp.where(lo, outs[0], outs[1])
        o_ref[:, c:c + LANES] = (o2 * _silu(pa_ref[:, 3 * A_WIDTH + c:3 * A_WIDTH + c + LANES])).astype(BF16)

    scores(0)
    for p in range(npair):
        if p + 1 < npair:
            scores(p + 1)
        finish(p)


def _context_attention(pa, seqs):
    b0, b = seqs
    n = pa.shape[1]
    g = 2 if b % 2 == 0 and b0 % 2 == 0 else 1
    return pl.pallas_call(
        _ctx_attn_kernel,
        out_shape=jax.ShapeDtypeStruct((b, n, A_WIDTH), BF16),
        grid=(b // g,),
        in_specs=[pl.BlockSpec((g, n, PA_COLS), lambda i: (b0 // g + i, 0, 0))],
        out_specs=pl.BlockSpec((g, n, A_WIDTH), lambda i: (i, 0, 0)),
        scratch_shapes=[pltpu.VMEM((2, 2, n, n), F32)],
        compiler_params=_cparams(("arbitrary",)),
        name="context_attention",
    )(pa)


QROWS = 4


def _nbr_blocks(rows):
    kh = min(WIN_H, rows)
    out = []
    for r_first in range(0, rows, QROWS):
        r0s = [min(max(r - kh // 2, 0), rows - kh) for r in range(r_first, r_first + QROWS)]
        lo, hi = min(r0s), max(r0s) + kh
        lo -= lo % 2
        span = hi - lo
        span += (-span) % 4
        if lo + span > rows:
            lo = rows - span
        assert lo >= 0 and lo % 2 == 0
        out.append((r_first, lo, span, r0s))
    return out, kh


N_REL_ROWS = 2 * WIN_H - 1
TABLE_ROWS = WIN_H * GRID_W


def _build_bias_tables(base_ref, ta_scr, tb_scr, heads):
    lane = lax.broadcasted_iota(jnp.int32, (GRID_W, LANES), 1)
    kc = lax.broadcasted_iota(jnp.int32, (GRID_W, LANES), 0)
    qc = lane % GRID_W
    lo = lane < GRID_W
    c0 = jnp.clip(qc - WIN_W // 2, 0, GRID_W - WIN_W)
    col_in = (kc >= c0) & (kc < c0 + WIN_W)

    for h in heads:
        def tile(i, lane_off):
            if not 0 <= i < N_REL_ROWS:
                return jnp.full((GRID_W, LANES), NEG_INF, F32)
            row = jnp.broadcast_to(base_ref[h, i:i + 1, :] * LOG2E, (GRID_W, LANES))
            return pltpu.roll(row, lane_off, 1, stride=1, stride_axis=0)

        for j in range(WIN_H):
            rows = slice(j * GRID_W, (j + 1) * GRID_W)
            ta_scr[h, rows, :] = jnp.where(col_in, jnp.where(lo, tile(2 * j + 1, 0), tile(2 * j, GRID_W)), NEG_INF)
            tb_scr[h, rows, :] = jnp.where(col_in, jnp.where(lo, tile(2 * j, 0), tile(2 * j - 1, GRID_W)), NEG_INF)


def _nbr_attn_kernel(q_ref, k_ref, v_ref, g_ref, kc_ref, vc_ref, base_ref, o_ref, s_scr, p_scr, ta_scr, tb_scr, *, rows):
    p = pl.program_id(1)

    @pl.when(pl.program_id(0) == 0)
    def _():
        _build_bias_tables(base_ref, ta_scr, tb_scr, (2 * p, 2 * p + 1))

    lo = _lane_lo()
    hi = jnp.logical_not(lo)
    blocks, kh = _nbr_blocks(rows)
    nq = QROWS * GRID_W
    past = kc_ref.shape[-1]
    ctx0 = s_scr.shape[2] - past
    kc_rows = kc_ref[...].reshape(LANES, past).T.astype(BF16)
    vct = vc_ref[...].reshape(LANES, past).astype(BF16)
    head_rows = [slice(hh * HEAD_DIM, (hh + 1) * HEAD_DIM) for hh in range(2)]
    ctx_tiles = [slice(ctx0 + t * GRID_W, ctx0 + (t + 1) * GRID_W) for t in range(past // GRID_W)]

    def window_tiles(bi, c, head):
        r_first, u0, span, r0s = blocks[bi]
        out = []
        for t in range(span):
            kr = u0 + t
            rs = slice(t * GRID_W, (t + 1) * GRID_W)
            ok = [r0s[2 * c + e] <= kr < r0s[2 * c + e] + kh for e in range(2)]
            if not (ok[0] or ok[1]):
                out.append((rs, None))
                continue
            i = kr - (r_first + 2 * c) + WIN_H - 1
            assert 0 <= i <= N_REL_ROWS
            tab, j = (ta_scr, (i - 1) // 2) if i % 2 else (tb_scr, i // 2)
            bias = tab[head, j * GRID_W:(j + 1) * GRID_W, :]
            if not ok[0]:
                bias = jnp.where(lo, NEG_INF, bias)
            if not ok[1]:
                bias = jnp.where(hi, NEG_INF, bias)
            out.append((rs, bias))
        return out

    def scores(bi):
        r_first, u0, span, _ = blocks[bi]
        qs, ks, nk = r_first * GRID_W, u0 * GRID_W, span * GRID_W
        q2 = q_ref[qs:qs + nq, :] * Q_SCALE
        ku = k_ref[ks:ks + nk, :].astype(BF16)
        for hh in range(2):
            qh = jnp.where(lo if hh == 0 else hi, q2, 0.0).astype(BF16)
            s_scr[bi % 2, hh, 0:nk, :] = _dot_nt(ku, qh)
            s_scr[bi % 2, hh, ctx0:ctx0 + past, :] = _dot_nt(kc_rows, qh)

    def fold_rows(x, op):
        parts = [x[r:r + 8] for r in range(0, GRID_W, 8)]
        while len(parts) > 1:
            parts = [op(parts[i], parts[i + 1]) for i in range(0, len(parts), 2)]
        return parts[0]

    def softmax(bi):
        par = bi % 2
        inv = []
        for hh in range(2):
            parts = []
            for c in range(QROWS // 2):
                cs = slice(c * LANES, (c + 1) * LANES)
                tiles = window_tiles(bi, c, 2 * p + hh)
                m_acc = None
                for rs, bias in tiles:
                    if bias is None:
                        continue
                    s = s_scr[par, hh, rs, cs] + bias
                    s_scr[par, hh, rs, cs] = s
                    t = fold_rows(s, jnp.maximum)
                    m_acc = t if m_acc is None else jnp.maximum(m_acc, t)
                for rs in ctx_tiles:
                    m_acc = jnp.maximum(m_acc, fold_rows(s_scr[par, hh, rs, cs], jnp.maximum))
                m = jnp.max(m_acc, axis=0, keepdims=True)
                l_acc = None
                for rs, bias in tiles:
                    if bias is None:
                        p_scr[par, hh, rs, cs] = jnp.zeros((GRID_W, LANES), BF16)
                        continue
                    e = jnp.exp2(s_scr[par, hh, rs, cs] - m)
                    p_scr[par, hh, rs, cs] = e.astype(BF16)
                    t = fold_rows(e, jnp.add)
                    l_acc = t if l_acc is None else l_acc + t
                for rs in ctx_tiles:
                    e = jnp.exp2(s_scr[par, hh, rs, cs] - m)
                    p_scr[par, hh, rs, cs] = e.astype(BF16)
                    l_acc = l_acc + fold_rows(e, jnp.add)
                parts.append(1.0 / jnp.sum(l_acc, axis=0, keepdims=True))
            inv.append(jnp.concatenate(parts, axis=1))
        return inv

    def values(bi, inv):
        r_first, u0, span, _ = blocks[bi]
        qs, ks, nk = r_first * GRID_W, u0 * GRID_W, span * GRID_W
        vut = v_ref[ks:ks + nk, :].T.astype(BF16)
        outs = []
        for hh in range(2):
            o = (_dot(vut[head_rows[hh]], p_scr[bi % 2, hh, 0:nk, :])
                 + _dot(vct[head_rows[hh]], p_scr[bi % 2, hh, ctx0:ctx0 + past, :]))
            outs.append(o * inv[hh])
        o2 = jnp.concatenate(outs, axis=0).T
        o_ref[qs:qs + nq, :] = (o2 * _silu(g_ref[qs:qs + nq, :])).astype(BF16)

    scores(0)
    for bi in range(len(blocks)):
        if bi + 1 < len(blocks):
            scores(bi + 1)
        values(bi, softmax(bi))


def _bias_base(rpb_l):
    mid = WIN_W - 1
    zeros = jnp.zeros(rpb_l.shape[:2] + (LANES - (2 * WIN_W - 1),), F32)
    rev = rpb_l[..., ::-1]
    base = jnp.concatenate([rev[..., mid:], zeros, rev[..., :mid]], axis=-1)
    return jnp.pad(base, ((0, 0), (0, 2 * WIN_H - N_REL_ROWS), (0, 0)))


def _neighbourhood_attention(pa, seqs, cache_kt, cache_vt, layer, rpb_l):
    b0, b = seqs
    n = pa.shape[1]
    rows = n // GRID_W
    past = cache_kt.shape[-1]
    blocks, _ = _nbr_blocks(rows)
    max_nk = max(s for (_, _, s, _) in blocks) * GRID_W
    npair = A_HEADS // 2
    col = lambda off: (lambda i, p: (b0 + i, 0, off + p))
    cache_spec = pl.BlockSpec((None, None, 2, HEAD_DIM, past), lambda i, p: (i, layer, p, 0, 0))
    return pl.pallas_call(
        functools.partial(_nbr_attn_kernel, rows=rows),
        out_shape=jax.ShapeDtypeStruct((b, n, A_WIDTH), BF16),
        grid=(b, npair),
        in_specs=[
            pl.BlockSpec((None, n, LANES), col(0)),
            pl.BlockSpec((None, n, LANES), col(npair)),
            pl.BlockSpec((None, n, LANES), col(2 * npair)),
            pl.BlockSpec((None, n, LANES), col(3 * npair)),
            cache_spec,
            cache_spec,
            pl.BlockSpec((A_HEADS, 2 * WIN_H, LANES), lambda i, p: (0, 0, 0)),
        ],
        out_specs=pl.BlockSpec((None, n, LANES), lambda i, p: (i, 0, p)),
        scratch_shapes=[
            pltpu.VMEM((2, 2, max_nk + past, QROWS * GRID_W), F32),
            pltpu.VMEM((2, 2, max_nk + past, QROWS * GRID_W), BF16),
            pltpu.VMEM((A_HEADS, TABLE_ROWS, LANES), F32),
            pltpu.VMEM((A_HEADS, TABLE_ROWS, LANES), F32),
        ],
        compiler_params=_cparams(("arbitrary", "arbitrary")),
        name="neighbourhood_attention",
    )(pa, pa, pa, pa, cache_kt, cache_vt, _bias_base(rpb_l))


COARSE_HALVES = (32, 16, 8)
FINE_HALVES = (4, 2, 1)
N_LEVELS = len(COARSE_HALVES) + len(FINE_HALVES)
ANCHOR_BLOCK = CHUNK
ANCHOR_LEVELS = 0
ANCHOR_MAX_EXPONENT = 80.0
MASK_DIAG = N_LEVELS
MASK_ANCHOR = N_LEVELS + 1
GROUP = 4


def _hgrn_constants():
    c = CHUNK
    idx = np.arange(c)
    mats = [np.tril(np.ones((c, c)))]
    masks = []
    for h in COARSE_HALVES + FINE_HALVES:
        blk = idx // (2 * h)
        mid = blk * 2 * h + h - 1
        upper = idx > mid
        if h in FINE_HALVES:
            m = np.zeros((c, c))
            for i in range(c):
                if upper[i]:
                    m[i, mid[i] + 1:i + 1] = 1.0
                else:
                    m[i, i + 1:mid[i] + 1] = 1.0
            mats.append(m)
        same = blk[:, None] == blk[None, :]
        masks.append((same & upper[:, None] & (~upper)[None, :]).astype(np.float64))
    masks.append(np.eye(c))
    same_block = (idx[:, None] // ANCHOR_BLOCK) == (idx[None, :] // ANCHOR_BLOCK)
    masks.append((same_block & (idx[None, :] <= idx[:, None])).astype(np.float64))
    fwd = np.concatenate(mats, axis=0)
    bwd = np.concatenate([m[::-1, ::-1] for m in mats], axis=0)
    mk_f = np.stack([np.tile(m, (1, R_HEADS)) for m in masks])
    mk_b = np.stack([np.tile(m[::-1, ::-1], (1, R_HEADS)) for m in masks])
    hid = np.arange(R_WIDTH) // R_DIM
    bd = (hid[:, None] == hid[None, :]).astype(np.float64)
    return (jnp.asarray(np.stack([fwd, bwd]), BF16), jnp.asarray(np.stack([mk_f, mk_b]), F32),
            jnp.asarray(bd, BF16))


def _hgrn_kernel(*refs, nc, has_init):
    pr_ref, lbp_ref, gh_ref, mstk_ref, lmask_ref, bd_ref = refs[:6]
    if has_init:
        s0_ref, o_ref = refs[6:8]
        sfin_ref = None
        scratch = refs[8:]
    else:
        s0_ref = None
        o_ref, sfin_ref = refs[7:9]
        scratch = refs[9:]
    q_scr, k_scr, g_scr, gc_scr, qg_scr, upd_scr, dec_scr, sbd_scr = scratch
    w = R_WIDTH
    c = CHUNK
    bd = bd_ref[...]
    hid = lax.broadcasted_iota(jnp.int32, (1, w), 1) // R_DIM

    q_scr[...] = _silu(pr_ref[:, 0:w])
    for d in range(2):
        z = pr_ref[:, (1 + d) * w:(2 + d) * w]
        e = jnp.exp(-jnp.abs(z))
        r = 1.0 / (1.0 + e)
        pos = z >= 0.0
        one_m_lb = lbp_ref[2, d:d + 1, :]
        f = lbp_ref[0, d:d + 1, :] + one_m_lb * (jnp.where(pos, 1.0, e) * r)
        log_f = jnp.where(f > 0.0, jnp.log(f), lbp_ref[1, d:d + 1, :] + z)
        g_hi, g_lo = _split2(log_f)
        g_scr[d, :, 0:w] = g_hi
        g_scr[d, :, w:2 * w] = g_lo
        k_scr[d] = one_m_lb * (jnp.where(pos, e, 1.0) * r)

    def tile4(x):
        return jnp.concatenate([x] * R_HEADS, axis=0)

    def head_diagonal(full):
        out = full[(R_HEADS - 1) * R_DIM:R_HEADS * R_DIM]
        for h in range(R_HEADS - 2, -1, -1):
            out = jnp.where(hid == h, full[h * R_DIM:(h + 1) * R_DIM], out)
        return out

    def chunk_rows(ci):
        return pl.ds(pl.multiple_of(ci * c, c), c)

    def cumulative(d, rs, n_mats):
        s = _dot(mstk_ref[d, 0:n_mats * c, :], g_scr[d, rs, :])
        return s[:, 0:w] + s[:, w:2 * w]

    def level_exponents(gcum, d, halves):
        out = []
        for h in halves:
            parts = []
            for s0 in range(0, c, 2 * h):
                anchor = s0 + h - 1 + d
                parts.append(-jnp.abs(gcum[s0:s0 + 2 * h] - gcum[anchor:anchor + 1]))
            out.append(parts[0] if len(parts) == 1 else jnp.concatenate(parts, axis=0))
        return out

    def anchor_shift(gcum, d):
        parts = []
        for s0 in range(0, c, ANCHOR_BLOCK):
            anchor = s0 + ANCHOR_BLOCK // 2 - 1 + d
            parts.append(gcum[s0:s0 + ANCHOR_BLOCK] - gcum[anchor:anchor + 1])
        return jnp.concatenate(parts, axis=0)

    groups = nc // GROUP
    pairs = [(j, d) for j in range(GROUP) for d in range(2)]

    def increment_group(gi, worst):
        rows = [chunk_rows(gi * GROUP + j) for j in range(GROUP)]
        gcums = [cumulative(d, rows[j], 1) for j, d in pairs]
        lasts = [g[c - 1:c, :] if d == 0 else g[0:1, :] for g, (j, d) in zip(gcums, pairs)]
        kls = [(k_scr[d, rows[j], :] * jnp.exp(last - g)).astype(BF16)
               for g, last, (j, d) in zip(gcums, lasts, pairs)]
        vs = [pr_ref[rows[j], 3 * w:4 * w].astype(BF16) for j in range(GROUP)]
        upds = [_dot_tn(vs[j], kl) for kl, (j, d) in zip(kls, pairs)]
        for g, last, upd, (j, d) in zip(gcums, lasts, upds, pairs):
            ci = gi * GROUP + j
            gc_scr[d, rows[j], :] = g
            upd_scr[d, ci] = head_diagonal(upd)
            dec_scr[d, ci] = jnp.broadcast_to(jnp.exp(last), (8, w))
            qg_scr[d, rows[j], :] = (q_scr[rows[j], :] * jnp.exp(g)).astype(BF16)
            sh = jnp.abs(anchor_shift(g, d))
            for s0 in range(0, c, 8):
                worst = jnp.maximum(worst, sh[s0:s0 + 8])
        return worst

    worst = lax.fori_loop(0, groups, increment_group, jnp.zeros((8, w), F32))
    q_max = jnp.max(jnp.max(jnp.abs(q_scr[...]), axis=0, keepdims=True), axis=1, keepdims=True)
    anchor_ok = jnp.max(worst + jnp.log(jnp.maximum(q_max, 1.0))) < ANCHOR_MAX_EXPONENT

    def scan_step(ci, carry):
        sf, sb = carry
        cb = nc - 1 - ci
        sbd_scr[0, ci] = tile4(sf.astype(BF16)) * bd
        sbd_scr[1, cb] = tile4(sb.astype(BF16)) * bd
        sf = sf * dec_scr[0, ci][0:1, :] + upd_scr[0, ci]
        sb = sb * dec_scr[1, cb][0:1, :] + upd_scr[1, cb]
        return sf, sb

    if has_init:
        init = tuple(jnp.concatenate([s0_ref[d], jnp.zeros((w, LANES - R_DIM), F32)], axis=1).T[0:R_DIM]
                     for d in range(2))
    else:
        init = (jnp.zeros((R_DIM, w), F32), jnp.zeros((R_DIM, w), F32))
    finals = lax.fori_loop(0, nc, scan_step, init)
    if sfin_ref is not None:
        for d in range(2):
            padded = jnp.concatenate([finals[d], jnp.zeros((LANES - R_DIM, w), F32)], axis=0)
            sfin_ref[d] = padded.T[:, 0:R_DIM]

    def finish(rs, o):
        x2_hi, x2_lo = _split2(o * o)
        ms = (_dot(x2_hi, bd) + _dot(x2_lo, bd)) * (1.0 / R_DIM)
        y = o * lax.rsqrt(ms + EPS) * gh_ref[...]
        o_ref[rs, :] = (y * _silu(pr_ref[rs, 4 * w:5 * w])).astype(BF16)

    def readout_group_anchor(gi, carry):
        rows = [chunk_rows(gi * GROUP + j) for j in range(GROUP)]
        qs = [q_scr[rows[j], :] for j in range(GROUP)]
        gcums = [gc_scr[d, rows[j], :] for j, d in pairs]
        factors = []
        for g, (j, d) in zip(gcums, pairs):
            es = [jnp.exp(x) for x in level_exponents(g, d, COARSE_HALVES[:ANCHOR_LEVELS])]
            sh = anchor_shift(g, d)
            fs = [(e, e, lv) for lv, e in enumerate(es)]
            fs.append((jnp.exp(sh), jnp.exp(-sh), MASK_ANCHOR))
            factors.append(fs)
        kbds = [tile4(k_scr[d, rows[j], :].astype(BF16)) * bd for j, d in pairs]
        accs = [None] * len(pairs)
        for lv in range(ANCHOR_LEVELS + 1):
            for i, (j, d) in enumerate(pairs):
                eq, ek, mask = factors[i][lv]
                part = jnp.where(lmask_ref[d, mask] > 0.5,
                                 _dot_nt((qs[j] * eq).astype(BF16), kbds[i] * tile4(ek.astype(BF16))), 0.0)
                accs[i] = part if accs[i] is None else accs[i] + part
        vbds = [tile4(pr_ref[rows[j], 3 * w:4 * w].astype(BF16)) * bd for j in range(GROUP)]
        outs = [None] * GROUP
        for i, (j, d) in enumerate(pairs):
            ci = gi * GROUP + j
            od = _dot(accs[i].astype(BF16), vbds[j]) + _dot_nt(qg_scr[d, rows[j], :], sbd_scr[d, ci])
            outs[j] = od if outs[j] is None else outs[j] + od
        for j in range(GROUP):
            finish(rows[j], outs[j])
        return carry

    def readout_step_split(ci, carry):
        rs = chunk_rows(ci)
        q = q_scr[rs, :]
        qb = q.astype(BF16)
        vbd = tile4(pr_ref[rs, 3 * w:4 * w].astype(BF16)) * bd
        dsts = [cumulative(d, rs, 1 + len(FINE_HALVES)) for d in range(2)]
        exps = [level_exponents(dsts[d][0:c], d, COARSE_HALVES)
                + [dsts[d][(1 + j) * c:(2 + j) * c] for j in range(len(FINE_HALVES))] for d in range(2)]
        kbd = [tile4(k_scr[d, rs, :].astype(BF16)) * bd for d in range(2)]
        a = [_dot_nt(qb, kbd[d]) * lmask_ref[d, MASK_DIAG] for d in range(2)]
        for j in range(N_LEVELS):
            for d in range(2):
                e = jnp.exp(exps[d][j])
                a[d] = a[d] + _dot_nt((q * e).astype(BF16), kbd[d] * tile4(e.astype(BF16))) * lmask_ref[d, j]
        o = None
        for d in range(2):
            od = _dot(a[d].astype(BF16), vbd) + _dot_nt(qg_scr[d, rs, :], sbd_scr[d, ci])
            o = od if o is None else o + od
        finish(rs, o)
        return carry

    @pl.when(anchor_ok)
    def _():
        lax.fori_loop(0, groups, readout_group_anchor, 0)

    @pl.when(jnp.logical_not(anchor_ok))
    def _():
        lax.fori_loop(0, nc, readout_step_split, 0)


def _hgrn(pr, seqs, lbp_l, g_hgrn_l, consts, layer, state=None, new_state=None):
    b0, b = seqs
    n = pr.shape[1]
    nc = n // CHUNK
    mstk, lmask, bd = consts
    w = R_WIDTH
    has_init = state is not None
    full = lambda *shape: pl.BlockSpec(shape, lambda i: (0,) * len(shape))
    in_specs = [
        pl.BlockSpec((None, n, PR_COLS), lambda i: (b0 + i, 0, 0)),
        full(3, 2, w),
        full(1, w),
        full(*mstk.shape),
        full(*lmask.shape),
        full(w, w),
    ]
    args = [pr, lbp_l, g_hgrn_l.reshape(1, w), mstk, lmask, bd]
    out_shape = [jax.ShapeDtypeStruct((b, n, w), BF16)]
    out_specs = [pl.BlockSpec((None, n, w), lambda i: (i, 0, 0))]
    state_spec = pl.BlockSpec((None, None, 2, w, R_DIM), lambda i: (i, layer, 0, 0, 0))
    aliases = {}
    if has_init:
        in_specs.append(state_spec)
        args.append(state)
    else:
        in_specs.append(pl.BlockSpec(memory_space=pl.ANY))
        args.append(new_state)
        out_shape.append(jax.ShapeDtypeStruct(new_state.shape, F32))
        out_specs.append(state_spec)
        aliases = {6: 1}
    return pl.pallas_call(
        functools.partial(_hgrn_kernel, nc=nc, has_init=has_init),
        out_shape=tuple(out_shape),
        grid=(b,),
        in_specs=in_specs,
        out_specs=tuple(out_specs),
        input_output_aliases=aliases,
        scratch_shapes=[
            pltpu.VMEM((n, w), F32),
            pltpu.VMEM((2, n, w), F32),
            pltpu.VMEM((2, n, 2 * w), BF16),
            pltpu.VMEM((2, n, w), F32),
            pltpu.VMEM((2, n, w), BF16),
            pltpu.VMEM((2, nc, R_DIM, w), F32),
            pltpu.VMEM((2, nc, 8, w), F32),
            pltpu.VMEM((2, nc, w, w), BF16),
        ],
        compiler_params=_cparams(("arbitrary",)),
        name="hgrn_scan",
    )(*args)


def _fnet_constants(n):
    j = np.arange(F_GROUP_DIM)
    ang = 2.0 * np.pi * ((j[:, None] * j[None, :]) % F_GROUP_DIM) / F_GROUP_DIM
    eye = np.eye(F_GROUPS)
    cs = np.concatenate([np.kron(eye, np.cos(ang)), np.kron(eye, np.sin(ang))], axis=1)
    t = np.arange(n)
    angn = 2.0 * np.pi * ((t[:, None] * t[None, :]) % n) / n
    return tuple(jnp.asarray(m, F32).astype(BF16) for m in (cs, np.cos(angn), np.sin(angn)))


def _fnet_kernel(pf_ref, cs_ref, cn_ref, sn_ref, wf_ref, o_ref, *, scale):
    w = F_WIDTH
    for s in range(pf_ref.shape[0]):
        t = _dot(pf_ref[s, :, 0:w].astype(BF16), cs_ref[...])
        y = (_dot(cn_ref[...], t[:, 0:w].astype(BF16)) - _dot(sn_ref[...], t[:, w:2 * w].astype(BF16))) * scale
        of = _dot(y.astype(BF16), wf_ref[...])
        o_ref[s] = (of * _silu(pf_ref[s, :, w:2 * w])).astype(BF16)


def _fourier(pf, seqs, consts, w_fnet_bf16):
    b0, b = seqs
    n = pf.shape[1]
    cs, cn, sn = consts
    w = F_WIDTH
    g = max(1, min(b, IN_TILE * 2 // n))
    assert b % g == 0 and b0 % g == 0
    full = lambda *shape: pl.BlockSpec(shape, lambda i: (0,) * len(shape))
    return pl.pallas_call(
        functools.partial(_fnet_kernel, scale=float((n * F_GROUP_DIM) ** -0.5)),
        out_shape=jax.ShapeDtypeStruct((b, n, w), BF16),
        grid=(b // g,),
        in_specs=[pl.BlockSpec((g, n, PF_COLS), lambda i: (b0 // g + i, 0, 0)),
                  full(w, 2 * w), full(n, n), full(n, n), full(w, w)],
        out_specs=pl.BlockSpec((g, n, w), lambda i: (i, 0, 0)),
        compiler_params=_cparams(("arbitrary",)),
        name="fourier_mixing",
    )(pf, cs, cn, sn, w_fnet_bf16)


def _outproj_kernel(mac_ref, mrc_ref, mfc_ref, mal_ref, mrl_ref, mfl_ref, xc_ref, xl_ref, mod_ref, g_ref, w_ref,
                    yc_ref, yl_ref, wb_scr, *, ctx_tiles):
    i = pl.program_id(0)

    @pl.when(i == 0)
    def _():
        wb_scr[...] = w_ref[...].astype(BF16)

    is_ctx = i < ctx_tiles
    pick = lambda c_ref, l_ref: jnp.where(is_ctx, c_ref[...], l_ref[...])
    out = (_dot(pick(mac_ref, mal_ref), wb_scr[0:A_WIDTH, :])
           + _dot(pick(mrc_ref, mrl_ref), wb_scr[A_WIDTH:A_WIDTH + R_WIDTH, :])
           + _dot(pick(mfc_ref, mfl_ref), wb_scr[A_WIDTH + R_WIDTH:D_MODEL, :]))
    ms = jnp.mean(out * out, axis=-1, keepdims=True)
    y = pick(xc_ref, xl_ref) + mod_ref[2:3, :] * (out * lax.rsqrt(ms + EPS) * g_ref[...])

    @pl.when(is_ctx)
    def _():
        yc_ref[...] = y

    @pl.when(jnp.logical_not(is_ctx))
    def _():
        yl_ref[...] = y


def _out_projection(mixed_ctx, mixed_lat, x_ctx, x_lat, mod, g_post, w_out, layer):
    bc, nc_, _ = x_ctx.shape
    bl, nl, _ = x_lat.shape
    tm = IN_TILE
    lat_tiles_per_seq = nl // tm
    ctx_tiles = bc * nc_ // tm
    lat_tiles = bl * nl // tm
    last_ctx = ctx_tiles - 1
    ctx_idx = lambda i: (jnp.minimum(i, last_ctx), 0)
    lat_idx = lambda i: (jnp.maximum(i - ctx_tiles, 0), 0)
    mod_idx = lambda i: (jnp.where(i < ctx_tiles, 0, 1 + jnp.maximum(i - ctx_tiles, 0) // lat_tiles_per_seq), 0, 0)
    widths = (A_WIDTH, R_WIDTH, F_WIDTH)
    flat = lambda t: t.reshape(-1, t.shape[-1])
    yc, yl = pl.pallas_call(
        functools.partial(_outproj_kernel, ctx_tiles=ctx_tiles),
        out_shape=(jax.ShapeDtypeStruct((bc * nc_, D_MODEL), F32), jax.ShapeDtypeStruct((bl * nl, D_MODEL), F32)),
        grid=(ctx_tiles + lat_tiles,),
        in_specs=(
            [pl.BlockSpec((tm, wd), ctx_idx) for wd in widths]
            + [pl.BlockSpec((tm, wd), lat_idx) for wd in widths]
            + [
                pl.BlockSpec((tm, D_MODEL), ctx_idx),
                pl.BlockSpec((tm, D_MODEL), lat_idx),
                pl.BlockSpec((None, 3, D_MODEL), mod_idx),
                pl.BlockSpec((None, 1, D_MODEL), lambda i: (layer, 0, 0)),
                pl.BlockSpec((None, D_MODEL, D_MODEL), lambda i: (layer, 0, 0), pipeline_mode=pl.Buffered(1)),
            ]),
        out_specs=(pl.BlockSpec((tm, D_MODEL), ctx_idx), pl.BlockSpec((tm, D_MODEL), lat_idx)),
        scratch_shapes=[pltpu.VMEM((D_MODEL, D_MODEL), BF16)],
        compiler_params=_cparams(("arbitrary",)),
        name="out_projection",
    )(*[flat(t) for t in mixed_ctx], *[flat(t) for t in mixed_lat], flat(x_ctx), flat(x_lat), mod,
      g_post.reshape(DEPTH, 1, D_MODEL), w_out)
    return yc.reshape(x_ctx.shape), yl.reshape(x_lat.shape)


def kernel(x_prompt, x_sample, cache_attn_k, cache_attn_v, state_hgrn, c, c_ctx,
           w_ada, b_ada, g_pre, w_in, rpb, lb_logits, g_hgrn, w_fnet, w_out, g_post):
    nb_ctx, n_ctx, _ = x_prompt.shape
    nb_lat, n_lat, _ = x_sample.shape

    pad_rows = (-(1 + nb_lat)) % 8
    cc = jnp.concatenate([c_ctx[None, :], c, jnp.zeros((pad_rows, D_MODEL), F32)], axis=0)
    mods = _modulations(cc, w_ada, b_ada)
    lbp = _lower_bounds(lb_logits)

    w_fnet_b = w_fnet.astype(BF16)
    hconsts = _hgrn_constants()
    fconsts_ctx = _fnet_constants(n_ctx)
    fconsts_lat = _fnet_constants(n_lat)

    state_rows = state_hgrn.reshape(nb_lat, DEPTH, 2, R_WIDTH, R_DIM)
    cache_kt = jnp.transpose(cache_attn_k, (0, 1, 3, 4, 2))
    cache_vt = jnp.transpose(cache_attn_v, (0, 1, 3, 4, 2))

    yp, ys = x_prompt, x_sample
    new_kv = [jnp.zeros((nb_ctx, DEPTH, A_HEADS, HEAD_DIM, n_ctx), F32) for _ in range(2)]
    new_rows = jnp.zeros((nb_ctx, DEPTH, 2, R_WIDTH, R_DIM), F32)
    ctx_tokens = nb_ctx * n_ctx
    ctx_seqs = (0, nb_ctx)
    lat_seqs = (ctx_tokens // n_lat, nb_lat)
    for l in range(DEPTH):
        lbp_l = lbp[:, l]
        mod_all = mods[l, 0:1 + nb_lat].reshape(1 + nb_lat, 3, D_MODEL)
        pa, pr, pf, *new_kv = _in_projection(yp, ys, mod_all, g_pre, w_in, l, new_kv)
        as_ctx = lambda t: t.reshape(-1, n_ctx, t.shape[-1])
        ma_c = _context_attention(as_ctx(pa), ctx_seqs)
        mr_c, new_rows = _hgrn(as_ctx(pr), ctx_seqs, lbp_l, g_hgrn[l], hconsts, l, new_state=new_rows)
        mf_c = _fourier(as_ctx(pf), ctx_seqs, fconsts_ctx, w_fnet_b[l])
        as_lat = lambda t: t.reshape(-1, n_lat, t.shape[-1])
        ma_l = _neighbourhood_attention(as_lat(pa), lat_seqs, cache_kt, cache_vt, l, rpb[l])
        (mr_l,) = _hgrn(as_lat(pr), lat_seqs, lbp_l, g_hgrn[l], hconsts, l, state=state_rows)
        mf_l = _fourier(as_lat(pf), lat_seqs, fconsts_lat, w_fnet_b[l])
        yp, ys = _out_projection((ma_c, mr_c, mf_c), (ma_l, mr_l, mf_l), yp, ys, mod_all, g_post, w_out, l)

    new_state = new_rows.reshape(nb_ctx, DEPTH, 2, R_HEADS, R_DIM, R_DIM)
    new_k, new_v = (jnp.transpose(t, (0, 1, 4, 2, 3)) for t in new_kv)
    return (yp, ys, new_k, new_v, new_state)
```

```python
import functools

import numpy as np
import jax
import jax.numpy as jnp
from jax import lax
from jax.experimental import pallas as pl
from jax.experimental.pallas import tpu as pltpu

F32 = jnp.float32
BF16 = jnp.bfloat16

D_MODEL = 1024
DEPTH = 4
GRID_W = 64
WIN_H = 8
WIN_W = 16
HEAD_DIM = 64
A_HEADS = 8
A_WIDTH = A_HEADS * HEAD_DIM
R_HEADS = 4
R_DIM = 64
R_WIDTH = R_HEADS * R_DIM
F_GROUPS = 4
F_GROUP_DIM = 64
F_WIDTH = F_GROUPS * F_GROUP_DIM
PA_COLS = 4 * A_WIDTH
PR_COLS = 5 * R_WIDTH
PF_COLS = 2 * F_WIDTH
IN_COLS = PA_COLS + PR_COLS + PF_COLS
CHUNK = 64
EPS = 1e-6
LANES = 128
NEG_INF = float("-inf")
VMEM_LIMIT = 56 * 1024 * 1024


def _cparams(sem):
    return pltpu.CompilerParams(dimension_semantics=sem, vmem_limit_bytes=VMEM_LIMIT)


def _silu(x):
    return x * (1.0 / (1.0 + jnp.exp(-x)))


def _dot(a, b):
    return jnp.dot(a, b, preferred_element_type=F32)


def _dot_nt(a, b):
    return lax.dot_general(a, b, (((1,), (1,)), ((), ())), preferred_element_type=F32)


def _dot_tn(a, b):
    return lax.dot_general(a, b, (((0,), (0,)), ((), ())), preferred_element_type=F32)


def _split2(x):
    hi = x.astype(BF16)
    lo = (x - hi.astype(F32)).astype(BF16)
    return hi, lo


def _mod_kernel(cc_ref, w_ref, b_ref, o_ref):
    a_hi, a_lo = _split2(_silu(cc_ref[...]))
    w_hi, w_lo = _split2(w_ref[...])
    acc = _dot(a_hi, w_hi) + _dot(a_hi, w_lo) + _dot(a_lo, w_hi)
    o_ref[...] = acc + b_ref[...]


def _modulations(cc, w_ada, b_ada):
    rows = cc.shape[0]
    tn = 3 * D_MODEL // 2
    return pl.pallas_call(
        _mod_kernel,
        out_shape=jax.ShapeDtypeStruct((DEPTH, rows, 3 * D_MODEL), F32),
        grid=(DEPTH, 3 * D_MODEL // tn),
        in_specs=[
            pl.BlockSpec((rows, D_MODEL), lambda l, j: (0, 0)),
            pl.BlockSpec((None, D_MODEL, tn), lambda l, j: (l, 0, j)),
            pl.BlockSpec((None, 1, tn), lambda l, j: (l, 0, j)),
        ],
        out_specs=pl.BlockSpec((None, rows, tn), lambda l, j: (l, 0, j)),
        compiler_params=_cparams(("arbitrary", "arbitrary")),
        name="adaln_mod",
    )(cc, w_ada, b_ada.reshape(DEPTH, 1, 3 * D_MODEL))


def _lb_kernel(x_ref, o_ref):
    xs = [x_ref[i] for i in range(DEPTH)]
    m = functools.reduce(jnp.maximum, xs)
    es = [jnp.exp(x - m) for x in xs]
    tot = functools.reduce(lambda a, b: a + b, es)
    cum = None
    first = None
    for i in range(DEPTH):
        p = es[i] / tot
        cum = p if cum is None else cum + p
        if first is None:
            first = cum
        lb = jnp.maximum(cum - first, 0.0)
        o_ref[0, i] = lb
        o_ref[1, i] = jnp.log1p(-lb)
        o_ref[2, i] = 1.0 - lb


def _lower_bounds(lb_logits):
    x = jnp.transpose(lb_logits, (1, 0, 2))
    return pl.pallas_call(
        _lb_kernel,
        out_shape=jax.ShapeDtypeStruct((3, DEPTH, 2, R_WIDTH), F32),
        name="hgrn_lower_bounds",
    )(x)


IN_TILE = 512


def _inproj_kernel(xc_ref, xl_ref, mod_ref, g_ref, w_ref, kprev_ref, vprev_ref,
                   pa_ref, pr_ref, pf_ref, ko_ref, vo_ref, wb_scr, *, ctx_tiles):
    del kprev_ref, vprev_ref
    i = pl.program_id(0)

    @pl.when(i == 0)
    def _():
        wb_scr[...] = w_ref[...].astype(BF16)

    is_ctx = i < ctx_tiles
    x = jnp.where(is_ctx, xc_ref[...], xl_ref[...])
    ms = jnp.mean(x * x, axis=-1, keepdims=True)
    y = x * lax.rsqrt(ms + EPS) * g_ref[...]
    h = (y * (1.0 + mod_ref[1:2, :]) + mod_ref[0:1, :]).astype(BF16)
    pa_ref[...] = _dot(h, wb_scr[:, 0:PA_COLS])
    pr_ref[...] = _dot(h, wb_scr[:, PA_COLS:PA_COLS + PR_COLS])
    pf_ref[...] = _dot(h, wb_scr[:, PA_COLS + PR_COLS:IN_COLS])

    @pl.when(is_ctx)
    def _():
        seqs, _, _, n = ko_ref.shape
        for s in range(seqs):
            rows = slice(s * n, (s + 1) * n)
            ko_ref[s] = pa_ref[rows, A_WIDTH:2 * A_WIDTH].T.reshape(A_HEADS, HEAD_DIM, n)
            vo_ref[s] = pa_ref[rows, 2 * A_WIDTH:3 * A_WIDTH].T.reshape(A_HEADS, HEAD_DIM, n)


def _in_projection(x_ctx, x_lat, mod, g_pre, w_in, layer, cache):
    bc, nc_, _ = x_ctx.shape
    bl, nl, _ = x_lat.shape
    tm = IN_TILE
    seqs_per_tile = tm // nc_
    lat_tiles_per_seq = nl // tm
    ctx_tiles = bc * nc_ // tm
    lat_tiles = bl * nl // tm
    tokens = (ctx_tiles + lat_tiles) * tm
    last_ctx = ctx_tiles - 1
    ctx_idx = lambda i: (jnp.minimum(i, last_ctx), 0)
    lat_idx = lambda i: (jnp.maximum(i - ctx_tiles, 0), 0)
    mod_idx = lambda i: (jnp.where(i < ctx_tiles, 0, 1 + jnp.maximum(i - ctx_tiles, 0) // lat_tiles_per_seq), 0, 0)
    tok = lambda i: (i, 0)
    cache_spec = pl.BlockSpec((seqs_per_tile, None, A_HEADS, HEAD_DIM, nc_),
                              lambda i: (jnp.minimum(i, last_ctx), layer, 0, 0, 0))
    cache_shape = jax.ShapeDtypeStruct(cache[0].shape, F32)
    return pl.pallas_call(
        functools.partial(_inproj_kernel, ctx_tiles=ctx_tiles),
        out_shape=(
            jax.ShapeDtypeStruct((tokens, PA_COLS), F32),
            jax.ShapeDtypeStruct((tokens, PR_COLS), F32),
            jax.ShapeDtypeStruct((tokens, PF_COLS), F32),
            cache_shape, cache_shape,
        ),
        grid=(ctx_tiles + lat_tiles,),
        in_specs=[
            pl.BlockSpec((tm, D_MODEL), ctx_idx),
            pl.BlockSpec((tm, D_MODEL), lat_idx),
            pl.BlockSpec((None, 3, D_MODEL), mod_idx),
            pl.BlockSpec((None, 1, D_MODEL), lambda i: (layer, 0, 0)),
            pl.BlockSpec((None, D_MODEL, IN_COLS), lambda i: (layer, 0, 0), pipeline_mode=pl.Buffered(1)),
            pl.BlockSpec(memory_space=pl.ANY),
            pl.BlockSpec(memory_space=pl.ANY),
        ],
        out_specs=(
            pl.BlockSpec((tm, PA_COLS), tok),
            pl.BlockSpec((tm, PR_COLS), tok),
            pl.BlockSpec((tm, PF_COLS), tok),
            cache_spec, cache_spec,
        ),
        scratch_shapes=[pltpu.VMEM((D_MODEL, IN_COLS), BF16)],
        input_output_aliases={5: 3, 6: 4},
        compiler_params=_cparams(("arbitrary",)),
        name="in_projection",
    )(x_ctx.reshape(bc * nc_, D_MODEL), x_lat.reshape(bl * nl, D_MODEL), mod,
      g_pre.reshape(DEPTH, 1, D_MODEL), w_in, cache[0], cache[1])


LOG2E = 1.4426950408889634
Q_SCALE = HEAD_DIM ** -0.5 * LOG2E


def _lane_lo():
    return lax.broadcasted_iota(jnp.int32, (1, LANES), 1) < HEAD_DIM


def _ctx_attn_kernel(pas_ref, os_ref, s_scr):
    for s in range(pas_ref.shape[0]):
        _ctx_attn_sequence(pas_ref.at[s], os_ref.at[s], s_scr)


def _ctx_attn_sequence(pa_ref, o_ref, s_scr):
    lo = _lane_lo()
    npair = A_HEADS // 2

    def scores(p):
        c = p * LANES
        q2 = pa_ref[:, c:c + LANES] * Q_SCALE
        k2 = pa_ref[:, A_WIDTH + c:A_WIDTH + c + LANES].astype(BF16)
        for hh in range(2):
            sel = lo if hh == 0 else jnp.logical_not(lo)
            s_scr[p % 2, hh] = _dot_nt(jnp.where(sel, q2, 0.0).astype(BF16), k2)

    def finish(p):
        c = p * LANES
        v2 = pa_ref[:, 2 * A_WIDTH + c:2 * A_WIDTH + c + LANES].astype(BF16)
        outs = []
        for hh in range(2):
            s = s_scr[p % 2, hh]
            e = jnp.exp2(s - jnp.max(s, axis=-1, keepdims=True))
            inv = 1.0 / jnp.sum(e, axis=-1, keepdims=True)
            outs.append(_dot(e.astype(BF16), v2) * inv)
        o2 = jnp.where(lo, outs[0], outs[1])
        o_ref[:, c:c + LANES] = (o2 * _silu(pa_ref[:, 3 * A_WIDTH + c:3 * A_WIDTH + c + LANES])).astype(BF16)

    scores(0)
    for p in range(npair):
        if p + 1 < npair:
            scores(p + 1)
        finish(p)


def _context_attention(pa, seqs):
    b0, b = seqs
    n = pa.shape[1]
    g = 2 if b % 2 == 0 and b0 % 2 == 0 else 1
    return pl.pallas_call(
        _ctx_attn_kernel,
        out_shape=jax.ShapeDtypeStruct((b, n, A_WIDTH), BF16),
        grid=(b // g,),
        in_specs=[pl.BlockSpec((g, n, PA_COLS), lambda i: (b0 // g + i, 0, 0))],
        out_specs=pl.BlockSpec((g, n, A_WIDTH), lambda i: (i, 0, 0)),
        scratch_shapes=[pltpu.VMEM((2, 2, n, n), F32)],
        compiler_params=_cparams(("arbitrary",)),
        name="context_attention",
    )(pa)


QROWS = 4


def _nbr_blocks(rows):
    kh = min(WIN_H, rows)
    out = []
    for r_first in range(0, rows, QROWS):
        r0s = [min(max(r - kh // 2, 0), rows - kh) for r in range(r_first, r_first + QROWS)]
        lo, hi = min(r0s), max(r0s) + kh
        lo -= lo % 2
        span = hi - lo
        span += (-span) % 4
        if lo + span > rows:
            lo = rows - span
        assert lo >= 0 and lo % 2 == 0
        out.append((r_first, lo, span, r0s))
    return out, kh


N_REL_ROWS = 2 * WIN_H - 1
TABLE_ROWS = WIN_H * GRID_W


def _build_bias_tables(base_ref, ta_scr, tb_scr, heads):
    lane = lax.broadcasted_iota(jnp.int32, (GRID_W, LANES), 1)
    kc = lax.broadcasted_iota(jnp.int32, (GRID_W, LANES), 0)
    qc = lane % GRID_W
    lo = lane < GRID_W
    c0 = jnp.clip(qc - WIN_W // 2, 0, GRID_W - WIN_W)
    col_in = (kc >= c0) & (kc < c0 + WIN_W)

    for h in heads:
        def tile(i, lane_off):
            if not 0 <= i < N_REL_ROWS:
                return jnp.full((GRID_W, LANES), NEG_INF, F32)
            row = jnp.broadcast_to(base_ref[h, i:i + 1, :] * LOG2E, (GRID_W, LANES))
            return pltpu.roll(row, lane_off, 1, stride=1, stride_axis=0)

        for j in range(WIN_H):
            rows = slice(j * GRID_W, (j + 1) * GRID_W)
            ta_scr[h, rows, :] = jnp.where(col_in, jnp.where(lo, tile(2 * j + 1, 0), tile(2 * j, GRID_W)), NEG_INF)
            tb_scr[h, rows, :] = jnp.where(col_in, jnp.where(lo, tile(2 * j, 0), tile(2 * j - 1, GRID_W)), NEG_INF)


def _nbr_attn_kernel(q_ref, k_ref, v_ref, g_ref, kc_ref, vc_ref, base_ref, o_ref, s_scr, p_scr, ta_scr, tb_scr, *, rows):
    p = pl.program_id(1)

    @pl.when(pl.program_id(0) == 0)
    def _():
        _build_bias_tables(base_ref, ta_scr, tb_scr, (2 * p, 2 * p + 1))

    lo = _lane_lo()
    hi = jnp.logical_not(lo)
    blocks, kh = _nbr_blocks(rows)
    nq = QROWS * GRID_W
    past = kc_ref.shape[-1]
    ctx0 = s_scr.shape[2] - past
    kc_rows = kc_ref[...].reshape(LANES, past).T.astype(BF16)
    vct = vc_ref[...].reshape(LANES, past).astype(BF16)
    head_rows = [slice(hh * HEAD_DIM, (hh + 1) * HEAD_DIM) for hh in range(2)]
    ctx_tiles = [slice(ctx0 + t * GRID_W, ctx0 + (t + 1) * GRID_W) for t in range(past // GRID_W)]

    def window_tiles(bi, c, head):
        r_first, u0, span, r0s = blocks[bi]
        out = []
        for t in range(span):
            kr = u0 + t
            rs = slice(t * GRID_W, (t + 1) * GRID_W)
            ok = [r0s[2 * c + e] <= kr < r0s[2 * c + e] + kh for e in range(2)]
            if not (ok[0] or ok[1]):
                out.append((rs, None))
                continue
            i = kr - (r_first + 2 * c) + WIN_H - 1
            assert 0 <= i <= N_REL_ROWS
            tab, j = (ta_scr, (i - 1) // 2) if i % 2 else (tb_scr, i // 2)
            bias = tab[head, j * GRID_W:(j + 1) * GRID_W, :]
            if not ok[0]:
                bias = jnp.where(lo, NEG_INF, bias)
            if not ok[1]:
                bias = jnp.where(hi, NEG_INF, bias)
            out.append((rs, bias))
        return out

    def scores(bi):
        r_first, u0, span, _ = blocks[bi]
        qs, ks, nk = r_first * GRID_W, u0 * GRID_W, span * GRID_W
        q2 = q_ref[qs:qs + nq, :] * Q_SCALE
        ku = k_ref[ks:ks + nk, :].astype(BF16)
        for hh in range(2):
            qh = jnp.where(lo if hh == 0 else hi, q2, 0.0).astype(BF16)
            s_scr[bi % 2, hh, 0:nk, :] = _dot_nt(ku, qh)
            s_scr[bi % 2, hh, ctx0:ctx0 + past, :] = _dot_nt(kc_rows, qh)

    def fold_rows(x, op):
        parts = [x[r:r + 8] for r in range(0, GRID_W, 8)]
        while len(parts) > 1:
            parts = [op(parts[i], parts[i + 1]) for i in range(0, len(parts), 2)]
        return parts[0]

    def softmax(bi):
        par = bi % 2
        inv = []
        for hh in range(2):
            parts = []
            for c in range(QROWS // 2):
                cs = slice(c * LANES, (c + 1) * LANES)
                tiles = window_tiles(bi, c, 2 * p + hh)
                m_acc = None
                for rs, bias in tiles:
                    if bias is None:
                        continue
                    s = s_scr[par, hh, rs, cs] + bias
                    s_scr[par, hh, rs, cs] = s
                    t = fold_rows(s, jnp.maximum)
                    m_acc = t if m_acc is None else jnp.maximum(m_acc, t)
                for rs in ctx_tiles:
                    m_acc = jnp.maximum(m_acc, fold_rows(s_scr[par, hh, rs, cs], jnp.maximum))
                m = jnp.max(m_acc, axis=0, keepdims=True)
                l_acc = None
                for rs, bias in tiles:
                    if bias is None:
                        p_scr[par, hh, rs, cs] = jnp.zeros((GRID_W, LANES), BF16)
                        continue
                    e = jnp.exp2(s_scr[par, hh, rs, cs] - m)
                    p_scr[par, hh, rs, cs] = e.astype(BF16)
                    t = fold_rows(e, jnp.add)
                    l_acc = t if l_acc is None else l_acc + t
                for rs in ctx_tiles:
                    e = jnp.exp2(s_scr[par, hh, rs, cs] - m)
                    p_scr[par, hh, rs, cs] = e.astype(BF16)
                    l_acc = l_acc + fold_rows(e, jnp.add)
                parts.append(1.0 / jnp.sum(l_acc, axis=0, keepdims=True))
            inv.append(jnp.concatenate(parts, axis=1))
        return inv

    def values(bi, inv):
        r_first, u0, span, _ = blocks[bi]
        qs, ks, nk = r_first * GRID_W, u0 * GRID_W, span * GRID_W
        vut = v_ref[ks:ks + nk, :].T.astype(BF16)
        outs = []
        for hh in range(2):
            o = (_dot(vut[head_rows[hh]], p_scr[bi % 2, hh, 0:nk, :])
                 + _dot(vct[head_rows[hh]], p_scr[bi % 2, hh, ctx0:ctx0 + past, :]))
            outs.append(o * inv[hh])
        o2 = jnp.concatenate(outs, axis=0).T
        o_ref[qs:qs + nq, :] = (o2 * _silu(g_ref[qs:qs + nq, :])).astype(BF16)

    scores(0)
    for bi in range(len(blocks)):
        if bi + 1 < len(blocks):
            scores(bi + 1)
        values(bi, softmax(bi))


def _bias_base(rpb_l):
    mid = WIN_W - 1
    zeros = jnp.zeros(rpb_l.shape[:2] + (LANES - (2 * WIN_W - 1),), F32)
    rev = rpb_l[..., ::-1]
    base = jnp.concatenate([rev[..., mid:], zeros, rev[..., :mid]], axis=-1)
    return jnp.pad(base, ((0, 0), (0, 2 * WIN_H - N_REL_ROWS), (0, 0)))


def _neighbourhood_attention(pa, seqs, cache_kt, cache_vt, layer, rpb_l):
    b0, b = seqs
    n = pa.shape[1]
    rows = n // GRID_W
    past = cache_kt.shape[-1]
    blocks, _ = _nbr_blocks(rows)
    max_nk = max(s for (_, _, s, _) in blocks) * GRID_W
    npair = A_HEADS // 2
    col = lambda off: (lambda i, p: (b0 + i, 0, off + p))
    cache_spec = pl.BlockSpec((None, None, 2, HEAD_DIM, past), lambda i, p: (i, layer, p, 0, 0))
    return pl.pallas_call(
        functools.partial(_nbr_attn_kernel, rows=rows),
        out_shape=jax.ShapeDtypeStruct((b, n, A_WIDTH), BF16),
        grid=(b, npair),
        in_specs=[
            pl.BlockSpec((None, n, LANES), col(0)),
            pl.BlockSpec((None, n, LANES), col(npair)),
            pl.BlockSpec((None, n, LANES), col(2 * npair)),
            pl.BlockSpec((None, n, LANES), col(3 * npair)),
            cache_spec,
            cache_spec,
            pl.BlockSpec((A_HEADS, 2 * WIN_H, LANES), lambda i, p: (0, 0, 0)),
        ],
        out_specs=pl.BlockSpec((None, n, LANES), lambda i, p: (i, 0, p)),
        scratch_shapes=[
            pltpu.VMEM((2, 2, max_nk + past, QROWS * GRID_W), F32),
            pltpu.VMEM((2, 2, max_nk + past, QROWS * GRID_W), BF16),
            pltpu.VMEM((A_HEADS, TABLE_ROWS, LANES), F32),
            pltpu.VMEM((A_HEADS, TABLE_ROWS, LANES), F32),
        ],
        compiler_params=_cparams(("arbitrary", "arbitrary")),
        name="neighbourhood_attention",
    )(pa, pa, pa, pa, cache_kt, cache_vt, _bias_base(rpb_l))


COARSE_HALVES = (32, 16, 8)
FINE_HALVES = (4, 2, 1)
N_LEVELS = len(COARSE_HALVES) + len(FINE_HALVES)
ANCHOR_BLOCK = CHUNK
ANCHOR_LEVELS = 0
ANCHOR_MAX_EXPONENT = 80.0
MASK_DIAG = N_LEVELS
MASK_ANCHOR = N_LEVELS + 1
GROUP = 4


def _hgrn_constants():
    c = CHUNK
    idx = np.arange(c)
    mats = [np.tril(np.ones((c, c)))]
    masks = []
    for h in COARSE_HALVES + FINE_HALVES:
        blk = idx // (2 * h)
        mid = blk * 2 * h + h - 1
        upper = idx > mid
        if h in FINE_HALVES:
            m = np.zeros((c, c))
            for i in range(c):
                if upper[i]:
                    m[i, mid[i] + 1:i + 1] = 1.0
                else:
                    m[i, i + 1:mid[i] + 1] = 1.0
            mats.append(m)
        same = blk[:, None] == blk[None, :]
        masks.append((same & upper[:, None] & (~upper)[None, :]).astype(np.float64))
    masks.append(np.eye(c))
    same_block = (idx[:, None] // ANCHOR_BLOCK) == (idx[None, :] // ANCHOR_BLOCK)
    masks.append((same_block & (idx[None, :] <= idx[:, None])).astype(np.float64))
    fwd = np.concatenate(mats, axis=0)
    bwd = np.concatenate([m[::-1, ::-1] for m in mats], axis=0)
    mk_f = np.stack([np.tile(m, (1, R_HEADS)) for m in masks])
    mk_b = np.stack([np.tile(m[::-1, ::-1], (1, R_HEADS)) for m in masks])
    hid = np.arange(R_WIDTH) // R_DIM
    bd = (hid[:, None] == hid[None, :]).astype(np.float64)
    return (jnp.asarray(np.stack([fwd, bwd]), BF16), jnp.asarray(np.stack([mk_f, mk_b]), F32),
            jnp.asarray(bd, BF16))


def _hgrn_kernel(*refs, nc, has_init):
    prs_ref, lbp_ref, gh_ref, mstk_ref, lmask_ref, bd_ref = refs[:6]
    if has_init:
        s0s_ref, os_ref = refs[6:8]
        sfins_ref = None
        scratch = refs[8:]
    else:
        s0s_ref = None
        os_ref, sfins_ref = refs[7:9]
        scratch = refs[9:]
    for s in range(prs_ref.shape[0]):
        _hgrn_sequence(prs_ref.at[s], lbp_ref, gh_ref, mstk_ref, lmask_ref, bd_ref,
                       None if s0s_ref is None else s0s_ref.at[s], os_ref.at[s],
                       None if sfins_ref is None else sfins_ref.at[s], scratch, nc)


def _hgrn_sequence(pr_ref, lbp_ref, gh_ref, mstk_ref, lmask_ref, bd_ref, s0_ref, o_ref, sfin_ref, scratch, nc):
    has_init = s0_ref is not None
    q_scr, k_scr, g_scr, gc_scr, qg_scr, upd_scr, dec_scr, sbd_scr = scratch
    w = R_WIDTH
    c = CHUNK
    bd = bd_ref[...]
    hid = lax.broadcasted_iota(jnp.int32, (1, w), 1) // R_DIM

    q_scr[...] = _silu(pr_ref[:, 0:w])
    for d in range(2):
        z = pr_ref[:, (1 + d) * w:(2 + d) * w]
        e = jnp.exp(-jnp.abs(z))
        r = 1.0 / (1.0 + e)
        pos = z >= 0.0
        one_m_lb = lbp_ref[2, d:d + 1, :]
        f = lbp_ref[0, d:d + 1, :] + one_m_lb * (jnp.where(pos, 1.0, e) * r)
        log_f = jnp.where(f > 0.0, jnp.log(f), lbp_ref[1, d:d + 1, :] + z)
        g_hi, g_lo = _split2(log_f)
        g_scr[d, :, 0:w] = g_hi
        g_scr[d, :, w:2 * w] = g_lo
        k_scr[d] = one_m_lb * (jnp.where(pos, e, 1.0) * r)

    def tile4(x):
        return jnp.concatenate([x] * R_HEADS, axis=0)

    def head_diagonal(full):
        out = full[(R_HEADS - 1) * R_DIM:R_HEADS * R_DIM]
        for h in range(R_HEADS - 2, -1, -1):
            out = jnp.where(hid == h, full[h * R_DIM:(h + 1) * R_DIM], out)
        return out

    def chunk_rows(ci):
        return pl.ds(pl.multiple_of(ci * c, c), c)

    def cumulative(d, rs, n_mats):
        s = _dot(mstk_ref[d, 0:n_mats * c, :], g_scr[d, rs, :])
        return s[:, 0:w] + s[:, w:2 * w]

    def level_exponents(gcum, d, halves):
        out = []
        for h in halves:
            parts = []
            for s0 in range(0, c, 2 * h):
                anchor = s0 + h - 1 + d
                parts.append(-jnp.abs(gcum[s0:s0 + 2 * h] - gcum[anchor:anchor + 1]))
            out.append(parts[0] if len(parts) == 1 else jnp.concatenate(parts, axis=0))
        return out

    def anchor_shift(gcum, d):
        parts = []
        for s0 in range(0, c, ANCHOR_BLOCK):
            anchor = s0 + ANCHOR_BLOCK // 2 - 1 + d
            parts.append(gcum[s0:s0 + ANCHOR_BLOCK] - gcum[anchor:anchor + 1])
        return jnp.concatenate(parts, axis=0)

    groups = nc // GROUP
    pairs = [(j, d) for j in range(GROUP) for d in range(2)]

    def increment_group(gi, worst):
        rows = [chunk_rows(gi * GROUP + j) for j in range(GROUP)]
        gcums = [cumulative(d, rows[j], 1) for j, d in pairs]
        lasts = [g[c - 1:c, :] if d == 0 else g[0:1, :] for g, (j, d) in zip(gcums, pairs)]
        kls = [(k_scr[d, rows[j], :] * jnp.exp(last - g)).astype(BF16)
               for g, last, (j, d) in zip(gcums, lasts, pairs)]
        vs = [pr_ref[rows[j], 3 * w:4 * w].astype(BF16) for j in range(GROUP)]
        upds = [_dot_tn(vs[j], kl) for kl, (j, d) in zip(kls, pairs)]
        for g, last, upd, (j, d) in zip(gcums, lasts, upds, pairs):
            ci = gi * GROUP + j
            gc_scr[d, rows[j], :] = g
            upd_scr[d, ci] = head_diagonal(upd)
            dec_scr[d, ci] = jnp.broadcast_to(jnp.exp(last), (8, w))
            qg_scr[d, rows[j], :] = (q_scr[rows[j], :] * jnp.exp(g)).astype(BF16)
            sh = jnp.abs(anchor_shift(g, d))
            for s0 in range(0, c, 8):
                worst = jnp.maximum(worst, sh[s0:s0 + 8])
        return worst

    worst = lax.fori_loop(0, groups, increment_group, jnp.zeros((8, w), F32))
    q_max = jnp.max(jnp.max(jnp.abs(q_scr[...]), axis=0, keepdims=True), axis=1, keepdims=True)
    anchor_ok = jnp.max(worst + jnp.log(jnp.maximum(q_max, 1.0))) < ANCHOR_MAX_EXPONENT

    def scan_step(ci, carry):
        sf, sb = carry
        cb = nc - 1 - ci
        sbd_scr[0, ci] = sf.astype(BF16)
        sbd_scr[1, cb] = sb.astype(BF16)
        sf = sf * dec_scr[0, ci][0:1, :] + upd_scr[0, ci]
        sb = sb * dec_scr[1, cb][0:1, :] + upd_scr[1, cb]
        return sf, sb

    if has_init:
        init = tuple(jnp.concatenate([s0_ref[d], jnp.zeros((w, LANES - R_DIM), F32)], axis=1).T[0:R_DIM]
                     for d in range(2))
    else:
        init = (jnp.zeros((R_DIM, w), F32), jnp.zeros((R_DIM, w), F32))
    finals = lax.fori_loop(0, nc, scan_step, init)
    if sfin_ref is not None:
        for d in range(2):
            padded = jnp.concatenate([finals[d], jnp.zeros((LANES - R_DIM, w), F32)], axis=0)
            sfin_ref[d] = padded.T[:, 0:R_DIM]

    def finish(rs, o):
        x2_hi, x2_lo = _split2(o * o)
        ms = (_dot(x2_hi, bd) + _dot(x2_lo, bd)) * (1.0 / R_DIM)
        y = o * lax.rsqrt(ms + EPS) * gh_ref[...]
        o_ref[rs, :] = (y * _silu(pr_ref[rs, 4 * w:5 * w])).astype(BF16)

    def readout_group_anchor(gi, carry):
        rows = [chunk_rows(gi * GROUP + j) for j in range(GROUP)]
        qs = [q_scr[rows[j], :] for j in range(GROUP)]
        gcums = [gc_scr[d, rows[j], :] for j, d in pairs]
        factors = []
        for g, (j, d) in zip(gcums, pairs):
            es = [jnp.exp(x) for x in level_exponents(g, d, COARSE_HALVES[:ANCHOR_LEVELS])]
            sh = anchor_shift(g, d)
            fs = [(e, e, lv) for lv, e in enumerate(es)]
            fs.append((jnp.exp(sh), jnp.exp(-sh), MASK_ANCHOR))
            factors.append(fs)
        kbds = [tile4(k_scr[d, rows[j], :].astype(BF16)) * bd for j, d in pairs]
        accs = [None] * len(pairs)
        for lv in range(ANCHOR_LEVELS + 1):
            for i, (j, d) in enumerate(pairs):
                eq, ek, mask = factors[i][lv]
                part = jnp.where(lmask_ref[d, mask] > 0.5,
                                 _dot_nt((qs[j] * eq).astype(BF16), kbds[i] * tile4(ek.astype(BF16))), 0.0)
                accs[i] = part if accs[i] is None else accs[i] + part
        vbds = [tile4(pr_ref[rows[j], 3 * w:4 * w].astype(BF16)) * bd for j in range(GROUP)]
        outs = [None] * GROUP
        for i, (j, d) in enumerate(pairs):
            ci = gi * GROUP + j
            od = _dot(accs[i].astype(BF16), vbds[j]) + _dot_nt(qg_scr[d, rows[j], :], tile4(sbd_scr[d, ci]) * bd)
            outs[j] = od if outs[j] is None else outs[j] + od
        for j in range(GROUP):
            finish(rows[j], outs[j])
        return carry

    def readout_step_split(ci, carry):
        rs = chunk_rows(ci)
        q = q_scr[rs, :]
        qb = q.astype(BF16)
        vbd = tile4(pr_ref[rs, 3 * w:4 * w].astype(BF16)) * bd
        dsts = [cumulative(d, rs, 1 + len(FINE_HALVES)) for d in range(2)]
        exps = [level_exponents(dsts[d][0:c], d, COARSE_HALVES)
                + [dsts[d][(1 + j) * c:(2 + j) * c] for j in range(len(FINE_HALVES))] for d in range(2)]
        kbd = [tile4(k_scr[d, rs, :].astype(BF16)) * bd for d in range(2)]
        a = [_dot_nt(qb, kbd[d]) * lmask_ref[d, MASK_DIAG] for d in range(2)]
        for j in range(N_LEVELS):
            for d in range(2):
                e = jnp.exp(exps[d][j])
                a[d] = a[d] + _dot_nt((q * e).astype(BF16), kbd[d] * tile4(e.astype(BF16))) * lmask_ref[d, j]
        o = None
        for d in range(2):
            od = _dot(a[d].astype(BF16), vbd) + _dot_nt(qg_scr[d, rs, :], tile4(sbd_scr[d, ci]) * bd)
            o = od if o is None else o + od
        finish(rs, o)
        return carry

    @pl.when(anchor_ok)
    def _():
        lax.fori_loop(0, groups, readout_group_anchor, 0)

    @pl.when(jnp.logical_not(anchor_ok))
    def _():
        lax.fori_loop(0, nc, readout_step_split, 0)


def _hgrn(pr, seqs, lbp_l, g_hgrn_l, consts, layer, state=None, new_state=None):
    b0, b = seqs
    n = pr.shape[1]
    nc = n // CHUNK
    mstk, lmask, bd = consts
    w = R_WIDTH
    has_init = state is not None
    g = max(1, min(b, IN_TILE // n))
    assert b % g == 0 and b0 % g == 0
    full = lambda *shape: pl.BlockSpec(shape, lambda i: (0,) * len(shape))
    in_specs = [
        pl.BlockSpec((g, n, PR_COLS), lambda i: (b0 // g + i, 0, 0)),
        full(3, 2, w),
        full(1, w),
        full(*mstk.shape),
        full(*lmask.shape),
        full(w, w),
    ]
    args = [pr, lbp_l, g_hgrn_l.reshape(1, w), mstk, lmask, bd]
    out_shape = [jax.ShapeDtypeStruct((b, n, w), BF16)]
    out_specs = [pl.BlockSpec((g, n, w), lambda i: (i, 0, 0))]
    state_spec = pl.BlockSpec((g, None, 2, w, R_DIM), lambda i: (i, layer, 0, 0, 0))
    aliases = {}
    if has_init:
        in_specs.append(state_spec)
        args.append(state)
    else:
        in_specs.append(pl.BlockSpec(memory_space=pl.ANY))
        args.append(new_state)
        out_shape.append(jax.ShapeDtypeStruct(new_state.shape, F32))
        out_specs.append(state_spec)
        aliases = {6: 1}
    return pl.pallas_call(
        functools.partial(_hgrn_kernel, nc=nc, has_init=has_init),
        out_shape=tuple(out_shape),
        grid=(b // g,),
        in_specs=in_specs,
        out_specs=tuple(out_specs),
        input_output_aliases=aliases,
        scratch_shapes=[
            pltpu.VMEM((n, w), F32),
            pltpu.VMEM((2, n, w), F32),
            pltpu.VMEM((2, n, 2 * w), BF16),
            pltpu.VMEM((2, n, w), F32),
            pltpu.VMEM((2, n, w), BF16),
            pltpu.VMEM((2, nc, R_DIM, w), F32),
            pltpu.VMEM((2, nc, 8, w), F32),
            pltpu.VMEM((2, nc, R_DIM, w), BF16),
        ],
        compiler_params=_cparams(("arbitrary",)),
        name="hgrn_scan",
    )(*args)


def _fnet_constants(n):
    j = np.arange(F_GROUP_DIM)
    ang = 2.0 * np.pi * ((j[:, None] * j[None, :]) % F_GROUP_DIM) / F_GROUP_DIM
    eye = np.eye(F_GROUPS)
    cs = np.concatenate([np.kron(eye, np.cos(ang)), np.kron(eye, np.sin(ang))], axis=1)
    t = np.arange(n)
    angn = 2.0 * np.pi * ((t[:, None] * t[None, :]) % n) / n
    return tuple(jnp.asarray(m, F32).astype(BF16) for m in (cs, np.cos(angn), np.sin(angn)))


def _fnet_kernel(pf_ref, cs_ref, cn_ref, sn_ref, wf_ref, o_ref, *, scale):
    w = F_WIDTH
    for s in range(pf_ref.shape[0]):
        t = _dot(pf_ref[s, :, 0:w].astype(BF16), cs_ref[...])
        y = (_dot(cn_ref[...], t[:, 0:w].astype(BF16)) - _dot(sn_ref[...], t[:, w:2 * w].astype(BF16))) * scale
        of = _dot(y.astype(BF16), wf_ref[...])
        o_ref[s] = (of * _silu(pf_ref[s, :, w:2 * w])).astype(BF16)


def _fourier(pf, seqs, consts, w_fnet_bf16):
    b0, b = seqs
    n = pf.shape[1]
    cs, cn, sn = consts
    w = F_WIDTH
    g = max(1, min(b, IN_TILE * 2 // n))
    assert b % g == 0 and b0 % g == 0
    full = lambda *shape: pl.BlockSpec(shape, lambda i: (0,) * len(shape))
    return pl.pallas_call(
        functools.partial(_fnet_kernel, scale=float((n * F_GROUP_DIM) ** -0.5)),
        out_shape=jax.ShapeDtypeStruct((b, n, w), BF16),
        grid=(b // g,),
        in_specs=[pl.BlockSpec((g, n, PF_COLS), lambda i: (b0 // g + i, 0, 0)),
                  full(w, 2 * w), full(n, n), full(n, n), full(w, w)],
        out_specs=pl.BlockSpec((g, n, w), lambda i: (i, 0, 0)),
        compiler_params=_cparams(("arbitrary",)),
        name="fourier_mixing",
    )(pf, cs, cn, sn, w_fnet_bf16)


def _outproj_kernel(mac_ref, mrc_ref, mfc_ref, mal_ref, mrl_ref, mfl_ref, xc_ref, xl_ref, mod_ref, g_ref, w_ref,
                    yc_ref, yl_ref, wb_scr, *, ctx_tiles):
    i = pl.program_id(0)

    @pl.when(i == 0)
    def _():
        wb_scr[...] = w_ref[...].astype(BF16)

    is_ctx = i < ctx_tiles
    pick = lambda c_ref, l_ref: jnp.where(is_ctx, c_ref[...], l_ref[...])
    out = (_dot(pick(mac_ref, mal_ref), wb_scr[0:A_WIDTH, :])
           + _dot(pick(mrc_ref, mrl_ref), wb_scr[A_WIDTH:A_WIDTH + R_WIDTH, :])
           + _dot(pick(mfc_ref, mfl_ref), wb_scr[A_WIDTH + R_WIDTH:D_MODEL, :]))
    ms = jnp.mean(out * out, axis=-1, keepdims=True)
    y = pick(xc_ref, xl_ref) + mod_ref[2:3, :] * (out * lax.rsqrt(ms + EPS) * g_ref[...])

    @pl.when(is_ctx)
    def _():
        yc_ref[...] = y

    @pl.when(jnp.logical_not(is_ctx))
    def _():
        yl_ref[...] = y


def _out_projection(mixed_ctx, mixed_lat, x_ctx, x_lat, mod, g_post, w_out, layer):
    bc, nc_, _ = x_ctx.shape
    bl, nl, _ = x_lat.shape
    tm = IN_TILE
    lat_tiles_per_seq = nl // tm
    ctx_tiles = bc * nc_ // tm
    lat_tiles = bl * nl // tm
    last_ctx = ctx_tiles - 1
    ctx_idx = lambda i: (jnp.minimum(i, last_ctx), 0)
    lat_idx = lambda i: (jnp.maximum(i - ctx_tiles, 0), 0)
    mod_idx = lambda i: (jnp.where(i < ctx_tiles, 0, 1 + jnp.maximum(i - ctx_tiles, 0) // lat_tiles_per_seq), 0, 0)
    widths = (A_WIDTH, R_WIDTH, F_WIDTH)
    flat = lambda t: t.reshape(-1, t.shape[-1])
    yc, yl = pl.pallas_call(
        functools.partial(_outproj_kernel, ctx_tiles=ctx_tiles),
        out_shape=(jax.ShapeDtypeStruct((bc * nc_, D_MODEL), F32), jax.ShapeDtypeStruct((bl * nl, D_MODEL), F32)),
        grid=(ctx_tiles + lat_tiles,),
        in_specs=(
            [pl.BlockSpec((tm, wd), ctx_idx) for wd in widths]
            + [pl.BlockSpec((tm, wd), lat_idx) for wd in widths]
            + [
                pl.BlockSpec((tm, D_MODEL), ctx_idx),
                pl.BlockSpec((tm, D_MODEL), lat_idx),
                pl.BlockSpec((None, 3, D_MODEL), mod_idx),
                pl.BlockSpec((None, 1, D_MODEL), lambda i: (layer, 0, 0)),
                pl.BlockSpec((None, D_MODEL, D_MODEL), lambda i: (layer, 0, 0), pipeline_mode=pl.Buffered(1)),
            ]),
        out_specs=(pl.BlockSpec((tm, D_MODEL), ctx_idx), pl.BlockSpec((tm, D_MODEL), lat_idx)),
        scratch_shapes=[pltpu.VMEM((D_MODEL, D_MODEL), BF16)],
        compiler_params=_cparams(("arbitrary",)),
        name="out_projection",
    )(*[flat(t) for t in mixed_ctx], *[flat(t) for t in mixed_lat], flat(x_ctx), flat(x_lat), mod,
      g_post.reshape(DEPTH, 1, D_MODEL), w_out)
    return yc.reshape(x_ctx.shape), yl.reshape(x_lat.shape)


def kernel(x_prompt, x_sample, cache_attn_k, cache_attn_v, state_hgrn, c, c_ctx,
           w_ada, b_ada, g_pre, w_in, rpb, lb_logits, g_hgrn, w_fnet, w_out, g_post):
    nb_ctx, n_ctx, _ = x_prompt.shape
    nb_lat, n_lat, _ = x_sample.shape

    pad_rows = (-(1 + nb_lat)) % 8
    cc = jnp.concatenate([c_ctx[None, :], c, jnp.zeros((pad_rows, D_MODEL), F32)], axis=0)
    mods = _modulations(cc, w_ada, b_ada)
    lbp = _lower_bounds(lb_logits)

    w_fnet_b = w_fnet.astype(BF16)
    hconsts = _hgrn_constants()
    fconsts_ctx = _fnet_constants(n_ctx)
    fconsts_lat = _fnet_constants(n_lat)

    state_rows = state_hgrn.reshape(nb_lat, DEPTH, 2, R_WIDTH, R_DIM)
    cache_kt = jnp.transpose(cache_attn_k, (0, 1, 3, 4, 2))
    cache_vt = jnp.transpose(cache_attn_v, (0, 1, 3, 4, 2))

    yp, ys = x_prompt, x_sample
    new_kv = [jnp.zeros((nb_ctx, DEPTH, A_HEADS, HEAD_DIM, n_ctx), F32) for _ in range(2)]
    new_rows = jnp.zeros((nb_ctx, DEPTH, 2, R_WIDTH, R_DIM), F32)
    ctx_tokens = nb_ctx * n_ctx
    ctx_seqs = (0, nb_ctx)
    lat_seqs = (ctx_tokens // n_lat, nb_lat)
    for l in range(DEPTH):
        lbp_l = lbp[:, l]
        mod_all = mods[l, 0:1 + nb_lat].reshape(1 + nb_lat, 3, D_MODEL)
        pa, pr, pf, *new_kv = _in_projection(yp, ys, mod_all, g_pre, w_in, l, new_kv)
        as_ctx = lambda t: t.reshape(-1, n_ctx, t.shape[-1])
        ma_c = _context_attention(as_ctx(pa), ctx_seqs)
        mr_c, new_rows = _hgrn(as_ctx(pr), ctx_seqs, lbp_l, g_hgrn[l], hconsts, l, new_state=new_rows)
        mf_c = _fourier(as_ctx(pf), ctx_seqs, fconsts_ctx, w_fnet_b[l])
        as_lat = lambda t: t.reshape(-1, n_lat, t.shape[-1])
        ma_l = _neighbourhood_attention(as_lat(pa), lat_seqs, cache_kt, cache_vt, l, rpb[l])
        (mr_l,) = _hgrn(as_lat(pr), lat_seqs, lbp_l, g_hgrn[l], hconsts, l, state=state_rows)
        mf_l = _fourier(as_lat(pf), lat_seqs, fconsts_lat, w_fnet_b[l])
        yp, ys = _out_projection((ma_c, mr_c, mf_c), (ma_l, mr_l, mf_l), yp, ys, mod_all, g_post, w_out, l)

    new_state = new_rows.reshape(nb_ctx, DEPTH, 2, R_HEADS, R_DIM, R_DIM)
    new_k, new_v = (jnp.transpose(t, (0, 1, 4, 2, 3)) for t in new_kv)
    return (yp, ys, new_k, new_v, new_state)
```

```python
import functools

import numpy as np
import jax
import jax.numpy as jnp
from jax import lax
from jax.experimental import pallas as pl
from jax.experimental.pallas import tpu as pltpu

F32 = jnp.float32
BF16 = jnp.bfloat16

D_MODEL = 1024
DEPTH = 4
GRID_W = 64
WIN_H = 8
WIN_W = 16
HEAD_DIM = 64
A_HEADS = 8
A_WIDTH = A_HEADS * HEAD_DIM
R_HEADS = 4
R_DIM = 64
R_WIDTH = R_HEADS * R_DIM
F_GROUPS = 4
F_GROUP_DIM = 64
F_WIDTH = F_GROUPS * F_GROUP_DIM
PA_COLS = 4 * A_WIDTH
PR_COLS = 5 * R_WIDTH
PF_COLS = 2 * F_WIDTH
PG_COLS = 4 * R_WIDTH
IN_COLS = PA_COLS + PR_COLS + PF_COLS
CHUNK = 64
EPS = 1e-6
LANES = 128
NEG_INF = float("-inf")
VMEM_LIMIT = 56 * 1024 * 1024


def _cparams(sem):
    return pltpu.CompilerParams(dimension_semantics=sem, vmem_limit_bytes=VMEM_LIMIT)


def _silu(x):
    return x * (1.0 / (1.0 + jnp.exp(-x)))


def _dot(a, b):
    return jnp.dot(a, b, preferred_element_type=F32)


def _dot_nt(a, b):
    return lax.dot_general(a, b, (((1,), (1,)), ((), ())), preferred_element_type=F32)


def _dot_tn(a, b):
    return lax.dot_general(a, b, (((0,), (0,)), ((), ())), preferred_element_type=F32)


def _split2(x):
    hi = x.astype(BF16)
    lo = (x - hi.astype(F32)).astype(BF16)
    return hi, lo


def _mod_kernel(cc_ref, w_ref, b_ref, o_ref):
    a_hi, a_lo = _split2(_silu(cc_ref[...]))
    w_hi, w_lo = _split2(w_ref[...])
    acc = _dot(a_hi, w_hi) + _dot(a_hi, w_lo) + _dot(a_lo, w_hi)
    o_ref[...] = acc + b_ref[...]


def _modulations(cc, w_ada, b_ada):
    rows = cc.shape[0]
    tn = 3 * D_MODEL // 2
    return pl.pallas_call(
        _mod_kernel,
        out_shape=jax.ShapeDtypeStruct((DEPTH, rows, 3 * D_MODEL), F32),
        grid=(DEPTH, 3 * D_MODEL // tn),
        in_specs=[
            pl.BlockSpec((rows, D_MODEL), lambda l, j: (0, 0)),
            pl.BlockSpec((None, D_MODEL, tn), lambda l, j: (l, 0, j)),
            pl.BlockSpec((None, 1, tn), lambda l, j: (l, 0, j)),
        ],
        out_specs=pl.BlockSpec((None, rows, tn), lambda l, j: (l, 0, j)),
        compiler_params=_cparams(("arbitrary", "arbitrary")),
        name="adaln_mod",
    )(cc, w_ada, b_ada.reshape(DEPTH, 1, 3 * D_MODEL))


def _lb_kernel(x_ref, o_ref):
    xs = [x_ref[i] for i in range(DEPTH)]
    m = functools.reduce(jnp.maximum, xs)
    es = [jnp.exp(x - m) for x in xs]
    tot = functools.reduce(lambda a, b: a + b, es)
    cum = None
    first = None
    for i in range(DEPTH):
        p = es[i] / tot
        cum = p if cum is None else cum + p
        if first is None:
            first = cum
        lb = jnp.maximum(cum - first, 0.0)
        o_ref[0, i] = lb
        o_ref[1, i] = jnp.log1p(-lb)
        o_ref[2, i] = 1.0 - lb


def _lower_bounds(lb_logits):
    x = jnp.transpose(lb_logits, (1, 0, 2))
    return pl.pallas_call(
        _lb_kernel,
        out_shape=jax.ShapeDtypeStruct((3, DEPTH, 2, R_WIDTH), F32),
        name="hgrn_lower_bounds",
    )(x)


IN_TILE = 512


def _inproj_kernel(xc_ref, xl_ref, mod_ref, g_ref, lbp_ref, w_ref, kprev_ref, vprev_ref,
                   pa_ref, pr_ref, pg_ref, pf_ref, ko_ref, vo_ref, wb_scr, *, ctx_tiles):
    del kprev_ref, vprev_ref
    i = pl.program_id(0)

    @pl.when(i == 0)
    def _():
        wb_scr[...] = w_ref[...].astype(BF16)

    is_ctx = i < ctx_tiles
    x = jnp.where(is_ctx, xc_ref[...], xl_ref[...])
    ms = jnp.mean(x * x, axis=-1, keepdims=True)
    y = x * lax.rsqrt(ms + EPS) * g_ref[...]
    h = (y * (1.0 + mod_ref[1:2, :]) + mod_ref[0:1, :]).astype(BF16)

    w = R_WIDTH
    pr = _dot(h, wb_scr[:, PA_COLS:PA_COLS + PR_COLS])
    pf_ref[...] = _dot(h, wb_scr[:, PA_COLS + PR_COLS:IN_COLS])
    pr_ref[:, 0:w] = _silu(pr[:, 0:w])
    for d in range(2):
        z = pr[:, (1 + d) * w:(2 + d) * w]
        e = jnp.exp(-jnp.abs(z))
        r = 1.0 / (1.0 + e)
        pos = z >= 0.0
        one_m_lb = lbp_ref[2, d:d + 1, :]
        f = lbp_ref[0, d:d + 1, :] + one_m_lb * (jnp.where(pos, 1.0, e) * r)
        log_f = jnp.where(f > 0.0, jnp.log(f), lbp_ref[1, d:d + 1, :] + z)
        g_hi, g_lo = _split2(log_f)
        pg_ref[:, 2 * d * w:(2 * d + 1) * w] = g_hi
        pg_ref[:, (2 * d + 1) * w:(2 * d + 2) * w] = g_lo
        pr_ref[:, (1 + d) * w:(2 + d) * w] = one_m_lb * (jnp.where(pos, e, 1.0) * r)
    pr_ref[:, 3 * w:4 * w] = pr[:, 3 * w:4 * w]
    pr_ref[:, 4 * w:5 * w] = _silu(pr[:, 4 * w:5 * w])
    pa_ref[...] = _dot(h, wb_scr[:, 0:PA_COLS])

    @pl.when(is_ctx)
    def _():
        seqs, _, _, n = ko_ref.shape
        for s in range(seqs):
            rows = slice(s * n, (s + 1) * n)
            ko_ref[s] = pa_ref[rows, A_WIDTH:2 * A_WIDTH].T.reshape(A_HEADS, HEAD_DIM, n)
            vo_ref[s] = pa_ref[rows, 2 * A_WIDTH:3 * A_WIDTH].T.reshape(A_HEADS, HEAD_DIM, n)


def _in_projection(x_ctx, x_lat, mod, g_pre, lbp, w_in, layer, cache):
    bc, nc_, _ = x_ctx.shape
    bl, nl, _ = x_lat.shape
    tm = IN_TILE
    seqs_per_tile = tm // nc_
    lat_tiles_per_seq = nl // tm
    ctx_tiles = bc * nc_ // tm
    lat_tiles = bl * nl // tm
    tokens = (ctx_tiles + lat_tiles) * tm
    last_ctx = ctx_tiles - 1
    ctx_idx = lambda i: (jnp.minimum(i, last_ctx), 0)
    lat_idx = lambda i: (jnp.maximum(i - ctx_tiles, 0), 0)
    mod_idx = lambda i: (jnp.where(i < ctx_tiles, 0, 1 + jnp.maximum(i - ctx_tiles, 0) // lat_tiles_per_seq), 0, 0)
    tok = lambda i: (i, 0)
    cache_spec = pl.BlockSpec((seqs_per_tile, None, A_HEADS, HEAD_DIM, nc_),
                              lambda i: (jnp.minimum(i, last_ctx), layer, 0, 0, 0))
    cache_shape = jax.ShapeDtypeStruct(cache[0].shape, F32)
    return pl.pallas_call(
        functools.partial(_inproj_kernel, ctx_tiles=ctx_tiles),
        out_shape=(
            jax.ShapeDtypeStruct((tokens, PA_COLS), F32),
            jax.ShapeDtypeStruct((tokens, PR_COLS), F32),
            jax.ShapeDtypeStruct((tokens, PG_COLS), BF16),
            jax.ShapeDtypeStruct((tokens, PF_COLS), F32),
            cache_shape, cache_shape,
        ),
        grid=(ctx_tiles + lat_tiles,),
        in_specs=[
            pl.BlockSpec((tm, D_MODEL), ctx_idx),
            pl.BlockSpec((tm, D_MODEL), lat_idx),
            pl.BlockSpec((None, 3, D_MODEL), mod_idx),
            pl.BlockSpec((None, 1, D_MODEL), lambda i: (layer, 0, 0)),
            pl.BlockSpec((3, None, 2, R_WIDTH), lambda i: (0, layer, 0, 0)),
            pl.BlockSpec((None, D_MODEL, IN_COLS), lambda i: (layer, 0, 0), pipeline_mode=pl.Buffered(1)),
            pl.BlockSpec(memory_space=pl.ANY),
            pl.BlockSpec(memory_space=pl.ANY),
        ],
        out_specs=(
            pl.BlockSpec((tm, PA_COLS), tok),
            pl.BlockSpec((tm, PR_COLS), tok),
            pl.BlockSpec((tm, PG_COLS), tok),
            pl.BlockSpec((tm, PF_COLS), tok),
            cache_spec, cache_spec,
        ),
        scratch_shapes=[pltpu.VMEM((D_MODEL, IN_COLS), BF16)],
        input_output_aliases={6: 4, 7: 5},
        compiler_params=_cparams(("arbitrary",)),
        name="in_projection",
    )(x_ctx.reshape(bc * nc_, D_MODEL), x_lat.reshape(bl * nl, D_MODEL), mod,
      g_pre.reshape(DEPTH, 1, D_MODEL), lbp, w_in, cache[0], cache[1])


LOG2E = 1.4426950408889634
Q_SCALE = HEAD_DIM ** -0.5 * LOG2E


def _lane_lo():
    return lax.broadcasted_iota(jnp.int32, (1, LANES), 1) < HEAD_DIM


def _ctx_attn_kernel(pas_ref, os_ref, s_scr):
    for s in range(pas_ref.shape[0]):
        _ctx_attn_sequence(pas_ref.at[s], os_ref.at[s], s_scr)


def _ctx_attn_sequence(pa_ref, o_ref, s_scr):
    lo = _lane_lo()
    npair = A_HEADS // 2

    def scores(p):
        c = p * LANES
        q2 = pa_ref[:, c:c + LANES] * Q_SCALE
        k2 = pa_ref[:, A_WIDTH + c:A_WIDTH + c + LANES].astype(BF16)
        for hh in range(2):
            sel = lo if hh == 0 else jnp.logical_not(lo)
            s_scr[p % 2, hh] = _dot_nt(jnp.where(sel, q2, 0.0).astype(BF16), k2)

    def finish(p):
        c = p * LANES
        v2 = pa_ref[:, 2 * A_WIDTH + c:2 * A_WIDTH + c + LANES].astype(BF16)
        outs = []
        for hh in range(2):
            s = s_scr[p % 2, hh]
            e = jnp.exp2(s - jnp.max(s, axis=-1, keepdims=True))
            inv = 1.0 / jnp.sum(e, axis=-1, keepdims=True)
            outs.append(_dot(e.astype(BF16), v2) * inv)
        o2 = jnp.where(lo, outs[0], outs[1])
        o_ref[:, c:c + LANES] = (o2 * _silu(pa_ref[:, 3 * A_WIDTH + c:3 * A_WIDTH + c + LANES])).astype(BF16)

    scores(0)
    for p in range(npair):
        if p + 1 < npair:
            scores(p + 1)
        finish(p)


def _context_attention(pa, seqs):
    b0, b = seqs
    n = pa.shape[1]
    g = 2 if b % 2 == 0 and b0 % 2 == 0 else 1
    return pl.pallas_call(
        _ctx_attn_kernel,
        out_shape=jax.ShapeDtypeStruct((b, n, A_WIDTH), BF16),
        grid=(b // g,),
        in_specs=[pl.BlockSpec((g, n, PA_COLS), lambda i: (b0 // g + i, 0, 0))],
        out_specs=pl.BlockSpec((g, n, A_WIDTH), lambda i: (i, 0, 0)),
        scratch_shapes=[pltpu.VMEM((2, 2, n, n), F32)],
        compiler_params=_cparams(("arbitrary",)),
        name="context_attention",
    )(pa)


QROWS = 4


def _nbr_blocks(rows):
    kh = min(WIN_H, rows)
    out = []
    for r_first in range(0, rows, QROWS):
        r0s = [min(max(r - kh // 2, 0), rows - kh) for r in range(r_first, r_first + QROWS)]
        lo, hi = min(r0s), max(r0s) + kh
        lo -= lo % 2
        span = hi - lo
        span += (-span) % 4
        if lo + span > rows:
            lo = rows - span
        assert lo >= 0 and lo % 2 == 0
        out.append((r_first, lo, span, r0s))
    return out, kh


N_REL_ROWS = 2 * WIN_H - 1
TABLE_ROWS = WIN_H * GRID_W


def _build_bias_tables(base_ref, ta_scr, tb_scr, heads):
    lane = lax.broadcasted_iota(jnp.int32, (GRID_W, LANES), 1)
    kc = lax.broadcasted_iota(jnp.int32, (GRID_W, LANES), 0)
    qc = lane % GRID_W
    lo = lane < GRID_W
    c0 = jnp.clip(qc - WIN_W // 2, 0, GRID_W - WIN_W)
    col_in = (kc >= c0) & (kc < c0 + WIN_W)

    for h in heads:
        def tile(i, lane_off):
            if not 0 <= i < N_REL_ROWS:
                return jnp.full((GRID_W, LANES), NEG_INF, F32)
            row = jnp.broadcast_to(base_ref[h, i:i + 1, :] * LOG2E, (GRID_W, LANES))
            return pltpu.roll(row, lane_off, 1, stride=1, stride_axis=0)

        for j in range(WIN_H):
            rows = slice(j * GRID_W, (j + 1) * GRID_W)
            ta_scr[h, rows, :] = jnp.where(col_in, jnp.where(lo, tile(2 * j + 1, 0), tile(2 * j, GRID_W)), NEG_INF)
            tb_scr[h, rows, :] = jnp.where(col_in, jnp.where(lo, tile(2 * j, 0), tile(2 * j - 1, GRID_W)), NEG_INF)


def _nbr_attn_kernel(q_ref, k_ref, v_ref, g_ref, kc_ref, vc_ref, base_ref, o_ref, s_scr, p_scr, ta_scr, tb_scr, *, rows):
    p = pl.program_id(1)

    @pl.when(pl.program_id(0) == 0)
    def _():
        _build_bias_tables(base_ref, ta_scr, tb_scr, (2 * p, 2 * p + 1))

    lo = _lane_lo()
    hi = jnp.logical_not(lo)
    blocks, kh = _nbr_blocks(rows)
    nq = QROWS * GRID_W
    past = kc_ref.shape[-1]
    ctx0 = s_scr.shape[2] - past
    kc_rows = kc_ref[...].reshape(LANES, past).T.astype(BF16)
    vct = vc_ref[...].reshape(LANES, past).astype(BF16)
    head_rows = [slice(hh * HEAD_DIM, (hh + 1) * HEAD_DIM) for hh in range(2)]
    ctx_tiles = [slice(ctx0 + t * GRID_W, ctx0 + (t + 1) * GRID_W) for t in range(past // GRID_W)]

    def window_tiles(bi, c, head):
        r_first, u0, span, r0s = blocks[bi]
        out = []
        for t in range(span):
            kr = u0 + t
            rs = slice(t * GRID_W, (t + 1) * GRID_W)
            ok = [r0s[2 * c + e] <= kr < r0s[2 * c + e] + kh for e in range(2)]
            if not (ok[0] or ok[1]):
                out.append((rs, None))
                continue
            i = kr - (r_first + 2 * c) + WIN_H - 1
            assert 0 <= i <= N_REL_ROWS
            tab, j = (ta_scr, (i - 1) // 2) if i % 2 else (tb_scr, i // 2)
            bias = tab[head, j * GRID_W:(j + 1) * GRID_W, :]
            if not ok[0]:
                bias = jnp.where(lo, NEG_INF, bias)
            if not ok[1]:
                bias = jnp.where(hi, NEG_INF, bias)
            out.append((rs, bias))
        return out

    def scores(bi):
        r_first, u0, span, _ = blocks[bi]
        qs, ks, nk = r_first * GRID_W, u0 * GRID_W, span * GRID_W
        q2 = q_ref[qs:qs + nq, :] * Q_SCALE
        ku = k_ref[ks:ks + nk, :].astype(BF16)
        for hh in range(2):
            qh = jnp.where(lo if hh == 0 else hi, q2, 0.0).astype(BF16)
            s_scr[bi % 2, hh, 0:nk, :] = _dot_nt(ku, qh)
            s_scr[bi % 2, hh, ctx0:ctx0 + past, :] = _dot_nt(kc_rows, qh)

    def fold_rows(x, op):
        parts = [x[r:r + 8] for r in range(0, GRID_W, 8)]
        while len(parts) > 1:
            parts = [op(parts[i], parts[i + 1]) for i in range(0, len(parts), 2)]
        return parts[0]

    def softmax(bi):
        par = bi % 2
        inv = []
        for hh in range(2):
            parts = []
            for c in range(QROWS // 2):
                cs = slice(c * LANES, (c + 1) * LANES)
                tiles = window_tiles(bi, c, 2 * p + hh)
                m_acc = None
                for rs, bias in tiles:
                    if bias is None:
                        continue
                    s = s_scr[par, hh, rs, cs] + bias
                    s_scr[par, hh, rs, cs] = s
                    t = fold_rows(s, jnp.maximum)
                    m_acc = t if m_acc is None else jnp.maximum(m_acc, t)
                for rs in ctx_tiles:
                    m_acc = jnp.maximum(m_acc, fold_rows(s_scr[par, hh, rs, cs], jnp.maximum))
                m = jnp.max(m_acc, axis=0, keepdims=True)
                l_acc = None
                for rs, bias in tiles:
                    if bias is None:
                        p_scr[par, hh, rs, cs] = jnp.zeros((GRID_W, LANES), BF16)
                        continue
                    e = jnp.exp2(s_scr[par, hh, rs, cs] - m)
                    p_scr[par, hh, rs, cs] = e.astype(BF16)
                    t = fold_rows(e, jnp.add)
                    l_acc = t if l_acc is None else l_acc + t
                for rs in ctx_tiles:
                    e = jnp.exp2(s_scr[par, hh, rs, cs] - m)
                    p_scr[par, hh, rs, cs] = e.astype(BF16)
                    l_acc = l_acc + fold_rows(e, jnp.add)
                parts.append(1.0 / jnp.sum(l_acc, axis=0, keepdims=True))
            inv.append(jnp.concatenate(parts, axis=1))
        return inv

    def values(bi, inv):
        r_first, u0, span, _ = blocks[bi]
        qs, ks, nk = r_first * GRID_W, u0 * GRID_W, span * GRID_W
        vut = v_ref[ks:ks + nk, :].T.astype(BF16)
        outs = []
        for hh in range(2):
            o = (_dot(vut[head_rows[hh]], p_scr[bi % 2, hh, 0:nk, :])
                 + _dot(vct[head_rows[hh]], p_scr[bi % 2, hh, ctx0:ctx0 + past, :]))
            outs.append(o * inv[hh])
        o2 = jnp.concatenate(outs, axis=0).T
        o_ref[qs:qs + nq, :] = (o2 * _silu(g_ref[qs:qs + nq, :])).astype(BF16)

    scores(0)
    for bi in range(len(blocks)):
        if bi + 1 < len(blocks):
            scores(bi + 1)
        values(bi, softmax(bi))


def _bias_base(rpb_l):
    mid = WIN_W - 1
    zeros = jnp.zeros(rpb_l.shape[:2] + (LANES - (2 * WIN_W - 1),), F32)
    rev = rpb_l[..., ::-1]
    base = jnp.concatenate([rev[..., mid:], zeros, rev[..., :mid]], axis=-1)
    return jnp.pad(base, ((0, 0), (0, 2 * WIN_H - N_REL_ROWS), (0, 0)))


def _neighbourhood_attention(pa, seqs, cache_kt, cache_vt, layer, rpb_l):
    b0, b = seqs
    n = pa.shape[1]
    rows = n // GRID_W
    past = cache_kt.shape[-1]
    blocks, _ = _nbr_blocks(rows)
    max_nk = max(s for (_, _, s, _) in blocks) * GRID_W
    npair = A_HEADS // 2
    col = lambda off: (lambda i, p: (b0 + i, 0, off + p))
    cache_spec = pl.BlockSpec((None, None, 2, HEAD_DIM, past), lambda i, p: (i, layer, p, 0, 0))
    return pl.pallas_call(
        functools.partial(_nbr_attn_kernel, rows=rows),
        out_shape=jax.ShapeDtypeStruct((b, n, A_WIDTH), BF16),
        grid=(b, npair),
        in_specs=[
            pl.BlockSpec((None, n, LANES), col(0)),
            pl.BlockSpec((None, n, LANES), col(npair)),
            pl.BlockSpec((None, n, LANES), col(2 * npair)),
            pl.BlockSpec((None, n, LANES), col(3 * npair)),
            cache_spec,
            cache_spec,
            pl.BlockSpec((A_HEADS, 2 * WIN_H, LANES), lambda i, p: (0, 0, 0)),
        ],
        out_specs=pl.BlockSpec((None, n, LANES), lambda i, p: (i, 0, p)),
        scratch_shapes=[
            pltpu.VMEM((2, 2, max_nk + past, QROWS * GRID_W), F32),
            pltpu.VMEM((2, 2, max_nk + past, QROWS * GRID_W), BF16),
            pltpu.VMEM((A_HEADS, TABLE_ROWS, LANES), F32),
            pltpu.VMEM((A_HEADS, TABLE_ROWS, LANES), F32),
        ],
        compiler_params=_cparams(("arbitrary", "arbitrary")),
        name="neighbourhood_attention",
    )(pa, pa, pa, pa, cache_kt, cache_vt, _bias_base(rpb_l))


COARSE_HALVES = (32, 16, 8)
FINE_HALVES = (4, 2, 1)
N_LEVELS = len(COARSE_HALVES) + len(FINE_HALVES)
ANCHOR_BLOCK = CHUNK
ANCHOR_LEVELS = 0
ANCHOR_MAX_EXPONENT = 80.0
MASK_DIAG = N_LEVELS
MASK_ANCHOR = N_LEVELS + 1
GROUP = 4


def _hgrn_constants():
    c = CHUNK
    idx = np.arange(c)
    mats = [np.tril(np.ones((c, c)))]
    masks = []
    for h in COARSE_HALVES + FINE_HALVES:
        blk = idx // (2 * h)
        mid = blk * 2 * h + h - 1
        upper = idx > mid
        if h in FINE_HALVES:
            m = np.zeros((c, c))
            for i in range(c):
                if upper[i]:
                    m[i, mid[i] + 1:i + 1] = 1.0
                else:
                    m[i, i + 1:mid[i] + 1] = 1.0
            mats.append(m)
        same = blk[:, None] == blk[None, :]
        masks.append((same & upper[:, None] & (~upper)[None, :]).astype(np.float64))
    masks.append(np.eye(c))
    same_block = (idx[:, None] // ANCHOR_BLOCK) == (idx[None, :] // ANCHOR_BLOCK)
    masks.append((same_block & (idx[None, :] <= idx[:, None])).astype(np.float64))
    fwd = np.concatenate(mats, axis=0)
    bwd = np.concatenate([m[::-1, ::-1] for m in mats], axis=0)
    mk_f = np.stack([np.tile(m, (1, R_HEADS)) for m in masks])
    mk_b = np.stack([np.tile(m[::-1, ::-1], (1, R_HEADS)) for m in masks])
    hid = np.arange(R_WIDTH) // R_DIM
    bd = (hid[:, None] == hid[None, :]).astype(np.float64)
    return (jnp.asarray(np.stack([fwd, bwd]), BF16), jnp.asarray(np.stack([mk_f, mk_b]), F32),
            jnp.asarray(bd, BF16))


def _hgrn_kernel(*refs, nc, has_init):
    prs_ref, pgs_ref, gh_ref, mstk_ref, lmask_ref, bd_ref = refs[:6]
    if has_init:
        s0s_ref, os_ref = refs[6:8]
        sfins_ref = None
        scratch = refs[8:]
    else:
        s0s_ref = None
        os_ref, sfins_ref = refs[7:9]
        scratch = refs[9:]
    for s in range(prs_ref.shape[0]):
        _hgrn_sequence(prs_ref.at[s], pgs_ref.at[s], gh_ref, mstk_ref, lmask_ref, bd_ref,
                       None if s0s_ref is None else s0s_ref.at[s], os_ref.at[s],
                       None if sfins_ref is None else sfins_ref.at[s], scratch, nc)


def _hgrn_sequence(pr_ref, pg_ref, gh_ref, mstk_ref, lmask_ref, bd_ref, s0_ref, o_ref, sfin_ref, scratch, nc):
    has_init = s0_ref is not None
    gc_scr, qg_scr, upd_scr, dec_scr, sbd_scr = scratch
    w = R_WIDTH
    c = CHUNK
    bd = bd_ref[...]
    hid = lax.broadcasted_iota(jnp.int32, (1, w), 1) // R_DIM
    q_at = lambda rs: pr_ref[rs, 0:w]
    k_at = lambda d, rs: pr_ref[rs, (1 + d) * w:(2 + d) * w]
    g_at = lambda d, rs: pg_ref[rs, 2 * d * w:2 * (d + 1) * w]

    def tile4(x):
        return jnp.concatenate([x] * R_HEADS, axis=0)

    def head_diagonal(full):
        out = full[(R_HEADS - 1) * R_DIM:R_HEADS * R_DIM]
        for h in range(R_HEADS - 2, -1, -1):
            out = jnp.where(hid == h, full[h * R_DIM:(h + 1) * R_DIM], out)
        return out

    def chunk_rows(ci):
        return pl.ds(pl.multiple_of(ci * c, c), c)

    def cumulative(d, rs, n_mats):
        s = _dot(mstk_ref[d, 0:n_mats * c, :], g_at(d, rs))
        return s[:, 0:w] + s[:, w:2 * w]

    def level_exponents(gcum, d, halves):
        out = []
        for h in halves:
            parts = []
            for s0 in range(0, c, 2 * h):
                anchor = s0 + h - 1 + d
                parts.append(-jnp.abs(gcum[s0:s0 + 2 * h] - gcum[anchor:anchor + 1]))
            out.append(parts[0] if len(parts) == 1 else jnp.concatenate(parts, axis=0))
        return out

    def anchor_shift(gcum, d):
        parts = []
        for s0 in range(0, c, ANCHOR_BLOCK):
            anchor = s0 + ANCHOR_BLOCK // 2 - 1 + d
            parts.append(gcum[s0:s0 + ANCHOR_BLOCK] - gcum[anchor:anchor + 1])
        return jnp.concatenate(parts, axis=0)

    groups = nc // GROUP
    pairs = [(j, d) for j in range(GROUP) for d in range(2)]

    def increment_group(gi, worst):
        rows = [chunk_rows(gi * GROUP + j) for j in range(GROUP)]
        gcums = [cumulative(d, rows[j], 1) for j, d in pairs]
        lasts = [g[c - 1:c, :] if d == 0 else g[0:1, :] for g, (j, d) in zip(gcums, pairs)]
        kls = [(k_at(d, rows[j]) * jnp.exp(last - g)).astype(BF16)
               for g, last, (j, d) in zip(gcums, lasts, pairs)]
        vs = [pr_ref[rows[j], 3 * w:4 * w].astype(BF16) for j in range(GROUP)]
        upds = [_dot_tn(vs[j], kl) for kl, (j, d) in zip(kls, pairs)]
        for g, last, upd, (j, d) in zip(gcums, lasts, upds, pairs):
            ci = gi * GROUP + j
            gc_scr[d, rows[j], :] = g
            upd_scr[d, ci] = head_diagonal(upd)
            dec_scr[d, ci] = jnp.broadcast_to(jnp.exp(last), (8, w))
            qg_scr[d, rows[j], :] = (q_at(rows[j]) * jnp.exp(g)).astype(BF16)
            sh = jnp.abs(anchor_shift(g, d))
            for s0 in range(0, c, 8):
                worst = jnp.maximum(worst, sh[s0:s0 + 8])
        return worst

    worst = lax.fori_loop(0, groups, increment_group, jnp.zeros((8, w), F32))
    q_max = jnp.max(jnp.max(jnp.abs(pr_ref[:, 0:w]), axis=0, keepdims=True), axis=1, keepdims=True)
    anchor_ok = jnp.max(worst + jnp.log(jnp.maximum(q_max, 1.0))) < ANCHOR_MAX_EXPONENT

    def scan_step(ci, carry):
        sf, sb = carry
        cb = nc - 1 - ci
        sbd_scr[0, ci] = sf.astype(BF16)
        sbd_scr[1, cb] = sb.astype(BF16)
        sf = sf * dec_scr[0, ci][0:1, :] + upd_scr[0, ci]
        sb = sb * dec_scr[1, cb][0:1, :] + upd_scr[1, cb]
        return sf, sb

    if has_init:
        init = tuple(jnp.concatenate([s0_ref[d], jnp.zeros((w, LANES - R_DIM), F32)], axis=1).T[0:R_DIM]
                     for d in range(2))
    else:
        init = (jnp.zeros((R_DIM, w), F32), jnp.zeros((R_DIM, w), F32))
    finals = lax.fori_loop(0, nc, scan_step, init)
    if sfin_ref is not None:
        for d in range(2):
            padded = jnp.concatenate([finals[d], jnp.zeros((LANES - R_DIM, w), F32)], axis=0)
            sfin_ref[d] = padded.T[:, 0:R_DIM]

    def finish(rs, o):
        x2_hi, x2_lo = _split2(o * o)
        ms = (_dot(x2_hi, bd) + _dot(x2_lo, bd)) * (1.0 / R_DIM)
        y = o * lax.rsqrt(ms + EPS) * gh_ref[...]
        o_ref[rs, :] = (y * pr_ref[rs, 4 * w:5 * w]).astype(BF16)

    def readout_group_anchor(gi, carry):
        rows = [chunk_rows(gi * GROUP + j) for j in range(GROUP)]
        qs = [q_at(rows[j]) for j in range(GROUP)]
        gcums = [gc_scr[d, rows[j], :] for j, d in pairs]
        factors = []
        for g, (j, d) in zip(gcums, pairs):
            es = [jnp.exp(x) for x in level_exponents(g, d, COARSE_HALVES[:ANCHOR_LEVELS])]
            sh = anchor_shift(g, d)
            fs = [(e, e, lv) for lv, e in enumerate(es)]
            fs.append((jnp.exp(sh), jnp.exp(-sh), MASK_ANCHOR))
            factors.append(fs)
        kbds = [tile4(k_at(d, rows[j]).astype(BF16)) * bd for j, d in pairs]
        accs = [None] * len(pairs)
        for lv in range(ANCHOR_LEVELS + 1):
            for i, (j, d) in enumerate(pairs):
                eq, ek, mask = factors[i][lv]
                part = jnp.where(lmask_ref[d, mask] > 0.5,
                                 _dot_nt((qs[j] * eq).astype(BF16), kbds[i] * tile4(ek.astype(BF16))), 0.0)
                accs[i] = part if accs[i] is None else accs[i] + part
        vbds = [tile4(pr_ref[rows[j], 3 * w:4 * w].astype(BF16)) * bd for j in range(GROUP)]
        outs = [None] * GROUP
        for i, (j, d) in enumerate(pairs):
            ci = gi * GROUP + j
            od = _dot(accs[i].astype(BF16), vbds[j]) + _dot_nt(qg_scr[d, rows[j], :], tile4(sbd_scr[d, ci]) * bd)
            outs[j] = od if outs[j] is None else outs[j] + od
        for j in range(GROUP):
            finish(rows[j], outs[j])
        return carry

    def readout_step_split(ci, carry):
        rs = chunk_rows(ci)
        q = q_at(rs)
        qb = q.astype(BF16)
        vbd = tile4(pr_ref[rs, 3 * w:4 * w].astype(BF16)) * bd
        dsts = [cumulative(d, rs, 1 + len(FINE_HALVES)) for d in range(2)]
        exps = [level_exponents(dsts[d][0:c], d, COARSE_HALVES)
                + [dsts[d][(1 + j) * c:(2 + j) * c] for j in range(len(FINE_HALVES))] for d in range(2)]
        kbd = [tile4(k_at(d, rs).astype(BF16)) * bd for d in range(2)]
        a = [_dot_nt(qb, kbd[d]) * lmask_ref[d, MASK_DIAG] for d in range(2)]
        for j in range(N_LEVELS):
            for d in range(2):
                e = jnp.exp(exps[d][j])
                a[d] = a[d] + _dot_nt((q * e).astype(BF16), kbd[d] * tile4(e.astype(BF16))) * lmask_ref[d, j]
        o = None
        for d in range(2):
            od = _dot(a[d].astype(BF16), vbd) + _dot_nt(qg_scr[d, rs, :], tile4(sbd_scr[d, ci]) * bd)
            o = od if o is None else o + od
        finish(rs, o)
        return carry

    @pl.when(anchor_ok)
    def _():
        lax.fori_loop(0, groups, readout_group_anchor, 0)

    @pl.when(jnp.logical_not(anchor_ok))
    def _():
        lax.fori_loop(0, nc, readout_step_split, 0)


def _hgrn(pr, pg, seqs, g_hgrn_l, consts, layer, state=None, new_state=None):
    b0, b = seqs
    n = pr.shape[1]
    nc = n // CHUNK
    mstk, lmask, bd = consts
    w = R_WIDTH
    has_init = state is not None
    g = max(1, min(b, IN_TILE // n))
    assert b % g == 0 and b0 % g == 0
    full = lambda *shape: pl.BlockSpec(shape, lambda i: (0,) * len(shape))
    in_specs = [
        pl.BlockSpec((g, n, PR_COLS), lambda i: (b0 // g + i, 0, 0)),
        pl.BlockSpec((g, n, PG_COLS), lambda i: (b0 // g + i, 0, 0)),
        full(1, w),
        full(*mstk.shape),
        full(*lmask.shape),
        full(w, w),
    ]
    args = [pr, pg, g_hgrn_l.reshape(1, w), mstk, lmask, bd]
    out_shape = [jax.ShapeDtypeStruct((b, n, w), BF16)]
    out_specs = [pl.BlockSpec((g, n, w), lambda i: (i, 0, 0))]
    state_spec = pl.BlockSpec((g, None, 2, w, R_DIM), lambda i: (i, layer, 0, 0, 0))
    aliases = {}
    if has_init:
        in_specs.append(state_spec)
        args.append(state)
    else:
        in_specs.append(pl.BlockSpec(memory_space=pl.ANY))
        args.append(new_state)
        out_shape.append(jax.ShapeDtypeStruct(new_state.shape, F32))
        out_specs.append(state_spec)
        aliases = {6: 1}
    return pl.pallas_call(
        functools.partial(_hgrn_kernel, nc=nc, has_init=has_init),
        out_shape=tuple(out_shape),
        grid=(b // g,),
        in_specs=in_specs,
        out_specs=tuple(out_specs),
        input_output_aliases=aliases,
        scratch_shapes=[
            pltpu.VMEM((2, n, w), F32),
            pltpu.VMEM((2, n, w), BF16),
            pltpu.VMEM((2, nc, R_DIM, w), F32),
            pltpu.VMEM((2, nc, 8, w), F32),
            pltpu.VMEM((2, nc, R_DIM, w), BF16),
        ],
        compiler_params=_cparams(("arbitrary",)),
        name="hgrn_scan",
    )(*args)


def _fnet_constants(n):
    j = np.arange(F_GROUP_DIM)
    ang = 2.0 * np.pi * ((j[:, None] * j[None, :]) % F_GROUP_DIM) / F_GROUP_DIM
    eye = np.eye(F_GROUPS)
    cs = np.concatenate([np.kron(eye, np.cos(ang)), np.kron(eye, np.sin(ang))], axis=1)
    t = np.arange(n)
    angn = 2.0 * np.pi * ((t[:, None] * t[None, :]) % n) / n
    return tuple(jnp.asarray(m, F32).astype(BF16) for m in (cs, np.cos(angn), np.sin(angn)))


def _fnet_kernel(pf_ref, cs_ref, cn_ref, sn_ref, wf_ref, o_ref, *, scale):
    w = F_WIDTH
    for s in range(pf_ref.shape[0]):
        t = _dot(pf_ref[s, :, 0:w].astype(BF16), cs_ref[...])
        y = (_dot(cn_ref[...], t[:, 0:w].astype(BF16)) - _dot(sn_ref[...], t[:, w:2 * w].astype(BF16))) * scale
        of = _dot(y.astype(BF16), wf_ref[...])
        o_ref[s] = (of * _silu(pf_ref[s, :, w:2 * w])).astype(BF16)


def _fourier(pf, seqs, consts, w_fnet_bf16):
    b0, b = seqs
    n = pf.shape[1]
    cs, cn, sn = consts
    w = F_WIDTH
    g = max(1, min(b, IN_TILE * 2 // n))
    assert b % g == 0 and b0 % g == 0
    full = lambda *shape: pl.BlockSpec(shape, lambda i: (0,) * len(shape))
    return pl.pallas_call(
        functools.partial(_fnet_kernel, scale=float((n * F_GROUP_DIM) ** -0.5)),
        out_shape=jax.ShapeDtypeStruct((b, n, w), BF16),
        grid=(b // g,),
        in_specs=[pl.BlockSpec((g, n, PF_COLS), lambda i: (b0 // g + i, 0, 0)),
                  full(w, 2 * w), full(n, n), full(n, n), full(w, w)],
        out_specs=pl.BlockSpec((g, n, w), lambda i: (i, 0, 0)),
        compiler_params=_cparams(("arbitrary",)),
        name="fourier_mixing",
    )(pf, cs, cn, sn, w_fnet_bf16)


def _outproj_kernel(mac_ref, mrc_ref, mfc_ref, mal_ref, mrl_ref, mfl_ref, xc_ref, xl_ref, mod_ref, g_ref, w_ref,
                    yc_ref, yl_ref, wb_scr, *, ctx_tiles):
    i = pl.program_id(0)

    @pl.when(i == 0)
    def _():
        wb_scr[...] = w_ref[...].astype(BF16)

    is_ctx = i < ctx_tiles
    pick = lambda c_ref, l_ref: jnp.where(is_ctx, c_ref[...], l_ref[...])
    out = (_dot(pick(mac_ref, mal_ref), wb_scr[0:A_WIDTH, :])
           + _dot(pick(mrc_ref, mrl_ref), wb_scr[A_WIDTH:A_WIDTH + R_WIDTH, :])
           + _dot(pick(mfc_ref, mfl_ref), wb_scr[A_WIDTH + R_WIDTH:D_MODEL, :]))
    ms = jnp.mean(out * out, axis=-1, keepdims=True)
    y = pick(xc_ref, xl_ref) + mod_ref[2:3, :] * (out * lax.rsqrt(ms + EPS) * g_ref[...])

    @pl.when(is_ctx)
    def _():
        yc_ref[...] = y

    @pl.when(jnp.logical_not(is_ctx))
    def _():
        yl_ref[...] = y


def _out_projection(mixed_ctx, mixed_lat, x_ctx, x_lat, mod, g_post, w_out, layer):
    bc, nc_, _ = x_ctx.shape
    bl, nl, _ = x_lat.shape
    tm = IN_TILE
    lat_tiles_per_seq = nl // tm
    ctx_tiles = bc * nc_ // tm
    lat_tiles = bl * nl // tm
    last_ctx = ctx_tiles - 1
    ctx_idx = lambda i: (jnp.minimum(i, last_ctx), 0)
    lat_idx = lambda i: (jnp.maximum(i - ctx_tiles, 0), 0)
    mod_idx = lambda i: (jnp.where(i < ctx_tiles, 0, 1 + jnp.maximum(i - ctx_tiles, 0) // lat_tiles_per_seq), 0, 0)
    widths = (A_WIDTH, R_WIDTH, F_WIDTH)
    flat = lambda t: t.reshape(-1, t.shape[-1])
    yc, yl = pl.pallas_call(
        functools.partial(_outproj_kernel, ctx_tiles=ctx_tiles),
        out_shape=(jax.ShapeDtypeStruct((bc * nc_, D_MODEL), F32), jax.ShapeDtypeStruct((bl * nl, D_MODEL), F32)),
        grid=(ctx_tiles + lat_tiles,),
        in_specs=(
            [pl.BlockSpec((tm, wd), ctx_idx) for wd in widths]
            + [pl.BlockSpec((tm, wd), lat_idx) for wd in widths]
            + [
                pl.BlockSpec((tm, D_MODEL), ctx_idx),
                pl.BlockSpec((tm, D_MODEL), lat_idx),
                pl.BlockSpec((None, 3, D_MODEL), mod_idx),
                pl.BlockSpec((None, 1, D_MODEL), lambda i: (layer, 0, 0)),
                pl.BlockSpec((None, D_MODEL, D_MODEL), lambda i: (layer, 0, 0), pipeline_mode=pl.Buffered(1)),
            ]),
        out_specs=(pl.BlockSpec((tm, D_MODEL), ctx_idx), pl.BlockSpec((tm, D_MODEL), lat_idx)),
        scratch_shapes=[pltpu.VMEM((D_MODEL, D_MODEL), BF16)],
        compiler_params=_cparams(("arbitrary",)),
        name="out_projection",
    )(*[flat(t) for t in mixed_ctx], *[flat(t) for t in mixed_lat], flat(x_ctx), flat(x_lat), mod,
      g_post.reshape(DEPTH, 1, D_MODEL), w_out)
    return yc.reshape(x_ctx.shape), yl.reshape(x_lat.shape)


def kernel(x_prompt, x_sample, cache_attn_k, cache_attn_v, state_hgrn, c, c_ctx,
           w_ada, b_ada, g_pre, w_in, rpb, lb_logits, g_hgrn, w_fnet, w_out, g_post):
    nb_ctx, n_ctx, _ = x_prompt.shape
    nb_lat, n_lat, _ = x_sample.shape

    pad_rows = (-(1 + nb_lat)) % 8
    cc = jnp.concatenate([c_ctx[None, :], c, jnp.zeros((pad_rows, D_MODEL), F32)], axis=0)
    mods = _modulations(cc, w_ada, b_ada)
    lbp = _lower_bounds(lb_logits)

    w_fnet_b = w_fnet.astype(BF16)
    hconsts = _hgrn_constants()
    fconsts_ctx = _fnet_constants(n_ctx)
    fconsts_lat = _fnet_constants(n_lat)

    state_rows = state_hgrn.reshape(nb_lat, DEPTH, 2, R_WIDTH, R_DIM)
    cache_kt = jnp.transpose(cache_attn_k, (0, 1, 3, 4, 2))
    cache_vt = jnp.transpose(cache_attn_v, (0, 1, 3, 4, 2))

    yp, ys = x_prompt, x_sample
    new_kv = [jnp.zeros((nb_ctx, DEPTH, A_HEADS, HEAD_DIM, n_ctx), F32) for _ in range(2)]
    new_rows = jnp.zeros((nb_ctx, DEPTH, 2, R_WIDTH, R_DIM), F32)
    ctx_tokens = nb_ctx * n_ctx
    ctx_seqs = (0, nb_ctx)
    lat_seqs = (ctx_tokens // n_lat, nb_lat)
    for l in range(DEPTH):
        mod_all = mods[l, 0:1 + nb_lat].reshape(1 + nb_lat, 3, D_MODEL)
        pa, pr, pg, pf, *new_kv = _in_projection(yp, ys, mod_all, g_pre, lbp, w_in, l, new_kv)
        as_ctx = lambda t: t.reshape(-1, n_ctx, t.shape[-1])
        ma_c = _context_attention(as_ctx(pa), ctx_seqs)
        mr_c, new_rows = _hgrn(as_ctx(pr), as_ctx(pg), ctx_seqs, g_hgrn[l], hconsts, l, new_state=new_rows)
        mf_c = _fourier(as_ctx(pf), ctx_seqs, fconsts_ctx, w_fnet_b[l])
        as_lat = lambda t: t.reshape(-1, n_lat, t.shape[-1])
        ma_l = _neighbourhood_attention(as_lat(pa), lat_seqs, cache_kt, cache_vt, l, rpb[l])
        (mr_l,) = _hgrn(as_lat(pr), as_lat(pg), lat_seqs, g_hgrn[l], hconsts, l, state=state_rows)
        mf_l = _fourier(as_lat(pf), lat_seqs, fconsts_lat, w_fnet_b[l])
        yp, ys = _out_projection((ma_c, mr_c, mf_c), (ma_l, mr_l, mf_l), yp, ys, mod_all, g_post, w_out, l)

    new_state = new_rows.reshape(nb_ctx, DEPTH, 2, R_HEADS, R_DIM, R_DIM)
    new_k, new_v = (jnp.transpose(t, (0, 1, 4, 2, 3)) for t in new_kv)
    return (yp, ys, new_k, new_v, new_state)
```

```python
import functools

import numpy as np
import jax
import jax.numpy as jnp
from jax import lax
from jax.experimental import pallas as pl
from jax.experimental.pallas import tpu as pltpu

F32 = jnp.float32
BF16 = jnp.bfloat16

D_MODEL = 1024
DEPTH = 4
GRID_W = 64
WIN_H = 8
WIN_W = 16
HEAD_DIM = 64
A_HEADS = 8
A_WIDTH = A_HEADS * HEAD_DIM
R_HEADS = 4
R_DIM = 64
R_WIDTH = R_HEADS * R_DIM
F_GROUPS = 4
F_GROUP_DIM = 64
F_WIDTH = F_GROUPS * F_GROUP_DIM
PA_COLS = 4 * A_WIDTH
PR_COLS = 5 * R_WIDTH
PF_COLS = 2 * F_WIDTH
PG_COLS = 4 * R_WIDTH
IN_COLS = PA_COLS + PR_COLS + PF_COLS
CHUNK = 64
EPS = 1e-6
LANES = 128
NEG_INF = float("-inf")
VMEM_LIMIT = 56 * 1024 * 1024


def _cparams(sem):
    return pltpu.CompilerParams(dimension_semantics=sem, vmem_limit_bytes=VMEM_LIMIT)


def _silu(x):
    return x * (1.0 / (1.0 + jnp.exp(-x)))


def _dot(a, b):
    return jnp.dot(a, b, preferred_element_type=F32)


def _dot_nt(a, b):
    return lax.dot_general(a, b, (((1,), (1,)), ((), ())), preferred_element_type=F32)


def _dot_tn(a, b):
    return lax.dot_general(a, b, (((0,), (0,)), ((), ())), preferred_element_type=F32)


def _split2(x):
    hi = x.astype(BF16)
    lo = (x - hi.astype(F32)).astype(BF16)
    return hi, lo


def _mod_kernel(cc_ref, w_ref, b_ref, o_ref):
    a_hi, a_lo = _split2(_silu(cc_ref[...]))
    w_hi, w_lo = _split2(w_ref[...])
    acc = _dot(a_hi, w_hi) + _dot(a_hi, w_lo) + _dot(a_lo, w_hi)
    o_ref[...] = acc + b_ref[...]


def _modulations(cc, w_ada, b_ada):
    rows = cc.shape[0]
    tn = 3 * D_MODEL // 2
    return pl.pallas_call(
        _mod_kernel,
        out_shape=jax.ShapeDtypeStruct((DEPTH, rows, 3 * D_MODEL), F32),
        grid=(DEPTH, 3 * D_MODEL // tn),
        in_specs=[
            pl.BlockSpec((rows, D_MODEL), lambda l, j: (0, 0)),
            pl.BlockSpec((None, D_MODEL, tn), lambda l, j: (l, 0, j)),
            pl.BlockSpec((None, 1, tn), lambda l, j: (l, 0, j)),
        ],
        out_specs=pl.BlockSpec((None, rows, tn), lambda l, j: (l, 0, j)),
        compiler_params=_cparams(("arbitrary", "arbitrary")),
        name="adaln_mod",
    )(cc, w_ada, b_ada.reshape(DEPTH, 1, 3 * D_MODEL))


def _lb_kernel(x_ref, o_ref):
    xs = [x_ref[i] for i in range(DEPTH)]
    m = functools.reduce(jnp.maximum, xs)
    es = [jnp.exp(x - m) for x in xs]
    tot = functools.reduce(lambda a, b: a + b, es)
    cum = None
    first = None
    for i in range(DEPTH):
        p = es[i] / tot
        cum = p if cum is None else cum + p
        if first is None:
            first = cum
        lb = jnp.maximum(cum - first, 0.0)
        o_ref[0, i] = lb
        o_ref[1, i] = jnp.log1p(-lb)
        o_ref[2, i] = 1.0 - lb


def _lower_bounds(lb_logits):
    x = jnp.transpose(lb_logits, (1, 0, 2))
    return pl.pallas_call(
        _lb_kernel,
        out_shape=jax.ShapeDtypeStruct((3, DEPTH, 2, R_WIDTH), F32),
        name="hgrn_lower_bounds",
    )(x)


IN_TILE = 512


def _inproj_kernel(xc_ref, xl_ref, mod_ref, g_ref, lbp_ref, w_ref, kprev_ref, vprev_ref,
                   pa_ref, pr_ref, pg_ref, pf_ref, ko_ref, vo_ref, wb_scr, *, ctx_tiles):
    del kprev_ref, vprev_ref
    i = pl.program_id(0)

    @pl.when(i == 0)
    def _():
        wb_scr[...] = w_ref[...].astype(BF16)

    is_ctx = i < ctx_tiles
    x = jnp.where(is_ctx, xc_ref[...], xl_ref[...])
    ms = jnp.mean(x * x, axis=-1, keepdims=True)
    y = x * lax.rsqrt(ms + EPS) * g_ref[...]
    h = (y * (1.0 + mod_ref[1:2, :]) + mod_ref[0:1, :]).astype(BF16)

    w = R_WIDTH
    pr = _dot(h, wb_scr[:, PA_COLS:PA_COLS + PR_COLS])
    pf_ref[...] = _dot(h, wb_scr[:, PA_COLS + PR_COLS:IN_COLS])
    pr_ref[:, 0:w] = _silu(pr[:, 0:w])
    for d in range(2):
        z = pr[:, (1 + d) * w:(2 + d) * w]
        e = jnp.exp(-jnp.abs(z))
        r = 1.0 / (1.0 + e)
        pos = z >= 0.0
        one_m_lb = lbp_ref[2, d:d + 1, :]
        f = lbp_ref[0, d:d + 1, :] + one_m_lb * (jnp.where(pos, 1.0, e) * r)
        log_f = jnp.where(f > 0.0, jnp.log(f), lbp_ref[1, d:d + 1, :] + z)
        g_hi, g_lo = _split2(log_f)
        pg_ref[:, 2 * d * w:(2 * d + 1) * w] = g_hi
        pg_ref[:, (2 * d + 1) * w:(2 * d + 2) * w] = g_lo
        pr_ref[:, (1 + d) * w:(2 + d) * w] = one_m_lb * (jnp.where(pos, e, 1.0) * r)
    pr_ref[:, 3 * w:4 * w] = pr[:, 3 * w:4 * w]
    pr_ref[:, 4 * w:5 * w] = _silu(pr[:, 4 * w:5 * w])
    pa_ref[...] = _dot(h, wb_scr[:, 0:PA_COLS])

    @pl.when(is_ctx)
    def _():
        seqs, _, _, n = ko_ref.shape
        for s in range(seqs):
            rows = slice(s * n, (s + 1) * n)
            ko_ref[s] = pa_ref[rows, A_WIDTH:2 * A_WIDTH].T.reshape(A_HEADS, HEAD_DIM, n)
            vo_ref[s] = pa_ref[rows, 2 * A_WIDTH:3 * A_WIDTH].T.reshape(A_HEADS, HEAD_DIM, n)


def _in_projection(x_ctx, x_lat, mod, g_pre, lbp, w_in, layer, cache):
    bc, nc_, _ = x_ctx.shape
    bl, nl, _ = x_lat.shape
    tm = IN_TILE
    seqs_per_tile = tm // nc_
    lat_tiles_per_seq = nl // tm
    ctx_tiles = bc * nc_ // tm
    lat_tiles = bl * nl // tm
    tokens = (ctx_tiles + lat_tiles) * tm
    last_ctx = ctx_tiles - 1
    ctx_idx = lambda i: (jnp.minimum(i, last_ctx), 0)
    lat_idx = lambda i: (jnp.maximum(i - ctx_tiles, 0), 0)
    mod_idx = lambda i: (jnp.where(i < ctx_tiles, 0, 1 + jnp.maximum(i - ctx_tiles, 0) // lat_tiles_per_seq), 0, 0)
    tok = lambda i: (i, 0)
    cache_spec = pl.BlockSpec((seqs_per_tile, None, A_HEADS, HEAD_DIM, nc_),
                              lambda i: (jnp.minimum(i, last_ctx), layer, 0, 0, 0))
    cache_shape = jax.ShapeDtypeStruct(cache[0].shape, F32)
    return pl.pallas_call(
        functools.partial(_inproj_kernel, ctx_tiles=ctx_tiles),
        out_shape=(
            jax.ShapeDtypeStruct((tokens, PA_COLS), F32),
            jax.ShapeDtypeStruct((tokens, PR_COLS), F32),
            jax.ShapeDtypeStruct((tokens, PG_COLS), BF16),
            jax.ShapeDtypeStruct((tokens, PF_COLS), F32),
            cache_shape, cache_shape,
        ),
        grid=(ctx_tiles + lat_tiles,),
        in_specs=[
            pl.BlockSpec((tm, D_MODEL), ctx_idx),
            pl.BlockSpec((tm, D_MODEL), lat_idx),
            pl.BlockSpec((None, 3, D_MODEL), mod_idx),
            pl.BlockSpec((None, 1, D_MODEL), lambda i: (layer, 0, 0)),
            pl.BlockSpec((3, None, 2, R_WIDTH), lambda i: (0, layer, 0, 0)),
            pl.BlockSpec((None, D_MODEL, IN_COLS), lambda i: (layer, 0, 0), pipeline_mode=pl.Buffered(1)),
            pl.BlockSpec(memory_space=pl.ANY),
            pl.BlockSpec(memory_space=pl.ANY),
        ],
        out_specs=(
            pl.BlockSpec((tm, PA_COLS), tok),
            pl.BlockSpec((tm, PR_COLS), tok),
            pl.BlockSpec((tm, PG_COLS), tok),
            pl.BlockSpec((tm, PF_COLS), tok),
            cache_spec, cache_spec,
        ),
        scratch_shapes=[pltpu.VMEM((D_MODEL, IN_COLS), BF16)],
        input_output_aliases={6: 4, 7: 5},
        compiler_params=_cparams(("arbitrary",)),
        name="in_projection",
    )(x_ctx.reshape(bc * nc_, D_MODEL), x_lat.reshape(bl * nl, D_MODEL), mod,
      g_pre.reshape(DEPTH, 1, D_MODEL), lbp, w_in, cache[0], cache[1])


LOG2E = 1.4426950408889634
Q_SCALE = HEAD_DIM ** -0.5 * LOG2E


def _lane_lo():
    return lax.broadcasted_iota(jnp.int32, (1, LANES), 1) < HEAD_DIM


def _ctx_attn_kernel(pas_ref, os_ref, s_scr):
    for s in range(pas_ref.shape[0]):
        _ctx_attn_sequence(pas_ref.at[s], os_ref.at[s], s_scr)


def _ctx_attn_sequence(pa_ref, o_ref, s_scr):
    lo = _lane_lo()
    npair = A_HEADS // 2

    def scores(p):
        c = p * LANES
        q2 = pa_ref[:, c:c + LANES] * Q_SCALE
        k2 = pa_ref[:, A_WIDTH + c:A_WIDTH + c + LANES].astype(BF16)
        for hh in range(2):
            sel = lo if hh == 0 else jnp.logical_not(lo)
            s_scr[p % 2, hh] = _dot_nt(jnp.where(sel, q2, 0.0).astype(BF16), k2)

    def finish(p):
        c = p * LANES
        v2 = pa_ref[:, 2 * A_WIDTH + c:2 * A_WIDTH + c + LANES].astype(BF16)
        outs = []
        for hh in range(2):
            s = s_scr[p % 2, hh]
            e = jnp.exp2(s - jnp.max(s, axis=-1, keepdims=True))
            inv = 1.0 / jnp.sum(e, axis=-1, keepdims=True)
            outs.append(_dot(e.astype(BF16), v2) * inv)
        o2 = jnp.where(lo, outs[0], outs[1])
        o_ref[:, c:c + LANES] = (o2 * _silu(pa_ref[:, 3 * A_WIDTH + c:3 * A_WIDTH + c + LANES])).astype(BF16)

    scores(0)
    for p in range(npair):
        if p + 1 < npair:
            scores(p + 1)
        finish(p)


def _context_attention(pa, seqs):
    b0, b = seqs
    n = pa.shape[1]
    g = 2 if b % 2 == 0 and b0 % 2 == 0 else 1
    return pl.pallas_call(
        _ctx_attn_kernel,
        out_shape=jax.ShapeDtypeStruct((b, n, A_WIDTH), BF16),
        grid=(b // g,),
        in_specs=[pl.BlockSpec((g, n, PA_COLS), lambda i: (b0 // g + i, 0, 0))],
        out_specs=pl.BlockSpec((g, n, A_WIDTH), lambda i: (i, 0, 0)),
        scratch_shapes=[pltpu.VMEM((2, 2, n, n), F32)],
        compiler_params=_cparams(("arbitrary",)),
        name="context_attention",
    )(pa)


QROWS = 4


def _nbr_blocks(rows):
    kh = min(WIN_H, rows)
    out = []
    for r_first in range(0, rows, QROWS):
        r0s = [min(max(r - kh // 2, 0), rows - kh) for r in range(r_first, r_first + QROWS)]
        lo, hi = min(r0s), max(r0s) + kh
        lo -= lo % 2
        span = hi - lo
        span += (-span) % 4
        if lo + span > rows:
            lo = rows - span
        assert lo >= 0 and lo % 2 == 0
        out.append((r_first, lo, span, r0s))
    return out, kh


N_REL_ROWS = 2 * WIN_H - 1
TABLE_ROWS = WIN_H * GRID_W


def _build_bias_tables(base_ref, ta_scr, tb_scr, heads):
    lane = lax.broadcasted_iota(jnp.int32, (GRID_W, LANES), 1)
    kc = lax.broadcasted_iota(jnp.int32, (GRID_W, LANES), 0)
    qc = lane % GRID_W
    lo = lane < GRID_W
    c0 = jnp.clip(qc - WIN_W // 2, 0, GRID_W - WIN_W)
    col_in = (kc >= c0) & (kc < c0 + WIN_W)

    for h in heads:
        def tile(i, lane_off):
            if not 0 <= i < N_REL_ROWS:
                return jnp.full((GRID_W, LANES), NEG_INF, F32)
            row = jnp.broadcast_to(base_ref[h, i:i + 1, :] * LOG2E, (GRID_W, LANES))
            return pltpu.roll(row, lane_off, 1, stride=1, stride_axis=0)

        for j in range(WIN_H):
            rows = slice(j * GRID_W, (j + 1) * GRID_W)
            ta_scr[h, rows, :] = jnp.where(col_in, jnp.where(lo, tile(2 * j + 1, 0), tile(2 * j, GRID_W)), NEG_INF)
            tb_scr[h, rows, :] = jnp.where(col_in, jnp.where(lo, tile(2 * j, 0), tile(2 * j - 1, GRID_W)), NEG_INF)


def _nbr_attn_kernel(q_ref, k_ref, v_ref, g_ref, kc_ref, vc_ref, base_ref, o_ref, s_scr, p_scr, ta_scr, tb_scr, *, rows):
    p = pl.program_id(1)

    @pl.when(pl.program_id(0) == 0)
    def _():
        _build_bias_tables(base_ref, ta_scr, tb_scr, (2 * p, 2 * p + 1))

    lo = _lane_lo()
    hi = jnp.logical_not(lo)
    blocks, kh = _nbr_blocks(rows)
    nq = QROWS * GRID_W
    past = kc_ref.shape[-1]
    ctx0 = s_scr.shape[2] - past
    kc_rows = kc_ref[...].reshape(LANES, past).T.astype(BF16)
    vct = vc_ref[...].reshape(LANES, past).astype(BF16)
    head_rows = [slice(hh * HEAD_DIM, (hh + 1) * HEAD_DIM) for hh in range(2)]
    ctx_tiles = [slice(ctx0 + t * GRID_W, ctx0 + (t + 1) * GRID_W) for t in range(past // GRID_W)]

    def window_tiles(bi, c, head):
        r_first, u0, span, r0s = blocks[bi]
        out = []
        for t in range(span):
            kr = u0 + t
            rs = slice(t * GRID_W, (t + 1) * GRID_W)
            ok = [r0s[2 * c + e] <= kr < r0s[2 * c + e] + kh for e in range(2)]
            if not (ok[0] or ok[1]):
                out.append((rs, None))
                continue
            i = kr - (r_first + 2 * c) + WIN_H - 1
            assert 0 <= i <= N_REL_ROWS
            tab, j = (ta_scr, (i - 1) // 2) if i % 2 else (tb_scr, i // 2)
            bias = tab[head, j * GRID_W:(j + 1) * GRID_W, :]
            if not ok[0]:
                bias = jnp.where(lo, NEG_INF, bias)
            if not ok[1]:
                bias = jnp.where(hi, NEG_INF, bias)
            out.append((rs, bias))
        return out

    def scores(bi):
        r_first, u0, span, _ = blocks[bi]
        qs, ks, nk = r_first * GRID_W, u0 * GRID_W, span * GRID_W
        q2 = q_ref[qs:qs + nq, :] * Q_SCALE
        ku = k_ref[ks:ks + nk, :].astype(BF16)
        for hh in range(2):
            qh = jnp.where(lo if hh == 0 else hi, q2, 0.0).astype(BF16)
            s_scr[bi % 2, hh, 0:nk, :] = _dot_nt(ku, qh)
            s_scr[bi % 2, hh, ctx0:ctx0 + past, :] = _dot_nt(kc_rows, qh)

    def fold_rows(x, op):
        parts = [x[r:r + 8] for r in range(0, GRID_W, 8)]
        while len(parts) > 1:
            parts = [op(parts[i], parts[i + 1]) for i in range(0, len(parts), 2)]
        return parts[0]

    def softmax(bi):
        par = bi % 2
        inv = []
        for hh in range(2):
            parts = []
            for c in range(QROWS // 2):
                cs = slice(c * LANES, (c + 1) * LANES)
                tiles = window_tiles(bi, c, 2 * p + hh)
                m_acc = None
                for rs, bias in tiles:
                    if bias is None:
                        continue
                    s = s_scr[par, hh, rs, cs] + bias
                    s_scr[par, hh, rs, cs] = s
                    t = fold_rows(s, jnp.maximum)
                    m_acc = t if m_acc is None else jnp.maximum(m_acc, t)
                for rs in ctx_tiles:
                    m_acc = jnp.maximum(m_acc, fold_rows(s_scr[par, hh, rs, cs], jnp.maximum))
                m = jnp.max(m_acc, axis=0, keepdims=True)
                l_acc = None
                for rs, bias in tiles:
                    if bias is None:
                        p_scr[par, hh, rs, cs] = jnp.zeros((GRID_W, LANES), BF16)
                        continue
                    e = jnp.exp2(s_scr[par, hh, rs, cs] - m)
                    p_scr[par, hh, rs, cs] = e.astype(BF16)
                    t = fold_rows(e, jnp.add)
                    l_acc = t if l_acc is None else l_acc + t
                for rs in ctx_tiles:
                    e = jnp.exp2(s_scr[par, hh, rs, cs] - m)
                    p_scr[par, hh, rs, cs] = e.astype(BF16)
                    l_acc = l_acc + fold_rows(e, jnp.add)
                parts.append(1.0 / jnp.sum(l_acc, axis=0, keepdims=True))
            inv.append(jnp.concatenate(parts, axis=1))
        return inv

    def values(bi, inv):
        r_first, u0, span, _ = blocks[bi]
        qs, ks, nk = r_first * GRID_W, u0 * GRID_W, span * GRID_W
        vut = v_ref[ks:ks + nk, :].T.astype(BF16)
        outs = []
        for hh in range(2):
            o = (_dot(vut[head_rows[hh]], p_scr[bi % 2, hh, 0:nk, :])
                 + _dot(vct[head_rows[hh]], p_scr[bi % 2, hh, ctx0:ctx0 + past, :]))
            outs.append(o * inv[hh])
        o2 = jnp.concatenate(outs, axis=0).T
        o_ref[qs:qs + nq, :] = (o2 * _silu(g_ref[qs:qs + nq, :])).astype(BF16)

    scores(0)
    for bi in range(len(blocks)):
        if bi + 1 < len(blocks):
            scores(bi + 1)
        values(bi, softmax(bi))


def _bias_base(rpb_l):
    mid = WIN_W - 1
    zeros = jnp.zeros(rpb_l.shape[:2] + (LANES - (2 * WIN_W - 1),), F32)
    rev = rpb_l[..., ::-1]
    base = jnp.concatenate([rev[..., mid:], zeros, rev[..., :mid]], axis=-1)
    return jnp.pad(base, ((0, 0), (0, 2 * WIN_H - N_REL_ROWS), (0, 0)))


def _neighbourhood_attention(pa, seqs, cache_kt, cache_vt, layer, rpb_l):
    b0, b = seqs
    n = pa.shape[1]
    rows = n // GRID_W
    past = cache_kt.shape[-1]
    blocks, _ = _nbr_blocks(rows)
    max_nk = max(s for (_, _, s, _) in blocks) * GRID_W
    npair = A_HEADS // 2
    col = lambda off: (lambda i, p: (b0 + i, 0, off + p))
    cache_spec = pl.BlockSpec((None, None, 2, HEAD_DIM, past), lambda i, p: (i, layer, p, 0, 0))
    return pl.pallas_call(
        functools.partial(_nbr_attn_kernel, rows=rows),
        out_shape=jax.ShapeDtypeStruct((b, n, A_WIDTH), BF16),
        grid=(b, npair),
        in_specs=[
            pl.BlockSpec((None, n, LANES), col(0)),
            pl.BlockSpec((None, n, LANES), col(npair)),
            pl.BlockSpec((None, n, LANES), col(2 * npair)),
            pl.BlockSpec((None, n, LANES), col(3 * npair)),
            cache_spec,
            cache_spec,
            pl.BlockSpec((A_HEADS, 2 * WIN_H, LANES), lambda i, p: (0, 0, 0)),
        ],
        out_specs=pl.BlockSpec((None, n, LANES), lambda i, p: (i, 0, p)),
        scratch_shapes=[
            pltpu.VMEM((2, 2, max_nk + past, QROWS * GRID_W), F32),
            pltpu.VMEM((2, 2, max_nk + past, QROWS * GRID_W), BF16),
            pltpu.VMEM((A_HEADS, TABLE_ROWS, LANES), F32),
            pltpu.VMEM((A_HEADS, TABLE_ROWS, LANES), F32),
        ],
        compiler_params=_cparams(("arbitrary", "arbitrary")),
        name="neighbourhood_attention",
    )(pa, pa, pa, pa, cache_kt, cache_vt, _bias_base(rpb_l))


COARSE_HALVES = (32, 16, 8)
FINE_HALVES = (4, 2, 1)
N_LEVELS = len(COARSE_HALVES) + len(FINE_HALVES)
ANCHOR_BLOCK = CHUNK
ANCHOR_LEVELS = 0
ANCHOR_MAX_EXPONENT = 80.0
MASK_DIAG = N_LEVELS
MASK_ANCHOR = N_LEVELS + 1
GROUP = 4


def _hgrn_constants():
    c = CHUNK
    idx = np.arange(c)
    mats = [np.tril(np.ones((c, c)))]
    masks = []
    for h in COARSE_HALVES + FINE_HALVES:
        blk = idx // (2 * h)
        mid = blk * 2 * h + h - 1
        upper = idx > mid
        if h in FINE_HALVES:
            m = np.zeros((c, c))
            for i in range(c):
                if upper[i]:
                    m[i, mid[i] + 1:i + 1] = 1.0
                else:
                    m[i, i + 1:mid[i] + 1] = 1.0
            mats.append(m)
        same = blk[:, None] == blk[None, :]
        masks.append((same & upper[:, None] & (~upper)[None, :]).astype(np.float64))
    masks.append(np.eye(c))
    same_block = (idx[:, None] // ANCHOR_BLOCK) == (idx[None, :] // ANCHOR_BLOCK)
    masks.append((same_block & (idx[None, :] <= idx[:, None])).astype(np.float64))
    fwd = np.concatenate(mats, axis=0)
    bwd = np.concatenate([m[::-1, ::-1] for m in mats], axis=0)
    mk_f = np.stack([np.tile(m, (1, R_HEADS)) for m in masks])
    mk_b = np.stack([np.tile(m[::-1, ::-1], (1, R_HEADS)) for m in masks])
    hid = np.arange(R_WIDTH) // R_DIM
    bd = (hid[:, None] == hid[None, :]).astype(np.float64)
    return (jnp.asarray(np.stack([fwd, bwd]), BF16), jnp.asarray(np.stack([mk_f, mk_b]), F32),
            jnp.asarray(bd, BF16))


def _hgrn_kernel(*refs, nc, has_init):
    prs_ref, pgs_ref, gh_ref, mstk_ref, lmask_ref, bd_ref = refs[:6]
    if has_init:
        s0s_ref, os_ref = refs[6:8]
        sfins_ref = None
        scratch = refs[8:]
    else:
        s0s_ref = None
        os_ref, sfins_ref = refs[7:9]
        scratch = refs[9:]
    for s in range(prs_ref.shape[0]):
        _hgrn_sequence(prs_ref.at[s], pgs_ref.at[s], gh_ref, mstk_ref, lmask_ref, bd_ref,
                       None if s0s_ref is None else s0s_ref.at[s], os_ref.at[s],
                       None if sfins_ref is None else sfins_ref.at[s], scratch, nc)


def _hgrn_sequence(pr_ref, pg_ref, gh_ref, mstk_ref, lmask_ref, bd_ref, s0_ref, o_ref, sfin_ref, scratch, nc):
    has_init = s0_ref is not None
    gc_scr, qg_scr, upd_scr, dec_scr, sbd_scr = scratch
    w = R_WIDTH
    c = CHUNK
    bd = bd_ref[...]
    hid = lax.broadcasted_iota(jnp.int32, (1, w), 1) // R_DIM
    q_at = lambda rs: pr_ref[rs, 0:w]
    k_at = lambda d, rs: pr_ref[rs, (1 + d) * w:(2 + d) * w]
    g_at = lambda d, rs: pg_ref[rs, 2 * d * w:2 * (d + 1) * w]

    def tile4(x):
        return jnp.concatenate([x] * R_HEADS, axis=0)

    def head_diagonal(full):
        out = full[(R_HEADS - 1) * R_DIM:R_HEADS * R_DIM]
        for h in range(R_HEADS - 2, -1, -1):
            out = jnp.where(hid == h, full[h * R_DIM:(h + 1) * R_DIM], out)
        return out

    def chunk_rows(ci):
        return pl.ds(pl.multiple_of(ci * c, c), c)

    def cumulative(d, rs, n_mats):
        s = _dot(mstk_ref[d, 0:n_mats * c, :], g_at(d, rs))
        return s[:, 0:w] + s[:, w:2 * w]

    def level_exponents(gcum, d, halves):
        out = []
        for h in halves:
            parts = []
            for s0 in range(0, c, 2 * h):
                anchor = s0 + h - 1 + d
                parts.append(-jnp.abs(gcum[s0:s0 + 2 * h] - gcum[anchor:anchor + 1]))
            out.append(parts[0] if len(parts) == 1 else jnp.concatenate(parts, axis=0))
        return out

    def anchor_shift(gcum, d):
        parts = []
        for s0 in range(0, c, ANCHOR_BLOCK):
            anchor = s0 + ANCHOR_BLOCK // 2 - 1 + d
            parts.append(gcum[s0:s0 + ANCHOR_BLOCK] - gcum[anchor:anchor + 1])
        return jnp.concatenate(parts, axis=0)

    groups = nc // GROUP
    pairs = [(j, d) for j in range(GROUP) for d in range(2)]

    def increment_group(gi, worst):
        rows = [chunk_rows(gi * GROUP + j) for j in range(GROUP)]
        gcums = [cumulative(d, rows[j], 1) for j, d in pairs]
        lasts = [g[c - 1:c, :] if d == 0 else g[0:1, :] for g, (j, d) in zip(gcums, pairs)]
        kls = [(k_at(d, rows[j]) * jnp.exp(last - g)).astype(BF16)
               for g, last, (j, d) in zip(gcums, lasts, pairs)]
        vs = [pr_ref[rows[j], 3 * w:4 * w].astype(BF16) for j in range(GROUP)]
        upds = [_dot_tn(vs[j], kl) for kl, (j, d) in zip(kls, pairs)]
        for g, last, upd, (j, d) in zip(gcums, lasts, upds, pairs):
            ci = gi * GROUP + j
            gc_scr[d, rows[j], :] = g
            upd_scr[d, ci] = head_diagonal(upd)
            dec_scr[d, ci] = jnp.broadcast_to(jnp.exp(last), (8, w))
            qg_scr[d, rows[j], :] = (q_at(rows[j]) * jnp.exp(g)).astype(BF16)
            sh = jnp.abs(anchor_shift(g, d))
            for s0 in range(0, c, 8):
                worst = jnp.maximum(worst, sh[s0:s0 + 8])
        return worst

    worst = lax.fori_loop(0, groups, increment_group, jnp.zeros((8, w), F32))
    q_max = jnp.max(jnp.max(jnp.abs(pr_ref[:, 0:w]), axis=0, keepdims=True), axis=1, keepdims=True)
    anchor_ok = jnp.max(worst + jnp.log(jnp.maximum(q_max, 1.0))) < ANCHOR_MAX_EXPONENT

    def scan_step(ci, carry):
        sf, sb = carry
        cb = nc - 1 - ci
        sbd_scr[0, ci] = sf.astype(BF16)
        sbd_scr[1, cb] = sb.astype(BF16)
        sf = sf * dec_scr[0, ci][0:1, :] + upd_scr[0, ci]
        sb = sb * dec_scr[1, cb][0:1, :] + upd_scr[1, cb]
        return sf, sb

    if has_init:
        init = tuple(jnp.concatenate([s0_ref[d], jnp.zeros((w, LANES - R_DIM), F32)], axis=1).T[0:R_DIM]
                     for d in range(2))
    else:
        init = (jnp.zeros((R_DIM, w), F32), jnp.zeros((R_DIM, w), F32))
    finals = lax.fori_loop(0, nc, scan_step, init)
    if sfin_ref is not None:
        for d in range(2):
            padded = jnp.concatenate([finals[d], jnp.zeros((LANES - R_DIM, w), F32)], axis=0)
            sfin_ref[d] = padded.T[:, 0:R_DIM]

    def finish(rs, o):
        x2_hi, x2_lo = _split2(o * o)
        ms = (_dot(x2_hi, bd) + _dot(x2_lo, bd)) * (1.0 / R_DIM)
        y = o * lax.rsqrt(ms + EPS) * gh_ref[...]
        o_ref[rs, :] = (y * pr_ref[rs, 4 * w:5 * w]).astype(BF16)

    def readout_group_anchor(gi, carry):
        rows = [chunk_rows(gi * GROUP + j) for j in range(GROUP)]
        qs = [q_at(rows[j]) for j in range(GROUP)]
        gcums = [gc_scr[d, rows[j], :] for j, d in pairs]
        factors = []
        for g, (j, d) in zip(gcums, pairs):
            es = [jnp.exp(x) for x in level_exponents(g, d, COARSE_HALVES[:ANCHOR_LEVELS])]
            sh = anchor_shift(g, d)
            fs = [(e, e, lv) for lv, e in enumerate(es)]
            fs.append((jnp.exp(sh), jnp.exp(-sh), MASK_ANCHOR))
            factors.append(fs)
        kbds = [tile4(k_at(d, rows[j]).astype(BF16)) * bd for j, d in pairs]
        accs = [None] * len(pairs)
        for lv in range(ANCHOR_LEVELS + 1):
            for i, (j, d) in enumerate(pairs):
                eq, ek, mask = factors[i][lv]
                part = jnp.where(lmask_ref[d, mask] > 0.5,
                                 _dot_nt((qs[j] * eq).astype(BF16), kbds[i] * tile4(ek.astype(BF16))), 0.0)
                accs[i] = part if accs[i] is None else accs[i] + part
        vbds = [tile4(pr_ref[rows[j], 3 * w:4 * w].astype(BF16)) * bd for j in range(GROUP)]
        outs = [None] * GROUP
        for i, (j, d) in enumerate(pairs):
            ci = gi * GROUP + j
            od = _dot(accs[i].astype(BF16), vbds[j]) + _dot_nt(qg_scr[d, rows[j], :], tile4(sbd_scr[d, ci]) * bd)
            outs[j] = od if outs[j] is None else outs[j] + od
        for j in range(GROUP):
            finish(rows[j], outs[j])
        return carry

    def readout_step_split(ci, carry):
        rs = chunk_rows(ci)
        q = q_at(rs)
        qb = q.astype(BF16)
        vbd = tile4(pr_ref[rs, 3 * w:4 * w].astype(BF16)) * bd
        dsts = [cumulative(d, rs, 1 + len(FINE_HALVES)) for d in range(2)]
        exps = [level_exponents(dsts[d][0:c], d, COARSE_HALVES)
                + [dsts[d][(1 + j) * c:(2 + j) * c] for j in range(len(FINE_HALVES))] for d in range(2)]
        kbd = [tile4(k_at(d, rs).astype(BF16)) * bd for d in range(2)]
        a = [_dot_nt(qb, kbd[d]) * lmask_ref[d, MASK_DIAG] for d in range(2)]
        for j in range(N_LEVELS):
            for d in range(2):
                e = jnp.exp(exps[d][j])
                a[d] = a[d] + _dot_nt((q * e).astype(BF16), kbd[d] * tile4(e.astype(BF16))) * lmask_ref[d, j]
        o = None
        for d in range(2):
            od = _dot(a[d].astype(BF16), vbd) + _dot_nt(qg_scr[d, rs, :], tile4(sbd_scr[d, ci]) * bd)
            o = od if o is None else o + od
        finish(rs, o)
        return carry

    @pl.when(anchor_ok)
    def _():
        lax.fori_loop(0, groups, readout_group_anchor, 0)

    @pl.when(jnp.logical_not(anchor_ok))
    def _():
        lax.fori_loop(0, nc, readout_step_split, 0)


def _hgrn(pr, pg, seqs, g_hgrn_l, consts, layer, state=None, new_state=None):
    b0, b = seqs
    n = pr.shape[1]
    nc = n // CHUNK
    mstk, lmask, bd = consts
    w = R_WIDTH
    has_init = state is not None
    g = max(1, min(b, IN_TILE // n))
    assert b % g == 0 and b0 % g == 0
    full = lambda *shape: pl.BlockSpec(shape, lambda i: (0,) * len(shape))
    in_specs = [
        pl.BlockSpec((g, n, PR_COLS), lambda i: (b0 // g + i, 0, 0)),
        pl.BlockSpec((g, n, PG_COLS), lambda i: (b0 // g + i, 0, 0)),
        full(1, w),
        full(*mstk.shape),
        full(*lmask.shape),
        full(w, w),
    ]
    args = [pr, pg, g_hgrn_l.reshape(1, w), mstk, lmask, bd]
    out_shape = [jax.ShapeDtypeStruct((b, n, w), BF16)]
    out_specs = [pl.BlockSpec((g, n, w), lambda i: (i, 0, 0))]
    state_spec = pl.BlockSpec((g, None, 2, w, R_DIM), lambda i: (i, layer, 0, 0, 0))
    aliases = {}
    if has_init:
        in_specs.append(state_spec)
        args.append(state)
    else:
        in_specs.append(pl.BlockSpec(memory_space=pl.ANY))
        args.append(new_state)
        out_shape.append(jax.ShapeDtypeStruct(new_state.shape, F32))
        out_specs.append(state_spec)
        aliases = {6: 1}
    return pl.pallas_call(
        functools.partial(_hgrn_kernel, nc=nc, has_init=has_init),
        out_shape=tuple(out_shape),
        grid=(b // g,),
        in_specs=in_specs,
        out_specs=tuple(out_specs),
        input_output_aliases=aliases,
        scratch_shapes=[
            pltpu.VMEM((2, n, w), F32),
            pltpu.VMEM((2, n, w), BF16),
            pltpu.VMEM((2, nc, R_DIM, w), F32),
            pltpu.VMEM((2, nc, 8, w), F32),
            pltpu.VMEM((2, nc, R_DIM, w), BF16),
        ],
        compiler_params=_cparams(("arbitrary",)),
        name="hgrn_scan",
    )(*args)


def _fnet_constants(n):
    j = np.arange(F_GROUP_DIM)
    ang = 2.0 * np.pi * ((j[:, None] * j[None, :]) % F_GROUP_DIM) / F_GROUP_DIM
    eye = np.eye(F_GROUPS)
    cs = np.concatenate([np.kron(eye, np.cos(ang)), np.kron(eye, np.sin(ang))], axis=1)
    t = np.arange(n)
    angn = 2.0 * np.pi * ((t[:, None] * t[None, :]) % n) / n
    return tuple(jnp.asarray(m, F32).astype(BF16) for m in (cs, np.cos(angn), np.sin(angn)))


def _fnet_kernel(pf_ref, cs_ref, cn_ref, sn_ref, wf_ref, o_ref, *, scale):
    w = F_WIDTH
    for s in range(pf_ref.shape[0]):
        t = _dot(pf_ref[s, :, 0:w].astype(BF16), cs_ref[...])
        y = (_dot(cn_ref[...], t[:, 0:w].astype(BF16)) - _dot(sn_ref[...], t[:, w:2 * w].astype(BF16))) * scale
        of = _dot(y.astype(BF16), wf_ref[...])
        o_ref[s] = (of * _silu(pf_ref[s, :, w:2 * w])).astype(BF16)


def _fourier(pf, seqs, consts, w_fnet_bf16):
    b0, b = seqs
    n = pf.shape[1]
    cs, cn, sn = consts
    w = F_WIDTH
    g = max(1, min(b, IN_TILE * 2 // n))
    assert b % g == 0 and b0 % g == 0
    full = lambda *shape: pl.BlockSpec(shape, lambda i: (0,) * len(shape))
    return pl.pallas_call(
        functools.partial(_fnet_kernel, scale=float((n * F_GROUP_DIM) ** -0.5)),
        out_shape=jax.ShapeDtypeStruct((b, n, w), BF16),
        grid=(b // g,),
        in_specs=[pl.BlockSpec((g, n, PF_COLS), lambda i: (b0 // g + i, 0, 0)),
                  full(w, 2 * w), full(n, n), full(n, n), full(w, w)],
        out_specs=pl.BlockSpec((g, n, w), lambda i: (i, 0, 0)),
        compiler_params=_cparams(("arbitrary",)),
        name="fourier_mixing",
    )(pf, cs, cn, sn, w_fnet_bf16)


def _outproj_kernel(mac_ref, mrc_ref, mfc_ref, mal_ref, mrl_ref, mfl_ref, xc_ref, xl_ref, mod_ref, g_ref, w_ref,
                    yc_ref, yl_ref, wb_scr, *, ctx_tiles):
    i = pl.program_id(0)

    @pl.when(i == 0)
    def _():
        wb_scr[...] = w_ref[...].astype(BF16)

    def project(ma_ref, mr_ref, mf_ref, x_ref, y_ref):
        out = (_dot(ma_ref[...], wb_scr[0:A_WIDTH, :])
               + _dot(mr_ref[...], wb_scr[A_WIDTH:A_WIDTH + R_WIDTH, :])
               + _dot(mf_ref[...], wb_scr[A_WIDTH + R_WIDTH:D_MODEL, :]))
        ms = jnp.mean(out * out, axis=-1, keepdims=True)
        y_ref[...] = x_ref[...] + mod_ref[2:3, :] * (out * lax.rsqrt(ms + EPS) * g_ref[...])

    @pl.when(i < ctx_tiles)
    def _():
        project(mac_ref, mrc_ref, mfc_ref, xc_ref, yc_ref)

    @pl.when(i >= ctx_tiles)
    def _():
        project(mal_ref, mrl_ref, mfl_ref, xl_ref, yl_ref)


def _out_projection(mixed_ctx, mixed_lat, x_ctx, x_lat, mod, g_post, w_out, layer):
    bc, nc_, _ = x_ctx.shape
    bl, nl, _ = x_lat.shape
    tm = IN_TILE
    lat_tiles_per_seq = nl // tm
    ctx_tiles = bc * nc_ // tm
    lat_tiles = bl * nl // tm
    last_ctx = ctx_tiles - 1
    ctx_idx = lambda i: (jnp.minimum(i, last_ctx), 0)
    lat_idx = lambda i: (jnp.maximum(i - ctx_tiles, 0), 0)
    mod_idx = lambda i: (jnp.where(i < ctx_tiles, 0, 1 + jnp.maximum(i - ctx_tiles, 0) // lat_tiles_per_seq), 0, 0)
    widths = (A_WIDTH, R_WIDTH, F_WIDTH)
    flat = lambda t: t.reshape(-1, t.shape[-1])
    yc, yl = pl.pallas_call(
        functools.partial(_outproj_kernel, ctx_tiles=ctx_tiles),
        out_shape=(jax.ShapeDtypeStruct((bc * nc_, D_MODEL), F32), jax.ShapeDtypeStruct((bl * nl, D_MODEL), F32)),
        grid=(ctx_tiles + lat_tiles,),
        in_specs=(
            [pl.BlockSpec((tm, wd), ctx_idx) for wd in widths]
            + [pl.BlockSpec((tm, wd), lat_idx) for wd in widths]
            + [
                pl.BlockSpec((tm, D_MODEL), ctx_idx),
                pl.BlockSpec((tm, D_MODEL), lat_idx),
                pl.BlockSpec((None, 3, D_MODEL), mod_idx),
                pl.BlockSpec((None, 1, D_MODEL), lambda i: (layer, 0, 0)),
                pl.BlockSpec((None, D_MODEL, D_MODEL), lambda i: (layer, 0, 0), pipeline_mode=pl.Buffered(1)),
            ]),
        out_specs=(pl.BlockSpec((tm, D_MODEL), ctx_idx), pl.BlockSpec((tm, D_MODEL), lat_idx)),
        scratch_shapes=[pltpu.VMEM((D_MODEL, D_MODEL), BF16)],
        compiler_params=_cparams(("arbitrary",)),
        name="out_projection",
    )(*[flat(t) for t in mixed_ctx], *[flat(t) for t in mixed_lat], flat(x_ctx), flat(x_lat), mod,
      g_post.reshape(DEPTH, 1, D_MODEL), w_out)
    return yc.reshape(x_ctx.shape), yl.reshape(x_lat.shape)


def kernel(x_prompt, x_sample, cache_attn_k, cache_attn_v, state_hgrn, c, c_ctx,
           w_ada, b_ada, g_pre, w_in, rpb, lb_logits, g_hgrn, w_fnet, w_out, g_post):
    nb_ctx, n_ctx, _ = x_prompt.shape
    nb_lat, n_lat, _ = x_sample.shape

    pad_rows = (-(1 + nb_lat)) % 8
    cc = jnp.concatenate([c_ctx[None, :], c, jnp.zeros((pad_rows, D_MODEL), F32)], axis=0)
    mods = _modulations(cc, w_ada, b_ada)
    lbp = _lower_bounds(lb_logits)

    w_fnet_b = w_fnet.astype(BF16)
    hconsts = _hgrn_constants()
    fconsts_ctx = _fnet_constants(n_ctx)
    fconsts_lat = _fnet_constants(n_lat)

    state_rows = state_hgrn.reshape(nb_lat, DEPTH, 2, R_WIDTH, R_DIM)
    cache_kt = jnp.transpose(cache_attn_k, (0, 1, 3, 4, 2))
    cache_vt = jnp.transpose(cache_attn_v, (0, 1, 3, 4, 2))

    yp, ys = x_prompt, x_sample
    new_kv = [jnp.zeros((nb_ctx, DEPTH, A_HEADS, HEAD_DIM, n_ctx), F32) for _ in range(2)]
    new_rows = jnp.zeros((nb_ctx, DEPTH, 2, R_WIDTH, R_DIM), F32)
    ctx_tokens = nb_ctx * n_ctx
    ctx_seqs = (0, nb_ctx)
    lat_seqs = (ctx_tokens // n_lat, nb_lat)
    for l in range(DEPTH):
        mod_all = mods[l, 0:1 + nb_lat].reshape(1 + nb_lat, 3, D_MODEL)
        pa, pr, pg, pf, *new_kv = _in_projection(yp, ys, mod_all, g_pre, lbp, w_in, l, new_kv)
        as_ctx = lambda t: t.reshape(-1, n_ctx, t.shape[-1])
        ma_c = _context_attention(as_ctx(pa), ctx_seqs)
        mr_c, new_rows = _hgrn(as_ctx(pr), as_ctx(pg), ctx_seqs, g_hgrn[l], hconsts, l, new_state=new_rows)
        mf_c = _fourier(as_ctx(pf), ctx_seqs, fconsts_ctx, w_fnet_b[l])
        as_lat = lambda t: t.reshape(-1, n_lat, t.shape[-1])
        ma_l = _neighbourhood_attention(as_lat(pa), lat_seqs, cache_kt, cache_vt, l, rpb[l])
        (mr_l,) = _hgrn(as_lat(pr), as_lat(pg), lat_seqs, g_hgrn[l], hconsts, l, state=state_rows)
        mf_l = _fourier(as_lat(pf), lat_seqs, fconsts_lat, w_fnet_b[l])
        yp, ys = _out_projection((ma_c, mr_c, mf_c), (ma_l, mr_l, mf_l), yp, ys, mod_all, g_post, w_out, l)

    new_state = new_rows.reshape(nb_ctx, DEPTH, 2, R_HEADS, R_DIM, R_DIM)
    new_k, new_v = (jnp.transpose(t, (0, 1, 4, 2, 3)) for t in new_kv)
    return (yp, ys, new_k, new_v, new_state)
```

```python
import functools

import numpy as np
import jax
import jax.numpy as jnp
from jax import lax
from jax.experimental import pallas as pl
from jax.experimental.pallas import tpu as pltpu

F32 = jnp.float32
BF16 = jnp.bfloat16

D_MODEL = 1024
DEPTH = 4
GRID_W = 64
WIN_H = 8
WIN_W = 16
HEAD_DIM = 64
A_HEADS = 8
A_WIDTH = A_HEADS * HEAD_DIM
R_HEADS = 4
R_DIM = 64
R_WIDTH = R_HEADS * R_DIM
F_GROUPS = 4
F_GROUP_DIM = 64
F_WIDTH = F_GROUPS * F_GROUP_DIM
PA_COLS = 4 * A_WIDTH
PR_COLS = 5 * R_WIDTH
PF_COLS = 2 * F_WIDTH
PG_COLS = 4 * R_WIDTH
IN_COLS = PA_COLS + PR_COLS + PF_COLS
CHUNK = 64
EPS = 1e-6
LANES = 128
NEG_INF = float("-inf")
VMEM_LIMIT = 56 * 1024 * 1024


def _cparams(sem):
    return pltpu.CompilerParams(dimension_semantics=sem, vmem_limit_bytes=VMEM_LIMIT)


def _silu(x):
    return x * (1.0 / (1.0 + jnp.exp(-x)))


def _dot(a, b):
    return jnp.dot(a, b, preferred_element_type=F32)


def _dot_nt(a, b):
    return lax.dot_general(a, b, (((1,), (1,)), ((), ())), preferred_element_type=F32)


def _dot_tn(a, b):
    return lax.dot_general(a, b, (((0,), (0,)), ((), ())), preferred_element_type=F32)


def _split2(x):
    hi = x.astype(BF16)
    lo = (x - hi.astype(F32)).astype(BF16)
    return hi, lo


def _mod_kernel(cc_ref, w_ref, b_ref, o_ref):
    a_hi, a_lo = _split2(_silu(cc_ref[...]))
    w_hi, w_lo = _split2(w_ref[...])
    acc = _dot(a_hi, w_hi) + _dot(a_hi, w_lo) + _dot(a_lo, w_hi)
    o_ref[...] = acc + b_ref[...]


def _modulations(cc, w_ada, b_ada):
    rows = cc.shape[0]
    tn = 3 * D_MODEL // 2
    return pl.pallas_call(
        _mod_kernel,
        out_shape=jax.ShapeDtypeStruct((DEPTH, rows, 3 * D_MODEL), F32),
        grid=(DEPTH, 3 * D_MODEL // tn),
        in_specs=[
            pl.BlockSpec((rows, D_MODEL), lambda l, j: (0, 0)),
            pl.BlockSpec((None, D_MODEL, tn), lambda l, j: (l, 0, j)),
            pl.BlockSpec((None, 1, tn), lambda l, j: (l, 0, j)),
        ],
        out_specs=pl.BlockSpec((None, rows, tn), lambda l, j: (l, 0, j)),
        compiler_params=_cparams(("arbitrary", "arbitrary")),
        name="adaln_mod",
    )(cc, w_ada, b_ada.reshape(DEPTH, 1, 3 * D_MODEL))


def _lb_kernel(x_ref, o_ref):
    xs = [x_ref[i] for i in range(DEPTH)]
    m = functools.reduce(jnp.maximum, xs)
    es = [jnp.exp(x - m) for x in xs]
    tot = functools.reduce(lambda a, b: a + b, es)
    cum = None
    first = None
    for i in range(DEPTH):
        p = es[i] / tot
        cum = p if cum is None else cum + p
        if first is None:
            first = cum
        lb = jnp.maximum(cum - first, 0.0)
        o_ref[0, i] = lb
        o_ref[1, i] = jnp.log1p(-lb)
        o_ref[2, i] = 1.0 - lb


def _lower_bounds(lb_logits):
    x = jnp.transpose(lb_logits, (1, 0, 2))
    return pl.pallas_call(
        _lb_kernel,
        out_shape=jax.ShapeDtypeStruct((3, DEPTH, 2, R_WIDTH), F32),
        name="hgrn_lower_bounds",
    )(x)


IN_TILE = 512


def _inproj_kernel(xc_ref, xl_ref, mod_ref, g_ref, lbp_ref, w_ref, kprev_ref, vprev_ref,
                   pa_ref, pr_ref, pg_ref, pf_ref, ko_ref, vo_ref, wb_scr, *, ctx_tiles):
    del kprev_ref, vprev_ref
    i = pl.program_id(0)

    @pl.when(i == 0)
    def _():
        wb_scr[...] = w_ref[...].astype(BF16)

    def project(x_ref, write_cache):
        x = x_ref[...]
        ms = jnp.mean(x * x, axis=-1, keepdims=True)
        y = x * lax.rsqrt(ms + EPS) * g_ref[...]
        h = (y * (1.0 + mod_ref[1:2, :]) + mod_ref[0:1, :]).astype(BF16)

        w = R_WIDTH
        pr = _dot(h, wb_scr[:, PA_COLS:PA_COLS + PR_COLS])
        pf_ref[...] = _dot(h, wb_scr[:, PA_COLS + PR_COLS:IN_COLS])
        pr_ref[:, 0:w] = _silu(pr[:, 0:w])
        for d in range(2):
            z = pr[:, (1 + d) * w:(2 + d) * w]
            e = jnp.exp(-jnp.abs(z))
            r = 1.0 / (1.0 + e)
            pos = z >= 0.0
            one_m_lb = lbp_ref[2, d:d + 1, :]
            f = lbp_ref[0, d:d + 1, :] + one_m_lb * (jnp.where(pos, 1.0, e) * r)
            log_f = jnp.where(f > 0.0, jnp.log(f), lbp_ref[1, d:d + 1, :] + z)
            g_hi, g_lo = _split2(log_f)
            pg_ref[:, 2 * d * w:(2 * d + 1) * w] = g_hi
            pg_ref[:, (2 * d + 1) * w:(2 * d + 2) * w] = g_lo
            pr_ref[:, (1 + d) * w:(2 + d) * w] = one_m_lb * (jnp.where(pos, e, 1.0) * r)
        pr_ref[:, 3 * w:4 * w] = pr[:, 3 * w:4 * w]
        pr_ref[:, 4 * w:5 * w] = _silu(pr[:, 4 * w:5 * w])
        pa = _dot(h, wb_scr[:, 0:PA_COLS])
        pa_ref[...] = pa
        if write_cache:
            seqs, _, _, n = ko_ref.shape
            for s in range(seqs):
                rows = slice(s * n, (s + 1) * n)
                ko_ref[s] = pa[rows, A_WIDTH:2 * A_WIDTH].T.reshape(A_HEADS, HEAD_DIM, n)
                vo_ref[s] = pa[rows, 2 * A_WIDTH:3 * A_WIDTH].T.reshape(A_HEADS, HEAD_DIM, n)

    @pl.when(i < ctx_tiles)
    def _():
        project(xc_ref, True)

    @pl.when(i >= ctx_tiles)
    def _():
        project(xl_ref, False)


def _in_projection(x_ctx, x_lat, mod, g_pre, lbp, w_in, layer, cache):
    bc, nc_, _ = x_ctx.shape
    bl, nl, _ = x_lat.shape
    tm = IN_TILE
    seqs_per_tile = tm // nc_
    lat_tiles_per_seq = nl // tm
    ctx_tiles = bc * nc_ // tm
    lat_tiles = bl * nl // tm
    tokens = (ctx_tiles + lat_tiles) * tm
    last_ctx = ctx_tiles - 1
    ctx_idx = lambda i: (jnp.minimum(i, last_ctx), 0)
    lat_idx = lambda i: (jnp.maximum(i - ctx_tiles, 0), 0)
    mod_idx = lambda i: (jnp.where(i < ctx_tiles, 0, 1 + jnp.maximum(i - ctx_tiles, 0) // lat_tiles_per_seq), 0, 0)
    tok = lambda i: (i, 0)
    cache_spec = pl.BlockSpec((seqs_per_tile, None, A_HEADS, HEAD_DIM, nc_),
                              lambda i: (jnp.minimum(i, last_ctx), layer, 0, 0, 0))
    cache_shape = jax.ShapeDtypeStruct(cache[0].shape, F32)
    return pl.pallas_call(
        functools.partial(_inproj_kernel, ctx_tiles=ctx_tiles),
        out_shape=(
            jax.ShapeDtypeStruct((tokens, PA_COLS), F32),
            jax.ShapeDtypeStruct((tokens, PR_COLS), F32),
            jax.ShapeDtypeStruct((tokens, PG_COLS), BF16),
            jax.ShapeDtypeStruct((tokens, PF_COLS), F32),
            cache_shape, cache_shape,
        ),
        grid=(ctx_tiles + lat_tiles,),
        in_specs=[
            pl.BlockSpec((tm, D_MODEL), ctx_idx),
            pl.BlockSpec((tm, D_MODEL), lat_idx),
            pl.BlockSpec((None, 3, D_MODEL), mod_idx),
            pl.BlockSpec((None, 1, D_MODEL), lambda i: (layer, 0, 0)),
            pl.BlockSpec((3, None, 2, R_WIDTH), lambda i: (0, layer, 0, 0)),
            pl.BlockSpec((None, D_MODEL, IN_COLS), lambda i: (layer, 0, 0), pipeline_mode=pl.Buffered(1)),
            pl.BlockSpec(memory_space=pl.ANY),
            pl.BlockSpec(memory_space=pl.ANY),
        ],
        out_specs=(
            pl.BlockSpec((tm, PA_COLS), tok),
            pl.BlockSpec((tm, PR_COLS), tok),
            pl.BlockSpec((tm, PG_COLS), tok),
            pl.BlockSpec((tm, PF_COLS), tok),
            cache_spec, cache_spec,
        ),
        scratch_shapes=[pltpu.VMEM((D_MODEL, IN_COLS), BF16)],
        input_output_aliases={6: 4, 7: 5},
        compiler_params=_cparams(("arbitrary",)),
        name="in_projection",
    )(x_ctx.reshape(bc * nc_, D_MODEL), x_lat.reshape(bl * nl, D_MODEL), mod,
      g_pre.reshape(DEPTH, 1, D_MODEL), lbp, w_in, cache[0], cache[1])


LOG2E = 1.4426950408889634
Q_SCALE = HEAD_DIM ** -0.5 * LOG2E


def _lane_lo():
    return lax.broadcasted_iota(jnp.int32, (1, LANES), 1) < HEAD_DIM


def _ctx_attn_kernel(pas_ref, os_ref, s_scr):
    for s in range(pas_ref.shape[0]):
        _ctx_attn_sequence(pas_ref.at[s], os_ref.at[s], s_scr)


def _ctx_attn_sequence(pa_ref, o_ref, s_scr):
    lo = _lane_lo()
    npair = A_HEADS // 2

    def scores(p):
        c = p * LANES
        q2 = pa_ref[:, c:c + LANES] * Q_SCALE
        k2 = pa_ref[:, A_WIDTH + c:A_WIDTH + c + LANES].astype(BF16)
        for hh in range(2):
            sel = lo if hh == 0 else jnp.logical_not(lo)
            s_scr[p % 2, hh] = _dot_nt(jnp.where(sel, q2, 0.0).astype(BF16), k2)

    def finish(p):
        c = p * LANES
        v2 = pa_ref[:, 2 * A_WIDTH + c:2 * A_WIDTH + c + LANES].astype(BF16)
        outs = []
        for hh in range(2):
            s = s_scr[p % 2, hh]
            e = jnp.exp2(s - jnp.max(s, axis=-1, keepdims=True))
            inv = 1.0 / jnp.sum(e, axis=-1, keepdims=True)
            outs.append(_dot(e.astype(BF16), v2) * inv)
        o2 = jnp.where(lo, outs[0], outs[1])
        o_ref[:, c:c + LANES] = (o2 * _silu(pa_ref[:, 3 * A_WIDTH + c:3 * A_WIDTH + c + LANES])).astype(BF16)

    scores(0)
    for p in range(npair):
        if p + 1 < npair:
            scores(p + 1)
        finish(p)


def _context_attention(pa, seqs):
    b0, b = seqs
    n = pa.shape[1]
    g = 2 if b % 2 == 0 and b0 % 2 == 0 else 1
    return pl.pallas_call(
        _ctx_attn_kernel,
        out_shape=jax.ShapeDtypeStruct((b, n, A_WIDTH), BF16),
        grid=(b // g,),
        in_specs=[pl.BlockSpec((g, n, PA_COLS), lambda i: (b0 // g + i, 0, 0))],
        out_specs=pl.BlockSpec((g, n, A_WIDTH), lambda i: (i, 0, 0)),
        scratch_shapes=[pltpu.VMEM((2, 2, n, n), F32)],
        compiler_params=_cparams(("arbitrary",)),
        name="context_attention",
    )(pa)


QROWS = 4


def _nbr_blocks(rows):
    kh = min(WIN_H, rows)
    out = []
    for r_first in range(0, rows, QROWS):
        r0s = [min(max(r - kh // 2, 0), rows - kh) for r in range(r_first, r_first + QROWS)]
        lo, hi = min(r0s), max(r0s) + kh
        lo -= lo % 2
        span = hi - lo
        span += (-span) % 4
        if lo + span > rows:
            lo = rows - span
        assert lo >= 0 and lo % 2 == 0
        out.append((r_first, lo, span, r0s))
    return out, kh


N_REL_ROWS = 2 * WIN_H - 1
TABLE_ROWS = WIN_H * GRID_W


def _build_bias_tables(base_ref, ta_scr, tb_scr, heads):
    lane = lax.broadcasted_iota(jnp.int32, (GRID_W, LANES), 1)
    kc = lax.broadcasted_iota(jnp.int32, (GRID_W, LANES), 0)
    qc = lane % GRID_W
    lo = lane < GRID_W
    c0 = jnp.clip(qc - WIN_W // 2, 0, GRID_W - WIN_W)
    col_in = (kc >= c0) & (kc < c0 + WIN_W)

    for h in heads:
        def tile(i, lane_off):
            if not 0 <= i < N_REL_ROWS:
                return jnp.full((GRID_W, LANES), NEG_INF, F32)
            row = jnp.broadcast_to(base_ref[h, i:i + 1, :] * LOG2E, (GRID_W, LANES))
            return pltpu.roll(row, lane_off, 1, stride=1, stride_axis=0)

        for j in range(WIN_H):
            rows = slice(j * GRID_W, (j + 1) * GRID_W)
            ta_scr[h, rows, :] = jnp.where(col_in, jnp.where(lo, tile(2 * j + 1, 0), tile(2 * j, GRID_W)), NEG_INF)
            tb_scr[h, rows, :] = jnp.where(col_in, jnp.where(lo, tile(2 * j, 0), tile(2 * j - 1, GRID_W)), NEG_INF)


def _nbr_attn_kernel(q_ref, k_ref, v_ref, g_ref, kc_ref, vc_ref, base_ref, o_ref, s_scr, p_scr, ta_scr, tb_scr, *, rows):
    p = pl.program_id(1)

    @pl.when(pl.program_id(0) == 0)
    def _():
        _build_bias_tables(base_ref, ta_scr, tb_scr, (2 * p, 2 * p + 1))

    lo = _lane_lo()
    hi = jnp.logical_not(lo)
    blocks, kh = _nbr_blocks(rows)
    nq = QROWS * GRID_W
    past = kc_ref.shape[-1]
    ctx0 = s_scr.shape[2] - past
    kc_rows = kc_ref[...].reshape(LANES, past).T.astype(BF16)
    vct = vc_ref[...].reshape(LANES, past).astype(BF16)
    head_rows = [slice(hh * HEAD_DIM, (hh + 1) * HEAD_DIM) for hh in range(2)]
    ctx_tiles = [slice(ctx0 + t * GRID_W, ctx0 + (t + 1) * GRID_W) for t in range(past // GRID_W)]

    def window_tiles(bi, c, head):
        r_first, u0, span, r0s = blocks[bi]
        out = []
        for t in range(span):
            kr = u0 + t
            rs = slice(t * GRID_W, (t + 1) * GRID_W)
            ok = [r0s[2 * c + e] <= kr < r0s[2 * c + e] + kh for e in range(2)]
            if not (ok[0] or ok[1]):
                out.append((rs, None))
                continue
            i = kr - (r_first + 2 * c) + WIN_H - 1
            assert 0 <= i <= N_REL_ROWS
            tab, j = (ta_scr, (i - 1) // 2) if i % 2 else (tb_scr, i // 2)
            bias = tab[head, j * GRID_W:(j + 1) * GRID_W, :]
            if not ok[0]:
                bias = jnp.where(lo, NEG_INF, bias)
            if not ok[1]:
                bias = jnp.where(hi, NEG_INF, bias)
            out.append((rs, bias))
        return out

    def scores(bi):
        r_first, u0, span, _ = blocks[bi]
        qs, ks, nk = r_first * GRID_W, u0 * GRID_W, span * GRID_W
        q2 = q_ref[qs:qs + nq, :] * Q_SCALE
        ku = k_ref[ks:ks + nk, :].astype(BF16)
        for hh in range(2):
            qh = jnp.where(lo if hh == 0 else hi, q2, 0.0).astype(BF16)
            s_scr[bi % 2, hh, 0:nk, :] = _dot_nt(ku, qh)
            s_scr[bi % 2, hh, ctx0:ctx0 + past, :] = _dot_nt(kc_rows, qh)

    def fold_rows(x, op):
        parts = [x[r:r + 8] for r in range(0, GRID_W, 8)]
        while len(parts) > 1:
            parts = [op(parts[i], parts[i + 1]) for i in range(0, len(parts), 2)]
        return parts[0]

    def softmax(bi):
        par = bi % 2
        inv = []
        for hh in range(2):
            parts = []
            for c in range(QROWS // 2):
                cs = slice(c * LANES, (c + 1) * LANES)
                tiles = window_tiles(bi, c, 2 * p + hh)
                m_acc = None
                for rs, bias in tiles:
                    if bias is None:
                        continue
                    s = s_scr[par, hh, rs, cs] + bias
                    s_scr[par, hh, rs, cs] = s
                    t = fold_rows(s, jnp.maximum)
                    m_acc = t if m_acc is None else jnp.maximum(m_acc, t)
                for rs in ctx_tiles:
                    m_acc = jnp.maximum(m_acc, fold_rows(s_scr[par, hh, rs, cs], jnp.maximum))
                m = jnp.max(m_acc, axis=0, keepdims=True)
                l_acc = None
                for rs, bias in tiles:
                    if bias is None:
                        p_scr[par, hh, rs, cs] = jnp.zeros((GRID_W, LANES), BF16)
                        continue
                    e = jnp.exp2(s_scr[par, hh, rs, cs] - m)
                    p_scr[par, hh, rs, cs] = e.astype(BF16)
                    t = fold_rows(e, jnp.add)
                    l_acc = t if l_acc is None else l_acc + t
                for rs in ctx_tiles:
                    e = jnp.exp2(s_scr[par, hh, rs, cs] - m)
                    p_scr[par, hh, rs, cs] = e.astype(BF16)
                    l_acc = l_acc + fold_rows(e, jnp.add)
                parts.append(1.0 / jnp.sum(l_acc, axis=0, keepdims=True))
            inv.append(jnp.concatenate(parts, axis=1))
        return inv

    def values(bi, inv):
        r_first, u0, span, _ = blocks[bi]
        qs, ks, nk = r_first * GRID_W, u0 * GRID_W, span * GRID_W
        vut = v_ref[ks:ks + nk, :].T.astype(BF16)
        outs = []
        for hh in range(2):
            o = (_dot(vut[head_rows[hh]], p_scr[bi % 2, hh, 0:nk, :])
                 + _dot(vct[head_rows[hh]], p_scr[bi % 2, hh, ctx0:ctx0 + past, :]))
            outs.append(o * inv[hh])
        o2 = jnp.concatenate(outs, axis=0).T
        o_ref[qs:qs + nq, :] = (o2 * _silu(g_ref[qs:qs + nq, :])).astype(BF16)

    scores(0)
    for bi in range(len(blocks)):
        if bi + 1 < len(blocks):
            scores(bi + 1)
        values(bi, softmax(bi))


def _bias_base(rpb_l):
    mid = WIN_W - 1
    zeros = jnp.zeros(rpb_l.shape[:2] + (LANES - (2 * WIN_W - 1),), F32)
    rev = rpb_l[..., ::-1]
    base = jnp.concatenate([rev[..., mid:], zeros, rev[..., :mid]], axis=-1)
    return jnp.pad(base, ((0, 0), (0, 2 * WIN_H - N_REL_ROWS), (0, 0)))


def _neighbourhood_attention(pa, seqs, cache_kt, cache_vt, layer, rpb_l):
    b0, b = seqs
    n = pa.shape[1]
    rows = n // GRID_W
    past = cache_kt.shape[-1]
    blocks, _ = _nbr_blocks(rows)
    max_nk = max(s for (_, _, s, _) in blocks) * GRID_W
    npair = A_HEADS // 2
    col = lambda off: (lambda i, p: (b0 + i, 0, off + p))
    cache_spec = pl.BlockSpec((None, None, 2, HEAD_DIM, past), lambda i, p: (i, layer, p, 0, 0))
    return pl.pallas_call(
        functools.partial(_nbr_attn_kernel, rows=rows),
        out_shape=jax.ShapeDtypeStruct((b, n, A_WIDTH), BF16),
        grid=(b, npair),
        in_specs=[
            pl.BlockSpec((None, n, LANES), col(0)),
            pl.BlockSpec((None, n, LANES), col(npair)),
            pl.BlockSpec((None, n, LANES), col(2 * npair)),
            pl.BlockSpec((None, n, LANES), col(3 * npair)),
            cache_spec,
            cache_spec,
            pl.BlockSpec((A_HEADS, 2 * WIN_H, LANES), lambda i, p: (0, 0, 0)),
        ],
        out_specs=pl.BlockSpec((None, n, LANES), lambda i, p: (i, 0, p)),
        scratch_shapes=[
            pltpu.VMEM((2, 2, max_nk + past, QROWS * GRID_W), F32),
            pltpu.VMEM((2, 2, max_nk + past, QROWS * GRID_W), BF16),
            pltpu.VMEM((A_HEADS, TABLE_ROWS, LANES), F32),
            pltpu.VMEM((A_HEADS, TABLE_ROWS, LANES), F32),
        ],
        compiler_params=_cparams(("arbitrary", "arbitrary")),
        name="neighbourhood_attention",
    )(pa, pa, pa, pa, cache_kt, cache_vt, _bias_base(rpb_l))


COARSE_HALVES = (32, 16, 8)
FINE_HALVES = (4, 2, 1)
N_LEVELS = len(COARSE_HALVES) + len(FINE_HALVES)
ANCHOR_BLOCK = CHUNK
ANCHOR_LEVELS = 0
ANCHOR_MAX_EXPONENT = 80.0
MASK_DIAG = N_LEVELS
MASK_ANCHOR = N_LEVELS + 1
GROUP = 4


def _hgrn_constants():
    c = CHUNK
    idx = np.arange(c)
    mats = [np.tril(np.ones((c, c)))]
    masks = []
    for h in COARSE_HALVES + FINE_HALVES:
        blk = idx // (2 * h)
        mid = blk * 2 * h + h - 1
        upper = idx > mid
        if h in FINE_HALVES:
            m = np.zeros((c, c))
            for i in range(c):
                if upper[i]:
                    m[i, mid[i] + 1:i + 1] = 1.0
                else:
                    m[i, i + 1:mid[i] + 1] = 1.0
            mats.append(m)
        same = blk[:, None] == blk[None, :]
        masks.append((same & upper[:, None] & (~upper)[None, :]).astype(np.float64))
    masks.append(np.eye(c))
    same_block = (idx[:, None] // ANCHOR_BLOCK) == (idx[None, :] // ANCHOR_BLOCK)
    masks.append((same_block & (idx[None, :] <= idx[:, None])).astype(np.float64))
    fwd = np.concatenate(mats, axis=0)
    bwd = np.concatenate([m[::-1, ::-1] for m in mats], axis=0)
    mk_f = np.stack([np.tile(m, (1, R_HEADS)) for m in masks])
    mk_b = np.stack([np.tile(m[::-1, ::-1], (1, R_HEADS)) for m in masks])
    hid = np.arange(R_WIDTH) // R_DIM
    bd = (hid[:, None] == hid[None, :]).astype(np.float64)
    return (jnp.asarray(np.stack([fwd, bwd]), BF16), jnp.asarray(np.stack([mk_f, mk_b]), F32),
            jnp.asarray(bd, BF16))


def _hgrn_kernel(*refs, nc, has_init):
    prs_ref, pgs_ref, gh_ref, mstk_ref, lmask_ref, bd_ref = refs[:6]
    if has_init:
        s0s_ref, os_ref = refs[6:8]
        sfins_ref = None
        scratch = refs[8:]
    else:
        s0s_ref = None
        os_ref, sfins_ref = refs[7:9]
        scratch = refs[9:]
    for s in range(prs_ref.shape[0]):
        _hgrn_sequence(prs_ref.at[s], pgs_ref.at[s], gh_ref, mstk_ref, lmask_ref, bd_ref,
                       None if s0s_ref is None else s0s_ref.at[s], os_ref.at[s],
                       None if sfins_ref is None else sfins_ref.at[s], scratch, nc)


def _hgrn_sequence(pr_ref, pg_ref, gh_ref, mstk_ref, lmask_ref, bd_ref, s0_ref, o_ref, sfin_ref, scratch, nc):
    has_init = s0_ref is not None
    gc_scr, qg_scr, upd_scr, dec_scr, sbd_scr = scratch
    w = R_WIDTH
    c = CHUNK
    bd = bd_ref[...]
    hid = lax.broadcasted_iota(jnp.int32, (1, w), 1) // R_DIM
    q_at = lambda rs: pr_ref[rs, 0:w]
    k_at = lambda d, rs: pr_ref[rs, (1 + d) * w:(2 + d) * w]
    g_at = lambda d, rs: pg_ref[rs, 2 * d * w:2 * (d + 1) * w]

    def tile4(x):
        return jnp.concatenate([x] * R_HEADS, axis=0)

    def head_diagonal(full):
        out = full[(R_HEADS - 1) * R_DIM:R_HEADS * R_DIM]
        for h in range(R_HEADS - 2, -1, -1):
            out = jnp.where(hid == h, full[h * R_DIM:(h + 1) * R_DIM], out)
        return out

    def chunk_rows(ci):
        return pl.ds(pl.multiple_of(ci * c, c), c)

    def cumulative(d, rs, n_mats):
        s = _dot(mstk_ref[d, 0:n_mats * c, :], g_at(d, rs))
        return s[:, 0:w] + s[:, w:2 * w]

    def level_exponents(gcum, d, halves):
        out = []
        for h in halves:
            parts = []
            for s0 in range(0, c, 2 * h):
                anchor = s0 + h - 1 + d
                parts.append(-jnp.abs(gcum[s0:s0 + 2 * h] - gcum[anchor:anchor + 1]))
            out.append(parts[0] if len(parts) == 1 else jnp.concatenate(parts, axis=0))
        return out

    def anchor_shift(gcum, d):
        parts = []
        for s0 in range(0, c, ANCHOR_BLOCK):
            anchor = s0 + ANCHOR_BLOCK // 2 - 1 + d
            parts.append(gcum[s0:s0 + ANCHOR_BLOCK] - gcum[anchor:anchor + 1])
        return jnp.concatenate(parts, axis=0)

    groups = nc // GROUP
    pairs = [(j, d) for j in range(GROUP) for d in range(2)]

    def increment_group(gi, worst):
        rows = [chunk_rows(gi * GROUP + j) for j in range(GROUP)]
        gcums = [cumulative(d, rows[j], 1) for j, d in pairs]
        lasts = [g[c - 1:c, :] if d == 0 else g[0:1, :] for g, (j, d) in zip(gcums, pairs)]
        kls = [(k_at(d, rows[j]) * jnp.exp(last - g)).astype(BF16)
               for g, last, (j, d) in zip(gcums, lasts, pairs)]
        vs = [pr_ref[rows[j], 3 * w:4 * w].astype(BF16) for j in range(GROUP)]
        upds = [_dot_tn(vs[j], kl) for kl, (j, d) in zip(kls, pairs)]
        for g, last, upd, (j, d) in zip(gcums, lasts, upds, pairs):
            ci = gi * GROUP + j
            gc_scr[d, rows[j], :] = g
            upd_scr[d, ci] = head_diagonal(upd)
            dec_scr[d, ci] = jnp.broadcast_to(jnp.exp(last), (8, w))
            qg_scr[d, rows[j], :] = (q_at(rows[j]) * jnp.exp(g)).astype(BF16)
            sh = jnp.abs(anchor_shift(g, d))
            for s0 in range(0, c, 8):
                worst = jnp.maximum(worst, sh[s0:s0 + 8])
        return worst

    worst = lax.fori_loop(0, groups, increment_group, jnp.zeros((8, w), F32))
    q_max = jnp.max(jnp.max(jnp.abs(pr_ref[:, 0:w]), axis=0, keepdims=True), axis=1, keepdims=True)
    anchor_ok = jnp.max(worst + jnp.log(jnp.maximum(q_max, 1.0))) < ANCHOR_MAX_EXPONENT

    def scan_step(ci, carry):
        sf, sb = carry
        cb = nc - 1 - ci
        sbd_scr[0, ci] = sf.astype(BF16)
        sbd_scr[1, cb] = sb.astype(BF16)
        sf = sf * dec_scr[0, ci][0:1, :] + upd_scr[0, ci]
        sb = sb * dec_scr[1, cb][0:1, :] + upd_scr[1, cb]
        return sf, sb

    if has_init:
        init = tuple(jnp.concatenate([s0_ref[d], jnp.zeros((w, LANES - R_DIM), F32)], axis=1).T[0:R_DIM]
                     for d in range(2))
    else:
        init = (jnp.zeros((R_DIM, w), F32), jnp.zeros((R_DIM, w), F32))
    finals = lax.fori_loop(0, nc, scan_step, init)
    if sfin_ref is not None:
        for d in range(2):
            padded = jnp.concatenate([finals[d], jnp.zeros((LANES - R_DIM, w), F32)], axis=0)
            sfin_ref[d] = padded.T[:, 0:R_DIM]

    def finish(rs, o):
        x2_hi, x2_lo = _split2(o * o)
        ms = (_dot(x2_hi, bd) + _dot(x2_lo, bd)) * (1.0 / R_DIM)
        y = o * lax.rsqrt(ms + EPS) * gh_ref[...]
        o_ref[rs, :] = (y * pr_ref[rs, 4 * w:5 * w]).astype(BF16)

    def readout_group_anchor(gi, carry):
        rows = [chunk_rows(gi * GROUP + j) for j in range(GROUP)]
        qs = [q_at(rows[j]) for j in range(GROUP)]
        gcums = [gc_scr[d, rows[j], :] for j, d in pairs]
        factors = []
        for g, (j, d) in zip(gcums, pairs):
            es = [jnp.exp(x) for x in level_exponents(g, d, COARSE_HALVES[:ANCHOR_LEVELS])]
            sh = anchor_shift(g, d)
            fs = [(e, e, lv) for lv, e in enumerate(es)]
            fs.append((jnp.exp(sh), jnp.exp(-sh), MASK_ANCHOR))
            factors.append(fs)
        kbds = [tile4(k_at(d, rows[j]).astype(BF16)) * bd for j, d in pairs]
        accs = [None] * len(pairs)
        for lv in range(ANCHOR_LEVELS + 1):
            for i, (j, d) in enumerate(pairs):
                eq, ek, mask = factors[i][lv]
                part = jnp.where(lmask_ref[d, mask] > 0.5,
                                 _dot_nt((qs[j] * eq).astype(BF16), kbds[i] * tile4(ek.astype(BF16))), 0.0)
                accs[i] = part if accs[i] is None else accs[i] + part
        vbds = [tile4(pr_ref[rows[j], 3 * w:4 * w].astype(BF16)) * bd for j in range(GROUP)]
        outs = [None] * GROUP
        for i, (j, d) in enumerate(pairs):
            ci = gi * GROUP + j
            od = _dot(accs[i].astype(BF16), vbds[j]) + _dot_nt(qg_scr[d, rows[j], :], tile4(sbd_scr[d, ci]) * bd)
            outs[j] = od if outs[j] is None else outs[j] + od
        for j in range(GROUP):
            finish(rows[j], outs[j])
        return carry

    def readout_step_split(ci, carry):
        rs = chunk_rows(ci)
        q = q_at(rs)
        qb = q.astype(BF16)
        vbd = tile4(pr_ref[rs, 3 * w:4 * w].astype(BF16)) * bd
        dsts = [cumulative(d, rs, 1 + len(FINE_HALVES)) for d in range(2)]
        exps = [level_exponents(dsts[d][0:c], d, COARSE_HALVES)
                + [dsts[d][(1 + j) * c:(2 + j) * c] for j in range(len(FINE_HALVES))] for d in range(2)]
        kbd = [tile4(k_at(d, rs).astype(BF16)) * bd for d in range(2)]
        a = [_dot_nt(qb, kbd[d]) * lmask_ref[d, MASK_DIAG] for d in range(2)]
        for j in range(N_LEVELS):
            for d in range(2):
                e = jnp.exp(exps[d][j])
                a[d] = a[d] + _dot_nt((q * e).astype(BF16), kbd[d] * tile4(e.astype(BF16))) * lmask_ref[d, j]
        o = None
        for d in range(2):
            od = _dot(a[d].astype(BF16), vbd) + _dot_nt(qg_scr[d, rs, :], tile4(sbd_scr[d, ci]) * bd)
            o = od if o is None else o + od
        finish(rs, o)
        return carry

    @pl.when(anchor_ok)
    def _():
        lax.fori_loop(0, groups, readout_group_anchor, 0)

    @pl.when(jnp.logical_not(anchor_ok))
    def _():
        lax.fori_loop(0, nc, readout_step_split, 0)


def _hgrn(pr, pg, seqs, g_hgrn_l, consts, layer, state=None, new_state=None):
    b0, b = seqs
    n = pr.shape[1]
    nc = n // CHUNK
    mstk, lmask, bd = consts
    w = R_WIDTH
    has_init = state is not None
    g = max(1, min(b, IN_TILE // n))
    assert b % g == 0 and b0 % g == 0
    full = lambda *shape: pl.BlockSpec(shape, lambda i: (0,) * len(shape))
    in_specs = [
        pl.BlockSpec((g, n, PR_COLS), lambda i: (b0 // g + i, 0, 0)),
        pl.BlockSpec((g, n, PG_COLS), lambda i: (b0 // g + i, 0, 0)),
        full(1, w),
        full(*mstk.shape),
        full(*lmask.shape),
        full(w, w),
    ]
    args = [pr, pg, g_hgrn_l.reshape(1, w), mstk, lmask, bd]
    out_shape = [jax.ShapeDtypeStruct((b, n, w), BF16)]
    out_specs = [pl.BlockSpec((g, n, w), lambda i: (i, 0, 0))]
    state_spec = pl.BlockSpec((g, None, 2, w, R_DIM), lambda i: (i, layer, 0, 0, 0))
    aliases = {}
    if has_init:
        in_specs.append(state_spec)
        args.append(state)
    else:
        in_specs.append(pl.BlockSpec(memory_space=pl.ANY))
        args.append(new_state)
        out_shape.append(jax.ShapeDtypeStruct(new_state.shape, F32))
        out_specs.append(state_spec)
        aliases = {6: 1}
    return pl.pallas_call(
        functools.partial(_hgrn_kernel, nc=nc, has_init=has_init),
        out_shape=tuple(out_shape),
        grid=(b // g,),
        in_specs=in_specs,
        out_specs=tuple(out_specs),
        input_output_aliases=aliases,
        scratch_shapes=[
            pltpu.VMEM((2, n, w), F32),
            pltpu.VMEM((2, n, w), BF16),
            pltpu.VMEM((2, nc, R_DIM, w), F32),
            pltpu.VMEM((2, nc, 8, w), F32),
            pltpu.VMEM((2, nc, R_DIM, w), BF16),
        ],
        compiler_params=_cparams(("arbitrary",)),
        name="hgrn_scan",
    )(*args)


def _fnet_constants(n):
    j = np.arange(F_GROUP_DIM)
    ang = 2.0 * np.pi * ((j[:, None] * j[None, :]) % F_GROUP_DIM) / F_GROUP_DIM
    eye = np.eye(F_GROUPS)
    cs = np.concatenate([np.kron(eye, np.cos(ang)), np.kron(eye, np.sin(ang))], axis=1)
    t = np.arange(n)
    angn = 2.0 * np.pi * ((t[:, None] * t[None, :]) % n) / n
    return tuple(jnp.asarray(m, F32).astype(BF16) for m in (cs, np.cos(angn), np.sin(angn)))


def _fnet_kernel(pf_ref, cs_ref, cn_ref, sn_ref, wf_ref, o_ref, *, scale):
    w = F_WIDTH
    for s in range(pf_ref.shape[0]):
        t = _dot(pf_ref[s, :, 0:w].astype(BF16), cs_ref[...])
        y = (_dot(cn_ref[...], t[:, 0:w].astype(BF16)) - _dot(sn_ref[...], t[:, w:2 * w].astype(BF16))) * scale
        of = _dot(y.astype(BF16), wf_ref[...])
        o_ref[s] = (of * _silu(pf_ref[s, :, w:2 * w])).astype(BF16)


def _fourier(pf, seqs, consts, w_fnet_bf16):
    b0, b = seqs
    n = pf.shape[1]
    cs, cn, sn = consts
    w = F_WIDTH
    g = max(1, min(b, IN_TILE * 2 // n))
    assert b % g == 0 and b0 % g == 0
    full = lambda *shape: pl.BlockSpec(shape, lambda i: (0,) * len(shape))
    return pl.pallas_call(
        functools.partial(_fnet_kernel, scale=float((n * F_GROUP_DIM) ** -0.5)),
        out_shape=jax.ShapeDtypeStruct((b, n, w), BF16),
        grid=(b // g,),
        in_specs=[pl.BlockSpec((g, n, PF_COLS), lambda i: (b0 // g + i, 0, 0)),
                  full(w, 2 * w), full(n, n), full(n, n), full(w, w)],
        out_specs=pl.BlockSpec((g, n, w), lambda i: (i, 0, 0)),
        compiler_params=_cparams(("arbitrary",)),
        name="fourier_mixing",
    )(pf, cs, cn, sn, w_fnet_bf16)


def _outproj_kernel(mac_ref, mrc_ref, mfc_ref, mal_ref, mrl_ref, mfl_ref, xc_ref, xl_ref, mod_ref, g_ref, w_ref,
                    yc_ref, yl_ref, wb_scr, *, ctx_tiles):
    i = pl.program_id(0)

    @pl.when(i == 0)
    def _():
        wb_scr[...] = w_ref[...].astype(BF16)

    def project(ma_ref, mr_ref, mf_ref, x_ref, y_ref):
        out = (_dot(ma_ref[...], wb_scr[0:A_WIDTH, :])
               + _dot(mr_ref[...], wb_scr[A_WIDTH:A_WIDTH + R_WIDTH, :])
               + _dot(mf_ref[...], wb_scr[A_WIDTH + R_WIDTH:D_MODEL, :]))
        ms = jnp.mean(out * out, axis=-1, keepdims=True)
        y_ref[...] = x_ref[...] + mod_ref[2:3, :] * (out * lax.rsqrt(ms + EPS) * g_ref[...])

    @pl.when(i < ctx_tiles)
    def _():
        project(mac_ref, mrc_ref, mfc_ref, xc_ref, yc_ref)

    @pl.when(i >= ctx_tiles)
    def _():
        project(mal_ref, mrl_ref, mfl_ref, xl_ref, yl_ref)


def _out_projection(mixed_ctx, mixed_lat, x_ctx, x_lat, mod, g_post, w_out, layer):
    bc, nc_, _ = x_ctx.shape
    bl, nl, _ = x_lat.shape
    tm = IN_TILE
    lat_tiles_per_seq = nl // tm
    ctx_tiles = bc * nc_ // tm
    lat_tiles = bl * nl // tm
    last_ctx = ctx_tiles - 1
    ctx_idx = lambda i: (jnp.minimum(i, last_ctx), 0)
    lat_idx = lambda i: (jnp.maximum(i - ctx_tiles, 0), 0)
    mod_idx = lambda i: (jnp.where(i < ctx_tiles, 0, 1 + jnp.maximum(i - ctx_tiles, 0) // lat_tiles_per_seq), 0, 0)
    widths = (A_WIDTH, R_WIDTH, F_WIDTH)
    flat = lambda t: t.reshape(-1, t.shape[-1])
    yc, yl = pl.pallas_call(
        functools.partial(_outproj_kernel, ctx_tiles=ctx_tiles),
        out_shape=(jax.ShapeDtypeStruct((bc * nc_, D_MODEL), F32), jax.ShapeDtypeStruct((bl * nl, D_MODEL), F32)),
        grid=(ctx_tiles + lat_tiles,),
        in_specs=(
            [pl.BlockSpec((tm, wd), ctx_idx) for wd in widths]
            + [pl.BlockSpec((tm, wd), lat_idx) for wd in widths]
            + [
                pl.BlockSpec((tm, D_MODEL), ctx_idx),
                pl.BlockSpec((tm, D_MODEL), lat_idx),
                pl.BlockSpec((None, 3, D_MODEL), mod_idx),
                pl.BlockSpec((None, 1, D_MODEL), lambda i: (layer, 0, 0)),
                pl.BlockSpec((None, D_MODEL, D_MODEL), lambda i: (layer, 0, 0), pipeline_mode=pl.Buffered(1)),
            ]),
        out_specs=(pl.BlockSpec((tm, D_MODEL), ctx_idx), pl.BlockSpec((tm, D_MODEL), lat_idx)),
        scratch_shapes=[pltpu.VMEM((D_MODEL, D_MODEL), BF16)],
        compiler_params=_cparams(("arbitrary",)),
        name="out_projection",
    )(*[flat(t) for t in mixed_ctx], *[flat(t) for t in mixed_lat], flat(x_ctx), flat(x_lat), mod,
      g_post.reshape(DEPTH, 1, D_MODEL), w_out)
    return yc.reshape(x_ctx.shape), yl.reshape(x_lat.shape)


def kernel(x_prompt, x_sample, cache_attn_k, cache_attn_v, state_hgrn, c, c_ctx,
           w_ada, b_ada, g_pre, w_in, rpb, lb_logits, g_hgrn, w_fnet, w_out, g_post):
    nb_ctx, n_ctx, _ = x_prompt.shape
    nb_lat, n_lat, _ = x_sample.shape

    pad_rows = (-(1 + nb_lat)) % 8
    cc = jnp.concatenate([c_ctx[None, :], c, jnp.zeros((pad_rows, D_MODEL), F32)], axis=0)
    mods = _modulations(cc, w_ada, b_ada)
    lbp = _lower_bounds(lb_logits)

    w_fnet_b = w_fnet.astype(BF16)
    hconsts = _hgrn_constants()
    fconsts_ctx = _fnet_constants(n_ctx)
    fconsts_lat = _fnet_constants(n_lat)

    state_rows = state_hgrn.reshape(nb_lat, DEPTH, 2, R_WIDTH, R_DIM)
    cache_kt = jnp.transpose(cache_attn_k, (0, 1, 3, 4, 2))
    cache_vt = jnp.transpose(cache_attn_v, (0, 1, 3, 4, 2))

    yp, ys = x_prompt, x_sample
    new_kv = [jnp.zeros((nb_ctx, DEPTH, A_HEADS, HEAD_DIM, n_ctx), F32) for _ in range(2)]
    new_rows = jnp.zeros((nb_ctx, DEPTH, 2, R_WIDTH, R_DIM), F32)
    ctx_tokens = nb_ctx * n_ctx
    ctx_seqs = (0, nb_ctx)
    lat_seqs = (ctx_tokens // n_lat, nb_lat)
    for l in range(DEPTH):
        mod_all = mods[l, 0:1 + nb_lat].reshape(1 + nb_lat, 3, D_MODEL)
        pa, pr, pg, pf, *new_kv = _in_projection(yp, ys, mod_all, g_pre, lbp, w_in, l, new_kv)
        as_ctx = lambda t: t.reshape(-1, n_ctx, t.shape[-1])
        ma_c = _context_attention(as_ctx(pa), ctx_seqs)
        mr_c, new_rows = _hgrn(as_ctx(pr), as_ctx(pg), ctx_seqs, g_hgrn[l], hconsts, l, new_state=new_rows)
        mf_c = _fourier(as_ctx(pf), ctx_seqs, fconsts_ctx, w_fnet_b[l])
        as_lat = lambda t: t.reshape(-1, n_lat, t.shape[-1])
        ma_l = _neighbourhood_attention(as_lat(pa), lat_seqs, cache_kt, cache_vt, l, rpb[l])
        (mr_l,) = _hgrn(as_lat(pr), as_lat(pg), lat_seqs, g_hgrn[l], hconsts, l, state=state_rows)
        mf_l = _fourier(as_lat(pf), lat_seqs, fconsts_lat, w_fnet_b[l])
        yp, ys = _out_projection((ma_c, mr_c, mf_c), (ma_l, mr_l, mf_l), yp, ys, mod_all, g_post, w_out, l)

    new_state = new_rows.reshape(nb_ctx, DEPTH, 2, R_HEADS, R_DIM, R_DIM)
    new_k, new_v = (jnp.transpose(t, (0, 1, 4, 2, 3)) for t in new_kv)
    return (yp, ys, new_k, new_v, new_state)
```

```python
import functools

import numpy as np
import jax
import jax.numpy as jnp
from jax import lax
from jax.experimental import pallas as pl
from jax.experimental.pallas import tpu as pltpu

F32 = jnp.float32
BF16 = jnp.bfloat16

D_MODEL = 1024
DEPTH = 4
GRID_W = 64
WIN_H = 8
WIN_W = 16
HEAD_DIM = 64
A_HEADS = 8
A_WIDTH = A_HEADS * HEAD_DIM
R_HEADS = 4
R_DIM = 64
R_WIDTH = R_HEADS * R_DIM
F_GROUPS = 4
F_GROUP_DIM = 64
F_WIDTH = F_GROUPS * F_GROUP_DIM
PA_COLS = 4 * A_WIDTH
PR_COLS = 5 * R_WIDTH
PF_COLS = 2 * F_WIDTH
PG_COLS = 4 * R_WIDTH
IN_COLS = PA_COLS + PR_COLS + PF_COLS
CHUNK = 64
EPS = 1e-6
LANES = 128
NEG_INF = float("-inf")
VMEM_LIMIT = 56 * 1024 * 1024


def _cparams(sem):
    return pltpu.CompilerParams(dimension_semantics=sem, vmem_limit_bytes=VMEM_LIMIT)


def _silu(x):
    return x * (1.0 / (1.0 + jnp.exp(-x)))


def _dot(a, b):
    return jnp.dot(a, b, preferred_element_type=F32)


def _dot_nt(a, b):
    return lax.dot_general(a, b, (((1,), (1,)), ((), ())), preferred_element_type=F32)


def _dot_tn(a, b):
    return lax.dot_general(a, b, (((0,), (0,)), ((), ())), preferred_element_type=F32)


def _split2(x):
    hi = x.astype(BF16)
    lo = (x - hi.astype(F32)).astype(BF16)
    return hi, lo


def _mod_kernel(cc_ref, w_ref, b_ref, o_ref):
    a_hi, a_lo = _split2(_silu(cc_ref[...]))
    w_hi, w_lo = _split2(w_ref[...])
    acc = _dot(a_hi, w_hi) + _dot(a_hi, w_lo) + _dot(a_lo, w_hi)
    o_ref[...] = acc + b_ref[...]


def _modulations(cc, w_ada, b_ada):
    rows = cc.shape[0]
    tn = 3 * D_MODEL // 2
    return pl.pallas_call(
        _mod_kernel,
        out_shape=jax.ShapeDtypeStruct((DEPTH, rows, 3 * D_MODEL), F32),
        grid=(DEPTH, 3 * D_MODEL // tn),
        in_specs=[
            pl.BlockSpec((rows, D_MODEL), lambda l, j: (0, 0)),
            pl.BlockSpec((None, D_MODEL, tn), lambda l, j: (l, 0, j)),
            pl.BlockSpec((None, 1, tn), lambda l, j: (l, 0, j)),
        ],
        out_specs=pl.BlockSpec((None, rows, tn), lambda l, j: (l, 0, j)),
        compiler_params=_cparams(("arbitrary", "arbitrary")),
        name="adaln_mod",
    )(cc, w_ada, b_ada.reshape(DEPTH, 1, 3 * D_MODEL))


def _lb_kernel(x_ref, o_ref):
    xs = [x_ref[i] for i in range(DEPTH)]
    m = functools.reduce(jnp.maximum, xs)
    es = [jnp.exp(x - m) for x in xs]
    tot = functools.reduce(lambda a, b: a + b, es)
    cum = None
    first = None
    for i in range(DEPTH):
        p = es[i] / tot
        cum = p if cum is None else cum + p
        if first is None:
            first = cum
        lb = jnp.maximum(cum - first, 0.0)
        o_ref[0, i] = lb
        o_ref[1, i] = jnp.log1p(-lb)
        o_ref[2, i] = 1.0 - lb


def _lower_bounds(lb_logits):
    x = jnp.transpose(lb_logits, (1, 0, 2))
    return pl.pallas_call(
        _lb_kernel,
        out_shape=jax.ShapeDtypeStruct((3, DEPTH, 2, R_WIDTH), F32),
        name="hgrn_lower_bounds",
    )(x)


IN_TILE = 512


def _inproj_kernel(xc_ref, xl_ref, mod_ref, g_ref, lbp_ref, w_ref, kprev_ref, vprev_ref,
                   pa_ref, pr_ref, pg_ref, pf_ref, ko_ref, vo_ref, wb_scr, *, ctx_tiles):
    del kprev_ref, vprev_ref
    i = pl.program_id(0)

    @pl.when(i == 0)
    def _():
        wb_scr[...] = w_ref[...].astype(BF16)

    def project(x_ref, write_cache):
        w = R_WIDTH
        half = x_ref.shape[0] // 2

        def normalise(rows):
            x = x_ref[rows, :]
            ms = jnp.mean(x * x, axis=-1, keepdims=True)
            y = x * lax.rsqrt(ms + EPS) * g_ref[...]
            return (y * (1.0 + mod_ref[1:2, :]) + mod_ref[0:1, :]).astype(BF16)

        def matmuls(h):
            pr = _dot(h, wb_scr[:, PA_COLS:PA_COLS + PR_COLS])
            pf = _dot(h, wb_scr[:, PA_COLS + PR_COLS:IN_COLS])
            pa = _dot(h, wb_scr[:, 0:PA_COLS])
            return pr, pf, pa

        def finish(rows, pr, pf, pa):
            pf_ref[rows, :] = pf
            pa_ref[rows, :] = pa
            pr_ref[rows, 0:w] = _silu(pr[:, 0:w])
            for d in range(2):
                z = pr[:, (1 + d) * w:(2 + d) * w]
                e = jnp.exp(-jnp.abs(z))
                r = 1.0 / (1.0 + e)
                pos = z >= 0.0
                one_m_lb = lbp_ref[2, d:d + 1, :]
                f = lbp_ref[0, d:d + 1, :] + one_m_lb * (jnp.where(pos, 1.0, e) * r)
                log_f = jnp.where(f > 0.0, jnp.log(f), lbp_ref[1, d:d + 1, :] + z)
                g_hi, g_lo = _split2(log_f)
                pg_ref[rows, 2 * d * w:(2 * d + 1) * w] = g_hi
                pg_ref[rows, (2 * d + 1) * w:(2 * d + 2) * w] = g_lo
                pr_ref[rows, (1 + d) * w:(2 + d) * w] = one_m_lb * (jnp.where(pos, e, 1.0) * r)
            pr_ref[rows, 3 * w:4 * w] = pr[:, 3 * w:4 * w]
            pr_ref[rows, 4 * w:5 * w] = _silu(pr[:, 4 * w:5 * w])
            if write_cache:
                n = ko_ref.shape[-1]
                for s in range(rows.start // n, rows.stop // n):
                    local = slice(s * n - rows.start, (s + 1) * n - rows.start)
                    ko_ref[s] = pa[local, A_WIDTH:2 * A_WIDTH].T.reshape(A_HEADS, HEAD_DIM, n)
                    vo_ref[s] = pa[local, 2 * A_WIDTH:3 * A_WIDTH].T.reshape(A_HEADS, HEAD_DIM, n)

        rows = [slice(0, half), slice(half, 2 * half)]
        first = matmuls(normalise(rows[0]))
        second = matmuls(normalise(rows[1]))
        finish(rows[0], *first)
        finish(rows[1], *second)

    @pl.when(i < ctx_tiles)
    def _():
        project(xc_ref, True)

    @pl.when(i >= ctx_tiles)
    def _():
        project(xl_ref, False)


def _in_projection(x_ctx, x_lat, mod, g_pre, lbp, w_in, layer, cache):
    bc, nc_, _ = x_ctx.shape
    bl, nl, _ = x_lat.shape
    tm = IN_TILE
    seqs_per_tile = tm // nc_
    lat_tiles_per_seq = nl // tm
    ctx_tiles = bc * nc_ // tm
    lat_tiles = bl * nl // tm
    tokens = (ctx_tiles + lat_tiles) * tm
    last_ctx = ctx_tiles - 1
    ctx_idx = lambda i: (jnp.minimum(i, last_ctx), 0)
    lat_idx = lambda i: (jnp.maximum(i - ctx_tiles, 0), 0)
    mod_idx = lambda i: (jnp.where(i < ctx_tiles, 0, 1 + jnp.maximum(i - ctx_tiles, 0) // lat_tiles_per_seq), 0, 0)
    tok = lambda i: (i, 0)
    cache_spec = pl.BlockSpec((seqs_per_tile, None, A_HEADS, HEAD_DIM, nc_),
                              lambda i: (jnp.minimum(i, last_ctx), layer, 0, 0, 0))
    cache_shape = jax.ShapeDtypeStruct(cache[0].shape, F32)
    return pl.pallas_call(
        functools.partial(_inproj_kernel, ctx_tiles=ctx_tiles),
        out_shape=(
            jax.ShapeDtypeStruct((tokens, PA_COLS), F32),
            jax.ShapeDtypeStruct((tokens, PR_COLS), F32),
            jax.ShapeDtypeStruct((tokens, PG_COLS), BF16),
            jax.ShapeDtypeStruct((tokens, PF_COLS), F32),
            cache_shape, cache_shape,
        ),
        grid=(ctx_tiles + lat_tiles,),
        in_specs=[
            pl.BlockSpec((tm, D_MODEL), ctx_idx),
            pl.BlockSpec((tm, D_MODEL), lat_idx),
            pl.BlockSpec((None, 3, D_MODEL), mod_idx),
            pl.BlockSpec((None, 1, D_MODEL), lambda i: (layer, 0, 0)),
            pl.BlockSpec((3, None, 2, R_WIDTH), lambda i: (0, layer, 0, 0)),
            pl.BlockSpec((None, D_MODEL, IN_COLS), lambda i: (layer, 0, 0), pipeline_mode=pl.Buffered(1)),
            pl.BlockSpec(memory_space=pl.ANY),
            pl.BlockSpec(memory_space=pl.ANY),
        ],
        out_specs=(
            pl.BlockSpec((tm, PA_COLS), tok),
            pl.BlockSpec((tm, PR_COLS), tok),
            pl.BlockSpec((tm, PG_COLS), tok),
            pl.BlockSpec((tm, PF_COLS), tok),
            cache_spec, cache_spec,
        ),
        scratch_shapes=[pltpu.VMEM((D_MODEL, IN_COLS), BF16)],
        input_output_aliases={6: 4, 7: 5},
        compiler_params=_cparams(("arbitrary",)),
        name="in_projection",
    )(x_ctx.reshape(bc * nc_, D_MODEL), x_lat.reshape(bl * nl, D_MODEL), mod,
      g_pre.reshape(DEPTH, 1, D_MODEL), lbp, w_in, cache[0], cache[1])


LOG2E = 1.4426950408889634
Q_SCALE = HEAD_DIM ** -0.5 * LOG2E


def _lane_lo():
    return lax.broadcasted_iota(jnp.int32, (1, LANES), 1) < HEAD_DIM


def _ctx_attn_kernel(pas_ref, os_ref, s_scr):
    for s in range(pas_ref.shape[0]):
        _ctx_attn_sequence(pas_ref.at[s], os_ref.at[s], s_scr)


def _ctx_attn_sequence(pa_ref, o_ref, s_scr):
    lo = _lane_lo()
    npair = A_HEADS // 2

    def scores(p):
        c = p * LANES
        q2 = pa_ref[:, c:c + LANES] * Q_SCALE
        k2 = pa_ref[:, A_WIDTH + c:A_WIDTH + c + LANES].astype(BF16)
        for hh in range(2):
            sel = lo if hh == 0 else jnp.logical_not(lo)
            s_scr[p % 2, hh] = _dot_nt(jnp.where(sel, q2, 0.0).astype(BF16), k2)

    def finish(p):
        c = p * LANES
        v2 = pa_ref[:, 2 * A_WIDTH + c:2 * A_WIDTH + c + LANES].astype(BF16)
        outs = []
        for hh in range(2):
            s = s_scr[p % 2, hh]
            e = jnp.exp2(s - jnp.max(s, axis=-1, keepdims=True))
            inv = 1.0 / jnp.sum(e, axis=-1, keepdims=True)
            outs.append(_dot(e.astype(BF16), v2) * inv)
        o2 = jnp.where(lo, outs[0], outs[1])
        o_ref[:, c:c + LANES] = (o2 * _silu(pa_ref[:, 3 * A_WIDTH + c:3 * A_WIDTH + c + LANES])).astype(BF16)

    scores(0)
    for p in range(npair):
        if p + 1 < npair:
            scores(p + 1)
        finish(p)


def _context_attention(pa, seqs):
    b0, b = seqs
    n = pa.shape[1]
    g = 2 if b % 2 == 0 and b0 % 2 == 0 else 1
    return pl.pallas_call(
        _ctx_attn_kernel,
        out_shape=jax.ShapeDtypeStruct((b, n, A_WIDTH), BF16),
        grid=(b // g,),
        in_specs=[pl.BlockSpec((g, n, PA_COLS), lambda i: (b0 // g + i, 0, 0))],
        out_specs=pl.BlockSpec((g, n, A_WIDTH), lambda i: (i, 0, 0)),
        scratch_shapes=[pltpu.VMEM((2, 2, n, n), F32)],
        compiler_params=_cparams(("arbitrary",)),
        name="context_attention",
    )(pa)


QROWS = 4


def _nbr_blocks(rows):
    kh = min(WIN_H, rows)
    out = []
    for r_first in range(0, rows, QROWS):
        r0s = [min(max(r - kh // 2, 0), rows - kh) for r in range(r_first, r_first + QROWS)]
        lo, hi = min(r0s), max(r0s) + kh
        lo -= lo % 2
        span = hi - lo
        span += (-span) % 4
        if lo + span > rows:
            lo = rows - span
        assert lo >= 0 and lo % 2 == 0
        out.append((r_first, lo, span, r0s))
    return out, kh


N_REL_ROWS = 2 * WIN_H - 1
TABLE_ROWS = WIN_H * GRID_W


def _build_bias_tables(base_ref, ta_scr, tb_scr, heads):
    lane = lax.broadcasted_iota(jnp.int32, (GRID_W, LANES), 1)
    kc = lax.broadcasted_iota(jnp.int32, (GRID_W, LANES), 0)
    qc = lane % GRID_W
    lo = lane < GRID_W
    c0 = jnp.clip(qc - WIN_W // 2, 0, GRID_W - WIN_W)
    col_in = (kc >= c0) & (kc < c0 + WIN_W)

    for h in heads:
        def tile(i, lane_off):
            if not 0 <= i < N_REL_ROWS:
                return jnp.full((GRID_W, LANES), NEG_INF, F32)
            row = jnp.broadcast_to(base_ref[h, i:i + 1, :] * LOG2E, (GRID_W, LANES))
            return pltpu.roll(row, lane_off, 1, stride=1, stride_axis=0)

        for j in range(WIN_H):
            rows = slice(j * GRID_W, (j + 1) * GRID_W)
            ta_scr[h, rows, :] = jnp.where(col_in, jnp.where(lo, tile(2 * j + 1, 0), tile(2 * j, GRID_W)), NEG_INF)
            tb_scr[h, rows, :] = jnp.where(col_in, jnp.where(lo, tile(2 * j, 0), tile(2 * j - 1, GRID_W)), NEG_INF)


def _nbr_attn_kernel(q_ref, k_ref, v_ref, g_ref, kc_ref, vc_ref, base_ref, o_ref, s_scr, p_scr, ta_scr, tb_scr, *, rows):
    p = pl.program_id(1)

    @pl.when(pl.program_id(0) == 0)
    def _():
        _build_bias_tables(base_ref, ta_scr, tb_scr, (2 * p, 2 * p + 1))

    lo = _lane_lo()
    hi = jnp.logical_not(lo)
    blocks, kh = _nbr_blocks(rows)
    nq = QROWS * GRID_W
    past = kc_ref.shape[-1]
    ctx0 = s_scr.shape[2] - past
    kc_rows = kc_ref[...].reshape(LANES, past).T.astype(BF16)
    vct = vc_ref[...].reshape(LANES, past).astype(BF16)
    head_rows = [slice(hh * HEAD_DIM, (hh + 1) * HEAD_DIM) for hh in range(2)]
    ctx_tiles = [slice(ctx0 + t * GRID_W, ctx0 + (t + 1) * GRID_W) for t in range(past // GRID_W)]

    def window_tiles(bi, c, head):
        r_first, u0, span, r0s = blocks[bi]
        out = []
        for t in range(span):
            kr = u0 + t
            rs = slice(t * GRID_W, (t + 1) * GRID_W)
            ok = [r0s[2 * c + e] <= kr < r0s[2 * c + e] + kh for e in range(2)]
            if not (ok[0] or ok[1]):
                out.append((rs, None))
                continue
            i = kr - (r_first + 2 * c) + WIN_H - 1
            assert 0 <= i <= N_REL_ROWS
            tab, j = (ta_scr, (i - 1) // 2) if i % 2 else (tb_scr, i // 2)
            bias = tab[head, j * GRID_W:(j + 1) * GRID_W, :]
            if not ok[0]:
                bias = jnp.where(lo, NEG_INF, bias)
            if not ok[1]:
                bias = jnp.where(hi, NEG_INF, bias)
            out.append((rs, bias))
        return out

    def scores(bi):
        r_first, u0, span, _ = blocks[bi]
        qs, ks, nk = r_first * GRID_W, u0 * GRID_W, span * GRID_W
        q2 = q_ref[qs:qs + nq, :] * Q_SCALE
        ku = k_ref[ks:ks + nk, :].astype(BF16)
        for hh in range(2):
            qh = jnp.where(lo if hh == 0 else hi, q2, 0.0).astype(BF16)
            s_scr[bi % 2, hh, 0:nk, :] = _dot_nt(ku, qh)
            s_scr[bi % 2, hh, ctx0:ctx0 + past, :] = _dot_nt(kc_rows, qh)

    def fold_rows(x, op):
        parts = [x[r:r + 8] for r in range(0, GRID_W, 8)]
        while len(parts) > 1:
            parts = [op(parts[i], parts[i + 1]) for i in range(0, len(parts), 2)]
        return parts[0]

    def softmax(bi):
        par = bi % 2
        inv = []
        for hh in range(2):
            parts = []
            for c in range(QROWS // 2):
                cs = slice(c * LANES, (c + 1) * LANES)
                tiles = window_tiles(bi, c, 2 * p + hh)
                m_acc = None
                for rs, bias in tiles:
                    if bias is None:
                        continue
                    s = s_scr[par, hh, rs, cs] + bias
                    s_scr[par, hh, rs, cs] = s
                    t = fold_rows(s, jnp.maximum)
                    m_acc = t if m_acc is None else jnp.maximum(m_acc, t)
                for rs in ctx_tiles:
                    m_acc = jnp.maximum(m_acc, fold_rows(s_scr[par, hh, rs, cs], jnp.maximum))
                m = jnp.max(m_acc, axis=0, keepdims=True)
                l_acc = None
                for rs, bias in tiles:
                    if bias is None:
                        p_scr[par, hh, rs, cs] = jnp.zeros((GRID_W, LANES), BF16)
                        continue
                    e = jnp.exp2(s_scr[par, hh, rs, cs] - m)
                    p_scr[par, hh, rs, cs] = e.astype(BF16)
                    t = fold_rows(e, jnp.add)
                    l_acc = t if l_acc is None else l_acc + t
                for rs in ctx_tiles:
                    e = jnp.exp2(s_scr[par, hh, rs, cs] - m)
                    p_scr[par, hh, rs, cs] = e.astype(BF16)
                    l_acc = l_acc + fold_rows(e, jnp.add)
                parts.append(1.0 / jnp.sum(l_acc, axis=0, keepdims=True))
            inv.append(jnp.concatenate(parts, axis=1))
        return inv

    def values(bi, inv):
        r_first, u0, span, _ = blocks[bi]
        qs, ks, nk = r_first * GRID_W, u0 * GRID_W, span * GRID_W
        vut = v_ref[ks:ks + nk, :].T.astype(BF16)
        outs = []
        for hh in range(2):
            o = (_dot(vut[head_rows[hh]], p_scr[bi % 2, hh, 0:nk, :])
                 + _dot(vct[head_rows[hh]], p_scr[bi % 2, hh, ctx0:ctx0 + past, :]))
            outs.append(o * inv[hh])
        o2 = jnp.concatenate(outs, axis=0).T
        o_ref[qs:qs + nq, :] = (o2 * _silu(g_ref[qs:qs + nq, :])).astype(BF16)

    scores(0)
    for bi in range(len(blocks)):
        if bi + 1 < len(blocks):
            scores(bi + 1)
        values(bi, softmax(bi))


def _bias_base(rpb_l):
    mid = WIN_W - 1
    zeros = jnp.zeros(rpb_l.shape[:2] + (LANES - (2 * WIN_W - 1),), F32)
    rev = rpb_l[..., ::-1]
    base = jnp.concatenate([rev[..., mid:], zeros, rev[..., :mid]], axis=-1)
    return jnp.pad(base, ((0, 0), (0, 2 * WIN_H - N_REL_ROWS), (0, 0)))


def _neighbourhood_attention(pa, seqs, cache_kt, cache_vt, layer, rpb_l):
    b0, b = seqs
    n = pa.shape[1]
    rows = n // GRID_W
    past = cache_kt.shape[-1]
    blocks, _ = _nbr_blocks(rows)
    max_nk = max(s for (_, _, s, _) in blocks) * GRID_W
    npair = A_HEADS // 2
    col = lambda off: (lambda i, p: (b0 + i, 0, off + p))
    cache_spec = pl.BlockSpec((None, None, 2, HEAD_DIM, past), lambda i, p: (i, layer, p, 0, 0))
    return pl.pallas_call(
        functools.partial(_nbr_attn_kernel, rows=rows),
        out_shape=jax.ShapeDtypeStruct((b, n, A_WIDTH), BF16),
        grid=(b, npair),
        in_specs=[
            pl.BlockSpec((None, n, LANES), col(0)),
            pl.BlockSpec((None, n, LANES), col(npair)),
            pl.BlockSpec((None, n, LANES), col(2 * npair)),
            pl.BlockSpec((None, n, LANES), col(3 * npair)),
            cache_spec,
            cache_spec,
            pl.BlockSpec((A_HEADS, 2 * WIN_H, LANES), lambda i, p: (0, 0, 0)),
        ],
        out_specs=pl.BlockSpec((None, n, LANES), lambda i, p: (i, 0, p)),
        scratch_shapes=[
            pltpu.VMEM((2, 2, max_nk + past, QROWS * GRID_W), F32),
            pltpu.VMEM((2, 2, max_nk + past, QROWS * GRID_W), BF16),
            pltpu.VMEM((A_HEADS, TABLE_ROWS, LANES), F32),
            pltpu.VMEM((A_HEADS, TABLE_ROWS, LANES), F32),
        ],
        compiler_params=_cparams(("arbitrary", "arbitrary")),
        name="neighbourhood_attention",
    )(pa, pa, pa, pa, cache_kt, cache_vt, _bias_base(rpb_l))


COARSE_HALVES = (32, 16, 8)
FINE_HALVES = (4, 2, 1)
N_LEVELS = len(COARSE_HALVES) + len(FINE_HALVES)
ANCHOR_BLOCK = CHUNK
ANCHOR_LEVELS = 0
ANCHOR_MAX_EXPONENT = 80.0
MASK_DIAG = N_LEVELS
MASK_ANCHOR = N_LEVELS + 1
GROUP = 4


def _hgrn_constants():
    c = CHUNK
    idx = np.arange(c)
    mats = [np.tril(np.ones((c, c)))]
    masks = []
    for h in COARSE_HALVES + FINE_HALVES:
        blk = idx // (2 * h)
        mid = blk * 2 * h + h - 1
        upper = idx > mid
        if h in FINE_HALVES:
            m = np.zeros((c, c))
            for i in range(c):
                if upper[i]:
                    m[i, mid[i] + 1:i + 1] = 1.0
                else:
                    m[i, i + 1:mid[i] + 1] = 1.0
            mats.append(m)
        same = blk[:, None] == blk[None, :]
        masks.append((same & upper[:, None] & (~upper)[None, :]).astype(np.float64))
    masks.append(np.eye(c))
    same_block = (idx[:, None] // ANCHOR_BLOCK) == (idx[None, :] // ANCHOR_BLOCK)
    masks.append((same_block & (idx[None, :] <= idx[:, None])).astype(np.float64))
    fwd = np.concatenate(mats, axis=0)
    bwd = np.concatenate([m[::-1, ::-1] for m in mats], axis=0)
    mk_f = np.stack([np.tile(m, (1, R_HEADS)) for m in masks])
    mk_b = np.stack([np.tile(m[::-1, ::-1], (1, R_HEADS)) for m in masks])
    hid = np.arange(R_WIDTH) // R_DIM
    bd = (hid[:, None] == hid[None, :]).astype(np.float64)
    return (jnp.asarray(np.stack([fwd, bwd]), BF16), jnp.asarray(np.stack([mk_f, mk_b]), F32),
            jnp.asarray(bd, BF16))


def _hgrn_kernel(*refs, nc, has_init):
    prs_ref, pgs_ref, gh_ref, mstk_ref, lmask_ref, bd_ref = refs[:6]
    if has_init:
        s0s_ref, os_ref = refs[6:8]
        sfins_ref = None
        scratch = refs[8:]
    else:
        s0s_ref = None
        os_ref, sfins_ref = refs[7:9]
        scratch = refs[9:]
    for s in range(prs_ref.shape[0]):
        _hgrn_sequence(prs_ref.at[s], pgs_ref.at[s], gh_ref, mstk_ref, lmask_ref, bd_ref,
                       None if s0s_ref is None else s0s_ref.at[s], os_ref.at[s],
                       None if sfins_ref is None else sfins_ref.at[s], scratch, nc)


def _hgrn_sequence(pr_ref, pg_ref, gh_ref, mstk_ref, lmask_ref, bd_ref, s0_ref, o_ref, sfin_ref, scratch, nc):
    has_init = s0_ref is not None
    gc_scr, qg_scr, upd_scr, dec_scr, sbd_scr = scratch
    w = R_WIDTH
    c = CHUNK
    bd = bd_ref[...]
    hid = lax.broadcasted_iota(jnp.int32, (1, w), 1) // R_DIM
    q_at = lambda rs: pr_ref[rs, 0:w]
    k_at = lambda d, rs: pr_ref[rs, (1 + d) * w:(2 + d) * w]
    g_at = lambda d, rs: pg_ref[rs, 2 * d * w:2 * (d + 1) * w]

    def tile4(x):
        return jnp.concatenate([x] * R_HEADS, axis=0)

    def head_diagonal(full):
        out = full[(R_HEADS - 1) * R_DIM:R_HEADS * R_DIM]
        for h in range(R_HEADS - 2, -1, -1):
            out = jnp.where(hid == h, full[h * R_DIM:(h + 1) * R_DIM], out)
        return out

    def chunk_rows(ci):
        return pl.ds(pl.multiple_of(ci * c, c), c)

    def cumulative(d, rs, n_mats):
        s = _dot(mstk_ref[d, 0:n_mats * c, :], g_at(d, rs))
        return s[:, 0:w] + s[:, w:2 * w]

    def level_exponents(gcum, d, halves):
        out = []
        for h in halves:
            parts = []
            for s0 in range(0, c, 2 * h):
                anchor = s0 + h - 1 + d
                parts.append(-jnp.abs(gcum[s0:s0 + 2 * h] - gcum[anchor:anchor + 1]))
            out.append(parts[0] if len(parts) == 1 else jnp.concatenate(parts, axis=0))
        return out

    def anchor_shift(gcum, d):
        parts = []
        for s0 in range(0, c, ANCHOR_BLOCK):
            anchor = s0 + ANCHOR_BLOCK // 2 - 1 + d
            parts.append(gcum[s0:s0 + ANCHOR_BLOCK] - gcum[anchor:anchor + 1])
        return jnp.concatenate(parts, axis=0)

    groups = nc // GROUP
    pairs = [(j, d) for j in range(GROUP) for d in range(2)]

    def increment_group(gi, worst):
        rows = [chunk_rows(gi * GROUP + j) for j in range(GROUP)]
        gcums = [cumulative(d, rows[j], 1) for j, d in pairs]
        lasts = [g[c - 1:c, :] if d == 0 else g[0:1, :] for g, (j, d) in zip(gcums, pairs)]
        kls = [(k_at(d, rows[j]) * jnp.exp(last - g)).astype(BF16)
               for g, last, (j, d) in zip(gcums, lasts, pairs)]
        vs = [pr_ref[rows[j], 3 * w:4 * w].astype(BF16) for j in range(GROUP)]
        upds = [_dot_tn(vs[j], kl) for kl, (j, d) in zip(kls, pairs)]
        for g, last, upd, (j, d) in zip(gcums, lasts, upds, pairs):
            ci = gi * GROUP + j
            gc_scr[d, rows[j], :] = g
            upd_scr[d, ci] = head_diagonal(upd)
            dec_scr[d, ci] = jnp.broadcast_to(jnp.exp(last), (8, w))
            qg_scr[d, rows[j], :] = (q_at(rows[j]) * jnp.exp(g)).astype(BF16)
            sh = jnp.abs(anchor_shift(g, d))
            for s0 in range(0, c, 8):
                worst = jnp.maximum(worst, sh[s0:s0 + 8])
        return worst

    worst = lax.fori_loop(0, groups, increment_group, jnp.zeros((8, w), F32))
    q_max = jnp.max(jnp.max(jnp.abs(pr_ref[:, 0:w]), axis=0, keepdims=True), axis=1, keepdims=True)
    anchor_ok = jnp.max(worst + jnp.log(jnp.maximum(q_max, 1.0))) < ANCHOR_MAX_EXPONENT

    def scan_step(ci, carry):
        sf, sb = carry
        cb = nc - 1 - ci
        sbd_scr[0, ci] = sf.astype(BF16)
        sbd_scr[1, cb] = sb.astype(BF16)
        sf = sf * dec_scr[0, ci][0:1, :] + upd_scr[0, ci]
        sb = sb * dec_scr[1, cb][0:1, :] + upd_scr[1, cb]
        return sf, sb

    if has_init:
        init = tuple(jnp.concatenate([s0_ref[d], jnp.zeros((w, LANES - R_DIM), F32)], axis=1).T[0:R_DIM]
                     for d in range(2))
    else:
        init = (jnp.zeros((R_DIM, w), F32), jnp.zeros((R_DIM, w), F32))
    finals = lax.fori_loop(0, nc, scan_step, init)
    if sfin_ref is not None:
        for d in range(2):
            padded = jnp.concatenate([finals[d], jnp.zeros((LANES - R_DIM, w), F32)], axis=0)
            sfin_ref[d] = padded.T[:, 0:R_DIM]

    def finish(rs, o):
        x2_hi, x2_lo = _split2(o * o)
        ms = (_dot(x2_hi, bd) + _dot(x2_lo, bd)) * (1.0 / R_DIM)
        y = o * lax.rsqrt(ms + EPS) * gh_ref[...]
        o_ref[rs, :] = (y * pr_ref[rs, 4 * w:5 * w]).astype(BF16)

    def readout_group_anchor(gi, carry):
        rows = [chunk_rows(gi * GROUP + j) for j in range(GROUP)]
        qs = [q_at(rows[j]) for j in range(GROUP)]
        gcums = [gc_scr[d, rows[j], :] for j, d in pairs]
        factors = []
        for g, (j, d) in zip(gcums, pairs):
            es = [jnp.exp(x) for x in level_exponents(g, d, COARSE_HALVES[:ANCHOR_LEVELS])]
            sh = anchor_shift(g, d)
            fs = [(e, e, lv) for lv, e in enumerate(es)]
            fs.append((jnp.exp(sh), jnp.exp(-sh), MASK_ANCHOR))
            factors.append(fs)
        kbds = [tile4(k_at(d, rows[j]).astype(BF16)) * bd for j, d in pairs]
        accs = [None] * len(pairs)
        for lv in range(ANCHOR_LEVELS + 1):
            for i, (j, d) in enumerate(pairs):
                eq, ek, mask = factors[i][lv]
                part = jnp.where(lmask_ref[d, mask] > 0.5,
                                 _dot_nt((qs[j] * eq).astype(BF16), kbds[i] * tile4(ek.astype(BF16))), 0.0)
                accs[i] = part if accs[i] is None else accs[i] + part
        vbds = [tile4(pr_ref[rows[j], 3 * w:4 * w].astype(BF16)) * bd for j in range(GROUP)]
        outs = [None] * GROUP
        for i, (j, d) in enumerate(pairs):
            ci = gi * GROUP + j
            od = _dot(accs[i].astype(BF16), vbds[j]) + _dot_nt(qg_scr[d, rows[j], :], tile4(sbd_scr[d, ci]) * bd)
            outs[j] = od if outs[j] is None else outs[j] + od
        for j in range(GROUP):
            finish(rows[j], outs[j])
        return carry

    def readout_step_split(ci, carry):
        rs = chunk_rows(ci)
        q = q_at(rs)
        qb = q.astype(BF16)
        vbd = tile4(pr_ref[rs, 3 * w:4 * w].astype(BF16)) * bd
        dsts = [cumulative(d, rs, 1 + len(FINE_HALVES)) for d in range(2)]
        exps = [level_exponents(dsts[d][0:c], d, COARSE_HALVES)
                + [dsts[d][(1 + j) * c:(2 + j) * c] for j in range(len(FINE_HALVES))] for d in range(2)]
        kbd = [tile4(k_at(d, rs).astype(BF16)) * bd for d in range(2)]
        a = [_dot_nt(qb, kbd[d]) * lmask_ref[d, MASK_DIAG] for d in range(2)]
        for j in range(N_LEVELS):
            for d in range(2):
                e = jnp.exp(exps[d][j])
                a[d] = a[d] + _dot_nt((q * e).astype(BF16), kbd[d] * tile4(e.astype(BF16))) * lmask_ref[d, j]
        o = None
        for d in range(2):
            od = _dot(a[d].astype(BF16), vbd) + _dot_nt(qg_scr[d, rs, :], tile4(sbd_scr[d, ci]) * bd)
            o = od if o is None else o + od
        finish(rs, o)
        return carry

    @pl.when(anchor_ok)
    def _():
        lax.fori_loop(0, groups, readout_group_anchor, 0)

    @pl.when(jnp.logical_not(anchor_ok))
    def _():
        lax.fori_loop(0, nc, readout_step_split, 0)


def _hgrn(pr, pg, seqs, g_hgrn_l, consts, layer, state=None, new_state=None):
    b0, b = seqs
    n = pr.shape[1]
    nc = n // CHUNK
    mstk, lmask, bd = consts
    w = R_WIDTH
    has_init = state is not None
    g = max(1, min(b, IN_TILE // n))
    assert b % g == 0 and b0 % g == 0
    full = lambda *shape: pl.BlockSpec(shape, lambda i: (0,) * len(shape))
    in_specs = [
        pl.BlockSpec((g, n, PR_COLS), lambda i: (b0 // g + i, 0, 0)),
        pl.BlockSpec((g, n, PG_COLS), lambda i: (b0 // g + i, 0, 0)),
        full(1, w),
        full(*mstk.shape),
        full(*lmask.shape),
        full(w, w),
    ]
    args = [pr, pg, g_hgrn_l.reshape(1, w), mstk, lmask, bd]
    out_shape = [jax.ShapeDtypeStruct((b, n, w), BF16)]
    out_specs = [pl.BlockSpec((g, n, w), lambda i: (i, 0, 0))]
    state_spec = pl.BlockSpec((g, None, 2, w, R_DIM), lambda i: (i, layer, 0, 0, 0))
    aliases = {}
    if has_init:
        in_specs.append(state_spec)
        args.append(state)
    else:
        in_specs.append(pl.BlockSpec(memory_space=pl.ANY))
        args.append(new_state)
        out_shape.append(jax.ShapeDtypeStruct(new_state.shape, F32))
        out_specs.append(state_spec)
        aliases = {6: 1}
    return pl.pallas_call(
        functools.partial(_hgrn_kernel, nc=nc, has_init=has_init),
        out_shape=tuple(out_shape),
        grid=(b // g,),
        in_specs=in_specs,
        out_specs=tuple(out_specs),
        input_output_aliases=aliases,
        scratch_shapes=[
            pltpu.VMEM((2, n, w), F32),
            pltpu.VMEM((2, n, w), BF16),
            pltpu.VMEM((2, nc, R_DIM, w), F32),
            pltpu.VMEM((2, nc, 8, w), F32),
            pltpu.VMEM((2, nc, R_DIM, w), BF16),
        ],
        compiler_params=_cparams(("arbitrary",)),
        name="hgrn_scan",
    )(*args)


def _fnet_constants(n):
    j = np.arange(F_GROUP_DIM)
    ang = 2.0 * np.pi * ((j[:, None] * j[None, :]) % F_GROUP_DIM) / F_GROUP_DIM
    eye = np.eye(F_GROUPS)
    cs = np.concatenate([np.kron(eye, np.cos(ang)), np.kron(eye, np.sin(ang))], axis=1)
    t = np.arange(n)
    angn = 2.0 * np.pi * ((t[:, None] * t[None, :]) % n) / n
    return tuple(jnp.asarray(m, F32).astype(BF16) for m in (cs, np.cos(angn), np.sin(angn)))


def _fnet_kernel(pf_ref, cs_ref, cn_ref, sn_ref, wf_ref, o_ref, *, scale):
    w = F_WIDTH
    for s in range(pf_ref.shape[0]):
        t = _dot(pf_ref[s, :, 0:w].astype(BF16), cs_ref[...])
        y = (_dot(cn_ref[...], t[:, 0:w].astype(BF16)) - _dot(sn_ref[...], t[:, w:2 * w].astype(BF16))) * scale
        of = _dot(y.astype(BF16), wf_ref[...])
        o_ref[s] = (of * _silu(pf_ref[s, :, w:2 * w])).astype(BF16)


def _fourier(pf, seqs, consts, w_fnet_bf16):
    b0, b = seqs
    n = pf.shape[1]
    cs, cn, sn = consts
    w = F_WIDTH
    g = max(1, min(b, IN_TILE * 2 // n))
    assert b % g == 0 and b0 % g == 0
    full = lambda *shape: pl.BlockSpec(shape, lambda i: (0,) * len(shape))
    return pl.pallas_call(
        functools.partial(_fnet_kernel, scale=float((n * F_GROUP_DIM) ** -0.5)),
        out_shape=jax.ShapeDtypeStruct((b, n, w), BF16),
        grid=(b // g,),
        in_specs=[pl.BlockSpec((g, n, PF_COLS), lambda i: (b0 // g + i, 0, 0)),
                  full(w, 2 * w), full(n, n), full(n, n), full(w, w)],
        out_specs=pl.BlockSpec((g, n, w), lambda i: (i, 0, 0)),
        compiler_params=_cparams(("arbitrary",)),
        name="fourier_mixing",
    )(pf, cs, cn, sn, w_fnet_bf16)


def _outproj_kernel(mac_ref, mrc_ref, mfc_ref, mal_ref, mrl_ref, mfl_ref, xc_ref, xl_ref, mod_ref, g_ref, w_ref,
                    yc_ref, yl_ref, wb_scr, *, ctx_tiles):
    i = pl.program_id(0)

    @pl.when(i == 0)
    def _():
        wb_scr[...] = w_ref[...].astype(BF16)

    def project(ma_ref, mr_ref, mf_ref, x_ref, y_ref):
        out = (_dot(ma_ref[...], wb_scr[0:A_WIDTH, :])
               + _dot(mr_ref[...], wb_scr[A_WIDTH:A_WIDTH + R_WIDTH, :])
               + _dot(mf_ref[...], wb_scr[A_WIDTH + R_WIDTH:D_MODEL, :]))
        ms = jnp.mean(out * out, axis=-1, keepdims=True)
        y_ref[...] = x_ref[...] + mod_ref[2:3, :] * (out * lax.rsqrt(ms + EPS) * g_ref[...])

    @pl.when(i < ctx_tiles)
    def _():
        project(mac_ref, mrc_ref, mfc_ref, xc_ref, yc_ref)

    @pl.when(i >= ctx_tiles)
    def _():
        project(mal_ref, mrl_ref, mfl_ref, xl_ref, yl_ref)


def _out_projection(mixed_ctx, mixed_lat, x_ctx, x_lat, mod, g_post, w_out, layer):
    bc, nc_, _ = x_ctx.shape
    bl, nl, _ = x_lat.shape
    tm = IN_TILE
    lat_tiles_per_seq = nl // tm
    ctx_tiles = bc * nc_ // tm
    lat_tiles = bl * nl // tm
    last_ctx = ctx_tiles - 1
    ctx_idx = lambda i: (jnp.minimum(i, last_ctx), 0)
    lat_idx = lambda i: (jnp.maximum(i - ctx_tiles, 0), 0)
    mod_idx = lambda i: (jnp.where(i < ctx_tiles, 0, 1 + jnp.maximum(i - ctx_tiles, 0) // lat_tiles_per_seq), 0, 0)
    widths = (A_WIDTH, R_WIDTH, F_WIDTH)
    flat = lambda t: t.reshape(-1, t.shape[-1])
    yc, yl = pl.pallas_call(
        functools.partial(_outproj_kernel, ctx_tiles=ctx_tiles),
        out_shape=(jax.ShapeDtypeStruct((bc * nc_, D_MODEL), F32), jax.ShapeDtypeStruct((bl * nl, D_MODEL), F32)),
        grid=(ctx_tiles + lat_tiles,),
        in_specs=(
            [pl.BlockSpec((tm, wd), ctx_idx) for wd in widths]
            + [pl.BlockSpec((tm, wd), lat_idx) for wd in widths]
            + [
                pl.BlockSpec((tm, D_MODEL), ctx_idx),
                pl.BlockSpec((tm, D_MODEL), lat_idx),
                pl.BlockSpec((None, 3, D_MODEL), mod_idx),
                pl.BlockSpec((None, 1, D_MODEL), lambda i: (layer, 0, 0)),
                pl.BlockSpec((None, D_MODEL, D_MODEL), lambda i: (layer, 0, 0), pipeline_mode=pl.Buffered(1)),
            ]),
        out_specs=(pl.BlockSpec((tm, D_MODEL), ctx_idx), pl.BlockSpec((tm, D_MODEL), lat_idx)),
        scratch_shapes=[pltpu.VMEM((D_MODEL, D_MODEL), BF16)],
        compiler_params=_cparams(("arbitrary",)),
        name="out_projection",
    )(*[flat(t) for t in mixed_ctx], *[flat(t) for t in mixed_lat], flat(x_ctx), flat(x_lat), mod,
      g_post.reshape(DEPTH, 1, D_MODEL), w_out)
    return yc.reshape(x_ctx.shape), yl.reshape(x_lat.shape)


def kernel(x_prompt, x_sample, cache_attn_k, cache_attn_v, state_hgrn, c, c_ctx,
           w_ada, b_ada, g_pre, w_in, rpb, lb_logits, g_hgrn, w_fnet, w_out, g_post):
    nb_ctx, n_ctx, _ = x_prompt.shape
    nb_lat, n_lat, _ = x_sample.shape

    pad_rows = (-(1 + nb_lat)) % 8
    cc = jnp.concatenate([c_ctx[None, :], c, jnp.zeros((pad_rows, D_MODEL), F32)], axis=0)
    mods = _modulations(cc, w_ada, b_ada)
    lbp = _lower_bounds(lb_logits)

    w_fnet_b = w_fnet.astype(BF16)
    hconsts = _hgrn_constants()
    fconsts_ctx = _fnet_constants(n_ctx)
    fconsts_lat = _fnet_constants(n_lat)

    state_rows = state_hgrn.reshape(nb_lat, DEPTH, 2, R_WIDTH, R_DIM)
    cache_kt = jnp.transpose(cache_attn_k, (0, 1, 3, 4, 2))
    cache_vt = jnp.transpose(cache_attn_v, (0, 1, 3, 4, 2))

    yp, ys = x_prompt, x_sample
    new_kv = [jnp.zeros((nb_ctx, DEPTH, A_HEADS, HEAD_DIM, n_ctx), F32) for _ in range(2)]
    new_rows = jnp.zeros((nb_ctx, DEPTH, 2, R_WIDTH, R_DIM), F32)
    ctx_tokens = nb_ctx * n_ctx
    ctx_seqs = (0, nb_ctx)
    lat_seqs = (ctx_tokens // n_lat, nb_lat)
    for l in range(DEPTH):
        mod_all = mods[l, 0:1 + nb_lat].reshape(1 + nb_lat, 3, D_MODEL)
        pa, pr, pg, pf, *new_kv = _in_projection(yp, ys, mod_all, g_pre, lbp, w_in, l, new_kv)
        as_ctx = lambda t: t.reshape(-1, n_ctx, t.shape[-1])
        ma_c = _context_attention(as_ctx(pa), ctx_seqs)
        mr_c, new_rows = _hgrn(as_ctx(pr), as_ctx(pg), ctx_seqs, g_hgrn[l], hconsts, l, new_state=new_rows)
        mf_c = _fourier(as_ctx(pf), ctx_seqs, fconsts_ctx, w_fnet_b[l])
        as_lat = lambda t: t.reshape(-1, n_lat, t.shape[-1])
        ma_l = _neighbourhood_attention(as_lat(pa), lat_seqs, cache_kt, cache_vt, l, rpb[l])
        (mr_l,) = _hgrn(as_lat(pr), as_lat(pg), lat_seqs, g_hgrn[l], hconsts, l, state=state_rows)
        mf_l = _fourier(as_lat(pf), lat_seqs, fconsts_lat, w_fnet_b[l])
        yp, ys = _out_projection((ma_c, mr_c, mf_c), (ma_l, mr_l, mf_l), yp, ys, mod_all, g_post, w_out, l)

    new_state = new_rows.reshape(nb_ctx, DEPTH, 2, R_HEADS, R_DIM, R_DIM)
    new_k, new_v = (jnp.transpose(t, (0, 1, 4, 2, 3)) for t in new_kv)
    return (yp, ys, new_k, new_v, new_state)
```

```python
import functools

import numpy as np
import jax
import jax.numpy as jnp
from jax import lax
from jax.experimental import pallas as pl
from jax.experimental.pallas import tpu as pltpu

F32 = jnp.float32
BF16 = jnp.bfloat16

D_MODEL = 1024
DEPTH = 4
GRID_W = 64
WIN_H = 8
WIN_W = 16
HEAD_DIM = 64
A_HEADS = 8
A_WIDTH = A_HEADS * HEAD_DIM
R_HEADS = 4
R_DIM = 64
R_WIDTH = R_HEADS * R_DIM
F_GROUPS = 4
F_GROUP_DIM = 64
F_WIDTH = F_GROUPS * F_GROUP_DIM
PA_COLS = 4 * A_WIDTH
PR_COLS = 5 * R_WIDTH
PF_COLS = 2 * F_WIDTH
PG_COLS = 4 * R_WIDTH
IN_COLS = PA_COLS + PR_COLS + PF_COLS
CHUNK = 64
EPS = 1e-6
LANES = 128
NEG_INF = float("-inf")
VMEM_LIMIT = 56 * 1024 * 1024


def _cparams(sem):
    return pltpu.CompilerParams(dimension_semantics=sem, vmem_limit_bytes=VMEM_LIMIT)


def _silu(x):
    return x * (1.0 / (1.0 + jnp.exp(-x)))


def _dot(a, b):
    return jnp.dot(a, b, preferred_element_type=F32)


def _dot_nt(a, b):
    return lax.dot_general(a, b, (((1,), (1,)), ((), ())), preferred_element_type=F32)


def _dot_tn(a, b):
    return lax.dot_general(a, b, (((0,), (0,)), ((), ())), preferred_element_type=F32)


def _split2(x):
    hi = x.astype(BF16)
    lo = (x - hi.astype(F32)).astype(BF16)
    return hi, lo


def _mod_kernel(cc_ref, w_ref, b_ref, o_ref):
    a_hi, a_lo = _split2(_silu(cc_ref[...]))
    w_hi, w_lo = _split2(w_ref[...])
    acc = _dot(a_hi, w_hi) + _dot(a_hi, w_lo) + _dot(a_lo, w_hi)
    o_ref[...] = acc + b_ref[...]


def _modulations(cc, w_ada, b_ada):
    rows = cc.shape[0]
    tn = 3 * D_MODEL // 2
    return pl.pallas_call(
        _mod_kernel,
        out_shape=jax.ShapeDtypeStruct((DEPTH, rows, 3 * D_MODEL), F32),
        grid=(DEPTH, 3 * D_MODEL // tn),
        in_specs=[
            pl.BlockSpec((rows, D_MODEL), lambda l, j: (0, 0)),
            pl.BlockSpec((None, D_MODEL, tn), lambda l, j: (l, 0, j)),
            pl.BlockSpec((None, 1, tn), lambda l, j: (l, 0, j)),
        ],
        out_specs=pl.BlockSpec((None, rows, tn), lambda l, j: (l, 0, j)),
        compiler_params=_cparams(("arbitrary", "arbitrary")),
        name="adaln_mod",
    )(cc, w_ada, b_ada.reshape(DEPTH, 1, 3 * D_MODEL))


def _lb_kernel(x_ref, o_ref):
    xs = [x_ref[i] for i in range(DEPTH)]
    m = functools.reduce(jnp.maximum, xs)
    es = [jnp.exp(x - m) for x in xs]
    tot = functools.reduce(lambda a, b: a + b, es)
    cum = None
    first = None
    for i in range(DEPTH):
        p = es[i] / tot
        cum = p if cum is None else cum + p
        if first is None:
            first = cum
        lb = jnp.maximum(cum - first, 0.0)
        o_ref[0, i] = lb
        o_ref[1, i] = jnp.log1p(-lb)
        o_ref[2, i] = 1.0 - lb


def _lower_bounds(lb_logits):
    x = jnp.transpose(lb_logits, (1, 0, 2))
    return pl.pallas_call(
        _lb_kernel,
        out_shape=jax.ShapeDtypeStruct((3, DEPTH, 2, R_WIDTH), F32),
        name="hgrn_lower_bounds",
    )(x)


IN_TILE = 512


def _inproj_kernel(xc_ref, xl_ref, mod_ref, g_ref, lbp_ref, w_ref, kprev_ref, vprev_ref,
                   pa_ref, pr_ref, pg_ref, pf_ref, ko_ref, vo_ref, wb_scr, *, ctx_tiles):
    del kprev_ref, vprev_ref
    i = pl.program_id(0)

    @pl.when(i == 0)
    def _():
        wb_scr[...] = w_ref[...].astype(BF16)

    def project(x_ref, write_cache):
        w = R_WIDTH
        half = x_ref.shape[0] // 2

        def normalise(rows):
            x = x_ref[rows, :]
            ms = jnp.mean(x * x, axis=-1, keepdims=True)
            y = x * lax.rsqrt(ms + EPS) * g_ref[...]
            return (y * (1.0 + mod_ref[1:2, :]) + mod_ref[0:1, :]).astype(BF16)

        def matmuls(h):
            pr = _dot(h, wb_scr[:, PA_COLS:PA_COLS + PR_COLS])
            pf = _dot(h, wb_scr[:, PA_COLS + PR_COLS:IN_COLS])
            pa = _dot(h, wb_scr[:, 0:PA_COLS])
            return pr, pf, pa

        def finish(rows, pr, pf, pa):
            pf_ref[rows, :] = pf
            pa_ref[rows, :] = pa
            pr_ref[rows, 0:w] = _silu(pr[:, 0:w])
            for d in range(2):
                z = pr[:, (1 + d) * w:(2 + d) * w]
                e = jnp.exp(-jnp.abs(z))
                r = 1.0 / (1.0 + e)
                pos = z >= 0.0
                one_m_lb = lbp_ref[2, d:d + 1, :]
                f = lbp_ref[0, d:d + 1, :] + one_m_lb * (jnp.where(pos, 1.0, e) * r)
                log_f = jnp.where(f > 0.0, jnp.log(f), lbp_ref[1, d:d + 1, :] + z)
                g_hi, g_lo = _split2(log_f)
                pg_ref[rows, 2 * d * w:(2 * d + 1) * w] = g_hi
                pg_ref[rows, (2 * d + 1) * w:(2 * d + 2) * w] = g_lo
                pr_ref[rows, (1 + d) * w:(2 + d) * w] = one_m_lb * (jnp.where(pos, e, 1.0) * r)
            pr_ref[rows, 3 * w:4 * w] = pr[:, 3 * w:4 * w]
            pr_ref[rows, 4 * w:5 * w] = _silu(pr[:, 4 * w:5 * w])
            if write_cache:
                n = ko_ref.shape[-1]
                for s in range(rows.start // n, rows.stop // n):
                    local = slice(s * n - rows.start, (s + 1) * n - rows.start)
                    ko_ref[s] = pa[local, A_WIDTH:2 * A_WIDTH].T.reshape(A_HEADS, HEAD_DIM, n)
                    vo_ref[s] = pa[local, 2 * A_WIDTH:3 * A_WIDTH].T.reshape(A_HEADS, HEAD_DIM, n)

        rows = [slice(0, half), slice(half, 2 * half)]
        first = matmuls(normalise(rows[0]))
        second = matmuls(normalise(rows[1]))
        finish(rows[0], *first)
        finish(rows[1], *second)

    @pl.when(i < ctx_tiles)
    def _():
        project(xc_ref, True)

    @pl.when(i >= ctx_tiles)
    def _():
        project(xl_ref, False)


def _in_projection(x_ctx, x_lat, mod, g_pre, lbp, w_in, layer, cache):
    bc, nc_, _ = x_ctx.shape
    bl, nl, _ = x_lat.shape
    tm = IN_TILE
    seqs_per_tile = tm // nc_
    lat_tiles_per_seq = nl // tm
    ctx_tiles = bc * nc_ // tm
    lat_tiles = bl * nl // tm
    tokens = (ctx_tiles + lat_tiles) * tm
    last_ctx = ctx_tiles - 1
    ctx_idx = lambda i: (jnp.minimum(i, last_ctx), 0)
    lat_idx = lambda i: (jnp.maximum(i - ctx_tiles, 0), 0)
    mod_idx = lambda i: (jnp.where(i < ctx_tiles, 0, 1 + jnp.maximum(i - ctx_tiles, 0) // lat_tiles_per_seq), 0, 0)
    tok = lambda i: (i, 0)
    cache_spec = pl.BlockSpec((seqs_per_tile, None, A_HEADS, HEAD_DIM, nc_),
                              lambda i: (jnp.minimum(i, last_ctx), layer, 0, 0, 0))
    cache_shape = jax.ShapeDtypeStruct(cache[0].shape, F32)
    return pl.pallas_call(
        functools.partial(_inproj_kernel, ctx_tiles=ctx_tiles),
        out_shape=(
            jax.ShapeDtypeStruct((tokens, PA_COLS), F32),
            jax.ShapeDtypeStruct((tokens, PR_COLS), F32),
            jax.ShapeDtypeStruct((tokens, PG_COLS), BF16),
            jax.ShapeDtypeStruct((tokens, PF_COLS), F32),
            cache_shape, cache_shape,
        ),
        grid=(ctx_tiles + lat_tiles,),
        in_specs=[
            pl.BlockSpec((tm, D_MODEL), ctx_idx),
            pl.BlockSpec((tm, D_MODEL), lat_idx),
            pl.BlockSpec((None, 3, D_MODEL), mod_idx),
            pl.BlockSpec((None, 1, D_MODEL), lambda i: (layer, 0, 0)),
            pl.BlockSpec((3, None, 2, R_WIDTH), lambda i: (0, layer, 0, 0)),
            pl.BlockSpec((None, D_MODEL, IN_COLS), lambda i: (layer, 0, 0), pipeline_mode=pl.Buffered(1)),
            pl.BlockSpec(memory_space=pl.ANY),
            pl.BlockSpec(memory_space=pl.ANY),
        ],
        out_specs=(
            pl.BlockSpec((tm, PA_COLS), tok),
            pl.BlockSpec((tm, PR_COLS), tok),
            pl.BlockSpec((tm, PG_COLS), tok),
            pl.BlockSpec((tm, PF_COLS), tok),
            cache_spec, cache_spec,
        ),
        scratch_shapes=[pltpu.VMEM((D_MODEL, IN_COLS), BF16)],
        input_output_aliases={6: 4, 7: 5},
        compiler_params=_cparams(("arbitrary",)),
        name="in_projection",
    )(x_ctx.reshape(bc * nc_, D_MODEL), x_lat.reshape(bl * nl, D_MODEL), mod,
      g_pre.reshape(DEPTH, 1, D_MODEL), lbp, w_in, cache[0], cache[1])


LOG2E = 1.4426950408889634
Q_SCALE = HEAD_DIM ** -0.5 * LOG2E


def _lane_lo():
    return lax.broadcasted_iota(jnp.int32, (1, LANES), 1) < HEAD_DIM


def _ctx_attn_kernel(pas_ref, os_ref, s_scr):
    for s in range(pas_ref.shape[0]):
        _ctx_attn_sequence(pas_ref.at[s], os_ref.at[s], s_scr)


def _ctx_attn_sequence(pa_ref, o_ref, s_scr):
    lo = _lane_lo()
    npair = A_HEADS // 2

    def scores(p):
        c = p * LANES
        q2 = pa_ref[:, c:c + LANES] * Q_SCALE
        k2 = pa_ref[:, A_WIDTH + c:A_WIDTH + c + LANES].astype(BF16)
        for hh in range(2):
            sel = lo if hh == 0 else jnp.logical_not(lo)
            s_scr[p % 2, hh] = _dot_nt(jnp.where(sel, q2, 0.0).astype(BF16), k2)

    def finish(p):
        c = p * LANES
        v2 = pa_ref[:, 2 * A_WIDTH + c:2 * A_WIDTH + c + LANES].astype(BF16)
        outs = []
        for hh in range(2):
            s = s_scr[p % 2, hh]
            e = jnp.exp2(s - jnp.max(s, axis=-1, keepdims=True))
            inv = 1.0 / jnp.sum(e, axis=-1, keepdims=True)
            outs.append(_dot(e.astype(BF16), v2) * inv)
        o2 = jnp.where(lo, outs[0], outs[1])
        o_ref[:, c:c + LANES] = (o2 * _silu(pa_ref[:, 3 * A_WIDTH + c:3 * A_WIDTH + c + LANES])).astype(BF16)

    scores(0)
    for p in range(npair):
        if p + 1 < npair:
            scores(p + 1)
        finish(p)


def _context_attention(pa, seqs):
    b0, b = seqs
    n = pa.shape[1]
    g = 2 if b % 2 == 0 and b0 % 2 == 0 else 1
    return pl.pallas_call(
        _ctx_attn_kernel,
        out_shape=jax.ShapeDtypeStruct((b, n, A_WIDTH), BF16),
        grid=(b // g,),
        in_specs=[pl.BlockSpec((g, n, PA_COLS), lambda i: (b0 // g + i, 0, 0))],
        out_specs=pl.BlockSpec((g, n, A_WIDTH), lambda i: (i, 0, 0)),
        scratch_shapes=[pltpu.VMEM((2, 2, n, n), F32)],
        compiler_params=_cparams(("arbitrary",)),
        name="context_attention",
    )(pa)


QROWS = 4


def _nbr_blocks(rows):
    kh = min(WIN_H, rows)
    out = []
    for r_first in range(0, rows, QROWS):
        r0s = [min(max(r - kh // 2, 0), rows - kh) for r in range(r_first, r_first + QROWS)]
        lo, hi = min(r0s), max(r0s) + kh
        lo -= lo % 2
        span = hi - lo
        span += (-span) % 4
        if lo + span > rows:
            lo = rows - span
        assert lo >= 0 and lo % 2 == 0
        out.append((r_first, lo, span, r0s))
    return out, kh


N_REL_ROWS = 2 * WIN_H - 1
TABLE_ROWS = WIN_H * GRID_W


def _build_bias_tables(base_ref, ta_scr, tb_scr, heads):
    lane = lax.broadcasted_iota(jnp.int32, (GRID_W, LANES), 1)
    kc = lax.broadcasted_iota(jnp.int32, (GRID_W, LANES), 0)
    qc = lane % GRID_W
    lo = lane < GRID_W
    c0 = jnp.clip(qc - WIN_W // 2, 0, GRID_W - WIN_W)
    col_in = (kc >= c0) & (kc < c0 + WIN_W)

    for h in heads:
        def tile(i, lane_off):
            if not 0 <= i < N_REL_ROWS:
                return jnp.full((GRID_W, LANES), NEG_INF, F32)
            row = jnp.broadcast_to(base_ref[h, i:i + 1, :] * LOG2E, (GRID_W, LANES))
            return pltpu.roll(row, lane_off, 1, stride=1, stride_axis=0)

        for j in range(WIN_H):
            rows = slice(j * GRID_W, (j + 1) * GRID_W)
            ta_scr[h, rows, :] = jnp.where(col_in, jnp.where(lo, tile(2 * j + 1, 0), tile(2 * j, GRID_W)), NEG_INF)
            tb_scr[h, rows, :] = jnp.where(col_in, jnp.where(lo, tile(2 * j, 0), tile(2 * j - 1, GRID_W)), NEG_INF)


def _nbr_attn_kernel(q_ref, k_ref, v_ref, g_ref, kc_ref, vc_ref, base_ref, o_ref, s_scr, p_scr, ta_scr, tb_scr, *, rows):
    p = pl.program_id(1)

    @pl.when(pl.program_id(0) == 0)
    def _():
        _build_bias_tables(base_ref, ta_scr, tb_scr, (2 * p, 2 * p + 1))

    lo = _lane_lo()
    hi = jnp.logical_not(lo)
    blocks, kh = _nbr_blocks(rows)
    nq = QROWS * GRID_W
    past = kc_ref.shape[-1]
    ctx0 = s_scr.shape[2] - past
    kc_rows = kc_ref[...].reshape(LANES, past).T.astype(BF16)
    vct = vc_ref[...].reshape(LANES, past).astype(BF16)
    head_rows = [slice(hh * HEAD_DIM, (hh + 1) * HEAD_DIM) for hh in range(2)]
    ctx_tiles = [slice(ctx0 + t * GRID_W, ctx0 + (t + 1) * GRID_W) for t in range(past // GRID_W)]

    def window_tiles(bi, c, head):
        r_first, u0, span, r0s = blocks[bi]
        out = []
        for t in range(span):
            kr = u0 + t
            rs = slice(t * GRID_W, (t + 1) * GRID_W)
            ok = [r0s[2 * c + e] <= kr < r0s[2 * c + e] + kh for e in range(2)]
            if not (ok[0] or ok[1]):
                out.append((rs, None))
                continue
            i = kr - (r_first + 2 * c) + WIN_H - 1
            assert 0 <= i <= N_REL_ROWS
            tab, j = (ta_scr, (i - 1) // 2) if i % 2 else (tb_scr, i // 2)
            bias = tab[head, j * GRID_W:(j + 1) * GRID_W, :]
            if not ok[0]:
                bias = jnp.where(lo, NEG_INF, bias)
            if not ok[1]:
                bias = jnp.where(hi, NEG_INF, bias)
            out.append((rs, bias))
        return out

    def scores(bi):
        r_first, u0, span, _ = blocks[bi]
        qs, ks, nk = r_first * GRID_W, u0 * GRID_W, span * GRID_W
        q2 = q_ref[qs:qs + nq, :] * Q_SCALE
        ku = k_ref[ks:ks + nk, :].astype(BF16)
        for hh in range(2):
            qh = jnp.where(lo if hh == 0 else hi, q2, 0.0).astype(BF16)
            s_scr[bi % 2, hh, 0:nk, :] = _dot_nt(ku, qh)
            s_scr[bi % 2, hh, ctx0:ctx0 + past, :] = _dot_nt(kc_rows, qh)

    def fold_rows(x, op):
        parts = [x[r:r + 8] for r in range(0, GRID_W, 8)]
        while len(parts) > 1:
            parts = [op(parts[i], parts[i + 1]) for i in range(0, len(parts), 2)]
        return parts[0]

    def softmax(bi):
        par = bi % 2
        inv = []
        for hh in range(2):
            parts = []
            for c in range(QROWS // 2):
                cs = slice(c * LANES, (c + 1) * LANES)
                tiles = window_tiles(bi, c, 2 * p + hh)
                m_acc = None
                for rs, bias in tiles:
                    if bias is None:
                        continue
                    s = s_scr[par, hh, rs, cs] + bias
                    s_scr[par, hh, rs, cs] = s
                    t = fold_rows(s, jnp.maximum)
                    m_acc = t if m_acc is None else jnp.maximum(m_acc, t)
                for rs in ctx_tiles:
                    m_acc = jnp.maximum(m_acc, fold_rows(s_scr[par, hh, rs, cs], jnp.maximum))
                m = jnp.max(m_acc, axis=0, keepdims=True)
                l_acc = None
                for rs, bias in tiles:
                    if bias is None:
                        p_scr[par, hh, rs, cs] = jnp.zeros((GRID_W, LANES), BF16)
                        continue
                    e = jnp.exp2(s_scr[par, hh, rs, cs] - m)
                    p_scr[par, hh, rs, cs] = e.astype(BF16)
                    t = fold_rows(e, jnp.add)
                    l_acc = t if l_acc is None else l_acc + t
                for rs in ctx_tiles:
                    e = jnp.exp2(s_scr[par, hh, rs, cs] - m)
                    p_scr[par, hh, rs, cs] = e.astype(BF16)
                    l_acc = l_acc + fold_rows(e, jnp.add)
                parts.append(1.0 / jnp.sum(l_acc, axis=0, keepdims=True))
            inv.append(jnp.concatenate(parts, axis=1))
        return inv

    def values(bi, inv):
        r_first, u0, span, _ = blocks[bi]
        qs, ks, nk = r_first * GRID_W, u0 * GRID_W, span * GRID_W
        vut = v_ref[ks:ks + nk, :].T.astype(BF16)
        outs = []
        for hh in range(2):
            o = (_dot(vut[head_rows[hh]], p_scr[bi % 2, hh, 0:nk, :])
                 + _dot(vct[head_rows[hh]], p_scr[bi % 2, hh, ctx0:ctx0 + past, :]))
            outs.append(o * inv[hh])
        o2 = jnp.concatenate(outs, axis=0).T
        o_ref[qs:qs + nq, :] = (o2 * _silu(g_ref[qs:qs + nq, :])).astype(BF16)

    scores(0)
    for bi in range(len(blocks)):
        if bi + 1 < len(blocks):
            scores(bi + 1)
        values(bi, softmax(bi))


def _bias_base(rpb_l):
    mid = WIN_W - 1
    zeros = jnp.zeros(rpb_l.shape[:2] + (LANES - (2 * WIN_W - 1),), F32)
    rev = rpb_l[..., ::-1]
    base = jnp.concatenate([rev[..., mid:], zeros, rev[..., :mid]], axis=-1)
    return jnp.pad(base, ((0, 0), (0, 2 * WIN_H - N_REL_ROWS), (0, 0)))


def _neighbourhood_attention(pa, seqs, cache_kt, cache_vt, layer, rpb_l):
    b0, b = seqs
    n = pa.shape[1]
    rows = n // GRID_W
    past = cache_kt.shape[-1]
    blocks, _ = _nbr_blocks(rows)
    max_nk = max(s for (_, _, s, _) in blocks) * GRID_W
    npair = A_HEADS // 2
    col = lambda off: (lambda i, p: (b0 + i, 0, off + p))
    cache_spec = pl.BlockSpec((None, None, 2, HEAD_DIM, past), lambda i, p: (i, layer, p, 0, 0))
    return pl.pallas_call(
        functools.partial(_nbr_attn_kernel, rows=rows),
        out_shape=jax.ShapeDtypeStruct((b, n, A_WIDTH), BF16),
        grid=(b, npair),
        in_specs=[
            pl.BlockSpec((None, n, LANES), col(0)),
            pl.BlockSpec((None, n, LANES), col(npair)),
            pl.BlockSpec((None, n, LANES), col(2 * npair)),
            pl.BlockSpec((None, n, LANES), col(3 * npair)),
            cache_spec,
            cache_spec,
            pl.BlockSpec((A_HEADS, 2 * WIN_H, LANES), lambda i, p: (0, 0, 0)),
        ],
        out_specs=pl.BlockSpec((None, n, LANES), lambda i, p: (i, 0, p)),
        scratch_shapes=[
            pltpu.VMEM((2, 2, max_nk + past, QROWS * GRID_W), F32),
            pltpu.VMEM((2, 2, max_nk + past, QROWS * GRID_W), BF16),
            pltpu.VMEM((A_HEADS, TABLE_ROWS, LANES), F32),
            pltpu.VMEM((A_HEADS, TABLE_ROWS, LANES), F32),
        ],
        compiler_params=_cparams(("arbitrary", "arbitrary")),
        name="neighbourhood_attention",
    )(pa, pa, pa, pa, cache_kt, cache_vt, _bias_base(rpb_l))


COARSE_HALVES = (32, 16, 8)
FINE_HALVES = (4, 2, 1)
N_LEVELS = len(COARSE_HALVES) + len(FINE_HALVES)
ANCHOR_BLOCK = CHUNK
ANCHOR_LEVELS = 0
ANCHOR_MAX_EXPONENT = 80.0
MASK_DIAG = N_LEVELS
MASK_ANCHOR = N_LEVELS + 1
GROUP = 8


def _hgrn_constants():
    c = CHUNK
    idx = np.arange(c)
    mats = [np.tril(np.ones((c, c)))]
    masks = []
    for h in COARSE_HALVES + FINE_HALVES:
        blk = idx // (2 * h)
        mid = blk * 2 * h + h - 1
        upper = idx > mid
        if h in FINE_HALVES:
            m = np.zeros((c, c))
            for i in range(c):
                if upper[i]:
                    m[i, mid[i] + 1:i + 1] = 1.0
                else:
                    m[i, i + 1:mid[i] + 1] = 1.0
            mats.append(m)
        same = blk[:, None] == blk[None, :]
        masks.append((same & upper[:, None] & (~upper)[None, :]).astype(np.float64))
    masks.append(np.eye(c))
    same_block = (idx[:, None] // ANCHOR_BLOCK) == (idx[None, :] // ANCHOR_BLOCK)
    masks.append((same_block & (idx[None, :] <= idx[:, None])).astype(np.float64))
    fwd = np.concatenate(mats, axis=0)
    bwd = np.concatenate([m[::-1, ::-1] for m in mats], axis=0)
    mk_f = np.stack([np.tile(m, (1, R_HEADS)) for m in masks])
    mk_b = np.stack([np.tile(m[::-1, ::-1], (1, R_HEADS)) for m in masks])
    hid = np.arange(R_WIDTH) // R_DIM
    bd = (hid[:, None] == hid[None, :]).astype(np.float64)
    return (jnp.asarray(np.stack([fwd, bwd]), BF16), jnp.asarray(np.stack([mk_f, mk_b]), F32),
            jnp.asarray(bd, BF16))


def _hgrn_kernel(*refs, nc, has_init):
    prs_ref, pgs_ref, gh_ref, mstk_ref, lmask_ref, bd_ref = refs[:6]
    if has_init:
        s0s_ref, os_ref = refs[6:8]
        sfins_ref = None
        scratch = refs[8:]
    else:
        s0s_ref = None
        os_ref, sfins_ref = refs[7:9]
        scratch = refs[9:]
    for s in range(prs_ref.shape[0]):
        _hgrn_sequence(prs_ref.at[s], pgs_ref.at[s], gh_ref, mstk_ref, lmask_ref, bd_ref,
                       None if s0s_ref is None else s0s_ref.at[s], os_ref.at[s],
                       None if sfins_ref is None else sfins_ref.at[s], scratch, nc)


def _hgrn_sequence(pr_ref, pg_ref, gh_ref, mstk_ref, lmask_ref, bd_ref, s0_ref, o_ref, sfin_ref, scratch, nc):
    has_init = s0_ref is not None
    gc_scr, qg_scr, upd_scr, dec_scr, sbd_scr = scratch
    w = R_WIDTH
    c = CHUNK
    bd = bd_ref[...]
    hid = lax.broadcasted_iota(jnp.int32, (1, w), 1) // R_DIM
    q_at = lambda rs: pr_ref[rs, 0:w]
    k_at = lambda d, rs: pr_ref[rs, (1 + d) * w:(2 + d) * w]
    g_at = lambda d, rs: pg_ref[rs, 2 * d * w:2 * (d + 1) * w]

    def tile4(x):
        return jnp.concatenate([x] * R_HEADS, axis=0)

    def head_diagonal(full):
        out = full[(R_HEADS - 1) * R_DIM:R_HEADS * R_DIM]
        for h in range(R_HEADS - 2, -1, -1):
            out = jnp.where(hid == h, full[h * R_DIM:(h + 1) * R_DIM], out)
        return out

    def chunk_rows(ci):
        return pl.ds(pl.multiple_of(ci * c, c), c)

    def cumulative(d, rs, n_mats):
        s = _dot(mstk_ref[d, 0:n_mats * c, :], g_at(d, rs))
        return s[:, 0:w] + s[:, w:2 * w]

    def level_exponents(gcum, d, halves):
        out = []
        for h in halves:
            parts = []
            for s0 in range(0, c, 2 * h):
                anchor = s0 + h - 1 + d
                parts.append(-jnp.abs(gcum[s0:s0 + 2 * h] - gcum[anchor:anchor + 1]))
            out.append(parts[0] if len(parts) == 1 else jnp.concatenate(parts, axis=0))
        return out

    def anchor_shift(gcum, d):
        parts = []
        for s0 in range(0, c, ANCHOR_BLOCK):
            anchor = s0 + ANCHOR_BLOCK // 2 - 1 + d
            parts.append(gcum[s0:s0 + ANCHOR_BLOCK] - gcum[anchor:anchor + 1])
        return jnp.concatenate(parts, axis=0)

    group = min(GROUP, nc)
    groups = nc // group
    pairs = [(j, d) for j in range(group) for d in range(2)]

    def increment_group(gi, worst):
        rows = [chunk_rows(gi * group + j) for j in range(group)]
        gcums = [cumulative(d, rows[j], 1) for j, d in pairs]
        lasts = [g[c - 1:c, :] if d == 0 else g[0:1, :] for g, (j, d) in zip(gcums, pairs)]
        kls = [(k_at(d, rows[j]) * jnp.exp(last - g)).astype(BF16)
               for g, last, (j, d) in zip(gcums, lasts, pairs)]
        vs = [pr_ref[rows[j], 3 * w:4 * w].astype(BF16) for j in range(group)]
        upds = [_dot_tn(vs[j], kl) for kl, (j, d) in zip(kls, pairs)]
        for g, last, upd, (j, d) in zip(gcums, lasts, upds, pairs):
            ci = gi * group + j
            gc_scr[d, rows[j], :] = g
            upd_scr[d, ci] = head_diagonal(upd)
            dec_scr[d, ci] = jnp.broadcast_to(jnp.exp(last), (8, w))
            qg_scr[d, rows[j], :] = (q_at(rows[j]) * jnp.exp(g)).astype(BF16)
            sh = jnp.abs(anchor_shift(g, d))
            for s0 in range(0, c, 8):
                worst = jnp.maximum(worst, sh[s0:s0 + 8])
        return worst

    worst = lax.fori_loop(0, groups, increment_group, jnp.zeros((8, w), F32))
    q_max = jnp.max(jnp.max(jnp.abs(pr_ref[:, 0:w]), axis=0, keepdims=True), axis=1, keepdims=True)
    anchor_ok = jnp.max(worst + jnp.log(jnp.maximum(q_max, 1.0))) < ANCHOR_MAX_EXPONENT

    def scan_step(ci, carry):
        sf, sb = carry
        cb = nc - 1 - ci
        sbd_scr[0, ci] = sf.astype(BF16)
        sbd_scr[1, cb] = sb.astype(BF16)
        sf = sf * dec_scr[0, ci][0:1, :] + upd_scr[0, ci]
        sb = sb * dec_scr[1, cb][0:1, :] + upd_scr[1, cb]
        return sf, sb

    if has_init:
        init = tuple(jnp.concatenate([s0_ref[d], jnp.zeros((w, LANES - R_DIM), F32)], axis=1).T[0:R_DIM]
                     for d in range(2))
    else:
        init = (jnp.zeros((R_DIM, w), F32), jnp.zeros((R_DIM, w), F32))
    finals = lax.fori_loop(0, nc, scan_step, init)
    if sfin_ref is not None:
        for d in range(2):
            padded = jnp.concatenate([finals[d], jnp.zeros((LANES - R_DIM, w), F32)], axis=0)
            sfin_ref[d] = padded.T[:, 0:R_DIM]

    def finish(rs, o):
        x2_hi, x2_lo = _split2(o * o)
        ms = (_dot(x2_hi, bd) + _dot(x2_lo, bd)) * (1.0 / R_DIM)
        y = o * lax.rsqrt(ms + EPS) * gh_ref[...]
        o_ref[rs, :] = (y * pr_ref[rs, 4 * w:5 * w]).astype(BF16)

    def readout_group_anchor(gi, carry):
        rows = [chunk_rows(gi * group + j) for j in range(group)]
        qs = [q_at(rows[j]) for j in range(group)]
        gcums = [gc_scr[d, rows[j], :] for j, d in pairs]
        factors = []
        for g, (j, d) in zip(gcums, pairs):
            es = [jnp.exp(x) for x in level_exponents(g, d, COARSE_HALVES[:ANCHOR_LEVELS])]
            sh = anchor_shift(g, d)
            fs = [(e, e, lv) for lv, e in enumerate(es)]
            fs.append((jnp.exp(sh), jnp.exp(-sh), MASK_ANCHOR))
            factors.append(fs)
        kbds = [tile4(k_at(d, rows[j]).astype(BF16)) * bd for j, d in pairs]
        accs = [None] * len(pairs)
        for lv in range(ANCHOR_LEVELS + 1):
            for i, (j, d) in enumerate(pairs):
                eq, ek, mask = factors[i][lv]
                part = jnp.where(lmask_ref[d, mask] > 0.5,
                                 _dot_nt((qs[j] * eq).astype(BF16), kbds[i] * tile4(ek.astype(BF16))), 0.0)
                accs[i] = part if accs[i] is None else accs[i] + part
        vbds = [tile4(pr_ref[rows[j], 3 * w:4 * w].astype(BF16)) * bd for j in range(group)]
        outs = [None] * group
        for i, (j, d) in enumerate(pairs):
            ci = gi * group + j
            od = _dot(accs[i].astype(BF16), vbds[j]) + _dot_nt(qg_scr[d, rows[j], :], tile4(sbd_scr[d, ci]) * bd)
            outs[j] = od if outs[j] is None else outs[j] + od
        for j in range(group):
            finish(rows[j], outs[j])
        return carry

    def readout_step_split(ci, carry):
        rs = chunk_rows(ci)
        q = q_at(rs)
        qb = q.astype(BF16)
        vbd = tile4(pr_ref[rs, 3 * w:4 * w].astype(BF16)) * bd
        dsts = [cumulative(d, rs, 1 + len(FINE_HALVES)) for d in range(2)]
        exps = [level_exponents(dsts[d][0:c], d, COARSE_HALVES)
                + [dsts[d][(1 + j) * c:(2 + j) * c] for j in range(len(FINE_HALVES))] for d in range(2)]
        kbd = [tile4(k_at(d, rs).astype(BF16)) * bd for d in range(2)]
        a = [_dot_nt(qb, kbd[d]) * lmask_ref[d, MASK_DIAG] for d in range(2)]
        for j in range(N_LEVELS):
            for d in range(2):
                e = jnp.exp(exps[d][j])
                a[d] = a[d] + _dot_nt((q * e).astype(BF16), kbd[d] * tile4(e.astype(BF16))) * lmask_ref[d, j]
        o = None
        for d in range(2):
            od = _dot(a[d].astype(BF16), vbd) + _dot_nt(qg_scr[d, rs, :], tile4(sbd_scr[d, ci]) * bd)
            o = od if o is None else o + od
        finish(rs, o)
        return carry

    @pl.when(anchor_ok)
    def _():
        lax.fori_loop(0, groups, readout_group_anchor, 0)

    @pl.when(jnp.logical_not(anchor_ok))
    def _():
        lax.fori_loop(0, nc, readout_step_split, 0)


def _hgrn(pr, pg, seqs, g_hgrn_l, consts, layer, state=None, new_state=None):
    b0, b = seqs
    n = pr.shape[1]
    nc = n // CHUNK
    mstk, lmask, bd = consts
    w = R_WIDTH
    has_init = state is not None
    g = max(1, min(b, IN_TILE // n))
    assert b % g == 0 and b0 % g == 0
    full = lambda *shape: pl.BlockSpec(shape, lambda i: (0,) * len(shape))
    in_specs = [
        pl.BlockSpec((g, n, PR_COLS), lambda i: (b0 // g + i, 0, 0)),
        pl.BlockSpec((g, n, PG_COLS), lambda i: (b0 // g + i, 0, 0)),
        full(1, w),
        full(*mstk.shape),
        full(*lmask.shape),
        full(w, w),
    ]
    args = [pr, pg, g_hgrn_l.reshape(1, w), mstk, lmask, bd]
    out_shape = [jax.ShapeDtypeStruct((b, n, w), BF16)]
    out_specs = [pl.BlockSpec((g, n, w), lambda i: (i, 0, 0))]
    state_spec = pl.BlockSpec((g, None, 2, w, R_DIM), lambda i: (i, layer, 0, 0, 0))
    aliases = {}
    if has_init:
        in_specs.append(state_spec)
        args.append(state)
    else:
        in_specs.append(pl.BlockSpec(memory_space=pl.ANY))
        args.append(new_state)
        out_shape.append(jax.ShapeDtypeStruct(new_state.shape, F32))
        out_specs.append(state_spec)
        aliases = {6: 1}
    return pl.pallas_call(
        functools.partial(_hgrn_kernel, nc=nc, has_init=has_init),
        out_shape=tuple(out_shape),
        grid=(b // g,),
        in_specs=in_specs,
        out_specs=tuple(out_specs),
        input_output_aliases=aliases,
        scratch_shapes=[
            pltpu.VMEM((2, n, w), F32),
            pltpu.VMEM((2, n, w), BF16),
            pltpu.VMEM((2, nc, R_DIM, w), F32),
            pltpu.VMEM((2, nc, 8, w), F32),
            pltpu.VMEM((2, nc, R_DIM, w), BF16),
        ],
        compiler_params=_cparams(("arbitrary",)),
        name="hgrn_scan",
    )(*args)


def _fnet_constants(n):
    j = np.arange(F_GROUP_DIM)
    ang = 2.0 * np.pi * ((j[:, None] * j[None, :]) % F_GROUP_DIM) / F_GROUP_DIM
    eye = np.eye(F_GROUPS)
    cs = np.concatenate([np.kron(eye, np.cos(ang)), np.kron(eye, np.sin(ang))], axis=1)
    t = np.arange(n)
    angn = 2.0 * np.pi * ((t[:, None] * t[None, :]) % n) / n
    return tuple(jnp.asarray(m, F32).astype(BF16) for m in (cs, np.cos(angn), np.sin(angn)))


def _fnet_kernel(pf_ref, cs_ref, cn_ref, sn_ref, wf_ref, o_ref, *, scale):
    w = F_WIDTH
    for s in range(pf_ref.shape[0]):
        t = _dot(pf_ref[s, :, 0:w].astype(BF16), cs_ref[...])
        y = (_dot(cn_ref[...], t[:, 0:w].astype(BF16)) - _dot(sn_ref[...], t[:, w:2 * w].astype(BF16))) * scale
        of = _dot(y.astype(BF16), wf_ref[...])
        o_ref[s] = (of * _silu(pf_ref[s, :, w:2 * w])).astype(BF16)


def _fourier(pf, seqs, consts, w_fnet_bf16):
    b0, b = seqs
    n = pf.shape[1]
    cs, cn, sn = consts
    w = F_WIDTH
    g = max(1, min(b, IN_TILE * 2 // n))
    assert b % g == 0 and b0 % g == 0
    full = lambda *shape: pl.BlockSpec(shape, lambda i: (0,) * len(shape))
    return pl.pallas_call(
        functools.partial(_fnet_kernel, scale=float((n * F_GROUP_DIM) ** -0.5)),
        out_shape=jax.ShapeDtypeStruct((b, n, w), BF16),
        grid=(b // g,),
        in_specs=[pl.BlockSpec((g, n, PF_COLS), lambda i: (b0 // g + i, 0, 0)),
                  full(w, 2 * w), full(n, n), full(n, n), full(w, w)],
        out_specs=pl.BlockSpec((g, n, w), lambda i: (i, 0, 0)),
        compiler_params=_cparams(("arbitrary",)),
        name="fourier_mixing",
    )(pf, cs, cn, sn, w_fnet_bf16)


def _outproj_kernel(mac_ref, mrc_ref, mfc_ref, mal_ref, mrl_ref, mfl_ref, xc_ref, xl_ref, mod_ref, g_ref, w_ref,
                    yc_ref, yl_ref, wb_scr, *, ctx_tiles):
    i = pl.program_id(0)

    @pl.when(i == 0)
    def _():
        wb_scr[...] = w_ref[...].astype(BF16)

    def project(ma_ref, mr_ref, mf_ref, x_ref, y_ref):
        out = (_dot(ma_ref[...], wb_scr[0:A_WIDTH, :])
               + _dot(mr_ref[...], wb_scr[A_WIDTH:A_WIDTH + R_WIDTH, :])
               + _dot(mf_ref[...], wb_scr[A_WIDTH + R_WIDTH:D_MODEL, :]))
        ms = jnp.mean(out * out, axis=-1, keepdims=True)
        y_ref[...] = x_ref[...] + mod_ref[2:3, :] * (out * lax.rsqrt(ms + EPS) * g_ref[...])

    @pl.when(i < ctx_tiles)
    def _():
        project(mac_ref, mrc_ref, mfc_ref, xc_ref, yc_ref)

    @pl.when(i >= ctx_tiles)
    def _():
        project(mal_ref, mrl_ref, mfl_ref, xl_ref, yl_ref)


def _out_projection(mixed_ctx, mixed_lat, x_ctx, x_lat, mod, g_post, w_out, layer):
    bc, nc_, _ = x_ctx.shape
    bl, nl, _ = x_lat.shape
    tm = IN_TILE
    lat_tiles_per_seq = nl // tm
    ctx_tiles = bc * nc_ // tm
    lat_tiles = bl * nl // tm
    last_ctx = ctx_tiles - 1
    ctx_idx = lambda i: (jnp.minimum(i, last_ctx), 0)
    lat_idx = lambda i: (jnp.maximum(i - ctx_tiles, 0), 0)
    mod_idx = lambda i: (jnp.where(i < ctx_tiles, 0, 1 + jnp.maximum(i - ctx_tiles, 0) // lat_tiles_per_seq), 0, 0)
    widths = (A_WIDTH, R_WIDTH, F_WIDTH)
    flat = lambda t: t.reshape(-1, t.shape[-1])
    yc, yl = pl.pallas_call(
        functools.partial(_outproj_kernel, ctx_tiles=ctx_tiles),
        out_shape=(jax.ShapeDtypeStruct((bc * nc_, D_MODEL), F32), jax.ShapeDtypeStruct((bl * nl, D_MODEL), F32)),
        grid=(ctx_tiles + lat_tiles,),
        in_specs=(
            [pl.BlockSpec((tm, wd), ctx_idx) for wd in widths]
            + [pl.BlockSpec((tm, wd), lat_idx) for wd in widths]
            + [
                pl.BlockSpec((tm, D_MODEL), ctx_idx),
                pl.BlockSpec((tm, D_MODEL), lat_idx),
                pl.BlockSpec((None, 3, D_MODEL), mod_idx),
                pl.BlockSpec((None, 1, D_MODEL), lambda i: (layer, 0, 0)),
                pl.BlockSpec((None, D_MODEL, D_MODEL), lambda i: (layer, 0, 0), pipeline_mode=pl.Buffered(1)),
            ]),
        out_specs=(pl.BlockSpec((tm, D_MODEL), ctx_idx), pl.BlockSpec((tm, D_MODEL), lat_idx)),
        scratch_shapes=[pltpu.VMEM((D_MODEL, D_MODEL), BF16)],
        compiler_params=_cparams(("arbitrary",)),
        name="out_projection",
    )(*[flat(t) for t in mixed_ctx], *[flat(t) for t in mixed_lat], flat(x_ctx), flat(x_lat), mod,
      g_post.reshape(DEPTH, 1, D_MODEL), w_out)
    return yc.reshape(x_ctx.shape), yl.reshape(x_lat.shape)


def kernel(x_prompt, x_sample, cache_attn_k, cache_attn_v, state_hgrn, c, c_ctx,
           w_ada, b_ada, g_pre, w_in, rpb, lb_logits, g_hgrn, w_fnet, w_out, g_post):
    nb_ctx, n_ctx, _ = x_prompt.shape
    nb_lat, n_lat, _ = x_sample.shape

    pad_rows = (-(1 + nb_lat)) % 8
    cc = jnp.concatenate([c_ctx[None, :], c, jnp.zeros((pad_rows, D_MODEL), F32)], axis=0)
    mods = _modulations(cc, w_ada, b_ada)
    lbp = _lower_bounds(lb_logits)

    w_fnet_b = w_fnet.astype(BF16)
    hconsts = _hgrn_constants()
    fconsts_ctx = _fnet_constants(n_ctx)
    fconsts_lat = _fnet_constants(n_lat)

    state_rows = state_hgrn.reshape(nb_lat, DEPTH, 2, R_WIDTH, R_DIM)
    cache_kt = jnp.transpose(cache_attn_k, (0, 1, 3, 4, 2))
    cache_vt = jnp.transpose(cache_attn_v, (0, 1, 3, 4, 2))

    yp, ys = x_prompt, x_sample
    new_kv = [jnp.zeros((nb_ctx, DEPTH, A_HEADS, HEAD_DIM, n_ctx), F32) for _ in range(2)]
    new_rows = jnp.zeros((nb_ctx, DEPTH, 2, R_WIDTH, R_DIM), F32)
    ctx_tokens = nb_ctx * n_ctx
    ctx_seqs = (0, nb_ctx)
    lat_seqs = (ctx_tokens // n_lat, nb_lat)
    for l in range(DEPTH):
        mod_all = mods[l, 0:1 + nb_lat].reshape(1 + nb_lat, 3, D_MODEL)
        pa, pr, pg, pf, *new_kv = _in_projection(yp, ys, mod_all, g_pre, lbp, w_in, l, new_kv)
        as_ctx = lambda t: t.reshape(-1, n_ctx, t.shape[-1])
        ma_c = _context_attention(as_ctx(pa), ctx_seqs)
        mr_c, new_rows = _hgrn(as_ctx(pr), as_ctx(pg), ctx_seqs, g_hgrn[l], hconsts, l, new_state=new_rows)
        mf_c = _fourier(as_ctx(pf), ctx_seqs, fconsts_ctx, w_fnet_b[l])
        as_lat = lambda t: t.reshape(-1, n_lat, t.shape[-1])
        ma_l = _neighbourhood_attention(as_lat(pa), lat_seqs, cache_kt, cache_vt, l, rpb[l])
        (mr_l,) = _hgrn(as_lat(pr), as_lat(pg), lat_seqs, g_hgrn[l], hconsts, l, state=state_rows)
        mf_l = _fourier(as_lat(pf), lat_seqs, fconsts_lat, w_fnet_b[l])
        yp, ys = _out_projection((ma_c, mr_c, mf_c), (ma_l, mr_l, mf_l), yp, ys, mod_all, g_post, w_out, l)

    new_state = new_rows.reshape(nb_ctx, DEPTH, 2, R_HEADS, R_DIM, R_DIM)
    new_k, new_v = (jnp.transpose(t, (0, 1, 4, 2, 3)) for t in new_kv)
    return (yp, ys, new_k, new_v, new_state)
```

```python
import functools

import numpy as np
import jax
import jax.numpy as jnp
from jax import lax
from jax.experimental import pallas as pl
from jax.experimental.pallas import tpu as pltpu

F32 = jnp.float32
BF16 = jnp.bfloat16

D_MODEL = 1024
DEPTH = 4
GRID_W = 64
WIN_H = 8
WIN_W = 16
HEAD_DIM = 64
A_HEADS = 8
A_WIDTH = A_HEADS * HEAD_DIM
R_HEADS = 4
R_DIM = 64
R_WIDTH = R_HEADS * R_DIM
F_GROUPS = 4
F_GROUP_DIM = 64
F_WIDTH = F_GROUPS * F_GROUP_DIM
PA_COLS = 4 * A_WIDTH
PR_COLS = 5 * R_WIDTH
PF_COLS = 2 * F_WIDTH
PG_COLS = 4 * R_WIDTH
IN_COLS = PA_COLS + PR_COLS + PF_COLS
CHUNK = 64
EPS = 1e-6
LANES = 128
NEG_INF = float("-inf")
VMEM_LIMIT = 56 * 1024 * 1024


def _cparams(sem):
    return pltpu.CompilerParams(dimension_semantics=sem, vmem_limit_bytes=VMEM_LIMIT)


def _silu(x):
    return x * (1.0 / (1.0 + jnp.exp(-x)))


def _dot(a, b):
    return jnp.dot(a, b, preferred_element_type=F32)


def _dot_nt(a, b):
    return lax.dot_general(a, b, (((1,), (1,)), ((), ())), preferred_element_type=F32)


def _dot_tn(a, b):
    return lax.dot_general(a, b, (((0,), (0,)), ((), ())), preferred_element_type=F32)


def _split2(x):
    hi = x.astype(BF16)
    lo = (x - hi.astype(F32)).astype(BF16)
    return hi, lo


def _mod_kernel(cc_ref, w_ref, b_ref, o_ref):
    a_hi, a_lo = _split2(_silu(cc_ref[...]))
    w_hi, w_lo = _split2(w_ref[...])
    acc = _dot(a_hi, w_hi) + _dot(a_hi, w_lo) + _dot(a_lo, w_hi)
    o_ref[...] = acc + b_ref[...]


def _modulations(cc, w_ada, b_ada):
    rows = cc.shape[0]
    tn = 3 * D_MODEL // 2
    return pl.pallas_call(
        _mod_kernel,
        out_shape=jax.ShapeDtypeStruct((DEPTH, rows, 3 * D_MODEL), F32),
        grid=(DEPTH, 3 * D_MODEL // tn),
        in_specs=[
            pl.BlockSpec((rows, D_MODEL), lambda l, j: (0, 0)),
            pl.BlockSpec((None, D_MODEL, tn), lambda l, j: (l, 0, j)),
            pl.BlockSpec((None, 1, tn), lambda l, j: (l, 0, j)),
        ],
        out_specs=pl.BlockSpec((None, rows, tn), lambda l, j: (l, 0, j)),
        compiler_params=_cparams(("arbitrary", "arbitrary")),
        name="adaln_mod",
    )(cc, w_ada, b_ada.reshape(DEPTH, 1, 3 * D_MODEL))


def _lb_kernel(x_ref, o_ref):
    xs = [x_ref[i] for i in range(DEPTH)]
    m = functools.reduce(jnp.maximum, xs)
    es = [jnp.exp(x - m) for x in xs]
    tot = functools.reduce(lambda a, b: a + b, es)
    cum = None
    first = None
    for i in range(DEPTH):
        p = es[i] / tot
        cum = p if cum is None else cum + p
        if first is None:
            first = cum
        lb = jnp.maximum(cum - first, 0.0)
        o_ref[0, i] = lb
        o_ref[1, i] = jnp.log1p(-lb)
        o_ref[2, i] = 1.0 - lb


def _lower_bounds(lb_logits):
    x = jnp.transpose(lb_logits, (1, 0, 2))
    return pl.pallas_call(
        _lb_kernel,
        out_shape=jax.ShapeDtypeStruct((3, DEPTH, 2, R_WIDTH), F32),
        name="hgrn_lower_bounds",
    )(x)


IN_TILE = 512


def _inproj_kernel(xc_ref, xl_ref, mod_ref, g_ref, lbp_ref, w_ref, kprev_ref, vprev_ref,
                   pa_ref, pr_ref, pg_ref, pf_ref, ko_ref, vo_ref, wb_scr, *, ctx_tiles):
    del kprev_ref, vprev_ref
    i = pl.program_id(0)

    @pl.when(i == 0)
    def _():
        wb_scr[...] = w_ref[...].astype(BF16)

    def project(x_ref, write_cache):
        w = R_WIDTH
        half = x_ref.shape[0] // 2

        def normalise(rows):
            x = x_ref[rows, :]
            ms = jnp.mean(x * x, axis=-1, keepdims=True)
            y = x * lax.rsqrt(ms + EPS) * g_ref[...]
            return (y * (1.0 + mod_ref[1:2, :]) + mod_ref[0:1, :]).astype(BF16)

        def matmuls(h):
            pr = _dot(h, wb_scr[:, PA_COLS:PA_COLS + PR_COLS])
            pf = _dot(h, wb_scr[:, PA_COLS + PR_COLS:IN_COLS])
            pa = _dot(h, wb_scr[:, 0:PA_COLS])
            return pr, pf, pa

        def finish(rows, pr, pf, pa):
            pf_ref[rows, :] = pf
            pa_ref[rows, :] = pa
            pr_ref[rows, 0:w] = _silu(pr[:, 0:w])
            for d in range(2):
                z = pr[:, (1 + d) * w:(2 + d) * w]
                e = jnp.exp(-jnp.abs(z))
                r = 1.0 / (1.0 + e)
                pos = z >= 0.0
                one_m_lb = lbp_ref[2, d:d + 1, :]
                f = lbp_ref[0, d:d + 1, :] + one_m_lb * (jnp.where(pos, 1.0, e) * r)
                log_f = jnp.where(f > 0.0, jnp.log(f), lbp_ref[1, d:d + 1, :] + z)
                g_hi, g_lo = _split2(log_f)
                pg_ref[rows, 2 * d * w:(2 * d + 1) * w] = g_hi
                pg_ref[rows, (2 * d + 1) * w:(2 * d + 2) * w] = g_lo
                pr_ref[rows, (1 + d) * w:(2 + d) * w] = one_m_lb * (jnp.where(pos, e, 1.0) * r)
            pr_ref[rows, 3 * w:4 * w] = pr[:, 3 * w:4 * w]
            pr_ref[rows, 4 * w:5 * w] = _silu(pr[:, 4 * w:5 * w])
            if write_cache:
                n = ko_ref.shape[-1]
                for s in range(rows.start // n, rows.stop // n):
                    local = slice(s * n - rows.start, (s + 1) * n - rows.start)
                    ko_ref[s] = pa[local, A_WIDTH:2 * A_WIDTH].T.reshape(A_HEADS, HEAD_DIM, n)
                    vo_ref[s] = pa[local, 2 * A_WIDTH:3 * A_WIDTH].T.reshape(A_HEADS, HEAD_DIM, n)

        rows = [slice(0, half), slice(half, 2 * half)]
        first = matmuls(normalise(rows[0]))
        second = matmuls(normalise(rows[1]))
        finish(rows[0], *first)
        finish(rows[1], *second)

    @pl.when(i < ctx_tiles)
    def _():
        project(xc_ref, True)

    @pl.when(i >= ctx_tiles)
    def _():
        project(xl_ref, False)


def _in_projection(x_ctx, x_lat, mod, g_pre, lbp, w_in, layer, cache):
    bc, nc_, _ = x_ctx.shape
    bl, nl, _ = x_lat.shape
    tm = IN_TILE
    seqs_per_tile = tm // nc_
    lat_tiles_per_seq = nl // tm
    ctx_tiles = bc * nc_ // tm
    lat_tiles = bl * nl // tm
    tokens = (ctx_tiles + lat_tiles) * tm
    last_ctx = ctx_tiles - 1
    ctx_idx = lambda i: (jnp.minimum(i, last_ctx), 0)
    lat_idx = lambda i: (jnp.maximum(i - ctx_tiles, 0), 0)
    mod_idx = lambda i: (jnp.where(i < ctx_tiles, 0, 1 + jnp.maximum(i - ctx_tiles, 0) // lat_tiles_per_seq), 0, 0)
    tok = lambda i: (i, 0)
    cache_spec = pl.BlockSpec((seqs_per_tile, None, A_HEADS, HEAD_DIM, nc_),
                              lambda i: (jnp.minimum(i, last_ctx), layer, 0, 0, 0))
    cache_shape = jax.ShapeDtypeStruct(cache[0].shape, F32)
    return pl.pallas_call(
        functools.partial(_inproj_kernel, ctx_tiles=ctx_tiles),
        out_shape=(
            jax.ShapeDtypeStruct((tokens, PA_COLS), F32),
            jax.ShapeDtypeStruct((tokens, PR_COLS), F32),
            jax.ShapeDtypeStruct((tokens, PG_COLS), BF16),
            jax.ShapeDtypeStruct((tokens, PF_COLS), F32),
            cache_shape, cache_shape,
        ),
        grid=(ctx_tiles + lat_tiles,),
        in_specs=[
            pl.BlockSpec((tm, D_MODEL), ctx_idx),
            pl.BlockSpec((tm, D_MODEL), lat_idx),
            pl.BlockSpec((None, 3, D_MODEL), mod_idx),
            pl.BlockSpec((None, 1, D_MODEL), lambda i: (layer, 0, 0)),
            pl.BlockSpec((3, None, 2, R_WIDTH), lambda i: (0, layer, 0, 0)),
            pl.BlockSpec((None, D_MODEL, IN_COLS), lambda i: (layer, 0, 0), pipeline_mode=pl.Buffered(1)),
            pl.BlockSpec(memory_space=pl.ANY),
            pl.BlockSpec(memory_space=pl.ANY),
        ],
        out_specs=(
            pl.BlockSpec((tm, PA_COLS), tok),
            pl.BlockSpec((tm, PR_COLS), tok),
            pl.BlockSpec((tm, PG_COLS), tok),
            pl.BlockSpec((tm, PF_COLS), tok),
            cache_spec, cache_spec,
        ),
        scratch_shapes=[pltpu.VMEM((D_MODEL, IN_COLS), BF16)],
        input_output_aliases={6: 4, 7: 5},
        compiler_params=_cparams(("arbitrary",)),
        name="in_projection",
    )(x_ctx.reshape(bc * nc_, D_MODEL), x_lat.reshape(bl * nl, D_MODEL), mod,
      g_pre.reshape(DEPTH, 1, D_MODEL), lbp, w_in, cache[0], cache[1])


LOG2E = 1.4426950408889634
Q_SCALE = HEAD_DIM ** -0.5 * LOG2E


def _lane_lo():
    return lax.broadcasted_iota(jnp.int32, (1, LANES), 1) < HEAD_DIM


def _ctx_attn_kernel(pas_ref, os_ref, s_scr):
    for s in range(pas_ref.shape[0]):
        _ctx_attn_sequence(pas_ref.at[s], os_ref.at[s], s_scr)


def _ctx_attn_sequence(pa_ref, o_ref, s_scr):
    lo = _lane_lo()
    npair = A_HEADS // 2

    def scores(p):
        c = p * LANES
        q2 = pa_ref[:, c:c + LANES] * Q_SCALE
        k2 = pa_ref[:, A_WIDTH + c:A_WIDTH + c + LANES].astype(BF16)
        for hh in range(2):
            sel = lo if hh == 0 else jnp.logical_not(lo)
            s_scr[p % 2, hh] = _dot_nt(jnp.where(sel, q2, 0.0).astype(BF16), k2)

    def finish(p):
        c = p * LANES
        v2 = pa_ref[:, 2 * A_WIDTH + c:2 * A_WIDTH + c + LANES].astype(BF16)
        outs = []
        for hh in range(2):
            s = s_scr[p % 2, hh]
            e = jnp.exp2(s - jnp.max(s, axis=-1, keepdims=True))
            inv = 1.0 / jnp.sum(e, axis=-1, keepdims=True)
            outs.append(_dot(e.astype(BF16), v2) * inv)
        o2 = jnp.where(lo, outs[0], outs[1])
        o_ref[:, c:c + LANES] = (o2 * _silu(pa_ref[:, 3 * A_WIDTH + c:3 * A_WIDTH + c + LANES])).astype(BF16)

    scores(0)
    for p in range(npair):
        if p + 1 < npair:
            scores(p + 1)
        finish(p)


def _context_attention(pa, seqs):
    b0, b = seqs
    n = pa.shape[1]
    g = 2 if b % 2 == 0 and b0 % 2 == 0 else 1
    return pl.pallas_call(
        _ctx_attn_kernel,
        out_shape=jax.ShapeDtypeStruct((b, n, A_WIDTH), BF16),
        grid=(b // g,),
        in_specs=[pl.BlockSpec((g, n, PA_COLS), lambda i: (b0 // g + i, 0, 0))],
        out_specs=pl.BlockSpec((g, n, A_WIDTH), lambda i: (i, 0, 0)),
        scratch_shapes=[pltpu.VMEM((2, 2, n, n), F32)],
        compiler_params=_cparams(("arbitrary",)),
        name="context_attention",
    )(pa)


QROWS = 4


def _nbr_blocks(rows):
    kh = min(WIN_H, rows)
    out = []
    for r_first in range(0, rows, QROWS):
        r0s = [min(max(r - kh // 2, 0), rows - kh) for r in range(r_first, r_first + QROWS)]
        lo, hi = min(r0s), max(r0s) + kh
        lo -= lo % 2
        span = hi - lo
        span += (-span) % 4
        if lo + span > rows:
            lo = rows - span
        assert lo >= 0 and lo % 2 == 0
        out.append((r_first, lo, span, r0s))
    return out, kh


N_REL_ROWS = 2 * WIN_H - 1
TABLE_ROWS = WIN_H * GRID_W


def _build_bias_tables(base_ref, ta_scr, tb_scr, heads):
    lane = lax.broadcasted_iota(jnp.int32, (GRID_W, LANES), 1)
    kc = lax.broadcasted_iota(jnp.int32, (GRID_W, LANES), 0)
    qc = lane % GRID_W
    lo = lane < GRID_W
    c0 = jnp.clip(qc - WIN_W // 2, 0, GRID_W - WIN_W)
    col_in = (kc >= c0) & (kc < c0 + WIN_W)

    for h in heads:
        def tile(i, lane_off):
            if not 0 <= i < N_REL_ROWS:
                return jnp.full((GRID_W, LANES), NEG_INF, F32)
            row = jnp.broadcast_to(base_ref[h, i:i + 1, :] * LOG2E, (GRID_W, LANES))
            return pltpu.roll(row, lane_off, 1, stride=1, stride_axis=0)

        for j in range(WIN_H):
            rows = slice(j * GRID_W, (j + 1) * GRID_W)
            ta_scr[h, rows, :] = jnp.where(col_in, jnp.where(lo, tile(2 * j + 1, 0), tile(2 * j, GRID_W)), NEG_INF)
            tb_scr[h, rows, :] = jnp.where(col_in, jnp.where(lo, tile(2 * j, 0), tile(2 * j - 1, GRID_W)), NEG_INF)


def _nbr_attn_kernel(q_ref, k_ref, v_ref, g_ref, kc_ref, vc_ref, base_ref, o_ref, s_scr, p_scr, ta_scr, tb_scr, *, rows):
    p = pl.program_id(1)

    @pl.when(pl.program_id(0) == 0)
    def _():
        _build_bias_tables(base_ref, ta_scr, tb_scr, (2 * p, 2 * p + 1))

    lo = _lane_lo()
    hi = jnp.logical_not(lo)
    blocks, kh = _nbr_blocks(rows)
    nq = QROWS * GRID_W
    past = kc_ref.shape[-1]
    ctx0 = s_scr.shape[2] - past
    kc_rows = kc_ref[...].reshape(LANES, past).T.astype(BF16)
    vct = vc_ref[...].reshape(LANES, past).astype(BF16)
    head_rows = [slice(hh * HEAD_DIM, (hh + 1) * HEAD_DIM) for hh in range(2)]
    ctx_tiles = [slice(ctx0 + t * GRID_W, ctx0 + (t + 1) * GRID_W) for t in range(past // GRID_W)]

    def window_tiles(bi, c, head):
        r_first, u0, span, r0s = blocks[bi]
        out = []
        for t in range(span):
            kr = u0 + t
            rs = slice(t * GRID_W, (t + 1) * GRID_W)
            ok = [r0s[2 * c + e] <= kr < r0s[2 * c + e] + kh for e in range(2)]
            if not (ok[0] or ok[1]):
                out.append((rs, None))
                continue
            i = kr - (r_first + 2 * c) + WIN_H - 1
            assert 0 <= i <= N_REL_ROWS
            tab, j = (ta_scr, (i - 1) // 2) if i % 2 else (tb_scr, i // 2)
            bias = tab[head, j * GRID_W:(j + 1) * GRID_W, :]
            if not ok[0]:
                bias = jnp.where(lo, NEG_INF, bias)
            if not ok[1]:
                bias = jnp.where(hi, NEG_INF, bias)
            out.append((rs, bias))
        return out

    def scores(bi):
        r_first, u0, span, _ = blocks[bi]
        qs, ks, nk = r_first * GRID_W, u0 * GRID_W, span * GRID_W
        q2 = q_ref[qs:qs + nq, :] * Q_SCALE
        ku = k_ref[ks:ks + nk, :].astype(BF16)
        for hh in range(2):
            qh = jnp.where(lo if hh == 0 else hi, q2, 0.0).astype(BF16)
            s_scr[bi % 2, hh, 0:nk, :] = _dot_nt(ku, qh)
            s_scr[bi % 2, hh, ctx0:ctx0 + past, :] = _dot_nt(kc_rows, qh)

    def fold_rows(x, op):
        parts = [x[r:r + 8] for r in range(0, GRID_W, 8)]
        while len(parts) > 1:
            parts = [op(parts[i], parts[i + 1]) for i in range(0, len(parts), 2)]
        return parts[0]

    def softmax(bi):
        par = bi % 2
        inv = []
        for hh in range(2):
            parts = []
            for c in range(QROWS // 2):
                cs = slice(c * LANES, (c + 1) * LANES)
                tiles = window_tiles(bi, c, 2 * p + hh)
                m_acc = None
                for rs, bias in tiles:
                    if bias is None:
                        continue
                    s = s_scr[par, hh, rs, cs] + bias
                    s_scr[par, hh, rs, cs] = s
                    t = fold_rows(s, jnp.maximum)
                    m_acc = t if m_acc is None else jnp.maximum(m_acc, t)
                for rs in ctx_tiles:
                    m_acc = jnp.maximum(m_acc, fold_rows(s_scr[par, hh, rs, cs], jnp.maximum))
                m = jnp.max(m_acc, axis=0, keepdims=True)
                l_acc = None
                for rs, bias in tiles:
                    if bias is None:
                        p_scr[par, hh, rs, cs] = jnp.zeros((GRID_W, LANES), BF16)
                        continue
                    e = jnp.exp2(s_scr[par, hh, rs, cs] - m)
                    p_scr[par, hh, rs, cs] = e.astype(BF16)
                    t = fold_rows(e, jnp.add)
                    l_acc = t if l_acc is None else l_acc + t
                for rs in ctx_tiles:
                    e = jnp.exp2(s_scr[par, hh, rs, cs] - m)
                    p_scr[par, hh, rs, cs] = e.astype(BF16)
                    l_acc = l_acc + fold_rows(e, jnp.add)
                parts.append(1.0 / jnp.sum(l_acc, axis=0, keepdims=True))
            inv.append(jnp.concatenate(parts, axis=1))
        return inv

    def values(bi, inv):
        r_first, u0, span, _ = blocks[bi]
        qs, ks, nk = r_first * GRID_W, u0 * GRID_W, span * GRID_W
        vut = v_ref[ks:ks + nk, :].T.astype(BF16)
        outs = []
        for hh in range(2):
            o = (_dot(vut[head_rows[hh]], p_scr[bi % 2, hh, 0:nk, :])
                 + _dot(vct[head_rows[hh]], p_scr[bi % 2, hh, ctx0:ctx0 + past, :]))
            outs.append(o * inv[hh])
        o2 = jnp.concatenate(outs, axis=0).T
        o_ref[qs:qs + nq, :] = (o2 * _silu(g_ref[qs:qs + nq, :])).astype(BF16)

    scores(0)
    for bi in range(len(blocks)):
        if bi + 1 < len(blocks):
            scores(bi + 1)
        values(bi, softmax(bi))


def _bias_base(rpb_l):
    mid = WIN_W - 1
    zeros = jnp.zeros(rpb_l.shape[:2] + (LANES - (2 * WIN_W - 1),), F32)
    rev = rpb_l[..., ::-1]
    base = jnp.concatenate([rev[..., mid:], zeros, rev[..., :mid]], axis=-1)
    return jnp.pad(base, ((0, 0), (0, 2 * WIN_H - N_REL_ROWS), (0, 0)))


def _neighbourhood_attention(pa, seqs, cache_kt, cache_vt, layer, rpb_l):
    b0, b = seqs
    n = pa.shape[1]
    rows = n // GRID_W
    past = cache_kt.shape[-1]
    blocks, _ = _nbr_blocks(rows)
    max_nk = max(s for (_, _, s, _) in blocks) * GRID_W
    npair = A_HEADS // 2
    col = lambda off: (lambda i, p: (b0 + i, 0, off + p))
    cache_spec = pl.BlockSpec((None, None, 2, HEAD_DIM, past), lambda i, p: (i, layer, p, 0, 0))
    return pl.pallas_call(
        functools.partial(_nbr_attn_kernel, rows=rows),
        out_shape=jax.ShapeDtypeStruct((b, n, A_WIDTH), BF16),
        grid=(b, npair),
        in_specs=[
            pl.BlockSpec((None, n, LANES), col(0)),
            pl.BlockSpec((None, n, LANES), col(npair)),
            pl.BlockSpec((None, n, LANES), col(2 * npair)),
            pl.BlockSpec((None, n, LANES), col(3 * npair)),
            cache_spec,
            cache_spec,
            pl.BlockSpec((A_HEADS, 2 * WIN_H, LANES), lambda i, p: (0, 0, 0)),
        ],
        out_specs=pl.BlockSpec((None, n, LANES), lambda i, p: (i, 0, p)),
        scratch_shapes=[
            pltpu.VMEM((2, 2, max_nk + past, QROWS * GRID_W), F32),
            pltpu.VMEM((2, 2, max_nk + past, QROWS * GRID_W), BF16),
            pltpu.VMEM((A_HEADS, TABLE_ROWS, LANES), F32),
            pltpu.VMEM((A_HEADS, TABLE_ROWS, LANES), F32),
        ],
        compiler_params=_cparams(("arbitrary", "arbitrary")),
        name="neighbourhood_attention",
    )(pa, pa, pa, pa, cache_kt, cache_vt, _bias_base(rpb_l))


COARSE_HALVES = (32, 16, 8)
FINE_HALVES = (4, 2, 1)
N_LEVELS = len(COARSE_HALVES) + len(FINE_HALVES)
ANCHOR_BLOCK = CHUNK
ANCHOR_LEVELS = 0
ANCHOR_MAX_EXPONENT = 80.0
MASK_DIAG = N_LEVELS
MASK_ANCHOR = N_LEVELS + 1
GROUP = 16


def _hgrn_constants():
    c = CHUNK
    idx = np.arange(c)
    mats = [np.tril(np.ones((c, c)))]
    masks = []
    for h in COARSE_HALVES + FINE_HALVES:
        blk = idx // (2 * h)
        mid = blk * 2 * h + h - 1
        upper = idx > mid
        if h in FINE_HALVES:
            m = np.zeros((c, c))
            for i in range(c):
                if upper[i]:
                    m[i, mid[i] + 1:i + 1] = 1.0
                else:
                    m[i, i + 1:mid[i] + 1] = 1.0
            mats.append(m)
        same = blk[:, None] == blk[None, :]
        masks.append((same & upper[:, None] & (~upper)[None, :]).astype(np.float64))
    masks.append(np.eye(c))
    same_block = (idx[:, None] // ANCHOR_BLOCK) == (idx[None, :] // ANCHOR_BLOCK)
    masks.append((same_block & (idx[None, :] <= idx[:, None])).astype(np.float64))
    fwd = np.concatenate(mats, axis=0)
    bwd = np.concatenate([m[::-1, ::-1] for m in mats], axis=0)
    mk_f = np.stack([np.tile(m, (1, R_HEADS)) for m in masks])
    mk_b = np.stack([np.tile(m[::-1, ::-1], (1, R_HEADS)) for m in masks])
    hid = np.arange(R_WIDTH) // R_DIM
    bd = (hid[:, None] == hid[None, :]).astype(np.float64)
    return (jnp.asarray(np.stack([fwd, bwd]), BF16), jnp.asarray(np.stack([mk_f, mk_b]), F32),
            jnp.asarray(bd, BF16))


def _hgrn_kernel(*refs, nc, has_init):
    prs_ref, pgs_ref, gh_ref, mstk_ref, lmask_ref, bd_ref = refs[:6]
    if has_init:
        s0s_ref, os_ref = refs[6:8]
        sfins_ref = None
        scratch = refs[8:]
    else:
        s0s_ref = None
        os_ref, sfins_ref = refs[7:9]
        scratch = refs[9:]
    for s in range(prs_ref.shape[0]):
        _hgrn_sequence(prs_ref.at[s], pgs_ref.at[s], gh_ref, mstk_ref, lmask_ref, bd_ref,
                       None if s0s_ref is None else s0s_ref.at[s], os_ref.at[s],
                       None if sfins_ref is None else sfins_ref.at[s], scratch, nc)


def _hgrn_sequence(pr_ref, pg_ref, gh_ref, mstk_ref, lmask_ref, bd_ref, s0_ref, o_ref, sfin_ref, scratch, nc):
    has_init = s0_ref is not None
    gc_scr, qg_scr, upd_scr, dec_scr, sbd_scr = scratch
    w = R_WIDTH
    c = CHUNK
    bd = bd_ref[...]
    hid = lax.broadcasted_iota(jnp.int32, (1, w), 1) // R_DIM
    q_at = lambda rs: pr_ref[rs, 0:w]
    k_at = lambda d, rs: pr_ref[rs, (1 + d) * w:(2 + d) * w]
    g_at = lambda d, rs: pg_ref[rs, 2 * d * w:2 * (d + 1) * w]

    def tile4(x):
        return jnp.concatenate([x] * R_HEADS, axis=0)

    def head_diagonal(full):
        out = full[(R_HEADS - 1) * R_DIM:R_HEADS * R_DIM]
        for h in range(R_HEADS - 2, -1, -1):
            out = jnp.where(hid == h, full[h * R_DIM:(h + 1) * R_DIM], out)
        return out

    def chunk_rows(ci):
        return pl.ds(pl.multiple_of(ci * c, c), c)

    def cumulative(d, rs, n_mats):
        s = _dot(mstk_ref[d, 0:n_mats * c, :], g_at(d, rs))
        return s[:, 0:w] + s[:, w:2 * w]

    def level_exponents(gcum, d, halves):
        out = []
        for h in halves:
            parts = []
            for s0 in range(0, c, 2 * h):
                anchor = s0 + h - 1 + d
                parts.append(-jnp.abs(gcum[s0:s0 + 2 * h] - gcum[anchor:anchor + 1]))
            out.append(parts[0] if len(parts) == 1 else jnp.concatenate(parts, axis=0))
        return out

    def anchor_shift(gcum, d):
        parts = []
        for s0 in range(0, c, ANCHOR_BLOCK):
            anchor = s0 + ANCHOR_BLOCK // 2 - 1 + d
            parts.append(gcum[s0:s0 + ANCHOR_BLOCK] - gcum[anchor:anchor + 1])
        return jnp.concatenate(parts, axis=0)

    group = min(GROUP, nc)
    groups = nc // group
    pairs = [(j, d) for j in range(group) for d in range(2)]

    def increment_group(gi, worst):
        rows = [chunk_rows(gi * group + j) for j in range(group)]
        gcums = [cumulative(d, rows[j], 1) for j, d in pairs]
        lasts = [g[c - 1:c, :] if d == 0 else g[0:1, :] for g, (j, d) in zip(gcums, pairs)]
        kls = [(k_at(d, rows[j]) * jnp.exp(last - g)).astype(BF16)
               for g, last, (j, d) in zip(gcums, lasts, pairs)]
        vs = [pr_ref[rows[j], 3 * w:4 * w].astype(BF16) for j in range(group)]
        upds = [_dot_tn(vs[j], kl) for kl, (j, d) in zip(kls, pairs)]
        for g, last, upd, (j, d) in zip(gcums, lasts, upds, pairs):
            ci = gi * group + j
            gc_scr[d, rows[j], :] = g
            upd_scr[d, ci] = head_diagonal(upd)
            dec_scr[d, ci] = jnp.broadcast_to(jnp.exp(last), (8, w))
            qg_scr[d, rows[j], :] = (q_at(rows[j]) * jnp.exp(g)).astype(BF16)
            sh = jnp.abs(anchor_shift(g, d))
            for s0 in range(0, c, 8):
                worst = jnp.maximum(worst, sh[s0:s0 + 8])
        return worst

    worst = lax.fori_loop(0, groups, increment_group, jnp.zeros((8, w), F32))
    q_max = jnp.max(jnp.max(jnp.abs(pr_ref[:, 0:w]), axis=0, keepdims=True), axis=1, keepdims=True)
    anchor_ok = jnp.max(worst + jnp.log(jnp.maximum(q_max, 1.0))) < ANCHOR_MAX_EXPONENT

    def scan_step(ci, carry):
        sf, sb = carry
        cb = nc - 1 - ci
        sbd_scr[0, ci] = sf.astype(BF16)
        sbd_scr[1, cb] = sb.astype(BF16)
        sf = sf * dec_scr[0, ci][0:1, :] + upd_scr[0, ci]
        sb = sb * dec_scr[1, cb][0:1, :] + upd_scr[1, cb]
        return sf, sb

    if has_init:
        init = tuple(jnp.concatenate([s0_ref[d], jnp.zeros((w, LANES - R_DIM), F32)], axis=1).T[0:R_DIM]
                     for d in range(2))
    else:
        init = (jnp.zeros((R_DIM, w), F32), jnp.zeros((R_DIM, w), F32))
    finals = lax.fori_loop(0, nc, scan_step, init)
    if sfin_ref is not None:
        for d in range(2):
            padded = jnp.concatenate([finals[d], jnp.zeros((LANES - R_DIM, w), F32)], axis=0)
            sfin_ref[d] = padded.T[:, 0:R_DIM]

    def finish(rs, o):
        x2_hi, x2_lo = _split2(o * o)
        ms = (_dot(x2_hi, bd) + _dot(x2_lo, bd)) * (1.0 / R_DIM)
        y = o * lax.rsqrt(ms + EPS) * gh_ref[...]
        o_ref[rs, :] = (y * pr_ref[rs, 4 * w:5 * w]).astype(BF16)

    def readout_group_anchor(gi, carry):
        rows = [chunk_rows(gi * group + j) for j in range(group)]
        qs = [q_at(rows[j]) for j in range(group)]
        gcums = [gc_scr[d, rows[j], :] for j, d in pairs]
        factors = []
        for g, (j, d) in zip(gcums, pairs):
            es = [jnp.exp(x) for x in level_exponents(g, d, COARSE_HALVES[:ANCHOR_LEVELS])]
            sh = anchor_shift(g, d)
            fs = [(e, e, lv) for lv, e in enumerate(es)]
            fs.append((jnp.exp(sh), jnp.exp(-sh), MASK_ANCHOR))
            factors.append(fs)
        kbds = [tile4(k_at(d, rows[j]).astype(BF16)) * bd for j, d in pairs]
        accs = [None] * len(pairs)
        for lv in range(ANCHOR_LEVELS + 1):
            for i, (j, d) in enumerate(pairs):
                eq, ek, mask = factors[i][lv]
                part = jnp.where(lmask_ref[d, mask] > 0.5,
                                 _dot_nt((qs[j] * eq).astype(BF16), kbds[i] * tile4(ek.astype(BF16))), 0.0)
                accs[i] = part if accs[i] is None else accs[i] + part
        vbds = [tile4(pr_ref[rows[j], 3 * w:4 * w].astype(BF16)) * bd for j in range(group)]
        outs = [None] * group
        for i, (j, d) in enumerate(pairs):
            ci = gi * group + j
            od = _dot(accs[i].astype(BF16), vbds[j]) + _dot_nt(qg_scr[d, rows[j], :], tile4(sbd_scr[d, ci]) * bd)
            outs[j] = od if outs[j] is None else outs[j] + od
        for j in range(group):
            finish(rows[j], outs[j])
        return carry

    def readout_step_split(ci, carry):
        rs = chunk_rows(ci)
        q = q_at(rs)
        qb = q.astype(BF16)
        vbd = tile4(pr_ref[rs, 3 * w:4 * w].astype(BF16)) * bd
        dsts = [cumulative(d, rs, 1 + len(FINE_HALVES)) for d in range(2)]
        exps = [level_exponents(dsts[d][0:c], d, COARSE_HALVES)
                + [dsts[d][(1 + j) * c:(2 + j) * c] for j in range(len(FINE_HALVES))] for d in range(2)]
        kbd = [tile4(k_at(d, rs).astype(BF16)) * bd for d in range(2)]
        a = [_dot_nt(qb, kbd[d]) * lmask_ref[d, MASK_DIAG] for d in range(2)]
        for j in range(N_LEVELS):
            for d in range(2):
                e = jnp.exp(exps[d][j])
                a[d] = a[d] + _dot_nt((q * e).astype(BF16), kbd[d] * tile4(e.astype(BF16))) * lmask_ref[d, j]
        o = None
        for d in range(2):
            od = _dot(a[d].astype(BF16), vbd) + _dot_nt(qg_scr[d, rs, :], tile4(sbd_scr[d, ci]) * bd)
            o = od if o is None else o + od
        finish(rs, o)
        return carry

    @pl.when(anchor_ok)
    def _():
        lax.fori_loop(0, groups, readout_group_anchor, 0)

    @pl.when(jnp.logical_not(anchor_ok))
    def _():
        lax.fori_loop(0, nc, readout_step_split, 0)


def _hgrn(pr, pg, seqs, g_hgrn_l, consts, layer, state=None, new_state=None):
    b0, b = seqs
    n = pr.shape[1]
    nc = n // CHUNK
    mstk, lmask, bd = consts
    w = R_WIDTH
    has_init = state is not None
    g = max(1, min(b, IN_TILE // n))
    assert b % g == 0 and b0 % g == 0
    full = lambda *shape: pl.BlockSpec(shape, lambda i: (0,) * len(shape))
    in_specs = [
        pl.BlockSpec((g, n, PR_COLS), lambda i: (b0 // g + i, 0, 0)),
        pl.BlockSpec((g, n, PG_COLS), lambda i: (b0 // g + i, 0, 0)),
        full(1, w),
        full(*mstk.shape),
        full(*lmask.shape),
        full(w, w),
    ]
    args = [pr, pg, g_hgrn_l.reshape(1, w), mstk, lmask, bd]
    out_shape = [jax.ShapeDtypeStruct((b, n, w), BF16)]
    out_specs = [pl.BlockSpec((g, n, w), lambda i: (i, 0, 0))]
    state_spec = pl.BlockSpec((g, None, 2, w, R_DIM), lambda i: (i, layer, 0, 0, 0))
    aliases = {}
    if has_init:
        in_specs.append(state_spec)
        args.append(state)
    else:
        in_specs.append(pl.BlockSpec(memory_space=pl.ANY))
        args.append(new_state)
        out_shape.append(jax.ShapeDtypeStruct(new_state.shape, F32))
        out_specs.append(state_spec)
        aliases = {6: 1}
    return pl.pallas_call(
        functools.partial(_hgrn_kernel, nc=nc, has_init=has_init),
        out_shape=tuple(out_shape),
        grid=(b // g,),
        in_specs=in_specs,
        out_specs=tuple(out_specs),
        input_output_aliases=aliases,
        scratch_shapes=[
            pltpu.VMEM((2, n, w), F32),
            pltpu.VMEM((2, n, w), BF16),
            pltpu.VMEM((2, nc, R_DIM, w), F32),
            pltpu.VMEM((2, nc, 8, w), F32),
            pltpu.VMEM((2, nc, R_DIM, w), BF16),
        ],
        compiler_params=_cparams(("arbitrary",)),
        name="hgrn_scan",
    )(*args)


def _fnet_constants(n):
    j = np.arange(F_GROUP_DIM)
    ang = 2.0 * np.pi * ((j[:, None] * j[None, :]) % F_GROUP_DIM) / F_GROUP_DIM
    eye = np.eye(F_GROUPS)
    cs = np.concatenate([np.kron(eye, np.cos(ang)), np.kron(eye, np.sin(ang))], axis=1)
    t = np.arange(n)
    angn = 2.0 * np.pi * ((t[:, None] * t[None, :]) % n) / n
    return tuple(jnp.asarray(m, F32).astype(BF16) for m in (cs, np.cos(angn), np.sin(angn)))


def _fnet_kernel(pf_ref, cs_ref, cn_ref, sn_ref, wf_ref, o_ref, *, scale):
    w = F_WIDTH
    for s in range(pf_ref.shape[0]):
        t = _dot(pf_ref[s, :, 0:w].astype(BF16), cs_ref[...])
        y = (_dot(cn_ref[...], t[:, 0:w].astype(BF16)) - _dot(sn_ref[...], t[:, w:2 * w].astype(BF16))) * scale
        of = _dot(y.astype(BF16), wf_ref[...])
        o_ref[s] = (of * _silu(pf_ref[s, :, w:2 * w])).astype(BF16)


def _fourier(pf, seqs, consts, w_fnet_bf16):
    b0, b = seqs
    n = pf.shape[1]
    cs, cn, sn = consts
    w = F_WIDTH
    g = max(1, min(b, IN_TILE * 2 // n))
    assert b % g == 0 and b0 % g == 0
    full = lambda *shape: pl.BlockSpec(shape, lambda i: (0,) * len(shape))
    return pl.pallas_call(
        functools.partial(_fnet_kernel, scale=float((n * F_GROUP_DIM) ** -0.5)),
        out_shape=jax.ShapeDtypeStruct((b, n, w), BF16),
        grid=(b // g,),
        in_specs=[pl.BlockSpec((g, n, PF_COLS), lambda i: (b0 // g + i, 0, 0)),
                  full(w, 2 * w), full(n, n), full(n, n), full(w, w)],
        out_specs=pl.BlockSpec((g, n, w), lambda i: (i, 0, 0)),
        compiler_params=_cparams(("arbitrary",)),
        name="fourier_mixing",
    )(pf, cs, cn, sn, w_fnet_bf16)


def _outproj_kernel(mac_ref, mrc_ref, mfc_ref, mal_ref, mrl_ref, mfl_ref, xc_ref, xl_ref, mod_ref, g_ref, w_ref,
                    yc_ref, yl_ref, wb_scr, *, ctx_tiles):
    i = pl.program_id(0)

    @pl.when(i == 0)
    def _():
        wb_scr[...] = w_ref[...].astype(BF16)

    def project(ma_ref, mr_ref, mf_ref, x_ref, y_ref):
        out = (_dot(ma_ref[...], wb_scr[0:A_WIDTH, :])
               + _dot(mr_ref[...], wb_scr[A_WIDTH:A_WIDTH + R_WIDTH, :])
               + _dot(mf_ref[...], wb_scr[A_WIDTH + R_WIDTH:D_MODEL, :]))
        ms = jnp.mean(out * out, axis=-1, keepdims=True)
        y_ref[...] = x_ref[...] + mod_ref[2:3, :] * (out * lax.rsqrt(ms + EPS) * g_ref[...])

    @pl.when(i < ctx_tiles)
    def _():
        project(mac_ref, mrc_ref, mfc_ref, xc_ref, yc_ref)

    @pl.when(i >= ctx_tiles)
    def _():
        project(mal_ref, mrl_ref, mfl_ref, xl_ref, yl_ref)


def _out_projection(mixed_ctx, mixed_lat, x_ctx, x_lat, mod, g_post, w_out, layer):
    bc, nc_, _ = x_ctx.shape
    bl, nl, _ = x_lat.shape
    tm = IN_TILE
    lat_tiles_per_seq = nl // tm
    ctx_tiles = bc * nc_ // tm
    lat_tiles = bl * nl // tm
    last_ctx = ctx_tiles - 1
    ctx_idx = lambda i: (jnp.minimum(i, last_ctx), 0)
    lat_idx = lambda i: (jnp.maximum(i - ctx_tiles, 0), 0)
    mod_idx = lambda i: (jnp.where(i < ctx_tiles, 0, 1 + jnp.maximum(i - ctx_tiles, 0) // lat_tiles_per_seq), 0, 0)
    widths = (A_WIDTH, R_WIDTH, F_WIDTH)
    flat = lambda t: t.reshape(-1, t.shape[-1])
    yc, yl = pl.pallas_call(
        functools.partial(_outproj_kernel, ctx_tiles=ctx_tiles),
        out_shape=(jax.ShapeDtypeStruct((bc * nc_, D_MODEL), F32), jax.ShapeDtypeStruct((bl * nl, D_MODEL), F32)),
        grid=(ctx_tiles + lat_tiles,),
        in_specs=(
            [pl.BlockSpec((tm, wd), ctx_idx) for wd in widths]
            + [pl.BlockSpec((tm, wd), lat_idx) for wd in widths]
            + [
                pl.BlockSpec((tm, D_MODEL), ctx_idx),
                pl.BlockSpec((tm, D_MODEL), lat_idx),
                pl.BlockSpec((None, 3, D_MODEL), mod_idx),
                pl.BlockSpec((None, 1, D_MODEL), lambda i: (layer, 0, 0)),
                pl.BlockSpec((None, D_MODEL, D_MODEL), lambda i: (layer, 0, 0), pipeline_mode=pl.Buffered(1)),
            ]),
        out_specs=(pl.BlockSpec((tm, D_MODEL), ctx_idx), pl.BlockSpec((tm, D_MODEL), lat_idx)),
        scratch_shapes=[pltpu.VMEM((D_MODEL, D_MODEL), BF16)],
        compiler_params=_cparams(("arbitrary",)),
        name="out_projection",
    )(*[flat(t) for t in mixed_ctx], *[flat(t) for t in mixed_lat], flat(x_ctx), flat(x_lat), mod,
      g_post.reshape(DEPTH, 1, D_MODEL), w_out)
    return yc.reshape(x_ctx.shape), yl.reshape(x_lat.shape)


def kernel(x_prompt, x_sample, cache_attn_k, cache_attn_v, state_hgrn, c, c_ctx,
           w_ada, b_ada, g_pre, w_in, rpb, lb_logits, g_hgrn, w_fnet, w_out, g_post):
    nb_ctx, n_ctx, _ = x_prompt.shape
    nb_lat, n_lat, _ = x_sample.shape

    pad_rows = (-(1 + nb_lat)) % 8
    cc = jnp.concatenate([c_ctx[None, :], c, jnp.zeros((pad_rows, D_MODEL), F32)], axis=0)
    mods = _modulations(cc, w_ada, b_ada)
    lbp = _lower_bounds(lb_logits)

    w_fnet_b = w_fnet.astype(BF16)
    hconsts = _hgrn_constants()
    fconsts_ctx = _fnet_constants(n_ctx)
    fconsts_lat = _fnet_constants(n_lat)

    state_rows = state_hgrn.reshape(nb_lat, DEPTH, 2, R_WIDTH, R_DIM)
    cache_kt = jnp.transpose(cache_attn_k, (0, 1, 3, 4, 2))
    cache_vt = jnp.transpose(cache_attn_v, (0, 1, 3, 4, 2))

    yp, ys = x_prompt, x_sample
    new_kv = [jnp.zeros((nb_ctx, DEPTH, A_HEADS, HEAD_DIM, n_ctx), F32) for _ in range(2)]
    new_rows = jnp.zeros((nb_ctx, DEPTH, 2, R_WIDTH, R_DIM), F32)
    ctx_tokens = nb_ctx * n_ctx
    ctx_seqs = (0, nb_ctx)
    lat_seqs = (ctx_tokens // n_lat, nb_lat)
    for l in range(DEPTH):
        mod_all = mods[l, 0:1 + nb_lat].reshape(1 + nb_lat, 3, D_MODEL)
        pa, pr, pg, pf, *new_kv = _in_projection(yp, ys, mod_all, g_pre, lbp, w_in, l, new_kv)
        as_ctx = lambda t: t.reshape(-1, n_ctx, t.shape[-1])
        ma_c = _context_attention(as_ctx(pa), ctx_seqs)
        mr_c, new_rows = _hgrn(as_ctx(pr), as_ctx(pg), ctx_seqs, g_hgrn[l], hconsts, l, new_state=new_rows)
        mf_c = _fourier(as_ctx(pf), ctx_seqs, fconsts_ctx, w_fnet_b[l])
        as_lat = lambda t: t.reshape(-1, n_lat, t.shape[-1])
        ma_l = _neighbourhood_attention(as_lat(pa), lat_seqs, cache_kt, cache_vt, l, rpb[l])
        (mr_l,) = _hgrn(as_lat(pr), as_lat(pg), lat_seqs, g_hgrn[l], hconsts, l, state=state_rows)
        mf_l = _fourier(as_lat(pf), lat_seqs, fconsts_lat, w_fnet_b[l])
        yp, ys = _out_projection((ma_c, mr_c, mf_c), (ma_l, mr_l, mf_l), yp, ys, mod_all, g_post, w_out, l)

    new_state = new_rows.reshape(nb_ctx, DEPTH, 2, R_HEADS, R_DIM, R_DIM)
    new_k, new_v = (jnp.transpose(t, (0, 1, 4, 2, 3)) for t in new_kv)
    return (yp, ys, new_k, new_v, new_state)
```

```python
import functools

import numpy as np
import jax
import jax.numpy as jnp
from jax import lax
from jax.experimental import pallas as pl
from jax.experimental.pallas import tpu as pltpu

F32 = jnp.float32
BF16 = jnp.bfloat16

D_MODEL = 1024
DEPTH = 4
GRID_W = 64
WIN_H = 8
WIN_W = 16
HEAD_DIM = 64
A_HEADS = 8
A_WIDTH = A_HEADS * HEAD_DIM
R_HEADS = 4
R_DIM = 64
R_WIDTH = R_HEADS * R_DIM
F_GROUPS = 4
F_GROUP_DIM = 64
F_WIDTH = F_GROUPS * F_GROUP_DIM
PA_COLS = 4 * A_WIDTH
PR_COLS = 5 * R_WIDTH
PF_COLS = 2 * F_WIDTH
PG_COLS = 4 * R_WIDTH
IN_COLS = PA_COLS + PR_COLS + PF_COLS
CHUNK = 64
EPS = 1e-6
LANES = 128
NEG_INF = float("-inf")
VMEM_LIMIT = 56 * 1024 * 1024


def _cparams(sem):
    return pltpu.CompilerParams(dimension_semantics=sem, vmem_limit_bytes=VMEM_LIMIT)


def _silu(x):
    return x * (1.0 / (1.0 + jnp.exp(-x)))


def _dot(a, b):
    return jnp.dot(a, b, preferred_element_type=F32)


def _dot_nt(a, b):
    return lax.dot_general(a, b, (((1,), (1,)), ((), ())), preferred_element_type=F32)


def _dot_tn(a, b):
    return lax.dot_general(a, b, (((0,), (0,)), ((), ())), preferred_element_type=F32)


def _split2(x):
    hi = x.astype(BF16)
    lo = (x - hi.astype(F32)).astype(BF16)
    return hi, lo


def _mod_kernel(cc_ref, w_ref, b_ref, o_ref):
    a_hi, a_lo = _split2(_silu(cc_ref[...]))
    w_hi, w_lo = _split2(w_ref[...])
    acc = _dot(a_hi, w_hi) + _dot(a_hi, w_lo) + _dot(a_lo, w_hi)
    o_ref[...] = acc + b_ref[...]


def _modulations(cc, w_ada, b_ada):
    rows = cc.shape[0]
    tn = 3 * D_MODEL // 2
    return pl.pallas_call(
        _mod_kernel,
        out_shape=jax.ShapeDtypeStruct((DEPTH, rows, 3 * D_MODEL), F32),
        grid=(DEPTH, 3 * D_MODEL // tn),
        in_specs=[
            pl.BlockSpec((rows, D_MODEL), lambda l, j: (0, 0)),
            pl.BlockSpec((None, D_MODEL, tn), lambda l, j: (l, 0, j)),
            pl.BlockSpec((None, 1, tn), lambda l, j: (l, 0, j)),
        ],
        out_specs=pl.BlockSpec((None, rows, tn), lambda l, j: (l, 0, j)),
        compiler_params=_cparams(("arbitrary", "arbitrary")),
        name="adaln_mod",
    )(cc, w_ada, b_ada.reshape(DEPTH, 1, 3 * D_MODEL))


def _lb_kernel(x_ref, o_ref):
    xs = [x_ref[i] for i in range(DEPTH)]
    m = functools.reduce(jnp.maximum, xs)
    es = [jnp.exp(x - m) for x in xs]
    tot = functools.reduce(lambda a, b: a + b, es)
    cum = None
    first = None
    for i in range(DEPTH):
        p = es[i] / tot
        cum = p if cum is None else cum + p
        if first is None:
            first = cum
        lb = jnp.maximum(cum - first, 0.0)
        o_ref[0, i] = lb
        o_ref[1, i] = jnp.log1p(-lb)
        o_ref[2, i] = 1.0 - lb


def _lower_bounds(lb_logits):
    x = jnp.transpose(lb_logits, (1, 0, 2))
    return pl.pallas_call(
        _lb_kernel,
        out_shape=jax.ShapeDtypeStruct((3, DEPTH, 2, R_WIDTH), F32),
        name="hgrn_lower_bounds",
    )(x)


IN_TILE = 512


def _inproj_kernel(xc_ref, xl_ref, mod_ref, g_ref, lbp_ref, w_ref, kprev_ref, vprev_ref,
                   pa_ref, pr_ref, pg_ref, pf_ref, ko_ref, vo_ref, wb_scr, *, ctx_tiles):
    del kprev_ref, vprev_ref
    i = pl.program_id(0)

    @pl.when(i == 0)
    def _():
        wb_scr[...] = w_ref[...].astype(BF16)

    def project(x_ref, write_cache):
        w = R_WIDTH
        half = x_ref.shape[0] // 2

        def normalise(rows):
            x = x_ref[rows, :]
            ms = jnp.mean(x * x, axis=-1, keepdims=True)
            y = x * lax.rsqrt(ms + EPS) * g_ref[...]
            return (y * (1.0 + mod_ref[1:2, :]) + mod_ref[0:1, :]).astype(BF16)

        def matmuls(h):
            pr = _dot(h, wb_scr[:, PA_COLS:PA_COLS + PR_COLS])
            pf = _dot(h, wb_scr[:, PA_COLS + PR_COLS:IN_COLS])
            pa = _dot(h, wb_scr[:, 0:PA_COLS])
            return pr, pf, pa

        def finish(rows, pr, pf, pa):
            pf_ref[rows, :] = pf
            pa_ref[rows, :] = pa.astype(BF16)
            pr_ref[rows, 0:w] = _silu(pr[:, 0:w])
            for d in range(2):
                z = pr[:, (1 + d) * w:(2 + d) * w]
                e = jnp.exp(-jnp.abs(z))
                r = 1.0 / (1.0 + e)
                pos = z >= 0.0
                one_m_lb = lbp_ref[2, d:d + 1, :]
                f = lbp_ref[0, d:d + 1, :] + one_m_lb * (jnp.where(pos, 1.0, e) * r)
                log_f = jnp.where(f > 0.0, jnp.log(f), lbp_ref[1, d:d + 1, :] + z)
                g_hi, g_lo = _split2(log_f)
                pg_ref[rows, 2 * d * w:(2 * d + 1) * w] = g_hi
                pg_ref[rows, (2 * d + 1) * w:(2 * d + 2) * w] = g_lo
                pr_ref[rows, (1 + d) * w:(2 + d) * w] = one_m_lb * (jnp.where(pos, e, 1.0) * r)
            pr_ref[rows, 3 * w:4 * w] = pr[:, 3 * w:4 * w]
            pr_ref[rows, 4 * w:5 * w] = _silu(pr[:, 4 * w:5 * w])
            if write_cache:
                n = ko_ref.shape[-1]
                for s in range(rows.start // n, rows.stop // n):
                    local = slice(s * n - rows.start, (s + 1) * n - rows.start)
                    ko_ref[s] = pa[local, A_WIDTH:2 * A_WIDTH].T.reshape(A_HEADS, HEAD_DIM, n)
                    vo_ref[s] = pa[local, 2 * A_WIDTH:3 * A_WIDTH].T.reshape(A_HEADS, HEAD_DIM, n)

        rows = [slice(0, half), slice(half, 2 * half)]
        first = matmuls(normalise(rows[0]))
        second = matmuls(normalise(rows[1]))
        finish(rows[0], *first)
        finish(rows[1], *second)

    @pl.when(i < ctx_tiles)
    def _():
        project(xc_ref, True)

    @pl.when(i >= ctx_tiles)
    def _():
        project(xl_ref, False)


def _in_projection(x_ctx, x_lat, mod, g_pre, lbp, w_in, layer, cache):
    bc, nc_, _ = x_ctx.shape
    bl, nl, _ = x_lat.shape
    tm = IN_TILE
    seqs_per_tile = tm // nc_
    lat_tiles_per_seq = nl // tm
    ctx_tiles = bc * nc_ // tm
    lat_tiles = bl * nl // tm
    tokens = (ctx_tiles + lat_tiles) * tm
    last_ctx = ctx_tiles - 1
    ctx_idx = lambda i: (jnp.minimum(i, last_ctx), 0)
    lat_idx = lambda i: (jnp.maximum(i - ctx_tiles, 0), 0)
    mod_idx = lambda i: (jnp.where(i < ctx_tiles, 0, 1 + jnp.maximum(i - ctx_tiles, 0) // lat_tiles_per_seq), 0, 0)
    tok = lambda i: (i, 0)
    cache_spec = pl.BlockSpec((seqs_per_tile, None, A_HEADS, HEAD_DIM, nc_),
                              lambda i: (jnp.minimum(i, last_ctx), layer, 0, 0, 0))
    cache_shape = jax.ShapeDtypeStruct(cache[0].shape, F32)
    return pl.pallas_call(
        functools.partial(_inproj_kernel, ctx_tiles=ctx_tiles),
        out_shape=(
            jax.ShapeDtypeStruct((tokens, PA_COLS), BF16),
            jax.ShapeDtypeStruct((tokens, PR_COLS), F32),
            jax.ShapeDtypeStruct((tokens, PG_COLS), BF16),
            jax.ShapeDtypeStruct((tokens, PF_COLS), F32),
            cache_shape, cache_shape,
        ),
        grid=(ctx_tiles + lat_tiles,),
        in_specs=[
            pl.BlockSpec((tm, D_MODEL), ctx_idx),
            pl.BlockSpec((tm, D_MODEL), lat_idx),
            pl.BlockSpec((None, 3, D_MODEL), mod_idx),
            pl.BlockSpec((None, 1, D_MODEL), lambda i: (layer, 0, 0)),
            pl.BlockSpec((3, None, 2, R_WIDTH), lambda i: (0, layer, 0, 0)),
            pl.BlockSpec((None, D_MODEL, IN_COLS), lambda i: (layer, 0, 0), pipeline_mode=pl.Buffered(1)),
            pl.BlockSpec(memory_space=pl.ANY),
            pl.BlockSpec(memory_space=pl.ANY),
        ],
        out_specs=(
            pl.BlockSpec((tm, PA_COLS), tok),
            pl.BlockSpec((tm, PR_COLS), tok),
            pl.BlockSpec((tm, PG_COLS), tok),
            pl.BlockSpec((tm, PF_COLS), tok),
            cache_spec, cache_spec,
        ),
        scratch_shapes=[pltpu.VMEM((D_MODEL, IN_COLS), BF16)],
        input_output_aliases={6: 4, 7: 5},
        compiler_params=_cparams(("arbitrary",)),
        name="in_projection",
    )(x_ctx.reshape(bc * nc_, D_MODEL), x_lat.reshape(bl * nl, D_MODEL), mod,
      g_pre.reshape(DEPTH, 1, D_MODEL), lbp, w_in, cache[0], cache[1])


LOG2E = 1.4426950408889634
Q_SCALE = HEAD_DIM ** -0.5 * LOG2E


def _lane_lo():
    return lax.broadcasted_iota(jnp.int32, (1, LANES), 1) < HEAD_DIM


def _ctx_attn_kernel(pas_ref, os_ref, s_scr):
    for s in range(pas_ref.shape[0]):
        _ctx_attn_sequence(pas_ref.at[s], os_ref.at[s], s_scr)


def _ctx_attn_sequence(pa_ref, o_ref, s_scr):
    lo = _lane_lo()
    npair = A_HEADS // 2

    def scores(p):
        c = p * LANES
        q2 = pa_ref[:, c:c + LANES].astype(F32) * Q_SCALE
        k2 = pa_ref[:, A_WIDTH + c:A_WIDTH + c + LANES].astype(BF16)
        for hh in range(2):
            sel = lo if hh == 0 else jnp.logical_not(lo)
            s_scr[p % 2, hh] = _dot_nt(jnp.where(sel, q2, 0.0).astype(BF16), k2)

    def finish(p):
        c = p * LANES
        v2 = pa_ref[:, 2 * A_WIDTH + c:2 * A_WIDTH + c + LANES].astype(BF16)
        outs = []
        for hh in range(2):
            s = s_scr[p % 2, hh]
            e = jnp.exp2(s - jnp.max(s, axis=-1, keepdims=True))
            inv = 1.0 / jnp.sum(e, axis=-1, keepdims=True)
            outs.append(_dot(e.astype(BF16), v2) * inv)
        o2 = jnp.where(lo, outs[0], outs[1])
        gate = pa_ref[:, 3 * A_WIDTH + c:3 * A_WIDTH + c + LANES].astype(F32)
        o_ref[:, c:c + LANES] = (o2 * _silu(gate)).astype(BF16)

    scores(0)
    for p in range(npair):
        if p + 1 < npair:
            scores(p + 1)
        finish(p)


def _context_attention(pa, seqs):
    b0, b = seqs
    n = pa.shape[1]
    g = 2 if b % 2 == 0 and b0 % 2 == 0 else 1
    return pl.pallas_call(
        _ctx_attn_kernel,
        out_shape=jax.ShapeDtypeStruct((b, n, A_WIDTH), BF16),
        grid=(b // g,),
        in_specs=[pl.BlockSpec((g, n, PA_COLS), lambda i: (b0 // g + i, 0, 0))],
        out_specs=pl.BlockSpec((g, n, A_WIDTH), lambda i: (i, 0, 0)),
        scratch_shapes=[pltpu.VMEM((2, 2, n, n), F32)],
        compiler_params=_cparams(("arbitrary",)),
        name="context_attention",
    )(pa)


QROWS = 4


def _nbr_blocks(rows):
    kh = min(WIN_H, rows)
    out = []
    for r_first in range(0, rows, QROWS):
        r0s = [min(max(r - kh // 2, 0), rows - kh) for r in range(r_first, r_first + QROWS)]
        lo, hi = min(r0s), max(r0s) + kh
        lo -= lo % 2
        span = hi - lo
        span += (-span) % 4
        if lo + span > rows:
            lo = rows - span
        assert lo >= 0 and lo % 2 == 0
        out.append((r_first, lo, span, r0s))
    return out, kh


N_REL_ROWS = 2 * WIN_H - 1
TABLE_ROWS = WIN_H * GRID_W


def _build_bias_tables(base_ref, ta_scr, tb_scr, heads):
    lane = lax.broadcasted_iota(jnp.int32, (GRID_W, LANES), 1)
    kc = lax.broadcasted_iota(jnp.int32, (GRID_W, LANES), 0)
    qc = lane % GRID_W
    lo = lane < GRID_W
    c0 = jnp.clip(qc - WIN_W // 2, 0, GRID_W - WIN_W)
    col_in = (kc >= c0) & (kc < c0 + WIN_W)

    for h in heads:
        def tile(i, lane_off):
            if not 0 <= i < N_REL_ROWS:
                return jnp.full((GRID_W, LANES), NEG_INF, F32)
            row = jnp.broadcast_to(base_ref[h, i:i + 1, :] * LOG2E, (GRID_W, LANES))
            return pltpu.roll(row, lane_off, 1, stride=1, stride_axis=0)

        for j in range(WIN_H):
            rows = slice(j * GRID_W, (j + 1) * GRID_W)
            ta_scr[h, rows, :] = jnp.where(col_in, jnp.where(lo, tile(2 * j + 1, 0), tile(2 * j, GRID_W)), NEG_INF)
            tb_scr[h, rows, :] = jnp.where(col_in, jnp.where(lo, tile(2 * j, 0), tile(2 * j - 1, GRID_W)), NEG_INF)


def _nbr_attn_kernel(q_ref, k_ref, v_ref, g_ref, kc_ref, vc_ref, base_ref, o_ref, s_scr, p_scr, ta_scr, tb_scr, *, rows):
    p = pl.program_id(1)

    @pl.when(pl.program_id(0) == 0)
    def _():
        _build_bias_tables(base_ref, ta_scr, tb_scr, (2 * p, 2 * p + 1))

    lo = _lane_lo()
    hi = jnp.logical_not(lo)
    blocks, kh = _nbr_blocks(rows)
    nq = QROWS * GRID_W
    past = kc_ref.shape[-1]
    ctx0 = s_scr.shape[2] - past
    kc_rows = kc_ref[...].reshape(LANES, past).T.astype(BF16)
    vct = vc_ref[...].reshape(LANES, past).astype(BF16)
    head_rows = [slice(hh * HEAD_DIM, (hh + 1) * HEAD_DIM) for hh in range(2)]
    ctx_tiles = [slice(ctx0 + t * GRID_W, ctx0 + (t + 1) * GRID_W) for t in range(past // GRID_W)]

    def window_tiles(bi, c, head):
        r_first, u0, span, r0s = blocks[bi]
        out = []
        for t in range(span):
            kr = u0 + t
            rs = slice(t * GRID_W, (t + 1) * GRID_W)
            ok = [r0s[2 * c + e] <= kr < r0s[2 * c + e] + kh for e in range(2)]
            if not (ok[0] or ok[1]):
                out.append((rs, None))
                continue
            i = kr - (r_first + 2 * c) + WIN_H - 1
            assert 0 <= i <= N_REL_ROWS
            tab, j = (ta_scr, (i - 1) // 2) if i % 2 else (tb_scr, i // 2)
            bias = tab[head, j * GRID_W:(j + 1) * GRID_W, :]
            if not ok[0]:
                bias = jnp.where(lo, NEG_INF, bias)
            if not ok[1]:
                bias = jnp.where(hi, NEG_INF, bias)
            out.append((rs, bias))
        return out

    def scores(bi):
        r_first, u0, span, _ = blocks[bi]
        qs, ks, nk = r_first * GRID_W, u0 * GRID_W, span * GRID_W
        q2 = q_ref[qs:qs + nq, :].astype(F32) * Q_SCALE
        ku = k_ref[ks:ks + nk, :].astype(BF16)
        for hh in range(2):
            qh = jnp.where(lo if hh == 0 else hi, q2, 0.0).astype(BF16)
            s_scr[bi % 2, hh, 0:nk, :] = _dot_nt(ku, qh)
            s_scr[bi % 2, hh, ctx0:ctx0 + past, :] = _dot_nt(kc_rows, qh)

    def fold_rows(x, op):
        parts = [x[r:r + 8] for r in range(0, GRID_W, 8)]
        while len(parts) > 1:
            parts = [op(parts[i], parts[i + 1]) for i in range(0, len(parts), 2)]
        return parts[0]

    def softmax(bi):
        par = bi % 2
        inv = []
        for hh in range(2):
            parts = []
            for c in range(QROWS // 2):
                cs = slice(c * LANES, (c + 1) * LANES)
                tiles = window_tiles(bi, c, 2 * p + hh)
                m_acc = None
                for rs, bias in tiles:
                    if bias is None:
                        continue
                    s = s_scr[par, hh, rs, cs] + bias
                    s_scr[par, hh, rs, cs] = s
                    t = fold_rows(s, jnp.maximum)
                    m_acc = t if m_acc is None else jnp.maximum(m_acc, t)
                for rs in ctx_tiles:
                    m_acc = jnp.maximum(m_acc, fold_rows(s_scr[par, hh, rs, cs], jnp.maximum))
                m = jnp.max(m_acc, axis=0, keepdims=True)
                l_acc = None
                for rs, bias in tiles:
                    if bias is None:
                        p_scr[par, hh, rs, cs] = jnp.zeros((GRID_W, LANES), BF16)
                        continue
                    e = jnp.exp2(s_scr[par, hh, rs, cs] - m)
                    p_scr[par, hh, rs, cs] = e.astype(BF16)
                    t = fold_rows(e, jnp.add)
                    l_acc = t if l_acc is None else l_acc + t
                for rs in ctx_tiles:
                    e = jnp.exp2(s_scr[par, hh, rs, cs] - m)
                    p_scr[par, hh, rs, cs] = e.astype(BF16)
                    l_acc = l_acc + fold_rows(e, jnp.add)
                parts.append(1.0 / jnp.sum(l_acc, axis=0, keepdims=True))
            inv.append(jnp.concatenate(parts, axis=1))
        return inv

    def values(bi, inv):
        r_first, u0, span, _ = blocks[bi]
        qs, ks, nk = r_first * GRID_W, u0 * GRID_W, span * GRID_W
        vut = v_ref[ks:ks + nk, :].astype(F32).T.astype(BF16)
        outs = []
        for hh in range(2):
            o = (_dot(vut[head_rows[hh]], p_scr[bi % 2, hh, 0:nk, :])
                 + _dot(vct[head_rows[hh]], p_scr[bi % 2, hh, ctx0:ctx0 + past, :]))
            outs.append(o * inv[hh])
        o2 = jnp.concatenate(outs, axis=0).T
        o_ref[qs:qs + nq, :] = (o2 * _silu(g_ref[qs:qs + nq, :].astype(F32))).astype(BF16)

    scores(0)
    for bi in range(len(blocks)):
        if bi + 1 < len(blocks):
            scores(bi + 1)
        values(bi, softmax(bi))


def _bias_base(rpb_l):
    mid = WIN_W - 1
    zeros = jnp.zeros(rpb_l.shape[:2] + (LANES - (2 * WIN_W - 1),), F32)
    rev = rpb_l[..., ::-1]
    base = jnp.concatenate([rev[..., mid:], zeros, rev[..., :mid]], axis=-1)
    return jnp.pad(base, ((0, 0), (0, 2 * WIN_H - N_REL_ROWS), (0, 0)))


def _neighbourhood_attention(pa, seqs, cache_kt, cache_vt, layer, rpb_l):
    b0, b = seqs
    n = pa.shape[1]
    rows = n // GRID_W
    past = cache_kt.shape[-1]
    blocks, _ = _nbr_blocks(rows)
    max_nk = max(s for (_, _, s, _) in blocks) * GRID_W
    npair = A_HEADS // 2
    col = lambda off: (lambda i, p: (b0 + i, 0, off + p))
    cache_spec = pl.BlockSpec((None, None, 2, HEAD_DIM, past), lambda i, p: (i, layer, p, 0, 0))
    return pl.pallas_call(
        functools.partial(_nbr_attn_kernel, rows=rows),
        out_shape=jax.ShapeDtypeStruct((b, n, A_WIDTH), BF16),
        grid=(b, npair),
        in_specs=[
            pl.BlockSpec((None, n, LANES), col(0)),
            pl.BlockSpec((None, n, LANES), col(npair)),
            pl.BlockSpec((None, n, LANES), col(2 * npair)),
            pl.BlockSpec((None, n, LANES), col(3 * npair)),
            cache_spec,
            cache_spec,
            pl.BlockSpec((A_HEADS, 2 * WIN_H, LANES), lambda i, p: (0, 0, 0)),
        ],
        out_specs=pl.BlockSpec((None, n, LANES), lambda i, p: (i, 0, p)),
        scratch_shapes=[
            pltpu.VMEM((2, 2, max_nk + past, QROWS * GRID_W), F32),
            pltpu.VMEM((2, 2, max_nk + past, QROWS * GRID_W), BF16),
            pltpu.VMEM((A_HEADS, TABLE_ROWS, LANES), F32),
            pltpu.VMEM((A_HEADS, TABLE_ROWS, LANES), F32),
        ],
        compiler_params=_cparams(("arbitrary", "arbitrary")),
        name="neighbourhood_attention",
    )(pa, pa, pa, pa, cache_kt, cache_vt, _bias_base(rpb_l))


COARSE_HALVES = (32, 16, 8)
FINE_HALVES = (4, 2, 1)
N_LEVELS = len(COARSE_HALVES) + len(FINE_HALVES)
ANCHOR_BLOCK = CHUNK
ANCHOR_LEVELS = 0
ANCHOR_MAX_EXPONENT = 80.0
MASK_DIAG = N_LEVELS
MASK_ANCHOR = N_LEVELS + 1
GROUP = 16


def _hgrn_constants():
    c = CHUNK
    idx = np.arange(c)
    mats = [np.tril(np.ones((c, c)))]
    masks = []
    for h in COARSE_HALVES + FINE_HALVES:
        blk = idx // (2 * h)
        mid = blk * 2 * h + h - 1
        upper = idx > mid
        if h in FINE_HALVES:
            m = np.zeros((c, c))
            for i in range(c):
                if upper[i]:
                    m[i, mid[i] + 1:i + 1] = 1.0
                else:
                    m[i, i + 1:mid[i] + 1] = 1.0
            mats.append(m)
        same = blk[:, None] == blk[None, :]
        masks.append((same & upper[:, None] & (~upper)[None, :]).astype(np.float64))
    masks.append(np.eye(c))
    same_block = (idx[:, None] // ANCHOR_BLOCK) == (idx[None, :] // ANCHOR_BLOCK)
    masks.append((same_block & (idx[None, :] <= idx[:, None])).astype(np.float64))
    fwd = np.concatenate(mats, axis=0)
    bwd = np.concatenate([m[::-1, ::-1] for m in mats], axis=0)
    mk_f = np.stack([np.tile(m, (1, R_HEADS)) for m in masks])
    mk_b = np.stack([np.tile(m[::-1, ::-1], (1, R_HEADS)) for m in masks])
    hid = np.arange(R_WIDTH) // R_DIM
    bd = (hid[:, None] == hid[None, :]).astype(np.float64)
    return (jnp.asarray(np.stack([fwd, bwd]), BF16), jnp.asarray(np.stack([mk_f, mk_b]), F32),
            jnp.asarray(bd, BF16))


def _hgrn_kernel(*refs, nc, has_init):
    prs_ref, pgs_ref, gh_ref, mstk_ref, lmask_ref, bd_ref = refs[:6]
    if has_init:
        s0s_ref, os_ref = refs[6:8]
        sfins_ref = None
        scratch = refs[8:]
    else:
        s0s_ref = None
        os_ref, sfins_ref = refs[7:9]
        scratch = refs[9:]
    for s in range(prs_ref.shape[0]):
        _hgrn_sequence(prs_ref.at[s], pgs_ref.at[s], gh_ref, mstk_ref, lmask_ref, bd_ref,
                       None if s0s_ref is None else s0s_ref.at[s], os_ref.at[s],
                       None if sfins_ref is None else sfins_ref.at[s], scratch, nc)


def _hgrn_sequence(pr_ref, pg_ref, gh_ref, mstk_ref, lmask_ref, bd_ref, s0_ref, o_ref, sfin_ref, scratch, nc):
    has_init = s0_ref is not None
    gc_scr, qg_scr, upd_scr, dec_scr, sbd_scr = scratch
    w = R_WIDTH
    c = CHUNK
    bd = bd_ref[...]
    hid = lax.broadcasted_iota(jnp.int32, (1, w), 1) // R_DIM
    q_at = lambda rs: pr_ref[rs, 0:w]
    k_at = lambda d, rs: pr_ref[rs, (1 + d) * w:(2 + d) * w]
    g_at = lambda d, rs: pg_ref[rs, 2 * d * w:2 * (d + 1) * w]

    def tile4(x):
        return jnp.concatenate([x] * R_HEADS, axis=0)

    def head_diagonal(full):
        out = full[(R_HEADS - 1) * R_DIM:R_HEADS * R_DIM]
        for h in range(R_HEADS - 2, -1, -1):
            out = jnp.where(hid == h, full[h * R_DIM:(h + 1) * R_DIM], out)
        return out

    def chunk_rows(ci):
        return pl.ds(pl.multiple_of(ci * c, c), c)

    def cumulative(d, rs, n_mats):
        s = _dot(mstk_ref[d, 0:n_mats * c, :], g_at(d, rs))
        return s[:, 0:w] + s[:, w:2 * w]

    def level_exponents(gcum, d, halves):
        out = []
        for h in halves:
            parts = []
            for s0 in range(0, c, 2 * h):
                anchor = s0 + h - 1 + d
                parts.append(-jnp.abs(gcum[s0:s0 + 2 * h] - gcum[anchor:anchor + 1]))
            out.append(parts[0] if len(parts) == 1 else jnp.concatenate(parts, axis=0))
        return out

    def anchor_shift(gcum, d):
        parts = []
        for s0 in range(0, c, ANCHOR_BLOCK):
            anchor = s0 + ANCHOR_BLOCK // 2 - 1 + d
            parts.append(gcum[s0:s0 + ANCHOR_BLOCK] - gcum[anchor:anchor + 1])
        return jnp.concatenate(parts, axis=0)

    group = min(GROUP, nc)
    groups = nc // group
    pairs = [(j, d) for j in range(group) for d in range(2)]

    def increment_group(gi, worst):
        rows = [chunk_rows(gi * group + j) for j in range(group)]
        gcums = [cumulative(d, rows[j], 1) for j, d in pairs]
        lasts = [g[c - 1:c, :] if d == 0 else g[0:1, :] for g, (j, d) in zip(gcums, pairs)]
        kls = [(k_at(d, rows[j]) * jnp.exp(last - g)).astype(BF16)
               for g, last, (j, d) in zip(gcums, lasts, pairs)]
        vs = [pr_ref[rows[j], 3 * w:4 * w].astype(BF16) for j in range(group)]
        upds = [_dot_tn(vs[j], kl) for kl, (j, d) in zip(kls, pairs)]
        for g, last, upd, (j, d) in zip(gcums, lasts, upds, pairs):
            ci = gi * group + j
            gc_scr[d, rows[j], :] = g
            upd_scr[d, ci] = head_diagonal(upd)
            dec_scr[d, ci] = jnp.broadcast_to(jnp.exp(last), (8, w))
            qg_scr[d, rows[j], :] = (q_at(rows[j]) * jnp.exp(g)).astype(BF16)
            sh = jnp.abs(anchor_shift(g, d))
            for s0 in range(0, c, 8):
                worst = jnp.maximum(worst, sh[s0:s0 + 8])
        return worst

    worst = lax.fori_loop(0, groups, increment_group, jnp.zeros((8, w), F32))
    q_max = jnp.max(jnp.max(jnp.abs(pr_ref[:, 0:w]), axis=0, keepdims=True), axis=1, keepdims=True)
    anchor_ok = jnp.max(worst + jnp.log(jnp.maximum(q_max, 1.0))) < ANCHOR_MAX_EXPONENT

    def scan_step(ci, carry):
        sf, sb = carry
        cb = nc - 1 - ci
        sbd_scr[0, ci] = sf.astype(BF16)
        sbd_scr[1, cb] = sb.astype(BF16)
        sf = sf * dec_scr[0, ci][0:1, :] + upd_scr[0, ci]
        sb = sb * dec_scr[1, cb][0:1, :] + upd_scr[1, cb]
        return sf, sb

    if has_init:
        init = tuple(jnp.concatenate([s0_ref[d], jnp.zeros((w, LANES - R_DIM), F32)], axis=1).T[0:R_DIM]
                     for d in range(2))
    else:
        init = (jnp.zeros((R_DIM, w), F32), jnp.zeros((R_DIM, w), F32))
    finals = lax.fori_loop(0, nc, scan_step, init)
    if sfin_ref is not None:
        for d in range(2):
            padded = jnp.concatenate([finals[d], jnp.zeros((LANES - R_DIM, w), F32)], axis=0)
            sfin_ref[d] = padded.T[:, 0:R_DIM]

    def finish(rs, o):
        x2_hi, x2_lo = _split2(o * o)
        ms = (_dot(x2_hi, bd) + _dot(x2_lo, bd)) * (1.0 / R_DIM)
        y = o * lax.rsqrt(ms + EPS) * gh_ref[...]
        o_ref[rs, :] = (y * pr_ref[rs, 4 * w:5 * w]).astype(BF16)

    def readout_group_anchor(gi, carry):
        rows = [chunk_rows(gi * group + j) for j in range(group)]
        qs = [q_at(rows[j]) for j in range(group)]
        gcums = [gc_scr[d, rows[j], :] for j, d in pairs]
        factors = []
        for g, (j, d) in zip(gcums, pairs):
            es = [jnp.exp(x) for x in level_exponents(g, d, COARSE_HALVES[:ANCHOR_LEVELS])]
            sh = anchor_shift(g, d)
            fs = [(e, e, lv) for lv, e in enumerate(es)]
            fs.append((jnp.exp(sh), jnp.exp(-sh), MASK_ANCHOR))
            factors.append(fs)
        kbds = [tile4(k_at(d, rows[j]).astype(BF16)) * bd for j, d in pairs]
        accs = [None] * len(pairs)
        for lv in range(ANCHOR_LEVELS + 1):
            for i, (j, d) in enumerate(pairs):
                eq, ek, mask = factors[i][lv]
                part = jnp.where(lmask_ref[d, mask] > 0.5,
                                 _dot_nt((qs[j] * eq).astype(BF16), kbds[i] * tile4(ek.astype(BF16))), 0.0)
                accs[i] = part if accs[i] is None else accs[i] + part
        vbds = [tile4(pr_ref[rows[j], 3 * w:4 * w].astype(BF16)) * bd for j in range(group)]
        outs = [None] * group
        for i, (j, d) in enumerate(pairs):
            ci = gi * group + j
            od = _dot(accs[i].astype(BF16), vbds[j]) + _dot_nt(qg_scr[d, rows[j], :], tile4(sbd_scr[d, ci]) * bd)
            outs[j] = od if outs[j] is None else outs[j] + od
        for j in range(group):
            finish(rows[j], outs[j])
        return carry

    def readout_step_split(ci, carry):
        rs = chunk_rows(ci)
        q = q_at(rs)
        qb = q.astype(BF16)
        vbd = tile4(pr_ref[rs, 3 * w:4 * w].astype(BF16)) * bd
        dsts = [cumulative(d, rs, 1 + len(FINE_HALVES)) for d in range(2)]
        exps = [level_exponents(dsts[d][0:c], d, COARSE_HALVES)
                + [dsts[d][(1 + j) * c:(2 + j) * c] for j in range(len(FINE_HALVES))] for d in range(2)]
        kbd = [tile4(k_at(d, rs).astype(BF16)) * bd for d in range(2)]
        a = [_dot_nt(qb, kbd[d]) * lmask_ref[d, MASK_DIAG] for d in range(2)]
        for j in range(N_LEVELS):
            for d in range(2):
                e = jnp.exp(exps[d][j])
                a[d] = a[d] + _dot_nt((q * e).astype(BF16), kbd[d] * tile4(e.astype(BF16))) * lmask_ref[d, j]
        o = None
        for d in range(2):
            od = _dot(a[d].astype(BF16), vbd) + _dot_nt(qg_scr[d, rs, :], tile4(sbd_scr[d, ci]) * bd)
            o = od if o is None else o + od
        finish(rs, o)
        return carry

    @pl.when(anchor_ok)
    def _():
        lax.fori_loop(0, groups, readout_group_anchor, 0)

    @pl.when(jnp.logical_not(anchor_ok))
    def _():
        lax.fori_loop(0, nc, readout_step_split, 0)


def _hgrn(pr, pg, seqs, g_hgrn_l, consts, layer, state=None, new_state=None):
    b0, b = seqs
    n = pr.shape[1]
    nc = n // CHUNK
    mstk, lmask, bd = consts
    w = R_WIDTH
    has_init = state is not None
    g = max(1, min(b, IN_TILE // n))
    assert b % g == 0 and b0 % g == 0
    full = lambda *shape: pl.BlockSpec(shape, lambda i: (0,) * len(shape))
    in_specs = [
        pl.BlockSpec((g, n, PR_COLS), lambda i: (b0 // g + i, 0, 0)),
        pl.BlockSpec((g, n, PG_COLS), lambda i: (b0 // g + i, 0, 0)),
        full(1, w),
        full(*mstk.shape),
        full(*lmask.shape),
        full(w, w),
    ]
    args = [pr, pg, g_hgrn_l.reshape(1, w), mstk, lmask, bd]
    out_shape = [jax.ShapeDtypeStruct((b, n, w), BF16)]
    out_specs = [pl.BlockSpec((g, n, w), lambda i: (i, 0, 0))]
    state_spec = pl.BlockSpec((g, None, 2, w, R_DIM), lambda i: (i, layer, 0, 0, 0))
    aliases = {}
    if has_init:
        in_specs.append(state_spec)
        args.append(state)
    else:
        in_specs.append(pl.BlockSpec(memory_space=pl.ANY))
        args.append(new_state)
        out_shape.append(jax.ShapeDtypeStruct(new_state.shape, F32))
        out_specs.append(state_spec)
        aliases = {6: 1}
    return pl.pallas_call(
        functools.partial(_hgrn_kernel, nc=nc, has_init=has_init),
        out_shape=tuple(out_shape),
        grid=(b // g,),
        in_specs=in_specs,
        out_specs=tuple(out_specs),
        input_output_aliases=aliases,
        scratch_shapes=[
            pltpu.VMEM((2, n, w), F32),
            pltpu.VMEM((2, n, w), BF16),
            pltpu.VMEM((2, nc, R_DIM, w), F32),
            pltpu.VMEM((2, nc, 8, w), F32),
            pltpu.VMEM((2, nc, R_DIM, w), BF16),
        ],
        compiler_params=_cparams(("arbitrary",)),
        name="hgrn_scan",
    )(*args)


def _fnet_constants(n):
    j = np.arange(F_GROUP_DIM)
    ang = 2.0 * np.pi * ((j[:, None] * j[None, :]) % F_GROUP_DIM) / F_GROUP_DIM
    eye = np.eye(F_GROUPS)
    cs = np.concatenate([np.kron(eye, np.cos(ang)), np.kron(eye, np.sin(ang))], axis=1)
    t = np.arange(n)
    angn = 2.0 * np.pi * ((t[:, None] * t[None, :]) % n) / n
    return tuple(jnp.asarray(m, F32).astype(BF16) for m in (cs, np.cos(angn), np.sin(angn)))


def _fnet_kernel(pf_ref, cs_ref, cn_ref, sn_ref, wf_ref, o_ref, *, scale):
    w = F_WIDTH
    for s in range(pf_ref.shape[0]):
        t = _dot(pf_ref[s, :, 0:w].astype(BF16), cs_ref[...])
        y = (_dot(cn_ref[...], t[:, 0:w].astype(BF16)) - _dot(sn_ref[...], t[:, w:2 * w].astype(BF16))) * scale
        of = _dot(y.astype(BF16), wf_ref[...])
        o_ref[s] = (of * _silu(pf_ref[s, :, w:2 * w])).astype(BF16)


def _fourier(pf, seqs, consts, w_fnet_bf16):
    b0, b = seqs
    n = pf.shape[1]
    cs, cn, sn = consts
    w = F_WIDTH
    g = max(1, min(b, IN_TILE * 2 // n))
    assert b % g == 0 and b0 % g == 0
    full = lambda *shape: pl.BlockSpec(shape, lambda i: (0,) * len(shape))
    return pl.pallas_call(
        functools.partial(_fnet_kernel, scale=float((n * F_GROUP_DIM) ** -0.5)),
        out_shape=jax.ShapeDtypeStruct((b, n, w), BF16),
        grid=(b // g,),
        in_specs=[pl.BlockSpec((g, n, PF_COLS), lambda i: (b0 // g + i, 0, 0)),
                  full(w, 2 * w), full(n, n), full(n, n), full(w, w)],
        out_specs=pl.BlockSpec((g, n, w), lambda i: (i, 0, 0)),
        compiler_params=_cparams(("arbitrary",)),
        name="fourier_mixing",
    )(pf, cs, cn, sn, w_fnet_bf16)


def _outproj_kernel(mac_ref, mrc_ref, mfc_ref, mal_ref, mrl_ref, mfl_ref, xc_ref, xl_ref, mod_ref, g_ref, w_ref,
                    yc_ref, yl_ref, wb_scr, *, ctx_tiles):
    i = pl.program_id(0)

    @pl.when(i == 0)
    def _():
        wb_scr[...] = w_ref[...].astype(BF16)

    def project(ma_ref, mr_ref, mf_ref, x_ref, y_ref):
        out = (_dot(ma_ref[...], wb_scr[0:A_WIDTH, :])
               + _dot(mr_ref[...], wb_scr[A_WIDTH:A_WIDTH + R_WIDTH, :])
               + _dot(mf_ref[...], wb_scr[A_WIDTH + R_WIDTH:D_MODEL, :]))
        ms = jnp.mean(out * out, axis=-1, keepdims=True)
        y_ref[...] = x_ref[...] + mod_ref[2:3, :] * (out * lax.rsqrt(ms + EPS) * g_ref[...])

    @pl.when(i < ctx_tiles)
    def _():
        project(mac_ref, mrc_ref, mfc_ref, xc_ref, yc_ref)

    @pl.when(i >= ctx_tiles)
    def _():
        project(mal_ref, mrl_ref, mfl_ref, xl_ref, yl_ref)


def _out_projection(mixed_ctx, mixed_lat, x_ctx, x_lat, mod, g_post, w_out, layer):
    bc, nc_, _ = x_ctx.shape
    bl, nl, _ = x_lat.shape
    tm = IN_TILE
    lat_tiles_per_seq = nl // tm
    ctx_tiles = bc * nc_ // tm
    lat_tiles = bl * nl // tm
    last_ctx = ctx_tiles - 1
    ctx_idx = lambda i: (jnp.minimum(i, last_ctx), 0)
    lat_idx = lambda i: (jnp.maximum(i - ctx_tiles, 0), 0)
    mod_idx = lambda i: (jnp.where(i < ctx_tiles, 0, 1 + jnp.maximum(i - ctx_tiles, 0) // lat_tiles_per_seq), 0, 0)
    widths = (A_WIDTH, R_WIDTH, F_WIDTH)
    flat = lambda t: t.reshape(-1, t.shape[-1])
    yc, yl = pl.pallas_call(
        functools.partial(_outproj_kernel, ctx_tiles=ctx_tiles),
        out_shape=(jax.ShapeDtypeStruct((bc * nc_, D_MODEL), F32), jax.ShapeDtypeStruct((bl * nl, D_MODEL), F32)),
        grid=(ctx_tiles + lat_tiles,),
        in_specs=(
            [pl.BlockSpec((tm, wd), ctx_idx) for wd in widths]
            + [pl.BlockSpec((tm, wd), lat_idx) for wd in widths]
            + [
                pl.BlockSpec((tm, D_MODEL), ctx_idx),
                pl.BlockSpec((tm, D_MODEL), lat_idx),
                pl.BlockSpec((None, 3, D_MODEL), mod_idx),
                pl.BlockSpec((None, 1, D_MODEL), lambda i: (layer, 0, 0)),
                pl.BlockSpec((None, D_MODEL, D_MODEL), lambda i: (layer, 0, 0), pipeline_mode=pl.Buffered(1)),
            ]),
        out_specs=(pl.BlockSpec((tm, D_MODEL), ctx_idx), pl.BlockSpec((tm, D_MODEL), lat_idx)),
        scratch_shapes=[pltpu.VMEM((D_MODEL, D_MODEL), BF16)],
        compiler_params=_cparams(("arbitrary",)),
        name="out_projection",
    )(*[flat(t) for t in mixed_ctx], *[flat(t) for t in mixed_lat], flat(x_ctx), flat(x_lat), mod,
      g_post.reshape(DEPTH, 1, D_MODEL), w_out)
    return yc.reshape(x_ctx.shape), yl.reshape(x_lat.shape)


def kernel(x_prompt, x_sample, cache_attn_k, cache_attn_v, state_hgrn, c, c_ctx,
           w_ada, b_ada, g_pre, w_in, rpb, lb_logits, g_hgrn, w_fnet, w_out, g_post):
    nb_ctx, n_ctx, _ = x_prompt.shape
    nb_lat, n_lat, _ = x_sample.shape

    pad_rows = (-(1 + nb_lat)) % 8
    cc = jnp.concatenate([c_ctx[None, :], c, jnp.zeros((pad_rows, D_MODEL), F32)], axis=0)
    mods = _modulations(cc, w_ada, b_ada)
    lbp = _lower_bounds(lb_logits)

    w_fnet_b = w_fnet.astype(BF16)
    hconsts = _hgrn_constants()
    fconsts_ctx = _fnet_constants(n_ctx)
    fconsts_lat = _fnet_constants(n_lat)

    state_rows = state_hgrn.reshape(nb_lat, DEPTH, 2, R_WIDTH, R_DIM)
    cache_kt = jnp.transpose(cache_attn_k, (0, 1, 3, 4, 2))
    cache_vt = jnp.transpose(cache_attn_v, (0, 1, 3, 4, 2))

    yp, ys = x_prompt, x_sample
    new_kv = [jnp.zeros((nb_ctx, DEPTH, A_HEADS, HEAD_DIM, n_ctx), F32) for _ in range(2)]
    new_rows = jnp.zeros((nb_ctx, DEPTH, 2, R_WIDTH, R_DIM), F32)
    ctx_tokens = nb_ctx * n_ctx
    ctx_seqs = (0, nb_ctx)
    lat_seqs = (ctx_tokens // n_lat, nb_lat)
    for l in range(DEPTH):
        mod_all = mods[l, 0:1 + nb_lat].reshape(1 + nb_lat, 3, D_MODEL)
        pa, pr, pg, pf, *new_kv = _in_projection(yp, ys, mod_all, g_pre, lbp, w_in, l, new_kv)
        as_ctx = lambda t: t.reshape(-1, n_ctx, t.shape[-1])
        ma_c = _context_attention(as_ctx(pa), ctx_seqs)
        mr_c, new_rows = _hgrn(as_ctx(pr), as_ctx(pg), ctx_seqs, g_hgrn[l], hconsts, l, new_state=new_rows)
        mf_c = _fourier(as_ctx(pf), ctx_seqs, fconsts_ctx, w_fnet_b[l])
        as_lat = lambda t: t.reshape(-1, n_lat, t.shape[-1])
        ma_l = _neighbourhood_attention(as_lat(pa), lat_seqs, cache_kt, cache_vt, l, rpb[l])
        (mr_l,) = _hgrn(as_lat(pr), as_lat(pg), lat_seqs, g_hgrn[l], hconsts, l, state=state_rows)
        mf_l = _fourier(as_lat(pf), lat_seqs, fconsts_lat, w_fnet_b[l])
        yp, ys = _out_projection((ma_c, mr_c, mf_c), (ma_l, mr_l, mf_l), yp, ys, mod_all, g_post, w_out, l)

    new_state = new_rows.reshape(nb_ctx, DEPTH, 2, R_HEADS, R_DIM, R_DIM)
    new_k, new_v = (jnp.transpose(t, (0, 1, 4, 2, 3)) for t in new_kv)
    return (yp, ys, new_k, new_v, new_state)
```
